```python
import math
import jax, jax.numpy as jnp
from jax import lax
import numpy as np

D_MODEL = 1024
BATCH = 2
SEQ = 16384
DEPTH = 2

N_A_LAYERS = DEPTH // 2
N_B_LAYERS = DEPTH - N_A_LAYERS

LRU_WIDTH = D_MODEL
LRU_BLOCKS = 4
LRU_BLOCK_W = LRU_WIDTH // LRU_BLOCKS
CONV_WIDTH = 4
LRU_C = 8.0

ATT_HEADS = 8
HEAD_DIM = 64
DILATION_PATTERNS = ((128, 1), (512, 4), (2048, 16))
N_GROUPS = len(DILATION_PATTERNS)
ATT_BLOCK = 128
Q_WIDTH = N_GROUPS * ATT_HEADS * HEAD_DIM
O_WIDTH = ATT_HEADS * HEAD_DIM

N_EXPERTS = 256
TOP_K = 8
N_EXPERT_GROUPS = 8
TOPK_GROUPS = 4
D_EXPERT = 256
ROUTED_SCALE = 2.5
EXPERT_BLOCK = 128

DEEPNORM_ALPHA = (2 * DEPTH) ** 0.25
DEEPNORM_BETA = (8 * DEPTH) ** -0.25
LN_EPS = 1e-5
MASK_VALUE = -1e30

kernel_name = 'yoco_rglru_dilated_attn_moe_deepnorm'


def layer_norm(x, g, b):
    xf = x.astype(jnp.float32)
    mu = jnp.mean(xf, axis=-1, keepdims=True)
    var = jnp.mean(jnp.square(xf - mu), axis=-1, keepdims=True)
    y = (xf - mu) * lax.rsqrt(var + LN_EPS)
    return (y * g.astype(jnp.float32) + b.astype(jnp.float32)).astype(x.dtype)


def _linear_combine(c1, c2):
    a1, b1 = c1
    a2, b2 = c2
    return a1 * a2, a2 * b1 + b2


def rglru_block(x, w_in, conv_w, conv_b, w_gate_a, b_gate_a, w_gate_x, b_gate_x, lam, w_out):
    B_, S_, _ = x.shape
    xz = x @ w_in
    xr, gate = jnp.split(xz, 2, axis=-1)
    xc = lax.conv_general_dilated(
        xr, conv_w[:, None, :], window_strides=(1,), padding=[(CONV_WIDTH - 1, 0)],
        dimension_numbers=('NWC', 'WIO', 'NWC'), feature_group_count=LRU_WIDTH) + conv_b
    xb = xc.reshape(B_, S_, LRU_BLOCKS, LRU_BLOCK_W)
    r = jax.nn.sigmoid(jnp.einsum('bsni,nij->bsnj', xb, w_gate_a).reshape(B_, S_, LRU_WIDTH) + b_gate_a)
    i = jax.nn.sigmoid(jnp.einsum('bsni,nij->bsnj', xb, w_gate_x).reshape(B_, S_, LRU_WIDTH) + b_gate_x)
    log_a = -LRU_C * r.astype(jnp.float32) * jax.nn.softplus(-lam.astype(jnp.float32))
    a = jnp.exp(log_a)
    mult = jnp.sqrt(-jnp.expm1(2.0 * log_a))
    u = mult * (i * xc).astype(jnp.float32)
    _, h = lax.associative_scan(_linear_combine, (a, u), axis=1)
    y = h.astype(x.dtype) * jax.nn.gelu(gate, approximate=True)
    return y @ w_out


def alibi_slopes():
    return 2.0 ** (-8.0 * jnp.arange(1, ATT_HEADS + 1, dtype=jnp.float32) / ATT_HEADS)


def dilated_branch(q, k, v, window, dilation, slopes):
    B_, S_, H, Dh = q.shape
    T = ATT_BLOCK
    n_steps = window // dilation
    span = dilation * T
    s_pad = -(-S_ // span) * span
    L = s_pad // dilation
    nb = L // T

    def to_sub(t):
        t = jnp.pad(t, ((0, 0), (0, s_pad - S_), (0, 0), (0, 0)))
        t = t.reshape(B_, L, dilation, H, Dh).transpose(0, 2, 1, 3, 4)
        return t.reshape(B_, dilation, nb, T, H, Dh)

    def with_prev(t):
        prev = jnp.pad(t[:, :, :-1], ((0, 0), (0, 0), (1, 0), (0, 0), (0, 0), (0, 0)))
        return jnp.concatenate([prev, t], axis=3)

    qs = to_sub(q)
    kb = with_prev(to_sub(k))
    vb = with_prev(to_sub(v))
    scores = jnp.einsum('brnqhd,brnkhd->brnhqk', qs, kb,
                        preferred_element_type=jnp.float32) * (HEAD_DIM ** -0.5)
    dist = (T + jnp.arange(T))[:, None] - jnp.arange(2 * T)[None, :]
    band = (dist >= 0) & (dist <= n_steps)
    not_first = (jnp.arange(nb) > 0)[:, None, None] | (jnp.arange(2 * T) >= T)[None, None, :]
    valid = band[None] & not_first
    bias = -slopes[:, None, None] * (dilation * dist).astype(jnp.float32)[None]
    scores = jnp.where(valid[:, None], scores + bias, MASK_VALUE)
    m = jnp.max(scores, axis=-1, keepdims=True)
    p = jnp.exp(scores - m)
    l = jnp.sum(p, axis=-1, keepdims=True)
    o = jnp.einsum('brnhqk,brnkhd->brnqhd', p / l, vb.astype(jnp.float32))
    lse = (m + jnp.log(l))[..., 0]
    o = o.reshape(B_, dilation, L, H, Dh).transpose(0, 2, 1, 3, 4).reshape(B_, s_pad, H, Dh)[:, :S_]
    lse = lse.transpose(0, 1, 2, 4, 3).reshape(B_, dilation, L, H).transpose(0, 2, 1, 3)
    lse = lse.reshape(B_, s_pad, H)[:, :S_]
    return o, lse


def dilated_attention(x, k_shared, v_shared, w_q, w_o, slopes):
    B_, S_, _ = x.shape
    q = (x @ w_q).reshape(B_, S_, N_GROUPS, ATT_HEADS, HEAD_DIM)
    outs, lses = [], []
    for g, (window, dil) in enumerate(DILATION_PATTERNS):
        o, lse = dilated_branch(q[:, :, g], k_shared[:, :, g], v_shared[:, :, g], window, dil, slopes)
        outs.append(o)
        lses.append(lse)
    w = jax.nn.softmax(jnp.stack(lses), axis=0)
    o = jnp.sum(w[..., None] * jnp.stack(outs), axis=0)
    return o.astype(x.dtype).reshape(B_, S_, O_WIDTH) @ w_o


def moe_ffn(x, w_router, router_bias, w1, w3, w2, ws1, ws3, ws2):
    B_, S_, D = x.shape
    N = B_ * S_
    xf = x.reshape(N, D)
    scores = jax.nn.sigmoid((xf @ w_router).astype(jnp.float32))
    biased = scores + router_bias.astype(jnp.float32)
    grp_score = jnp.sum(lax.top_k(biased.reshape(N, N_EXPERT_GROUPS, -1), 2)[0], axis=-1)
    _, grp_idx = lax.top_k(grp_score, TOPK_GROUPS)
    grp_mask = jnp.sum(jax.nn.one_hot(grp_idx, N_EXPERT_GROUPS, dtype=jnp.float32), axis=1) > 0
    expert_mask = jnp.repeat(grp_mask, N_EXPERTS // N_EXPERT_GROUPS, axis=1)
    _, top_idx = lax.top_k(jnp.where(expert_mask, biased, MASK_VALUE), TOP_K)
    top_s = jnp.take_along_axis(scores, top_idx, axis=1)
    gates = top_s / jnp.sum(top_s, axis=-1, keepdims=True) * ROUTED_SCALE

    T = EXPERT_BLOCK
    n_assign = N * TOP_K
    n_rows = -(-(n_assign + N_EXPERTS * (T - 1)) // T) * T
    n_blocks = n_rows // T
    flat_e = top_idx.reshape(-1).astype(jnp.int32)
    order = jnp.argsort(flat_e).astype(jnp.int32)
    sorted_e = flat_e[order]
    counts = jnp.zeros((N_EXPERTS,), jnp.int32).at[flat_e].add(1)
    padded = ((counts + T - 1) // T) * T
    pad_end = jnp.cumsum(padded)
    pad_start = pad_end - padded
    start = jnp.cumsum(counts) - counts
    dest = pad_start[sorted_e] + jnp.arange(n_assign, dtype=jnp.int32) - start[sorted_e]
    row_assign = jnp.full((n_rows,), n_assign, jnp.int32).at[dest].set(order)
    row_tok = row_assign // TOP_K
    block_expert = jnp.clip(jnp.searchsorted(pad_end, jnp.arange(n_blocks, dtype=jnp.int32) * T,
                                             side='right'), 0, N_EXPERTS - 1)
    x_rows = jnp.concatenate([xf, jnp.zeros((1, D), xf.dtype)], axis=0)[row_tok].reshape(n_blocks, T, D)

    def expert_block(args):
        xb, e = args
        h = jax.nn.silu(xb @ w1[e]) * (xb @ w3[e])
        return h @ w2[e]

    y_rows = lax.map(expert_block, (x_rows, block_expert)).reshape(n_rows, D)
    gate_rows = jnp.concatenate([gates.reshape(-1), jnp.zeros((1,), jnp.float32)])[row_assign]
    routed = jnp.zeros((N + 1, D), jnp.float32).at[row_tok].add(
        y_rows.astype(jnp.float32) * gate_rows[:, None])[:N]
    shared = (jax.nn.silu(xf @ ws1) * (xf @ ws3)) @ ws2
    return (routed.astype(x.dtype) + shared).reshape(B_, S_, D)


def setup_inputs(seed: int = 0) -> dict:
    key = jax.random.key(seed)
    ks = jax.random.split(key, 24)

    def nrm(k, shape, scale):
        return jax.random.normal(k, shape, jnp.float32) * scale

    x = nrm(ks[0], (BATCH, SEQ, D_MODEL), 1.0)
    a_w_in = nrm(ks[1], (N_A_LAYERS, D_MODEL, 2 * LRU_WIDTH), D_MODEL ** -0.5)
    a_conv_w = nrm(ks[2], (N_A_LAYERS, CONV_WIDTH, LRU_WIDTH), CONV_WIDTH ** -0.5)
    a_conv_b = nrm(ks[3], (N_A_LAYERS, LRU_WIDTH), 0.01)
    a_w_gate_a = nrm(ks[4], (N_A_LAYERS, LRU_BLOCKS, LRU_BLOCK_W, LRU_BLOCK_W), LRU_BLOCK_W ** -0.5)
    a_b_gate_a = nrm(ks[5], (N_A_LAYERS, LRU_WIDTH), 0.01)
    a_w_gate_x = nrm(ks[6], (N_A_LAYERS, LRU_BLOCKS, LRU_BLOCK_W, LRU_BLOCK_W), LRU_BLOCK_W ** -0.5)
    a_b_gate_x = nrm(ks[7], (N_A_LAYERS, LRU_WIDTH), 0.01)
    u = jax.random.uniform(ks[8], (N_A_LAYERS, LRU_WIDTH), jnp.float32, 0.9, 0.999)
    base = u ** (1.0 / LRU_C)
    a_lambda = jnp.log(base) - jnp.log1p(-base)
    a_w_out = nrm(ks[9], (N_A_LAYERS, LRU_WIDTH, D_MODEL), LRU_WIDTH ** -0.5 * DEEPNORM_BETA)
    w_k = nrm(ks[10], (D_MODEL, Q_WIDTH), D_MODEL ** -0.5)
    w_v = nrm(ks[11], (D_MODEL, Q_WIDTH), D_MODEL ** -0.5 * DEEPNORM_BETA)
    w_kv_shared = jnp.concatenate([w_k, w_v], axis=1)
    b_w_q = nrm(ks[12], (N_B_LAYERS, D_MODEL, Q_WIDTH), D_MODEL ** -0.5)
    b_w_o = nrm(ks[13], (N_B_LAYERS, O_WIDTH, D_MODEL), O_WIDTH ** -0.5 * DEEPNORM_BETA)
    moe_w_router = nrm(ks[14], (DEPTH, D_MODEL, N_EXPERTS), D_MODEL ** -0.5)
    moe_router_bias = nrm(ks[15], (DEPTH, N_EXPERTS), 0.01)
    moe_w1 = nrm(ks[16], (DEPTH, N_EXPERTS, D_MODEL, D_EXPERT), D_MODEL ** -0.5)
    moe_w3 = nrm(ks[17], (DEPTH, N_EXPERTS, D_MODEL, D_EXPERT), D_MODEL ** -0.5)
    moe_w2 = nrm(ks[18], (DEPTH, N_EXPERTS, D_EXPERT, D_MODEL), D_EXPERT ** -0.5 * DEEPNORM_BETA)
    moe_ws1 = nrm(ks[19], (DEPTH, D_MODEL, D_EXPERT), D_MODEL ** -0.5)
    moe_ws3 = nrm(ks[20], (DEPTH, D_MODEL, D_EXPERT), D_MODEL ** -0.5)
    moe_ws2 = nrm(ks[21], (DEPTH, D_EXPERT, D_MODEL), D_EXPERT ** -0.5 * DEEPNORM_BETA)
    ln_g = 1.0 + nrm(ks[22], (DEPTH, 2, D_MODEL), 0.02)
    ln_b = nrm(ks[23], (DEPTH, 2, D_MODEL), 0.02)
    return {'x': x, 'a_w_in': a_w_in, 'a_conv_w': a_conv_w, 'a_conv_b': a_conv_b,
            'a_w_gate_a': a_w_gate_a, 'a_b_gate_a': a_b_gate_a, 'a_w_gate_x': a_w_gate_x,
            'a_b_gate_x': a_b_gate_x, 'a_lambda': a_lambda, 'a_w_out': a_w_out,
            'w_kv_shared': w_kv_shared, 'b_w_q': b_w_q, 'b_w_o': b_w_o,
            'moe_w_router': moe_w_router, 'moe_router_bias': moe_router_bias,
            'moe_w1': moe_w1, 'moe_w3': moe_w3, 'moe_w2': moe_w2,
            'moe_ws1': moe_ws1, 'moe_ws3': moe_ws3, 'moe_ws2': moe_ws2,
            'ln_g': ln_g, 'ln_b': ln_b}


def reference(x, a_w_in, a_conv_w, a_conv_b, a_w_gate_a, a_b_gate_a, a_w_gate_x, a_b_gate_x,
              a_lambda, a_w_out, w_kv_shared, b_w_q, b_w_o, moe_w_router, moe_router_bias,
              moe_w1, moe_w3, moe_w2, moe_ws1, moe_ws3, moe_ws2, ln_g, ln_b):
    B_, S_, _ = x.shape
    slopes = alibi_slopes()
    h = x
    k_shared = None
    v_shared = None
    for layer in range(DEPTH):
        if layer < N_A_LAYERS:
            i = layer
            mix = rglru_block(h, a_w_in[i], a_conv_w[i], a_conv_b[i], a_w_gate_a[i], a_b_gate_a[i],
                              a_w_gate_x[i], a_b_gate_x[i], a_lambda[i], a_w_out[i])
        else:
            j = layer - N_A_LAYERS
            mix = dilated_attention(h, k_shared, v_shared, b_w_q[j], b_w_o[j], slopes)
        h = layer_norm(DEEPNORM_ALPHA * h + mix, ln_g[layer, 0], ln_b[layer, 0])
        ffn = moe_ffn(h, moe_w_router[layer], moe_router_bias[layer], moe_w1[layer], moe_w3[layer],
                      moe_w2[layer], moe_ws1[layer], moe_ws3[layer], moe_ws2[layer])
        h = layer_norm(DEEPNORM_ALPHA * h + ffn, ln_g[layer, 1], ln_b[layer, 1])
        if layer == N_A_LAYERS - 1:
            kv = (h @ w_kv_shared).reshape(B_, S_, 2, N_GROUPS, ATT_HEADS, HEAD_DIM)
            k_shared = kv[:, :, 0]
            v_shared = kv[:, :, 1]
    return h
```

```python
import functools
import math

import jax
import jax.numpy as jnp
from jax import lax
from jax.experimental import pallas as pl
from jax.experimental.pallas import tpu as pltpu

F32 = jnp.float32
BF16 = jnp.bfloat16

D_MODEL = 1024
LRU_BLOCKS = 4
CONV_WIDTH = 4
LRU_C = 8.0
ATT_HEADS = 8
HEAD_DIM = 64
DILATION_PATTERNS = ((128, 1), (512, 4), (2048, 16))
N_GROUPS = len(DILATION_PATTERNS)
ATT_BLOCK = 128
O_WIDTH = ATT_HEADS * HEAD_DIM
N_EXPERTS = 256
TOP_K = 8
N_EXPERT_GROUPS = 8
TOPK_GROUPS = 4
D_EXPERT = 256
ROUTED_SCALE = 2.5
DEPTH = 2
DEEPNORM_ALPHA = (2 * DEPTH) ** 0.25
LN_EPS = 1e-5
MASK_VALUE = -1e30

SUBLANES = 8
LANES = 128
TOK_ROWS = D_MODEL // LANES
VMEM_LIMIT = 56 * 1024 * 1024

RGLRU_TS = 256
ROUTER_TT = 256
DISPATCH_TT = 256
EXPERT_TB = 256
COMBINE_TT = 128
PROJ_TT = 512
MERGE_TT = 256


def _rows_load(ref, n_tok):
    return jnp.concatenate(
        [ref[pl.ds(s, n_tok, stride=TOK_ROWS), :] for s in range(TOK_ROWS)], axis=1)


def _rows_store(ref, val, n_tok):
    for s in range(TOK_ROWS):
        ref[pl.ds(s, n_tok, stride=TOK_ROWS), :] = val[:, s * LANES:(s + 1) * LANES]


def _layer_norm(z, g, b):
    mu = jnp.mean(z, axis=-1, keepdims=True)
    zc = z - mu
    var = jnp.mean(zc * zc, axis=-1, keepdims=True)
    return zc * lax.rsqrt(var + LN_EPS) * g + b


def _silu(x):
    return x * jax.nn.sigmoid(x)


def _gelu_tanh(x):
    c = math.sqrt(2.0 / math.pi)
    return 0.5 * x * (1.0 + jnp.tanh(c * (x + 0.044715 * (x * x * x))))


def _bdot(a, b):
    return jnp.dot(a, b, preferred_element_type=F32)


def _rglru_body(x_ref, win_ref, cw_ref, cb_ref, wga_ref, bga_ref, wgx_ref, bgx_ref, lam_ref,
                wout_ref, g_ref, b_ref, o_ref, hc_ref, tail_ref, *, ts):
    width = D_MODEL
    bw = width // LRU_BLOCKS

    @pl.when(pl.program_id(1) == 0)
    def _():
        hc_ref[...] = jnp.zeros_like(hc_ref)
        tail_ref[...] = jnp.zeros_like(tail_ref)

    x = x_ref[...]
    xz = _bdot(x.astype(BF16), win_ref[...])
    xr = xz[:, :width]
    gate = xz[:, width:]

    tail = tail_ref[...]
    row8 = lax.broadcasted_iota(jnp.int32, (SUBLANES, width), 0)
    cw = cw_ref[...]
    xc = xr * cw[CONV_WIDTH - 1:CONV_WIDTH, :] + cb_ref[...]
    for j in range(1, CONV_WIDTH):
        rx = pltpu.roll(xr, j, 0)
        rp = pltpu.roll(tail, j, 0)
        top = jnp.where(row8 < j, rp, rx[:SUBLANES])
        shifted = jnp.concatenate([top, rx[SUBLANES:]], axis=0)
        xc = xc + shifted * cw[CONV_WIDTH - 1 - j:CONV_WIDTH - j, :]
    tail_ref[...] = xr[ts - SUBLANES:]

    xcb = xc.astype(BF16)

    def block_diag(w_ref):
        return jnp.concatenate(
            [_bdot(xcb[:, n * bw:(n + 1) * bw], w_ref[n]) for n in range(LRU_BLOCKS)], axis=1)

    r = jax.nn.sigmoid(block_diag(wga_ref) + bga_ref[...])
    i = jax.nn.sigmoid(block_diag(wgx_ref) + bgx_ref[...])
    lam = lam_ref[...]
    softplus_neg_lam = jnp.maximum(-lam, 0.0) + jnp.log1p(jnp.exp(-jnp.abs(lam)))
    log_a = (-LRU_C * r) * softplus_neg_lam
    a = jnp.exp(log_a)
    mult = jnp.sqrt(-jnp.tanh(log_a) * (a * a + 1.0))
    u = mult * (i * xc)

    rows = lax.broadcasted_iota(jnp.int32, (ts, width), 0)
    cum_a, cum_u = a, u
    sh = 1
    while sh < ts:
        a_prev = pltpu.roll(cum_a, sh, 0)
        u_prev = pltpu.roll(cum_u, sh, 0)
        live = rows >= sh
        cum_u = jnp.where(live, cum_a * u_prev, 0.0) + cum_u
        cum_a = jnp.where(live, cum_a * a_prev, cum_a)
        sh *= 2
    h = cum_a * hc_ref[...] + cum_u
    hc_ref[...] = h[ts - 1:ts]

    y = (h * _gelu_tanh(gate)).astype(BF16)
    mix = _bdot(y, wout_ref[...])
    z = DEEPNORM_ALPHA * x + mix
    _rows_store(o_ref, _layer_norm(z, g_ref[...], b_ref[...]), ts)


def _rglru_layer(x, w_in, conv_w, conv_b, wga, bga, wgx, bgx, lam, w_out, g, b):
    bsz, seq, d = x.shape
    ts = RGLRU_TS
    ns = seq // ts
    row = lambda v: v.reshape(1, -1)
    full = lambda shape: pl.BlockSpec(shape, lambda bi, si: (0,) * len(shape))
    return pl.pallas_call(
        functools.partial(_rglru_body, ts=ts),
        grid=(bsz, ns),
        in_specs=[
            pl.BlockSpec((None, ts, d), lambda bi, si: (bi, si, 0)),
            full((d, 2 * d)), full((CONV_WIDTH, d)), full((1, d)),
            full((LRU_BLOCKS, d // LRU_BLOCKS, d // LRU_BLOCKS)), full((1, d)),
            full((LRU_BLOCKS, d // LRU_BLOCKS, d // LRU_BLOCKS)), full((1, d)),
            full((1, d)), full((d, d)), full((1, d)), full((1, d)),
        ],
        out_specs=pl.BlockSpec((ts * TOK_ROWS, LANES), lambda bi, si: (bi * ns + si, 0)),
        out_shape=jax.ShapeDtypeStruct((bsz * seq * TOK_ROWS, LANES), F32),
        scratch_shapes=[pltpu.VMEM((1, d), F32), pltpu.VMEM((SUBLANES, d), F32)],
        compiler_params=pltpu.CompilerParams(
            dimension_semantics=("arbitrary", "arbitrary"), vmem_limit_bytes=VMEM_LIMIT),
        name="rglru_layer",
    )(x, w_in.astype(BF16), conv_w, row(conv_b), wga.astype(BF16), row(bga), wgx.astype(BF16),
      row(bgx), row(lam), w_out.astype(BF16), row(g), row(b))


def _router_body(h_ref, wrt_ref, bias_ref, idx_ref, gate_ref, rank_ref, cnt_ref, carry_ref, *, tt):
    n_e = N_EXPERTS
    per_group = n_e // N_EXPERT_GROUPS

    @pl.when(pl.program_id(0) == 0)
    def _():
        carry_ref[...] = jnp.zeros_like(carry_ref)

    h = _rows_load(h_ref, tt)
    logits = lax.dot_general(wrt_ref[...], h, (((1,), (1,)), ((), ())),
                             precision=lax.Precision.HIGHEST, preferred_element_type=F32)
    scores = jax.nn.sigmoid(logits)
    biased = scores + bias_ref[...]

    j_iota = lax.broadcasted_iota(jnp.int32, (per_group, tt), 0)
    group_score = []
    for g in range(N_EXPERT_GROUPS):
        bg = biased[g * per_group:(g + 1) * per_group]
        m1 = jnp.max(bg, axis=0, keepdims=True)
        i1 = jnp.min(jnp.where(bg == m1, j_iota, per_group), axis=0, keepdims=True)
        m2 = jnp.max(jnp.where(j_iota == i1, -jnp.inf, bg), axis=0, keepdims=True)
        group_score.append(m1 + m2)

    masked = []
    for g in range(N_EXPERT_GROUPS):
        beaten_by = jnp.zeros((1, tt), jnp.int32)
        for o in range(N_EXPERT_GROUPS):
            if o == g:
                continue
            wins = group_score[o] > group_score[g]
            if o < g:
                wins = wins | (group_score[o] == group_score[g])
            beaten_by = beaten_by + wins.astype(jnp.int32)
        keep = beaten_by < TOPK_GROUPS
        masked.append(jnp.where(keep, biased[g * per_group:(g + 1) * per_group], MASK_VALUE))
    cur = jnp.concatenate(masked, axis=0)

    e_iota = lax.broadcasted_iota(jnp.int32, (n_e, tt), 0)
    idx_rows, score_rows, sels = [], [], []
    multi_hot = jnp.zeros((n_e, tt), F32)
    for _ in range(TOP_K):
        m = jnp.max(cur, axis=0, keepdims=True)
        ik = jnp.min(jnp.where(cur == m, e_iota, n_e), axis=0, keepdims=True)
        sel = e_iota == ik
        score_rows.append(jnp.sum(jnp.where(sel, scores, 0.0), axis=0, keepdims=True))
        cur = jnp.where(sel, -jnp.inf, cur)
        multi_hot = multi_hot + jnp.where(sel, 1.0, 0.0)
        idx_rows.append(ik)
        sels.append(sel)
    top_s = jnp.concatenate(score_rows, axis=0)
    gate_ref[...] = top_s / jnp.sum(top_s, axis=0, keepdims=True) * ROUTED_SCALE
    idx_ref[...] = jnp.concatenate(idx_rows, axis=0)

    t_row = lax.broadcasted_iota(jnp.int32, (tt, tt), 0)
    t_col = lax.broadcasted_iota(jnp.int32, (tt, tt), 1)
    strict_upper = jnp.where(t_row < t_col, 1.0, 0.0).astype(BF16)
    before = _bdot(multi_hot.astype(BF16), strict_upper) + carry_ref[...]
    rank_rows = [jnp.sum(jnp.where(sel, before, 0.0), axis=0, keepdims=True) for sel in sels]
    rank_ref[...] = jnp.concatenate(rank_rows, axis=0).astype(jnp.int32)
    carry = carry_ref[...] + jnp.sum(multi_hot, axis=1, keepdims=True)
    carry_ref[...] = carry
    cnt_ref[...] = carry.astype(jnp.int32)


def _router(h_tiles, w_router, router_bias):
    n_tok = h_tiles.shape[0] // TOK_ROWS
    tt = ROUTER_TT
    kt = lambda dt: jax.ShapeDtypeStruct((TOP_K, n_tok), dt)
    tok_spec = pl.BlockSpec((TOP_K, tt), lambda i: (0, i))
    return pl.pallas_call(
        functools.partial(_router_body, tt=tt),
        grid=(n_tok // tt,),
        in_specs=[
            pl.BlockSpec((tt * TOK_ROWS, LANES), lambda i: (i, 0)),
            pl.BlockSpec((N_EXPERTS, D_MODEL), lambda i: (0, 0)),
            pl.BlockSpec((N_EXPERTS, 1), lambda i: (0, 0)),
        ],
        out_specs=[tok_spec, tok_spec, tok_spec, pl.BlockSpec((N_EXPERTS, 1), lambda i: (0, 0))],
        out_shape=[kt(jnp.int32), kt(F32), kt(jnp.int32),
                   jax.ShapeDtypeStruct((N_EXPERTS, 1), jnp.int32)],
        scratch_shapes=[pltpu.VMEM((N_EXPERTS, 1), F32)],
        compiler_params=pltpu.CompilerParams(
            dimension_semantics=("arbitrary",), vmem_limit_bytes=VMEM_LIMIT),
        name="moe_router",
    )(h_tiles, w_router.T, router_bias.reshape(N_EXPERTS, 1))


def _token_tile(ref, t):
    return ref.at[pl.ds(pl.multiple_of(t * TOK_ROWS, TOK_ROWS), TOK_ROWS), :]


def _dispatch_body(start_ref, idx_ref, rank_ref, h_ref, xs_in_ref, xs_ref, sem, *, tt):
    del xs_in_ref

    def row_copy(t, k):
        dest = start_ref[idx_ref[k, t]] + rank_ref[k, t]
        return pltpu.make_async_copy(_token_tile(h_ref, t), _token_tile(xs_ref, dest), sem)

    def issue(t, c):
        for k in range(TOP_K):
            row_copy(t, k).start()
        return c

    def drain(t, c):
        for k in range(TOP_K):
            row_copy(t, k).wait()
        return c

    lax.fori_loop(0, tt, issue, 0)
    lax.fori_loop(0, tt, drain, 0)


def _dispatch(h_tiles, top_idx, rank, pad_start, n_rows):
    n_tok = h_tiles.shape[0] // TOK_ROWS
    tt = DISPATCH_TT
    smem_tok = pl.BlockSpec((TOP_K, tt), lambda i, s: (0, i), memory_space=pltpu.SMEM)
    grid_spec = pltpu.PrefetchScalarGridSpec(
        num_scalar_prefetch=1,
        grid=(n_tok // tt,),
        in_specs=[smem_tok, smem_tok,
                  pl.BlockSpec((tt * TOK_ROWS, LANES), lambda i, s: (i, 0)),
                  pl.BlockSpec(memory_space=pl.ANY)],
        out_specs=pl.BlockSpec(memory_space=pl.ANY),
        scratch_shapes=[pltpu.SemaphoreType.DMA(())],
    )
    zeros = jnp.zeros((n_rows * TOK_ROWS, LANES), F32)
    return pl.pallas_call(
        functools.partial(_dispatch_body, tt=tt),
        grid_spec=grid_spec,
        out_shape=jax.ShapeDtypeStruct((n_rows * TOK_ROWS, LANES), F32),
        input_output_aliases={4: 0},
        compiler_params=pltpu.CompilerParams(
            dimension_semantics=("arbitrary",), vmem_limit_bytes=VMEM_LIMIT),
        name="moe_dispatch",
    )(pad_start, top_idx, rank, h_tiles, zeros)


def _expert_body(be_ref, na_ref, xs_ref, w1_ref, w3_ref, w2_ref, ys_ref, w1b, w3b, w2b, *, tb):
    i = pl.program_id(0)

    @pl.when(i < na_ref[0])
    def _():
        @pl.when((i == 0) | (be_ref[i] != be_ref[jnp.maximum(i - 1, 0)]))
        def _():
            w1b[...] = w1_ref[...].astype(BF16)
            w3b[...] = w3_ref[...].astype(BF16)
            w2b[...] = w2_ref[...].astype(BF16)

        x = _rows_load(xs_ref, tb).astype(BF16)
        mid = (_silu(_bdot(x, w1b[...])) * _bdot(x, w3b[...])).astype(BF16)
        _rows_store(ys_ref, _bdot(mid, w2b[...]), tb)


def _experts(xs, block_expert, n_active, w1, w3, w2):
    tb = EXPERT_TB
    n_blocks = xs.shape[0] // (tb * TOK_ROWS)
    row_block = lambda i, be, na: (jnp.minimum(i, na[0] - 1), 0)
    grid_spec = pltpu.PrefetchScalarGridSpec(
        num_scalar_prefetch=2,
        grid=(n_blocks,),
        in_specs=[
            pl.BlockSpec((tb * TOK_ROWS, LANES), row_block),
            pl.BlockSpec((None, D_MODEL, D_EXPERT), lambda i, be, na: (be[i], 0, 0)),
            pl.BlockSpec((None, D_MODEL, D_EXPERT), lambda i, be, na: (be[i], 0, 0)),
            pl.BlockSpec((None, D_EXPERT, D_MODEL), lambda i, be, na: (be[i], 0, 0)),
        ],
        out_specs=pl.BlockSpec((tb * TOK_ROWS, LANES), row_block),
        scratch_shapes=[pltpu.VMEM((D_MODEL, D_EXPERT), BF16), pltpu.VMEM((D_MODEL, D_EXPERT), BF16),
                        pltpu.VMEM((D_EXPERT, D_MODEL), BF16)],
    )
    return pl.pallas_call(
        functools.partial(_expert_body, tb=tb),
        grid_spec=grid_spec,
        out_shape=jax.ShapeDtypeStruct(xs.shape, F32),
        compiler_params=pltpu.CompilerParams(
            dimension_semantics=("arbitrary",), vmem_limit_bytes=VMEM_LIMIT),
        name="moe_experts",
    )(block_expert, n_active, xs, w1, w3, w2)


def _combine_body(start_ref, idx_ref, rank_ref, h_ref, gates_ref, ys_ref, ws1_ref, ws3_ref, ws2_ref,
                  g_ref, b_ref, o_ref, gbuf, sem, *, tt, token_tile_out):
    def row_copy(t, k):
        src = start_ref[idx_ref[k, t]] + rank_ref[k, t]
        return pltpu.make_async_copy(_token_tile(ys_ref, src), _token_tile(gbuf.at[k], t), sem)

    def issue(t, c):
        for k in range(TOP_K):
            row_copy(t, k).start()
        return c

    def drain(t, c):
        for k in range(TOP_K):
            row_copy(t, k).wait()
        return c

    lax.fori_loop(0, tt, issue, 0)

    h = _rows_load(h_ref, tt)
    hb = h.astype(BF16)
    mid = (_silu(_bdot(hb, ws1_ref[...])) * _bdot(hb, ws3_ref[...])).astype(BF16)
    shared = _bdot(mid, ws2_ref[...])

    lax.fori_loop(0, tt, drain, 0)

    gates = gates_ref[...]
    routed = gates[:, 0:1] * _rows_load(gbuf.at[0], tt)
    for k in range(1, TOP_K):
        routed = routed + gates[:, k:k + 1] * _rows_load(gbuf.at[k], tt)
    out = _layer_norm(DEEPNORM_ALPHA * h + (routed + shared), g_ref[...], b_ref[...])
    if token_tile_out:
        _rows_store(o_ref, out, tt)
    else:
        o_ref[...] = out


def _combine(h_tiles, top_idx, rank, gates_tk, pad_start, ys, ws1, ws3, ws2, g, b, token_tile_out):
    n_tok = h_tiles.shape[0] // TOK_ROWS
    tt = COMBINE_TT
    smem_tok = pl.BlockSpec((TOP_K, tt), lambda i, s: (0, i), memory_space=pltpu.SMEM)
    full = lambda shape: pl.BlockSpec(shape, lambda i, s: (0,) * len(shape))
    if token_tile_out:
        out_spec = pl.BlockSpec((tt * TOK_ROWS, LANES), lambda i, s: (i, 0))
        out_shape = jax.ShapeDtypeStruct((n_tok * TOK_ROWS, LANES), F32)
    else:
        out_spec = pl.BlockSpec((tt, D_MODEL), lambda i, s: (i, 0))
        out_shape = jax.ShapeDtypeStruct((n_tok, D_MODEL), F32)
    grid_spec = pltpu.PrefetchScalarGridSpec(
        num_scalar_prefetch=1,
        grid=(n_tok // tt,),
        in_specs=[smem_tok, smem_tok,
                  pl.BlockSpec((tt * TOK_ROWS, LANES), lambda i, s: (i, 0)),
                  pl.BlockSpec((tt, TOP_K), lambda i, s: (i, 0)),
                  pl.BlockSpec(memory_space=pl.ANY),
                  full((D_MODEL, D_EXPERT)), full((D_MODEL, D_EXPERT)), full((D_EXPERT, D_MODEL)),
                  full((1, D_MODEL)), full((1, D_MODEL))],
        out_specs=out_spec,
        scratch_shapes=[pltpu.VMEM((TOP_K, tt * TOK_ROWS, LANES), F32), pltpu.SemaphoreType.DMA(())],
    )
    return pl.pallas_call(
        functools.partial(_combine_body, tt=tt, token_tile_out=token_tile_out),
        grid_spec=grid_spec,
        out_shape=out_shape,
        compiler_params=pltpu.CompilerParams(
            dimension_semantics=("arbitrary",), vmem_limit_bytes=VMEM_LIMIT),
        name="moe_combine",
    )(pad_start, top_idx, rank, h_tiles, gates_tk, ys, ws1.astype(BF16), ws3.astype(BF16),
      ws2.astype(BF16), g.reshape(1, -1), b.reshape(1, -1))


def _moe_layer(h_tiles, w_router, router_bias, w1, w3, w2, ws1, ws3, ws2, g, b, token_tile_out):
    n_tok = h_tiles.shape[0] // TOK_ROWS
    tb = EXPERT_TB
    top_idx, gates, rank, counts = _router(h_tiles, w_router, router_bias)

    n_blocks = -(-(n_tok * TOP_K + N_EXPERTS * (tb - 1)) // tb)
    counts = counts.reshape(N_EXPERTS)
    padded = ((counts + tb - 1) // tb) * tb
    pad_end = jnp.cumsum(padded)
    pad_start = (pad_end - padded).astype(jnp.int32)
    block_expert = jnp.clip(
        jnp.searchsorted(pad_end, jnp.arange(n_blocks, dtype=jnp.int32) * tb, side='right'),
        0, N_EXPERTS - 1).astype(jnp.int32)
    n_active = (pad_end[-1:] // tb).astype(jnp.int32)

    xs = _dispatch(h_tiles, top_idx, rank, pad_start, n_blocks * tb)
    ys = _experts(xs, block_expert, n_active, w1, w3, w2)
    return _combine(h_tiles, top_idx, rank, gates.T, pad_start, ys, ws1, ws3, ws2, g, b,
                    token_tile_out)


def _proj_body(h_ref, w_ref, *o_refs, tt, scale):
    y = _bdot(_rows_load(h_ref, tt).astype(BF16), w_ref[...])
    for j, o_ref in enumerate(o_refs):
        o_ref[...] = (y[:, j * O_WIDTH:(j + 1) * O_WIDTH] * scale).astype(BF16)


def _project(h_tiles, w, scale):
    n_tok = h_tiles.shape[0] // TOK_ROWS
    tt = PROJ_TT
    n_out = w.shape[1] // O_WIDTH
    out_spec = pl.BlockSpec((tt, O_WIDTH), lambda i: (i, 0))
    return pl.pallas_call(
        functools.partial(_proj_body, tt=tt, scale=scale),
        grid=(n_tok // tt,),
        in_specs=[pl.BlockSpec((tt * TOK_ROWS, LANES), lambda i: (i, 0)),
                  pl.BlockSpec(w.shape, lambda i: (0, 0))],
        out_specs=[out_spec] * n_out,
        out_shape=[jax.ShapeDtypeStruct((n_tok, O_WIDTH), BF16)] * n_out,
        compiler_params=pltpu.CompilerParams(
            dimension_semantics=("arbitrary",), vmem_limit_bytes=VMEM_LIMIT),
        name="projection",
    )(h_tiles, w.astype(BF16))


def _attn_body(q_ref, kp_ref, kc_ref, vp_ref, vc_ref, o_ref, lse_ref, *, dilation, n_steps):
    t = ATT_BLOCK
    not_first = pl.program_id(2) > 0
    qi = lax.broadcasted_iota(jnp.int32, (t, 2 * t), 0)
    kj = lax.broadcasted_iota(jnp.int32, (t, 2 * t), 1)
    dist = t + qi - kj
    valid = (dist >= 0) & (dist <= n_steps) & (not_first | (kj >= t))
    token_dist = (dilation * dist).astype(F32)
    low_half = lax.broadcasted_iota(jnp.int32, (t, LANES), 1) < HEAD_DIM
    contract_last = (((1,), (1,)), ((), ()))

    for pair in range(ATT_HEADS // 2):
        sl = slice(pair * LANES, (pair + 1) * LANES)
        q2 = q_ref[:, sl]
        k2 = jnp.concatenate([kp_ref[:, sl], kc_ref[:, sl]], axis=0)
        v2 = jnp.concatenate([vp_ref[:, sl], vc_ref[:, sl]], axis=0)
        outs, lses = [], []
        for half in range(2):
            head = 2 * pair + half
            slope = 2.0 ** (-8.0 * (head + 1) / ATT_HEADS)
            keep = low_half if half == 0 else ~low_half
            qh = jnp.where(keep, q2, jnp.zeros_like(q2))
            s = lax.dot_general(qh, k2, contract_last, preferred_element_type=F32)
            s = jnp.where(valid, s - slope * token_dist, MASK_VALUE)
            m = jnp.max(s, axis=-1, keepdims=True)
            p = jnp.exp(s - m)
            l = jnp.sum(p, axis=-1, keepdims=True)
            outs.append(_bdot(p.astype(BF16), v2) / l)
            lses.append(m + jnp.log(l))
        o_ref[:, sl] = jnp.where(low_half, outs[0], outs[1])
        lse_ref[:, sl] = jnp.where(low_half, lses[0], lses[1])


def _attn_branch(q, k, v, bsz, seq, window, dilation):
    t = ATT_BLOCK
    sub_len = seq // dilation
    nb = sub_len // t
    shape3 = (bsz, sub_len, dilation * O_WIDTH)
    q, k, v = (a.reshape(shape3) for a in (q, k, v))
    cur = pl.BlockSpec((None, t, O_WIDTH), lambda bi, r, n: (bi, n, r))
    prev = pl.BlockSpec((None, t, O_WIDTH), lambda bi, r, n: (bi, jnp.maximum(n - 1, 0), r))
    o, lse = pl.pallas_call(
        functools.partial(_attn_body, dilation=dilation, n_steps=window // dilation),
        grid=(bsz, dilation, nb),
        in_specs=[cur, prev, cur, prev, cur],
        out_specs=[cur, cur],
        out_shape=[jax.ShapeDtypeStruct(shape3, F32)] * 2,
        compiler_params=pltpu.CompilerParams(
            dimension_semantics=("arbitrary",) * 3, vmem_limit_bytes=VMEM_LIMIT),
        name=f"dilated_attn_d{dilation}",
    )(q, k, k, v, v)
    return o.reshape(bsz * seq, O_WIDTH), lse.reshape(bsz * seq, O_WIDTH)


def _merge_body(*refs, tt):
    o_refs = refs[:N_GROUPS]
    lse_refs = refs[N_GROUPS:2 * N_GROUPS]
    h_ref, wo_ref, g_ref, b_ref, out_ref = refs[2 * N_GROUPS:]
    lses = [r[...] for r in lse_refs]
    m = functools.reduce(jnp.maximum, lses)
    ws = [jnp.exp(l - m) for l in lses]
    den = functools.reduce(lambda a, c: a + c, ws)
    o = functools.reduce(lambda a, c: a + c, [w * r[...] for w, r in zip(ws, o_refs)]) / den
    mix = _bdot(o.astype(BF16), wo_ref[...])
    z = DEEPNORM_ALPHA * _rows_load(h_ref, tt) + mix
    _rows_store(out_ref, _layer_norm(z, g_ref[...], b_ref[...]), tt)


def _merge(outs, lses, h_tiles, w_o, g, b):
    n_tok = h_tiles.shape[0] // TOK_ROWS
    tt = MERGE_TT
    tok = pl.BlockSpec((tt, O_WIDTH), lambda i: (i, 0))
    tiles = pl.BlockSpec((tt * TOK_ROWS, LANES), lambda i: (i, 0))
    full = lambda shape: pl.BlockSpec(shape, lambda i: (0,) * len(shape))
    return pl.pallas_call(
        functools.partial(_merge_body, tt=tt),
        grid=(n_tok // tt,),
        in_specs=[tok] * (2 * N_GROUPS) + [tiles, full((O_WIDTH, D_MODEL)), full((1, D_MODEL)),
                                           full((1, D_MODEL))],
        out_specs=tiles,
        out_shape=jax.ShapeDtypeStruct(h_tiles.shape, F32),
        compiler_params=pltpu.CompilerParams(
            dimension_semantics=("arbitrary",), vmem_limit_bytes=VMEM_LIMIT),
        name="attn_merge",
    )(*outs, *lses, h_tiles, w_o.astype(BF16), g.reshape(1, -1), b.reshape(1, -1))


def kernel(x, a_w_in, a_conv_w, a_conv_b, a_w_gate_a, a_b_gate_a, a_w_gate_x, a_b_gate_x, a_lambda, a_w_out, w_kv_shared, b_w_q, b_w_o, moe_w_router, moe_router_bias, moe_w1, moe_w3, moe_w2, moe_ws1, moe_ws3, moe_ws2, ln_g, ln_b):
    bsz, seq, d = x.shape
    assert d == D_MODEL and seq % (DILATION_PATTERNS[-1][1] * ATT_BLOCK) == 0

    def moe(h_tiles, layer, token_tile_out):
        return _moe_layer(h_tiles, moe_w_router[layer], moe_router_bias[layer], moe_w1[layer],
                          moe_w3[layer], moe_w2[layer], moe_ws1[layer], moe_ws3[layer],
                          moe_ws2[layer], ln_g[layer, 1], ln_b[layer, 1], token_tile_out)

    h = _rglru_layer(x, a_w_in[0], a_conv_w[0], a_conv_b[0], a_w_gate_a[0], a_b_gate_a[0],
                     a_w_gate_x[0], a_b_gate_x[0], a_lambda[0], a_w_out[0], ln_g[0, 0], ln_b[0, 0])
    h = moe(h, 0, True)
    kv = _project(h, w_kv_shared, 1.0)
    ks, vs = kv[:N_GROUPS], kv[N_GROUPS:]

    qs = _project(h, b_w_q[0], HEAD_DIM ** -0.5)
    outs, lses = [], []
    for grp, (window, dilation) in enumerate(DILATION_PATTERNS):
        o, lse = _attn_branch(qs[grp], ks[grp], vs[grp], bsz, seq, window, dilation)
        outs.append(o)
        lses.append(lse)
    h = _merge(outs, lses, h, b_w_o[0], ln_g[1, 0], ln_b[1, 0])
    h = moe(h, 1, False)
    return h.reshape(bsz, seq, d)
```

```python
import functools
import math

import jax
import jax.numpy as jnp
from jax import lax
from jax.experimental import pallas as pl
from jax.experimental.pallas import tpu as pltpu

F32 = jnp.float32
BF16 = jnp.bfloat16

D_MODEL = 1024
LRU_BLOCKS = 4
CONV_WIDTH = 4
LRU_C = 8.0
ATT_HEADS = 8
HEAD_DIM = 64
DILATION_PATTERNS = ((128, 1), (512, 4), (2048, 16))
N_GROUPS = len(DILATION_PATTERNS)
ATT_BLOCK = 128
O_WIDTH = ATT_HEADS * HEAD_DIM
N_EXPERTS = 256
TOP_K = 8
N_EXPERT_GROUPS = 8
TOPK_GROUPS = 4
D_EXPERT = 256
ROUTED_SCALE = 2.5
DEPTH = 2
DEEPNORM_ALPHA = (2 * DEPTH) ** 0.25
LN_EPS = 1e-5
MASK_VALUE = -1e30

SUBLANES = 8
LANES = 128
TOK_ROWS = D_MODEL // LANES
VMEM_LIMIT = 56 * 1024 * 1024

RGLRU_TS = 256
ROUTER_TT = 256
DISPATCH_TT = 256
EXPERT_TB = 256
COMBINE_TT = 128
PROJ_TT = 512
MERGE_TT = 256


def _rows_load(ref, n_tok):
    return jnp.concatenate(
        [ref[pl.ds(s, n_tok, stride=TOK_ROWS), :] for s in range(TOK_ROWS)], axis=1)


def _rows_store(ref, val, n_tok):
    for s in range(TOK_ROWS):
        ref[pl.ds(s, n_tok, stride=TOK_ROWS), :] = val[:, s * LANES:(s + 1) * LANES]


def _layer_norm(z, g, b):
    mu = jnp.mean(z, axis=-1, keepdims=True)
    zc = z - mu
    var = jnp.mean(zc * zc, axis=-1, keepdims=True)
    return zc * lax.rsqrt(var + LN_EPS) * g + b


def _silu(x):
    return x * jax.nn.sigmoid(x)


def _gelu_tanh(x):
    c = math.sqrt(2.0 / math.pi)
    return 0.5 * x * (1.0 + jnp.tanh(c * (x + 0.044715 * (x * x * x))))


def _bdot(a, b):
    return jnp.dot(a, b, preferred_element_type=F32)


def _rglru_body(x_ref, win_ref, cw_ref, cb_ref, wga_ref, bga_ref, wgx_ref, bgx_ref, lam_ref,
                wout_ref, g_ref, b_ref, o_ref, hc_ref, tail_ref, *, ts):
    width = D_MODEL
    bw = width // LRU_BLOCKS

    @pl.when(pl.program_id(1) == 0)
    def _():
        hc_ref[...] = jnp.zeros_like(hc_ref)
        tail_ref[...] = jnp.zeros_like(tail_ref)

    x = x_ref[...]
    xz = _bdot(x.astype(BF16), win_ref[...])
    xr = xz[:, :width]
    gate = xz[:, width:]

    tail = tail_ref[...]
    row8 = lax.broadcasted_iota(jnp.int32, (SUBLANES, width), 0)
    cw = cw_ref[...]
    xc = xr * cw[CONV_WIDTH - 1:CONV_WIDTH, :] + cb_ref[...]
    for j in range(1, CONV_WIDTH):
        rx = pltpu.roll(xr, j, 0)
        rp = pltpu.roll(tail, j, 0)
        top = jnp.where(row8 < j, rp, rx[:SUBLANES])
        shifted = jnp.concatenate([top, rx[SUBLANES:]], axis=0)
        xc = xc + shifted * cw[CONV_WIDTH - 1 - j:CONV_WIDTH - j, :]
    tail_ref[...] = xr[ts - SUBLANES:]

    xcb = xc.astype(BF16)

    def block_diag(w_ref):
        return jnp.concatenate(
            [_bdot(xcb[:, n * bw:(n + 1) * bw], w_ref[n]) for n in range(LRU_BLOCKS)], axis=1)

    r = jax.nn.sigmoid(block_diag(wga_ref) + bga_ref[...])
    i = jax.nn.sigmoid(block_diag(wgx_ref) + bgx_ref[...])
    lam = lam_ref[...]
    softplus_neg_lam = jnp.maximum(-lam, 0.0) + jnp.log1p(jnp.exp(-jnp.abs(lam)))
    log_a = (-LRU_C * r) * softplus_neg_lam
    a = jnp.exp(log_a)
    mult = jnp.sqrt(-jnp.tanh(log_a) * (a * a + 1.0))
    u = mult * (i * xc)

    rows = lax.broadcasted_iota(jnp.int32, (ts, width), 0)
    cum_a, cum_u = a, u
    sh = 1
    while sh < ts:
        a_prev = pltpu.roll(cum_a, sh, 0)
        u_prev = pltpu.roll(cum_u, sh, 0)
        live = rows >= sh
        cum_u = jnp.where(live, cum_a * u_prev, 0.0) + cum_u
        cum_a = jnp.where(live, cum_a * a_prev, cum_a)
        sh *= 2
    h = cum_a * hc_ref[...] + cum_u
    hc_ref[...] = h[ts - 1:ts]

    y = (h * _gelu_tanh(gate)).astype(BF16)
    mix = _bdot(y, wout_ref[...])
    z = DEEPNORM_ALPHA * x + mix
    _rows_store(o_ref, _layer_norm(z, g_ref[...], b_ref[...]), ts)


def _rglru_layer(x, w_in, conv_w, conv_b, wga, bga, wgx, bgx, lam, w_out, g, b):
    bsz, seq, d = x.shape
    ts = RGLRU_TS
    ns = seq // ts
    row = lambda v: v.reshape(1, -1)
    full = lambda shape: pl.BlockSpec(shape, lambda bi, si: (0,) * len(shape))
    return pl.pallas_call(
        functools.partial(_rglru_body, ts=ts),
        grid=(bsz, ns),
        in_specs=[
            pl.BlockSpec((None, ts, d), lambda bi, si: (bi, si, 0)),
            full((d, 2 * d)), full((CONV_WIDTH, d)), full((1, d)),
            full((LRU_BLOCKS, d // LRU_BLOCKS, d // LRU_BLOCKS)), full((1, d)),
            full((LRU_BLOCKS, d // LRU_BLOCKS, d // LRU_BLOCKS)), full((1, d)),
            full((1, d)), full((d, d)), full((1, d)), full((1, d)),
        ],
        out_specs=pl.BlockSpec((ts * TOK_ROWS, LANES), lambda bi, si: (bi * ns + si, 0)),
        out_shape=jax.ShapeDtypeStruct((bsz * seq * TOK_ROWS, LANES), F32),
        scratch_shapes=[pltpu.VMEM((1, d), F32), pltpu.VMEM((SUBLANES, d), F32)],
        compiler_params=pltpu.CompilerParams(
            dimension_semantics=("arbitrary", "arbitrary"), vmem_limit_bytes=VMEM_LIMIT),
        name="rglru_layer",
    )(x, w_in.astype(BF16), conv_w, row(conv_b), wga.astype(BF16), row(bga), wgx.astype(BF16),
      row(bgx), row(lam), w_out.astype(BF16), row(g), row(b))


def _router_body(h_ref, wrt_ref, bias_ref, idx_ref, gate_ref, rank_ref, cnt_ref, carry_ref, *, tt):
    n_e = N_EXPERTS
    per_group = n_e // N_EXPERT_GROUPS

    @pl.when(pl.program_id(0) == 0)
    def _():
        carry_ref[...] = jnp.zeros_like(carry_ref)

    h = _rows_load(h_ref, tt)
    logits = lax.dot_general(wrt_ref[...], h, (((1,), (1,)), ((), ())),
                             precision=lax.Precision.HIGHEST, preferred_element_type=F32)
    scores = jax.nn.sigmoid(logits)
    biased = scores + bias_ref[...]

    j_iota = lax.broadcasted_iota(jnp.int32, (per_group, tt), 0)
    group_score = []
    for g in range(N_EXPERT_GROUPS):
        bg = biased[g * per_group:(g + 1) * per_group]
        m1 = jnp.max(bg, axis=0, keepdims=True)
        i1 = jnp.min(jnp.where(bg == m1, j_iota, per_group), axis=0, keepdims=True)
        m2 = jnp.max(jnp.where(j_iota == i1, -jnp.inf, bg), axis=0, keepdims=True)
        group_score.append(m1 + m2)

    masked = []
    for g in range(N_EXPERT_GROUPS):
        beaten_by = jnp.zeros((1, tt), jnp.int32)
        for o in range(N_EXPERT_GROUPS):
            if o == g:
                continue
            wins = group_score[o] > group_score[g]
            if o < g:
                wins = wins | (group_score[o] == group_score[g])
            beaten_by = beaten_by + wins.astype(jnp.int32)
        keep = beaten_by < TOPK_GROUPS
        masked.append(jnp.where(keep, biased[g * per_group:(g + 1) * per_group], MASK_VALUE))
    cur = jnp.concatenate(masked, axis=0)

    e_iota = lax.broadcasted_iota(jnp.int32, (n_e, tt), 0)
    idx_rows, score_rows, sels = [], [], []
    multi_hot = jnp.zeros((n_e, tt), F32)
    for _ in range(TOP_K):
        m = jnp.max(cur, axis=0, keepdims=True)
        ik = jnp.min(jnp.where(cur == m, e_iota, n_e), axis=0, keepdims=True)
        sel = e_iota == ik
        score_rows.append(jnp.sum(jnp.where(sel, scores, 0.0), axis=0, keepdims=True))
        cur = jnp.where(sel, -jnp.inf, cur)
        multi_hot = multi_hot + jnp.where(sel, 1.0, 0.0)
        idx_rows.append(ik)
        sels.append(sel)
    top_s = jnp.concatenate(score_rows, axis=0)
    gate_ref[...] = top_s / jnp.sum(top_s, axis=0, keepdims=True) * ROUTED_SCALE
    idx_ref[...] = jnp.concatenate(idx_rows, axis=0)

    t_row = lax.broadcasted_iota(jnp.int32, (tt, tt), 0)
    t_col = lax.broadcasted_iota(jnp.int32, (tt, tt), 1)
    strict_upper = jnp.where(t_row < t_col, 1.0, 0.0).astype(BF16)
    before = _bdot(multi_hot.astype(BF16), strict_upper) + carry_ref[...]
    rank_rows = [jnp.sum(jnp.where(sel, before, 0.0), axis=0, keepdims=True) for sel in sels]
    rank_ref[...] = jnp.concatenate(rank_rows, axis=0).astype(jnp.int32)
    carry = carry_ref[...] + jnp.sum(multi_hot, axis=1, keepdims=True)
    carry_ref[...] = carry
    cnt_ref[...] = carry.astype(jnp.int32)


def _router(h_tiles, w_router, router_bias):
    n_tok = h_tiles.shape[0] // TOK_ROWS
    tt = ROUTER_TT
    kt = lambda dt: jax.ShapeDtypeStruct((TOP_K, n_tok), dt)
    tok_spec = pl.BlockSpec((TOP_K, tt), lambda i: (0, i))
    return pl.pallas_call(
        functools.partial(_router_body, tt=tt),
        grid=(n_tok // tt,),
        in_specs=[
            pl.BlockSpec((tt * TOK_ROWS, LANES), lambda i: (i, 0)),
            pl.BlockSpec((N_EXPERTS, D_MODEL), lambda i: (0, 0)),
            pl.BlockSpec((N_EXPERTS, 1), lambda i: (0, 0)),
        ],
        out_specs=[tok_spec, tok_spec, tok_spec, pl.BlockSpec((N_EXPERTS, 1), lambda i: (0, 0))],
        out_shape=[kt(jnp.int32), kt(F32), kt(jnp.int32),
                   jax.ShapeDtypeStruct((N_EXPERTS, 1), jnp.int32)],
        scratch_shapes=[pltpu.VMEM((N_EXPERTS, 1), F32)],
        compiler_params=pltpu.CompilerParams(
            dimension_semantics=("arbitrary",), vmem_limit_bytes=VMEM_LIMIT),
        name="moe_router",
    )(h_tiles, w_router.T, router_bias.reshape(N_EXPERTS, 1))


def _token_tile(ref, t):
    return ref.at[pl.ds(pl.multiple_of(t * TOK_ROWS, TOK_ROWS), TOK_ROWS), :]


def _dispatch_body(start_ref, cnt_ref, idx_ref, rank_ref, h_ref, xs_ref, zero_ref, sem, zero_sem,
                   *, tt, tb):
    @pl.when(pl.program_id(0) == 0)
    def _():
        zero_ref[...] = jnp.zeros_like(zero_ref)

        def pad_rows(e, fn):
            first = start_ref[e] + cnt_ref[e]
            last = start_ref[e] + ((cnt_ref[e] + (tb - 1)) & (-tb))

            def body(r, c):
                fn(pltpu.make_async_copy(zero_ref, _token_tile(xs_ref, r), zero_sem))
                return c
            lax.fori_loop(first, last, body, 0)

        def start_all(e, c):
            pad_rows(e, lambda cp: cp.start())
            return c

        def wait_all(e, c):
            pad_rows(e, lambda cp: cp.wait())
            return c

        lax.fori_loop(0, N_EXPERTS, start_all, 0)
        lax.fori_loop(0, N_EXPERTS, wait_all, 0)

    def row_copy(t, k):
        dest = start_ref[idx_ref[k, t]] + rank_ref[k, t]
        return pltpu.make_async_copy(_token_tile(h_ref, t), _token_tile(xs_ref, dest), sem)

    def issue(t, c):
        for k in range(TOP_K):
            row_copy(t, k).start()
        return c

    def drain(t, c):
        for k in range(TOP_K):
            row_copy(t, k).wait()
        return c

    lax.fori_loop(0, tt, issue, 0)
    lax.fori_loop(0, tt, drain, 0)


def _dispatch(h_tiles, top_idx, rank, pad_start, counts, n_rows):
    n_tok = h_tiles.shape[0] // TOK_ROWS
    tt = DISPATCH_TT
    smem_tok = pl.BlockSpec((TOP_K, tt), lambda i, s, c: (0, i), memory_space=pltpu.SMEM)
    grid_spec = pltpu.PrefetchScalarGridSpec(
        num_scalar_prefetch=2,
        grid=(n_tok // tt,),
        in_specs=[smem_tok, smem_tok,
                  pl.BlockSpec((tt * TOK_ROWS, LANES), lambda i, s, c: (i, 0))],
        out_specs=pl.BlockSpec(memory_space=pl.ANY),
        scratch_shapes=[pltpu.VMEM((TOK_ROWS, LANES), F32), pltpu.SemaphoreType.DMA(()),
                        pltpu.SemaphoreType.DMA(())],
    )
    return pl.pallas_call(
        functools.partial(_dispatch_body, tt=tt, tb=EXPERT_TB),
        grid_spec=grid_spec,
        out_shape=jax.ShapeDtypeStruct((n_rows * TOK_ROWS, LANES), F32),
        compiler_params=pltpu.CompilerParams(
            dimension_semantics=("arbitrary",), vmem_limit_bytes=VMEM_LIMIT),
        name="moe_dispatch",
    )(pad_start, counts, top_idx, rank, h_tiles)


def _expert_body(be_ref, na_ref, xs_ref, w1_ref, w3_ref, w2_ref, ys_ref, w1b, w3b, w2b, *, tb):
    i = pl.program_id(0)

    @pl.when(i < na_ref[0])
    def _():
        @pl.when((i == 0) | (be_ref[i] != be_ref[jnp.maximum(i - 1, 0)]))
        def _():
            w1b[...] = w1_ref[...].astype(BF16)
            w3b[...] = w3_ref[...].astype(BF16)
            w2b[...] = w2_ref[...].astype(BF16)

        x = _rows_load(xs_ref, tb).astype(BF16)
        mid = (_silu(_bdot(x, w1b[...])) * _bdot(x, w3b[...])).astype(BF16)
        _rows_store(ys_ref, _bdot(mid, w2b[...]), tb)


def _experts(xs, block_expert, n_active, w1, w3, w2, layer):
    tb = EXPERT_TB
    n_blocks = xs.shape[0] // (tb * TOK_ROWS)
    row_block = lambda i, be, na: (jnp.minimum(i, na[0] - 1), 0)
    grid_spec = pltpu.PrefetchScalarGridSpec(
        num_scalar_prefetch=2,
        grid=(n_blocks,),
        in_specs=[
            pl.BlockSpec((tb * TOK_ROWS, LANES), row_block),
            pl.BlockSpec((None, None, D_MODEL, D_EXPERT), lambda i, be, na: (layer, be[i], 0, 0)),
            pl.BlockSpec((None, None, D_MODEL, D_EXPERT), lambda i, be, na: (layer, be[i], 0, 0)),
            pl.BlockSpec((None, None, D_EXPERT, D_MODEL), lambda i, be, na: (layer, be[i], 0, 0)),
        ],
        out_specs=pl.BlockSpec((tb * TOK_ROWS, LANES), row_block),
        scratch_shapes=[pltpu.VMEM((D_MODEL, D_EXPERT), BF16), pltpu.VMEM((D_MODEL, D_EXPERT), BF16),
                        pltpu.VMEM((D_EXPERT, D_MODEL), BF16)],
    )
    return pl.pallas_call(
        functools.partial(_expert_body, tb=tb),
        grid_spec=grid_spec,
        out_shape=jax.ShapeDtypeStruct(xs.shape, F32),
        compiler_params=pltpu.CompilerParams(
            dimension_semantics=("arbitrary",), vmem_limit_bytes=VMEM_LIMIT),
        name="moe_experts",
    )(block_expert, n_active, xs, w1, w3, w2)


def _combine_body(start_ref, idx_ref, rank_ref, h_ref, gates_ref, ys_ref, ws1_ref, ws3_ref, ws2_ref,
                  g_ref, b_ref, o_ref, gbuf, sem, *, tt, token_tile_out):
    def row_copy(t, k):
        src = start_ref[idx_ref[k, t]] + rank_ref[k, t]
        return pltpu.make_async_copy(_token_tile(ys_ref, src), _token_tile(gbuf.at[k], t), sem)

    def issue(t, c):
        for k in range(TOP_K):
            row_copy(t, k).start()
        return c

    def drain(t, c):
        for k in range(TOP_K):
            row_copy(t, k).wait()
        return c

    lax.fori_loop(0, tt, issue, 0)

    h = _rows_load(h_ref, tt)
    hb = h.astype(BF16)
    mid = (_silu(_bdot(hb, ws1_ref[...])) * _bdot(hb, ws3_ref[...])).astype(BF16)
    shared = _bdot(mid, ws2_ref[...])

    lax.fori_loop(0, tt, drain, 0)

    gates = gates_ref[...]
    routed = gates[:, 0:1] * _rows_load(gbuf.at[0], tt)
    for k in range(1, TOP_K):
        routed = routed + gates[:, k:k + 1] * _rows_load(gbuf.at[k], tt)
    out = _layer_norm(DEEPNORM_ALPHA * h + (routed + shared), g_ref[...], b_ref[...])
    if token_tile_out:
        _rows_store(o_ref, out, tt)
    else:
        o_ref[...] = out


def _combine(h_tiles, top_idx, rank, gates_tk, pad_start, ys, ws1, ws3, ws2, g, b, token_tile_out):
    n_tok = h_tiles.shape[0] // TOK_ROWS
    tt = COMBINE_TT
    smem_tok = pl.BlockSpec((TOP_K, tt), lambda i, s: (0, i), memory_space=pltpu.SMEM)
    full = lambda shape: pl.BlockSpec(shape, lambda i, s: (0,) * len(shape))
    if token_tile_out:
        out_spec = pl.BlockSpec((tt * TOK_ROWS, LANES), lambda i, s: (i, 0))
        out_shape = jax.ShapeDtypeStruct((n_tok * TOK_ROWS, LANES), F32)
    else:
        out_spec = pl.BlockSpec((tt, D_MODEL), lambda i, s: (i, 0))
        out_shape = jax.ShapeDtypeStruct((n_tok, D_MODEL), F32)
    grid_spec = pltpu.PrefetchScalarGridSpec(
        num_scalar_prefetch=1,
        grid=(n_tok // tt,),
        in_specs=[smem_tok, smem_tok,
                  pl.BlockSpec((tt * TOK_ROWS, LANES), lambda i, s: (i, 0)),
                  pl.BlockSpec((tt, TOP_K), lambda i, s: (i, 0)),
                  pl.BlockSpec(memory_space=pl.ANY),
                  full((D_MODEL, D_EXPERT)), full((D_MODEL, D_EXPERT)), full((D_EXPERT, D_MODEL)),
                  full((1, D_MODEL)), full((1, D_MODEL))],
        out_specs=out_spec,
        scratch_shapes=[pltpu.VMEM((TOP_K, tt * TOK_ROWS, LANES), F32), pltpu.SemaphoreType.DMA(())],
    )
    return pl.pallas_call(
        functools.partial(_combine_body, tt=tt, token_tile_out=token_tile_out),
        grid_spec=grid_spec,
        out_shape=out_shape,
        compiler_params=pltpu.CompilerParams(
            dimension_semantics=("arbitrary",), vmem_limit_bytes=VMEM_LIMIT),
        name="moe_combine",
    )(pad_start, top_idx, rank, h_tiles, gates_tk, ys, ws1.astype(BF16), ws3.astype(BF16),
      ws2.astype(BF16), g.reshape(1, -1), b.reshape(1, -1))


def _moe_layer(h_tiles, w_router, router_bias, w1, w3, w2, layer, ws1, ws3, ws2, g, b,
               token_tile_out):
    n_tok = h_tiles.shape[0] // TOK_ROWS
    tb = EXPERT_TB
    top_idx, gates, rank, counts = _router(h_tiles, w_router, router_bias)

    n_blocks = -(-(n_tok * TOP_K + N_EXPERTS * (tb - 1)) // tb)
    counts = counts.reshape(N_EXPERTS)
    padded = ((counts + tb - 1) // tb) * tb
    pad_end = jnp.cumsum(padded)
    pad_start = (pad_end - padded).astype(jnp.int32)
    block_first_row = jnp.arange(n_blocks, dtype=jnp.int32) * tb
    block_expert = jnp.minimum(
        jnp.sum((pad_end[None, :] <= block_first_row[:, None]).astype(jnp.int32), axis=1),
        N_EXPERTS - 1)
    n_active = (pad_end[-1:] // tb).astype(jnp.int32)

    xs = _dispatch(h_tiles, top_idx, rank, pad_start, counts, n_blocks * tb)
    ys = _experts(xs, block_expert, n_active, w1, w3, w2, layer)
    return _combine(h_tiles, top_idx, rank, gates.T, pad_start, ys, ws1, ws3, ws2, g, b,
                    token_tile_out)


def _proj_body(h_ref, w_ref, *o_refs, tt, scale):
    y = _bdot(_rows_load(h_ref, tt).astype(BF16), w_ref[...])
    for j, o_ref in enumerate(o_refs):
        o_ref[...] = (y[:, j * O_WIDTH:(j + 1) * O_WIDTH] * scale).astype(BF16)


def _project(h_tiles, w, scale):
    n_tok = h_tiles.shape[0] // TOK_ROWS
    tt = PROJ_TT
    n_out = w.shape[1] // O_WIDTH
    out_spec = pl.BlockSpec((tt, O_WIDTH), lambda i: (i, 0))
    return pl.pallas_call(
        functools.partial(_proj_body, tt=tt, scale=scale),
        grid=(n_tok // tt,),
        in_specs=[pl.BlockSpec((tt * TOK_ROWS, LANES), lambda i: (i, 0)),
                  pl.BlockSpec(w.shape, lambda i: (0, 0))],
        out_specs=[out_spec] * n_out,
        out_shape=[jax.ShapeDtypeStruct((n_tok, O_WIDTH), BF16)] * n_out,
        compiler_params=pltpu.CompilerParams(
            dimension_semantics=("arbitrary",), vmem_limit_bytes=VMEM_LIMIT),
        name="projection",
    )(h_tiles, w.astype(BF16))


def _attn_body(q_ref, kp_ref, kc_ref, vp_ref, vc_ref, o_ref, lse_ref, *, dilation, n_steps):
    t = ATT_BLOCK
    not_first = pl.program_id(2) > 0
    qi = lax.broadcasted_iota(jnp.int32, (t, 2 * t), 0)
    kj = lax.broadcasted_iota(jnp.int32, (t, 2 * t), 1)
    dist = t + qi - kj
    valid = (dist >= 0) & (dist <= n_steps) & (not_first | (kj >= t))
    token_dist = (dilation * dist).astype(F32)
    low_half = lax.broadcasted_iota(jnp.int32, (t, LANES), 1) < HEAD_DIM
    contract_last = (((1,), (1,)), ((), ()))

    for pair in range(ATT_HEADS // 2):
        sl = slice(pair * LANES, (pair + 1) * LANES)
        q2 = q_ref[:, sl]
        k2 = jnp.concatenate([kp_ref[:, sl], kc_ref[:, sl]], axis=0)
        v2 = jnp.concatenate([vp_ref[:, sl], vc_ref[:, sl]], axis=0)
        outs, lses = [], []
        for half in range(2):
            head = 2 * pair + half
            slope = 2.0 ** (-8.0 * (head + 1) / ATT_HEADS)
            keep = low_half if half == 0 else ~low_half
            qh = jnp.where(keep, q2, jnp.zeros_like(q2))
            s = lax.dot_general(qh, k2, contract_last, preferred_element_type=F32)
            s = jnp.where(valid, s - slope * token_dist, MASK_VALUE)
            m = jnp.max(s, axis=-1, keepdims=True)
            p = jnp.exp(s - m)
            l = jnp.sum(p, axis=-1, keepdims=True)
            outs.append(_bdot(p.astype(BF16), v2) / l)
            lses.append(m + jnp.log(l))
        o_ref[:, sl] = jnp.where(low_half, outs[0], outs[1])
        lse_ref[:, sl] = jnp.where(low_half, lses[0], lses[1])


def _attn_branch(q, k, v, bsz, seq, window, dilation):
    t = ATT_BLOCK
    sub_len = seq // dilation
    nb = sub_len // t
    shape3 = (bsz, sub_len, dilation * O_WIDTH)
    q, k, v = (a.reshape(shape3) for a in (q, k, v))
    cur = pl.BlockSpec((None, t, O_WIDTH), lambda bi, r, n: (bi, n, r))
    prev = pl.BlockSpec((None, t, O_WIDTH), lambda bi, r, n: (bi, jnp.maximum(n - 1, 0), r))
    o, lse = pl.pallas_call(
        functools.partial(_attn_body, dilation=dilation, n_steps=window // dilation),
        grid=(bsz, dilation, nb),
        in_specs=[cur, prev, cur, prev, cur],
        out_specs=[cur, cur],
        out_shape=[jax.ShapeDtypeStruct(shape3, F32)] * 2,
        compiler_params=pltpu.CompilerParams(
            dimension_semantics=("arbitrary",) * 3, vmem_limit_bytes=VMEM_LIMIT),
        name=f"dilated_attn_d{dilation}",
    )(q, k, k, v, v)
    return o.reshape(bsz * seq, O_WIDTH), lse.reshape(bsz * seq, O_WIDTH)


def _merge_body(*refs, tt):
    o_refs = refs[:N_GROUPS]
    lse_refs = refs[N_GROUPS:2 * N_GROUPS]
    h_ref, wo_ref, g_ref, b_ref, out_ref = refs[2 * N_GROUPS:]
    lses = [r[...] for r in lse_refs]
    m = functools.reduce(jnp.maximum, lses)
    ws = [jnp.exp(l - m) for l in lses]
    den = functools.reduce(lambda a, c: a + c, ws)
    o = functools.reduce(lambda a, c: a + c, [w * r[...] for w, r in zip(ws, o_refs)]) / den
    mix = _bdot(o.astype(BF16), wo_ref[...])
    z = DEEPNORM_ALPHA * _rows_load(h_ref, tt) + mix
    _rows_store(out_ref, _layer_norm(z, g_ref[...], b_ref[...]), tt)


def _merge(outs, lses, h_tiles, w_o, g, b):
    n_tok = h_tiles.shape[0] // TOK_ROWS
    tt = MERGE_TT
    tok = pl.BlockSpec((tt, O_WIDTH), lambda i: (i, 0))
    tiles = pl.BlockSpec((tt * TOK_ROWS, LANES), lambda i: (i, 0))
    full = lambda shape: pl.BlockSpec(shape, lambda i: (0,) * len(shape))
    return pl.pallas_call(
        functools.partial(_merge_body, tt=tt),
        grid=(n_tok // tt,),
        in_specs=[tok] * (2 * N_GROUPS) + [tiles, full((O_WIDTH, D_MODEL)), full((1, D_MODEL)),
                                           full((1, D_MODEL))],
        out_specs=tiles,
        out_shape=jax.ShapeDtypeStruct(h_tiles.shape, F32),
        compiler_params=pltpu.CompilerParams(
            dimension_semantics=("arbitrary",), vmem_limit_bytes=VMEM_LIMIT),
        name="attn_merge",
    )(*outs, *lses, h_tiles, w_o.astype(BF16), g.reshape(1, -1), b.reshape(1, -1))


def kernel(x, a_w_in, a_conv_w, a_conv_b, a_w_gate_a, a_b_gate_a, a_w_gate_x, a_b_gate_x, a_lambda, a_w_out, w_kv_shared, b_w_q, b_w_o, moe_w_router, moe_router_bias, moe_w1, moe_w3, moe_w2, moe_ws1, moe_ws3, moe_ws2, ln_g, ln_b):
    bsz, seq, d = x.shape
    assert d == D_MODEL and seq % (DILATION_PATTERNS[-1][1] * ATT_BLOCK) == 0

    def moe(h_tiles, layer, token_tile_out):
        return _moe_layer(h_tiles, moe_w_router[layer], moe_router_bias[layer], moe_w1, moe_w3,
                          moe_w2, layer, moe_ws1[layer], moe_ws3[layer], moe_ws2[layer],
                          ln_g[layer, 1], ln_b[layer, 1], token_tile_out)

    h = _rglru_layer(x, a_w_in[0], a_conv_w[0], a_conv_b[0], a_w_gate_a[0], a_b_gate_a[0],
                     a_w_gate_x[0], a_b_gate_x[0], a_lambda[0], a_w_out[0], ln_g[0, 0], ln_b[0, 0])
    h = moe(h, 0, True)
    kv = _project(h, w_kv_shared, 1.0)
    ks, vs = kv[:N_GROUPS], kv[N_GROUPS:]

    qs = _project(h, b_w_q[0], HEAD_DIM ** -0.5)
    outs, lses = [], []
    for grp, (window, dilation) in enumerate(DILATION_PATTERNS):
        o, lse = _attn_branch(qs[grp], ks[grp], vs[grp], bsz, seq, window, dilation)
        outs.append(o)
        lses.append(lse)
    h = _merge(outs, lses, h, b_w_o[0], ln_g[1, 0], ln_b[1, 0])
    h = moe(h, 1, False)
    return h.reshape(bsz, seq, d)
```

```python
import functools
import math

import jax
import jax.numpy as jnp
from jax import lax
from jax.experimental import pallas as pl
from jax.experimental.pallas import tpu as pltpu

F32 = jnp.float32
BF16 = jnp.bfloat16

D_MODEL = 1024
LRU_BLOCKS = 4
CONV_WIDTH = 4
LRU_C = 8.0
ATT_HEADS = 8
HEAD_DIM = 64
DILATION_PATTERNS = ((128, 1), (512, 4), (2048, 16))
N_GROUPS = len(DILATION_PATTERNS)
ATT_BLOCK = 128
O_WIDTH = ATT_HEADS * HEAD_DIM
N_EXPERTS = 256
TOP_K = 8
N_EXPERT_GROUPS = 8
TOPK_GROUPS = 4
D_EXPERT = 256
ROUTED_SCALE = 2.5
DEPTH = 2
DEEPNORM_ALPHA = (2 * DEPTH) ** 0.25
LN_EPS = 1e-5
MASK_VALUE = -1e30

SUBLANES = 8
LANES = 128
TOK_ROWS = D_MODEL // LANES
VMEM_LIMIT = 56 * 1024 * 1024

RGLRU_TS = 256
ROUTER_TT = 256
EXPERT_TB = 256
COMBINE_TT = 128
PROJ_TT = 512
MERGE_TT = 256


def _rows_load(ref, n_tok):
    return jnp.concatenate(
        [ref[pl.ds(s, n_tok, stride=TOK_ROWS), :] for s in range(TOK_ROWS)], axis=1)


def _rows_store(ref, val, n_tok):
    for s in range(TOK_ROWS):
        ref[pl.ds(s, n_tok, stride=TOK_ROWS), :] = val[:, s * LANES:(s + 1) * LANES]


def _layer_norm(z, g, b):
    mu = jnp.mean(z, axis=-1, keepdims=True)
    zc = z - mu
    var = jnp.mean(zc * zc, axis=-1, keepdims=True)
    return zc * lax.rsqrt(var + LN_EPS) * g + b


def _silu(x):
    return x * jax.nn.sigmoid(x)


def _gelu_tanh(x):
    c = math.sqrt(2.0 / math.pi)
    return 0.5 * x * (1.0 + jnp.tanh(c * (x + 0.044715 * (x * x * x))))


def _bdot(a, b):
    return jnp.dot(a, b, preferred_element_type=F32)


def _rglru_body(x_ref, win_ref, cw_ref, cb_ref, wga_ref, bga_ref, wgx_ref, bgx_ref, lam_ref,
                wout_ref, g_ref, b_ref, o_ref, hc_ref, tail_ref, *, ts):
    width = D_MODEL
    bw = width // LRU_BLOCKS

    @pl.when(pl.program_id(1) == 0)
    def _():
        hc_ref[...] = jnp.zeros_like(hc_ref)
        tail_ref[...] = jnp.zeros_like(tail_ref)

    x = x_ref[...]
    xz = _bdot(x.astype(BF16), win_ref[...])
    xr = xz[:, :width]
    gate = xz[:, width:]

    tail = tail_ref[...]
    row8 = lax.broadcasted_iota(jnp.int32, (SUBLANES, width), 0)
    cw = cw_ref[...]
    xc = xr * cw[CONV_WIDTH - 1:CONV_WIDTH, :] + cb_ref[...]
    for j in range(1, CONV_WIDTH):
        rx = pltpu.roll(xr, j, 0)
        rp = pltpu.roll(tail, j, 0)
        top = jnp.where(row8 < j, rp, rx[:SUBLANES])
        shifted = jnp.concatenate([top, rx[SUBLANES:]], axis=0)
        xc = xc + shifted * cw[CONV_WIDTH - 1 - j:CONV_WIDTH - j, :]
    tail_ref[...] = xr[ts - SUBLANES:]

    xcb = xc.astype(BF16)

    def block_diag(w_ref):
        return jnp.concatenate(
            [_bdot(xcb[:, n * bw:(n + 1) * bw], w_ref[n]) for n in range(LRU_BLOCKS)], axis=1)

    r = jax.nn.sigmoid(block_diag(wga_ref) + bga_ref[...])
    i = jax.nn.sigmoid(block_diag(wgx_ref) + bgx_ref[...])
    lam = lam_ref[...]
    softplus_neg_lam = jnp.maximum(-lam, 0.0) + jnp.log1p(jnp.exp(-jnp.abs(lam)))
    log_a = (-LRU_C * r) * softplus_neg_lam
    a = jnp.exp(log_a)
    mult = jnp.sqrt(-jnp.tanh(log_a) * (a * a + 1.0))
    u = mult * (i * xc)

    rows = lax.broadcasted_iota(jnp.int32, (ts, width), 0)
    cum_a, cum_u = a, u
    sh = 1
    while sh < ts:
        a_prev = pltpu.roll(cum_a, sh, 0)
        u_prev = pltpu.roll(cum_u, sh, 0)
        live = rows >= sh
        cum_u = jnp.where(live, cum_a * u_prev, 0.0) + cum_u
        cum_a = jnp.where(live, cum_a * a_prev, cum_a)
        sh *= 2
    h = cum_a * hc_ref[...] + cum_u
    hc_ref[...] = h[ts - 1:ts]

    y = (h * _gelu_tanh(gate)).astype(BF16)
    mix = _bdot(y, wout_ref[...])
    z = DEEPNORM_ALPHA * x + mix
    _rows_store(o_ref, _layer_norm(z, g_ref[...], b_ref[...]), ts)


def _rglru_layer(x, w_in, conv_w, conv_b, wga, bga, wgx, bgx, lam, w_out, g, b):
    bsz, seq, d = x.shape
    ts = RGLRU_TS
    ns = seq // ts
    row = lambda v: v.reshape(1, -1)
    full = lambda shape: pl.BlockSpec(shape, lambda bi, si: (0,) * len(shape))
    return pl.pallas_call(
        functools.partial(_rglru_body, ts=ts),
        grid=(bsz, ns),
        in_specs=[
            pl.BlockSpec((None, ts, d), lambda bi, si: (bi, si, 0)),
            full((d, 2 * d)), full((CONV_WIDTH, d)), full((1, d)),
            full((LRU_BLOCKS, d // LRU_BLOCKS, d // LRU_BLOCKS)), full((1, d)),
            full((LRU_BLOCKS, d // LRU_BLOCKS, d // LRU_BLOCKS)), full((1, d)),
            full((1, d)), full((d, d)), full((1, d)), full((1, d)),
        ],
        out_specs=pl.BlockSpec((ts * TOK_ROWS, LANES), lambda bi, si: (bi * ns + si, 0)),
        out_shape=jax.ShapeDtypeStruct((bsz * seq * TOK_ROWS, LANES), F32),
        scratch_shapes=[pltpu.VMEM((1, d), F32), pltpu.VMEM((SUBLANES, d), F32)],
        compiler_params=pltpu.CompilerParams(
            dimension_semantics=("arbitrary", "arbitrary"), vmem_limit_bytes=VMEM_LIMIT),
        name="rglru_layer",
    )(x, w_in.astype(BF16), conv_w, row(conv_b), wga.astype(BF16), row(bga), wgx.astype(BF16),
      row(bgx), row(lam), w_out.astype(BF16), row(g), row(b))


def _router_body(h_ref, wrt_ref, bias_ref, idx_ref, gate_ref, rank_ref, cnt_ref, carry_ref, *, tt):
    n_e = N_EXPERTS
    per_group = n_e // N_EXPERT_GROUPS

    @pl.when(pl.program_id(0) == 0)
    def _():
        carry_ref[...] = jnp.zeros_like(carry_ref)

    h = _rows_load(h_ref, tt)
    logits = lax.dot_general(wrt_ref[...], h, (((1,), (1,)), ((), ())),
                             precision=lax.Precision.HIGHEST, preferred_element_type=F32)
    scores = jax.nn.sigmoid(logits)
    biased = scores + bias_ref[...]

    j_iota = lax.broadcasted_iota(jnp.int32, (per_group, tt), 0)
    group_score = []
    for g in range(N_EXPERT_GROUPS):
        bg = biased[g * per_group:(g + 1) * per_group]
        m1 = jnp.max(bg, axis=0, keepdims=True)
        i1 = jnp.min(jnp.where(bg == m1, j_iota, per_group), axis=0, keepdims=True)
        m2 = jnp.max(jnp.where(j_iota == i1, -jnp.inf, bg), axis=0, keepdims=True)
        group_score.append(m1 + m2)

    masked = []
    for g in range(N_EXPERT_GROUPS):
        beaten_by = jnp.zeros((1, tt), jnp.int32)
        for o in range(N_EXPERT_GROUPS):
            if o == g:
                continue
            wins = group_score[o] > group_score[g]
            if o < g:
                wins = wins | (group_score[o] == group_score[g])
            beaten_by = beaten_by + wins.astype(jnp.int32)
        keep = beaten_by < TOPK_GROUPS
        masked.append(jnp.where(keep, biased[g * per_group:(g + 1) * per_group], MASK_VALUE))
    cur = jnp.concatenate(masked, axis=0)

    e_iota = lax.broadcasted_iota(jnp.int32, (n_e, tt), 0)
    idx_rows, score_rows, sels = [], [], []
    multi_hot = jnp.zeros((n_e, tt), F32)
    for _ in range(TOP_K):
        m = jnp.max(cur, axis=0, keepdims=True)
        ik = jnp.min(jnp.where(cur == m, e_iota, n_e), axis=0, keepdims=True)
        sel = e_iota == ik
        score_rows.append(jnp.sum(jnp.where(sel, scores, 0.0), axis=0, keepdims=True))
        cur = jnp.where(sel, -jnp.inf, cur)
        multi_hot = multi_hot + jnp.where(sel, 1.0, 0.0)
        idx_rows.append(ik)
        sels.append(sel)
    top_s = jnp.concatenate(score_rows, axis=0)
    gate_ref[...] = top_s / jnp.sum(top_s, axis=0, keepdims=True) * ROUTED_SCALE
    idx_ref[...] = jnp.concatenate(idx_rows, axis=0)

    t_row = lax.broadcasted_iota(jnp.int32, (tt, tt), 0)
    t_col = lax.broadcasted_iota(jnp.int32, (tt, tt), 1)
    strict_upper = jnp.where(t_row < t_col, 1.0, 0.0).astype(BF16)
    before = _bdot(multi_hot.astype(BF16), strict_upper) + carry_ref[...]
    rank_rows = [jnp.sum(jnp.where(sel, before, 0.0), axis=0, keepdims=True) for sel in sels]
    rank_ref[...] = jnp.concatenate(rank_rows, axis=0).astype(jnp.int32)
    carry = carry_ref[...] + jnp.sum(multi_hot, axis=1, keepdims=True)
    carry_ref[...] = carry
    cnt_ref[...] = carry.astype(jnp.int32)


def _router(h_tiles, w_router, router_bias):
    n_tok = h_tiles.shape[0] // TOK_ROWS
    tt = ROUTER_TT
    kt = lambda dt: jax.ShapeDtypeStruct((TOP_K, n_tok), dt)
    tok_spec = pl.BlockSpec((TOP_K, tt), lambda i: (0, i))
    return pl.pallas_call(
        functools.partial(_router_body, tt=tt),
        grid=(n_tok // tt,),
        in_specs=[
            pl.BlockSpec((tt * TOK_ROWS, LANES), lambda i: (i, 0)),
            pl.BlockSpec((N_EXPERTS, D_MODEL), lambda i: (0, 0)),
            pl.BlockSpec((N_EXPERTS, 1), lambda i: (0, 0)),
        ],
        out_specs=[tok_spec, tok_spec, tok_spec, pl.BlockSpec((N_EXPERTS, 1), lambda i: (0, 0))],
        out_shape=[kt(jnp.int32), kt(F32), kt(jnp.int32),
                   jax.ShapeDtypeStruct((N_EXPERTS, 1), jnp.int32)],
        scratch_shapes=[pltpu.VMEM((N_EXPERTS, 1), F32)],
        compiler_params=pltpu.CompilerParams(
            dimension_semantics=("arbitrary",), vmem_limit_bytes=VMEM_LIMIT),
        name="moe_router",
    )(h_tiles, w_router.T, router_bias.reshape(N_EXPERTS, 1))


def _token_tile(ref, t):
    return ref.at[pl.ds(pl.multiple_of(t * TOK_ROWS, TOK_ROWS), TOK_ROWS), :]


def _expert_body(be_ref, na_ref, tok_ref, h_ref, w1_ref, w3_ref, w2_ref, ys_ref, xbuf, w1b, w3b, w2b,
                 sem, *, tb):
    j = pl.program_id(0)
    n_active = na_ref[0]

    @pl.when(j < n_active)
    def _():
        slot = j % 2
        for r in range(tb):
            pltpu.make_async_copy(_token_tile(h_ref, tok_ref[0, r]), _token_tile(xbuf.at[slot], r),
                                  sem.at[slot]).start()

    @pl.when((j >= 1) & (j <= n_active))
    def _():
        blk = j - 1
        slot = blk % 2

        @pl.when((blk == 0) | (be_ref[blk] != be_ref[jnp.maximum(blk - 1, 0)]))
        def _():
            w1b[...] = w1_ref[...].astype(BF16)
            w3b[...] = w3_ref[...].astype(BF16)
            w2b[...] = w2_ref[...].astype(BF16)

        pltpu.make_async_copy(h_ref.at[pl.ds(0, tb * TOK_ROWS), :], xbuf.at[slot], sem.at[slot]).wait()
        x = _rows_load(xbuf.at[slot], tb).astype(BF16)
        mid = (_silu(_bdot(x, w1b[...])) * _bdot(x, w3b[...])).astype(BF16)
        _rows_store(ys_ref, _bdot(mid, w2b[...]), tb)


def _experts(h_tiles, row_tok, block_expert, n_active, w1, w3, w2, layer):
    tb = EXPERT_TB
    n_blocks = row_tok.shape[0]
    prev_block = lambda j, na: jnp.clip(j - 1, 0, na[0] - 1)
    weight = lambda j, be, na: (layer, be[prev_block(j, na)], 0, 0)
    grid_spec = pltpu.PrefetchScalarGridSpec(
        num_scalar_prefetch=2,
        grid=(n_blocks + 1,),
        in_specs=[
            pl.BlockSpec((None, 1, tb), lambda j, be, na: (jnp.minimum(j, n_blocks - 1), 0, 0),
                         memory_space=pltpu.SMEM),
            pl.BlockSpec(memory_space=pl.ANY),
            pl.BlockSpec((None, None, D_MODEL, D_EXPERT), weight),
            pl.BlockSpec((None, None, D_MODEL, D_EXPERT), weight),
            pl.BlockSpec((None, None, D_EXPERT, D_MODEL), weight),
        ],
        out_specs=pl.BlockSpec((tb * TOK_ROWS, LANES), lambda j, be, na: (prev_block(j, na), 0)),
        scratch_shapes=[pltpu.VMEM((2, tb * TOK_ROWS, LANES), F32),
                        pltpu.VMEM((D_MODEL, D_EXPERT), BF16), pltpu.VMEM((D_MODEL, D_EXPERT), BF16),
                        pltpu.VMEM((D_EXPERT, D_MODEL), BF16), pltpu.SemaphoreType.DMA((2,))],
    )
    return pl.pallas_call(
        functools.partial(_expert_body, tb=tb),
        grid_spec=grid_spec,
        out_shape=jax.ShapeDtypeStruct((n_blocks * tb * TOK_ROWS, LANES), F32),
        compiler_params=pltpu.CompilerParams(
            dimension_semantics=("arbitrary",), vmem_limit_bytes=VMEM_LIMIT),
        name="moe_experts",
    )(block_expert, n_active, row_tok, h_tiles, w1, w3, w2)


def _combine_body(src_ref, h_ref, gates_ref, ys_ref, ws1_ref, ws3_ref, ws2_ref, g_ref, b_ref, o_ref,
                  gbuf, sem, *, tt, n_tiles, token_tile_out):
    j = pl.program_id(0)

    @pl.when(j < n_tiles)
    def _():
        slot = j % 2
        for t in range(tt):
            for k in range(TOP_K):
                pltpu.make_async_copy(_token_tile(ys_ref, src_ref[k, t]),
                                      _token_tile(gbuf.at[slot, k], t), sem.at[slot]).start()

    @pl.when(j >= 1)
    def _():
        slot = (j - 1) % 2
        h = _rows_load(h_ref, tt)
        hb = h.astype(BF16)
        mid = (_silu(_bdot(hb, ws1_ref[...])) * _bdot(hb, ws3_ref[...])).astype(BF16)
        shared = _bdot(mid, ws2_ref[...])

        for k in range(TOP_K):
            pltpu.make_async_copy(ys_ref.at[pl.ds(0, tt * TOK_ROWS), :], gbuf.at[slot, k],
                                  sem.at[slot]).wait()
        gates = gates_ref[...]
        routed = gates[:, 0:1] * _rows_load(gbuf.at[slot, 0], tt)
        for k in range(1, TOP_K):
            routed = routed + gates[:, k:k + 1] * _rows_load(gbuf.at[slot, k], tt)
        out = _layer_norm(DEEPNORM_ALPHA * h + (routed + shared), g_ref[...], b_ref[...])
        if token_tile_out:
            _rows_store(o_ref, out, tt)
        else:
            o_ref[...] = out


def _combine(h_tiles, src_rows, gates_tk, ys, ws1, ws3, ws2, g, b, token_tile_out):
    n_tok = h_tiles.shape[0] // TOK_ROWS
    tt = COMBINE_TT
    n_tiles = n_tok // tt
    prev_tile = lambda j: jnp.maximum(j - 1, 0)
    full = lambda shape: pl.BlockSpec(shape, lambda j: (0,) * len(shape))
    if token_tile_out:
        out_spec = pl.BlockSpec((tt * TOK_ROWS, LANES), lambda j: (prev_tile(j), 0))
        out_shape = jax.ShapeDtypeStruct((n_tok * TOK_ROWS, LANES), F32)
    else:
        out_spec = pl.BlockSpec((tt, D_MODEL), lambda j: (prev_tile(j), 0))
        out_shape = jax.ShapeDtypeStruct((n_tok, D_MODEL), F32)
    return pl.pallas_call(
        functools.partial(_combine_body, tt=tt, n_tiles=n_tiles, token_tile_out=token_tile_out),
        grid=(n_tiles + 1,),
        in_specs=[pl.BlockSpec((TOP_K, tt), lambda j: (0, jnp.minimum(j, n_tiles - 1)),
                               memory_space=pltpu.SMEM),
                  pl.BlockSpec((tt * TOK_ROWS, LANES), lambda j: (prev_tile(j), 0)),
                  pl.BlockSpec((tt, TOP_K), lambda j: (prev_tile(j), 0)),
                  pl.BlockSpec(memory_space=pl.ANY),
                  full((D_MODEL, D_EXPERT)), full((D_MODEL, D_EXPERT)), full((D_EXPERT, D_MODEL)),
                  full((1, D_MODEL)), full((1, D_MODEL))],
        out_specs=out_spec,
        out_shape=out_shape,
        scratch_shapes=[pltpu.VMEM((2, TOP_K, tt * TOK_ROWS, LANES), F32),
                        pltpu.SemaphoreType.DMA((2,))],
        compiler_params=pltpu.CompilerParams(
            dimension_semantics=("arbitrary",), vmem_limit_bytes=VMEM_LIMIT),
        name="moe_combine",
    )(src_rows, h_tiles, gates_tk, ys, ws1.astype(BF16), ws3.astype(BF16), ws2.astype(BF16),
      g.reshape(1, -1), b.reshape(1, -1))


def _moe_layer(h_tiles, w_router, router_bias, w1, w3, w2, layer, ws1, ws3, ws2, g, b,
               token_tile_out):
    n_tok = h_tiles.shape[0] // TOK_ROWS
    tb = EXPERT_TB
    top_idx, gates, rank, counts = _router(h_tiles, w_router, router_bias)

    n_blocks = -(-(n_tok * TOP_K + N_EXPERTS * (tb - 1)) // tb)
    counts = counts.reshape(N_EXPERTS)
    padded = ((counts + tb - 1) // tb) * tb
    pad_end = jnp.cumsum(padded)
    pad_start = (pad_end - padded).astype(jnp.int32)
    block_first_row = jnp.arange(n_blocks, dtype=jnp.int32) * tb
    block_expert = jnp.minimum(
        jnp.sum((pad_end[None, :] <= block_first_row[:, None]).astype(jnp.int32), axis=1),
        N_EXPERTS - 1)
    n_active = (pad_end[-1:] // tb).astype(jnp.int32)
    src_rows = pad_start[top_idx] + rank
    tok_ids = jnp.broadcast_to(jnp.arange(n_tok, dtype=jnp.int32), (TOP_K, n_tok))
    row_tok = jnp.zeros((n_blocks * tb,), jnp.int32).at[src_rows.reshape(-1)].set(
        tok_ids.reshape(-1), unique_indices=True).reshape(n_blocks, 1, tb)

    ys = _experts(h_tiles, row_tok, block_expert, n_active, w1, w3, w2, layer)
    return _combine(h_tiles, src_rows, gates.T, ys, ws1, ws3, ws2, g, b, token_tile_out)


def _proj_body(h_ref, w_ref, *o_refs, tt, scale):
    y = _bdot(_rows_load(h_ref, tt).astype(BF16), w_ref[...])
    for j, o_ref in enumerate(o_refs):
        o_ref[...] = (y[:, j * O_WIDTH:(j + 1) * O_WIDTH] * scale).astype(BF16)


def _project(h_tiles, w, scale):
    n_tok = h_tiles.shape[0] // TOK_ROWS
    tt = PROJ_TT
    n_out = w.shape[1] // O_WIDTH
    out_spec = pl.BlockSpec((tt, O_WIDTH), lambda i: (i, 0))
    return pl.pallas_call(
        functools.partial(_proj_body, tt=tt, scale=scale),
        grid=(n_tok // tt,),
        in_specs=[pl.BlockSpec((tt * TOK_ROWS, LANES), lambda i: (i, 0)),
                  pl.BlockSpec(w.shape, lambda i: (0, 0))],
        out_specs=[out_spec] * n_out,
        out_shape=[jax.ShapeDtypeStruct((n_tok, O_WIDTH), BF16)] * n_out,
        compiler_params=pltpu.CompilerParams(
            dimension_semantics=("arbitrary",), vmem_limit_bytes=VMEM_LIMIT),
        name="projection",
    )(h_tiles, w.astype(BF16))


def _attn_body(q_ref, kp_ref, kc_ref, vp_ref, vc_ref, o_ref, lse_ref, *, dilation, n_steps):
    t = ATT_BLOCK
    not_first = pl.program_id(2) > 0
    qi = lax.broadcasted_iota(jnp.int32, (t, 2 * t), 0)
    kj = lax.broadcasted_iota(jnp.int32, (t, 2 * t), 1)
    dist = t + qi - kj
    valid = (dist >= 0) & (dist <= n_steps) & (not_first | (kj >= t))
    token_dist = (dilation * dist).astype(F32)
    low_half = lax.broadcasted_iota(jnp.int32, (t, LANES), 1) < HEAD_DIM
    contract_last = (((1,), (1,)), ((), ()))

    for pair in range(ATT_HEADS // 2):
        sl = slice(pair * LANES, (pair + 1) * LANES)
        q2 = q_ref[:, sl]
        k2 = jnp.concatenate([kp_ref[:, sl], kc_ref[:, sl]], axis=0)
        v2 = jnp.concatenate([vp_ref[:, sl], vc_ref[:, sl]], axis=0)
        outs, lses = [], []
        for half in range(2):
            head = 2 * pair + half
            slope = 2.0 ** (-8.0 * (head + 1) / ATT_HEADS)
            keep = low_half if half == 0 else ~low_half
            qh = jnp.where(keep, q2, jnp.zeros_like(q2))
            s = lax.dot_general(qh, k2, contract_last, preferred_element_type=F32)
            s = jnp.where(valid, s - slope * token_dist, MASK_VALUE)
            m = jnp.max(s, axis=-1, keepdims=True)
            p = jnp.exp(s - m)
            l = jnp.sum(p, axis=-1, keepdims=True)
            outs.append(_bdot(p.astype(BF16), v2) / l)
            lses.append(m + jnp.log(l))
        o_ref[:, sl] = jnp.where(low_half, outs[0], outs[1])
        lse_ref[:, sl] = jnp.where(low_half, lses[0], lses[1])


def _attn_branch(q, k, v, bsz, seq, window, dilation):
    t = ATT_BLOCK
    sub_len = seq // dilation
    nb = sub_len // t
    shape3 = (bsz, sub_len, dilation * O_WIDTH)
    q, k, v = (a.reshape(shape3) for a in (q, k, v))
    cur = pl.BlockSpec((None, t, O_WIDTH), lambda bi, r, n: (bi, n, r))
    prev = pl.BlockSpec((None, t, O_WIDTH), lambda bi, r, n: (bi, jnp.maximum(n - 1, 0), r))
    o, lse = pl.pallas_call(
        functools.partial(_attn_body, dilation=dilation, n_steps=window // dilation),
        grid=(bsz, dilation, nb),
        in_specs=[cur, prev, cur, prev, cur],
        out_specs=[cur, cur],
        out_shape=[jax.ShapeDtypeStruct(shape3, F32)] * 2,
        compiler_params=pltpu.CompilerParams(
            dimension_semantics=("arbitrary",) * 3, vmem_limit_bytes=VMEM_LIMIT),
        name=f"dilated_attn_d{dilation}",
    )(q, k, k, v, v)
    return o.reshape(bsz * seq, O_WIDTH), lse.reshape(bsz * seq, O_WIDTH)


def _merge_body(*refs, tt):
    o_refs = refs[:N_GROUPS]
    lse_refs = refs[N_GROUPS:2 * N_GROUPS]
    h_ref, wo_ref, g_ref, b_ref, out_ref = refs[2 * N_GROUPS:]
    lses = [r[...] for r in lse_refs]
    m = functools.reduce(jnp.maximum, lses)
    ws = [jnp.exp(l - m) for l in lses]
    den = functools.reduce(lambda a, c: a + c, ws)
    o = functools.reduce(lambda a, c: a + c, [w * r[...] for w, r in zip(ws, o_refs)]) / den
    mix = _bdot(o.astype(BF16), wo_ref[...])
    z = DEEPNORM_ALPHA * _rows_load(h_ref, tt) + mix
    _rows_store(out_ref, _layer_norm(z, g_ref[...], b_ref[...]), tt)


def _merge(outs, lses, h_tiles, w_o, g, b):
    n_tok = h_tiles.shape[0] // TOK_ROWS
    tt = MERGE_TT
    tok = pl.BlockSpec((tt, O_WIDTH), lambda i: (i, 0))
    tiles = pl.BlockSpec((tt * TOK_ROWS, LANES), lambda i: (i, 0))
    full = lambda shape: pl.BlockSpec(shape, lambda i: (0,) * len(shape))
    return pl.pallas_call(
        functools.partial(_merge_body, tt=tt),
        grid=(n_tok // tt,),
        in_specs=[tok] * (2 * N_GROUPS) + [tiles, full((O_WIDTH, D_MODEL)), full((1, D_MODEL)),
                                           full((1, D_MODEL))],
        out_specs=tiles,
        out_shape=jax.ShapeDtypeStruct(h_tiles.shape, F32),
        compiler_params=pltpu.CompilerParams(
            dimension_semantics=("arbitrary",), vmem_limit_bytes=VMEM_LIMIT),
        name="attn_merge",
    )(*outs, *lses, h_tiles, w_o.astype(BF16), g.reshape(1, -1), b.reshape(1, -1))


def kernel(x, a_w_in, a_conv_w, a_conv_b, a_w_gate_a, a_b_gate_a, a_w_gate_x, a_b_gate_x, a_lambda, a_w_out, w_kv_shared, b_w_q, b_w_o, moe_w_router, moe_router_bias, moe_w1, moe_w3, moe_w2, moe_ws1, moe_ws3, moe_ws2, ln_g, ln_b):
    bsz, seq, d = x.shape
    assert d == D_MODEL and seq % (DILATION_PATTERNS[-1][1] * ATT_BLOCK) == 0

    def moe(h_tiles, layer, token_tile_out):
        return _moe_layer(h_tiles, moe_w_router[layer], moe_router_bias[layer], moe_w1, moe_w3,
                          moe_w2, layer, moe_ws1[layer], moe_ws3[layer], moe_ws2[layer],
                          ln_g[layer, 1], ln_b[layer, 1], token_tile_out)

    h = _rglru_layer(x, a_w_in[0], a_conv_w[0], a_conv_b[0], a_w_gate_a[0], a_b_gate_a[0],
                     a_w_gate_x[0], a_b_gate_x[0], a_lambda[0], a_w_out[0], ln_g[0, 0], ln_b[0, 0])
    h = moe(h, 0, True)
    kv = _project(h, w_kv_shared, 1.0)
    ks, vs = kv[:N_GROUPS], kv[N_GROUPS:]

    qs = _project(h, b_w_q[0], HEAD_DIM ** -0.5)
    outs, lses = [], []
    for grp, (window, dilation) in enumerate(DILATION_PATTERNS):
        o, lse = _attn_branch(qs[grp], ks[grp], vs[grp], bsz, seq, window, dilation)
        outs.append(o)
        lses.append(lse)
    h = _merge(outs, lses, h, b_w_o[0], ln_g[1, 0], ln_b[1, 0])
    h = moe(h, 1, False)
    return h.reshape(bsz, seq, d)
```

```python
import functools
import math

import jax
import jax.numpy as jnp
from jax import lax
from jax.experimental import pallas as pl
from jax.experimental.pallas import tpu as pltpu

F32 = jnp.float32
BF16 = jnp.bfloat16

D_MODEL = 1024
LRU_BLOCKS = 4
CONV_WIDTH = 4
LRU_C = 8.0
ATT_HEADS = 8
HEAD_DIM = 64
DILATION_PATTERNS = ((128, 1), (512, 4), (2048, 16))
N_GROUPS = len(DILATION_PATTERNS)
ATT_BLOCK = 128
O_WIDTH = ATT_HEADS * HEAD_DIM
N_EXPERTS = 256
TOP_K = 8
N_EXPERT_GROUPS = 8
TOPK_GROUPS = 4
D_EXPERT = 256
ROUTED_SCALE = 2.5
DEPTH = 2
DEEPNORM_ALPHA = (2 * DEPTH) ** 0.25
LN_EPS = 1e-5
MASK_VALUE = -1e30

SUBLANES = 8
LANES = 128
TOK_ROWS = D_MODEL // LANES
VMEM_LIMIT = 56 * 1024 * 1024

RGLRU_TS = 256
ROUTER_TT = 256
DISPATCH_TT = 128
EXPERT_TB = 256
COMBINE_TT = 128
PROJ_TT = 512
MERGE_TT = 256


def _rows_load(ref, n_tok):
    return jnp.concatenate(
        [ref[pl.ds(s, n_tok, stride=TOK_ROWS), :] for s in range(TOK_ROWS)], axis=1)


def _rows_store(ref, val, n_tok):
    for s in range(TOK_ROWS):
        ref[pl.ds(s, n_tok, stride=TOK_ROWS), :] = val[:, s * LANES:(s + 1) * LANES]


def _layer_norm(z, g, b):
    mu = jnp.mean(z, axis=-1, keepdims=True)
    zc = z - mu
    var = jnp.mean(zc * zc, axis=-1, keepdims=True)
    return zc * lax.rsqrt(var + LN_EPS) * g + b


def _silu(x):
    return x * jax.nn.sigmoid(x)


def _gelu_tanh(x):
    c = math.sqrt(2.0 / math.pi)
    return 0.5 * x * (1.0 + jnp.tanh(c * (x + 0.044715 * (x * x * x))))


def _bdot(a, b):
    return jnp.dot(a, b, preferred_element_type=F32)


def _rglru_body(x_ref, win_ref, cw_ref, cb_ref, wga_ref, bga_ref, wgx_ref, bgx_ref, lam_ref,
                wout_ref, g_ref, b_ref, o_ref, hc_ref, tail_ref, *, ts):
    width = D_MODEL
    bw = width // LRU_BLOCKS

    @pl.when(pl.program_id(1) == 0)
    def _():
        hc_ref[...] = jnp.zeros_like(hc_ref)
        tail_ref[...] = jnp.zeros_like(tail_ref)

    x = x_ref[...]
    xz = _bdot(x.astype(BF16), win_ref[...])
    xr = xz[:, :width]
    gate = xz[:, width:]

    tail = tail_ref[...]
    row8 = lax.broadcasted_iota(jnp.int32, (SUBLANES, width), 0)
    cw = cw_ref[...]
    xc = xr * cw[CONV_WIDTH - 1:CONV_WIDTH, :] + cb_ref[...]
    for j in range(1, CONV_WIDTH):
        rx = pltpu.roll(xr, j, 0)
        rp = pltpu.roll(tail, j, 0)
        top = jnp.where(row8 < j, rp, rx[:SUBLANES])
        shifted = jnp.concatenate([top, rx[SUBLANES:]], axis=0)
        xc = xc + shifted * cw[CONV_WIDTH - 1 - j:CONV_WIDTH - j, :]
    tail_ref[...] = xr[ts - SUBLANES:]

    xcb = xc.astype(BF16)

    def block_diag(w_ref):
        return jnp.concatenate(
            [_bdot(xcb[:, n * bw:(n + 1) * bw], w_ref[n]) for n in range(LRU_BLOCKS)], axis=1)

    r = jax.nn.sigmoid(block_diag(wga_ref) + bga_ref[...])
    i = jax.nn.sigmoid(block_diag(wgx_ref) + bgx_ref[...])
    lam = lam_ref[...]
    softplus_neg_lam = jnp.maximum(-lam, 0.0) + jnp.log1p(jnp.exp(-jnp.abs(lam)))
    log_a = (-LRU_C * r) * softplus_neg_lam
    a = jnp.exp(log_a)
    mult = jnp.sqrt(-jnp.tanh(log_a) * (a * a + 1.0))
    u = mult * (i * xc)

    rows = lax.broadcasted_iota(jnp.int32, (ts, width), 0)
    cum_a, cum_u = a, u
    sh = 1
    while sh < ts:
        a_prev = pltpu.roll(cum_a, sh, 0)
        u_prev = pltpu.roll(cum_u, sh, 0)
        live = rows >= sh
        cum_u = jnp.where(live, cum_a * u_prev, 0.0) + cum_u
        cum_a = jnp.where(live, cum_a * a_prev, cum_a)
        sh *= 2
    h = cum_a * hc_ref[...] + cum_u
    hc_ref[...] = h[ts - 1:ts]

    y = (h * _gelu_tanh(gate)).astype(BF16)
    mix = _bdot(y, wout_ref[...])
    z = DEEPNORM_ALPHA * x + mix
    _rows_store(o_ref, _layer_norm(z, g_ref[...], b_ref[...]), ts)


def _rglru_layer(x, w_in, conv_w, conv_b, wga, bga, wgx, bgx, lam, w_out, g, b):
    bsz, seq, d = x.shape
    ts = RGLRU_TS
    ns = seq // ts
    row = lambda v: v.reshape(1, -1)
    full = lambda shape: pl.BlockSpec(shape, lambda bi, si: (0,) * len(shape))
    return pl.pallas_call(
        functools.partial(_rglru_body, ts=ts),
        grid=(bsz, ns),
        in_specs=[
            pl.BlockSpec((None, ts, d), lambda bi, si: (bi, si, 0)),
            full((d, 2 * d)), full((CONV_WIDTH, d)), full((1, d)),
            full((LRU_BLOCKS, d // LRU_BLOCKS, d // LRU_BLOCKS)), full((1, d)),
            full((LRU_BLOCKS, d // LRU_BLOCKS, d // LRU_BLOCKS)), full((1, d)),
            full((1, d)), full((d, d)), full((1, d)), full((1, d)),
        ],
        out_specs=pl.BlockSpec((ts * TOK_ROWS, LANES), lambda bi, si: (bi * ns + si, 0)),
        out_shape=jax.ShapeDtypeStruct((bsz * seq * TOK_ROWS, LANES), F32),
        scratch_shapes=[pltpu.VMEM((1, d), F32), pltpu.VMEM((SUBLANES, d), F32)],
        compiler_params=pltpu.CompilerParams(
            dimension_semantics=("arbitrary", "arbitrary"), vmem_limit_bytes=VMEM_LIMIT),
        name="rglru_layer",
    )(x, w_in.astype(BF16), conv_w, row(conv_b), wga.astype(BF16), row(bga), wgx.astype(BF16),
      row(bgx), row(lam), w_out.astype(BF16), row(g), row(b))


def _router_body(h_ref, wrt_ref, bias_ref, idx_ref, gate_ref, rank_ref, cnt_ref, carry_ref, *, tt):
    n_e = N_EXPERTS
    per_group = n_e // N_EXPERT_GROUPS

    @pl.when(pl.program_id(0) == 0)
    def _():
        carry_ref[...] = jnp.zeros_like(carry_ref)

    h = _rows_load(h_ref, tt)
    logits = lax.dot_general(wrt_ref[...], h, (((1,), (1,)), ((), ())),
                             precision=lax.Precision.HIGHEST, preferred_element_type=F32)
    scores = jax.nn.sigmoid(logits)
    biased = scores + bias_ref[...]

    j_iota = lax.broadcasted_iota(jnp.int32, (per_group, tt), 0)
    group_score = []
    for g in range(N_EXPERT_GROUPS):
        bg = biased[g * per_group:(g + 1) * per_group]
        m1 = jnp.max(bg, axis=0, keepdims=True)
        i1 = jnp.min(jnp.where(bg == m1, j_iota, per_group), axis=0, keepdims=True)
        m2 = jnp.max(jnp.where(j_iota == i1, -jnp.inf, bg), axis=0, keepdims=True)
        group_score.append(m1 + m2)

    masked = []
    for g in range(N_EXPERT_GROUPS):
        beaten_by = jnp.zeros((1, tt), jnp.int32)
        for o in range(N_EXPERT_GROUPS):
            if o == g:
                continue
            wins = group_score[o] > group_score[g]
            if o < g:
                wins = wins | (group_score[o] == group_score[g])
            beaten_by = beaten_by + wins.astype(jnp.int32)
        keep = beaten_by < TOPK_GROUPS
        masked.append(jnp.where(keep, biased[g * per_group:(g + 1) * per_group], MASK_VALUE))
    cur = jnp.concatenate(masked, axis=0)

    e_iota = lax.broadcasted_iota(jnp.int32, (n_e, tt), 0)
    idx_rows, score_rows, sels = [], [], []
    multi_hot = jnp.zeros((n_e, tt), F32)
    for _ in range(TOP_K):
        m = jnp.max(cur, axis=0, keepdims=True)
        ik = jnp.min(jnp.where(cur == m, e_iota, n_e), axis=0, keepdims=True)
        sel = e_iota == ik
        score_rows.append(jnp.sum(jnp.where(sel, scores, 0.0), axis=0, keepdims=True))
        cur = jnp.where(sel, -jnp.inf, cur)
        multi_hot = multi_hot + jnp.where(sel, 1.0, 0.0)
        idx_rows.append(ik)
        sels.append(sel)
    top_s = jnp.concatenate(score_rows, axis=0)
    gate_ref[...] = top_s / jnp.sum(top_s, axis=0, keepdims=True) * ROUTED_SCALE
    idx_ref[...] = jnp.concatenate(idx_rows, axis=0)

    t_row = lax.broadcasted_iota(jnp.int32, (tt, tt), 0)
    t_col = lax.broadcasted_iota(jnp.int32, (tt, tt), 1)
    strict_upper = jnp.where(t_row < t_col, 1.0, 0.0).astype(BF16)
    before = _bdot(multi_hot.astype(BF16), strict_upper) + carry_ref[...]
    rank_rows = [jnp.sum(jnp.where(sel, before, 0.0), axis=0, keepdims=True) for sel in sels]
    rank_ref[...] = jnp.concatenate(rank_rows, axis=0).astype(jnp.int32)
    carry = carry_ref[...] + jnp.sum(multi_hot, axis=1, keepdims=True)
    carry_ref[...] = carry
    cnt_ref[...] = carry.astype(jnp.int32)


def _router(h_tiles, w_router, router_bias):
    n_tok = h_tiles.shape[0] // TOK_ROWS
    tt = ROUTER_TT
    kt = lambda dt: jax.ShapeDtypeStruct((TOP_K, n_tok), dt)
    tok_spec = pl.BlockSpec((TOP_K, tt), lambda i: (0, i))
    return pl.pallas_call(
        functools.partial(_router_body, tt=tt),
        grid=(n_tok // tt,),
        in_specs=[
            pl.BlockSpec((tt * TOK_ROWS, LANES), lambda i: (i, 0)),
            pl.BlockSpec((N_EXPERTS, D_MODEL), lambda i: (0, 0)),
            pl.BlockSpec((N_EXPERTS, 1), lambda i: (0, 0)),
        ],
        out_specs=[tok_spec, tok_spec, tok_spec, pl.BlockSpec((N_EXPERTS, 1), lambda i: (0, 0))],
        out_shape=[kt(jnp.int32), kt(F32), kt(jnp.int32),
                   jax.ShapeDtypeStruct((N_EXPERTS, 1), jnp.int32)],
        scratch_shapes=[pltpu.VMEM((N_EXPERTS, 1), F32)],
        compiler_params=pltpu.CompilerParams(
            dimension_semantics=("arbitrary",), vmem_limit_bytes=VMEM_LIMIT),
        name="moe_router",
    )(h_tiles, w_router.T, router_bias.reshape(N_EXPERTS, 1))


def _token_tile(ref, t):
    return ref.at[pl.ds(pl.multiple_of(t * TOK_ROWS, TOK_ROWS), TOK_ROWS), :]


def _token_tiles(ref, t, n):
    return ref.at[pl.ds(pl.multiple_of(t * TOK_ROWS, TOK_ROWS), n * TOK_ROWS), :]


def _dispatch_body(start_ref, cnt_ref, idx_ref, rank_ref, h_ref, xs_ref, zero_ref, sem, zero_sem,
                   *, tt, tb):
    @pl.when(pl.program_id(0) == 0)
    def _():
        zero_ref[...] = jnp.zeros_like(zero_ref)

        def pad_copies(e, fn):
            n_pad = ((cnt_ref[e] + (tb - 1)) & (-tb)) - cnt_ref[e]
            row = start_ref[e] + cnt_ref[e]
            for bit in range(tb.bit_length() - 1):
                size = 1 << bit
                has_bit = ((n_pad >> bit) & 1) == 1

                @pl.when(has_bit)
                def _():
                    fn(pltpu.make_async_copy(_token_tiles(zero_ref, 0, size),
                                             _token_tiles(xs_ref, row, size), zero_sem))
                row = row + jnp.where(has_bit, size, 0)

        def start_all(e, c):
            pad_copies(e, lambda cp: cp.start())
            return c

        def wait_all(e, c):
            pad_copies(e, lambda cp: cp.wait())
            return c

        lax.fori_loop(0, N_EXPERTS, start_all, 0)
        lax.fori_loop(0, N_EXPERTS, wait_all, 0)

    for t in range(tt):
        for k in range(TOP_K):
            dest = start_ref[idx_ref[k, t]] + rank_ref[k, t]
            pltpu.make_async_copy(_token_tile(h_ref, t), _token_tile(xs_ref, dest), sem).start(
                priority=k % 2)
    for k in range(TOP_K):
        pltpu.make_async_copy(h_ref, _token_tiles(xs_ref, 0, tt), sem).wait()


def _dispatch(h_tiles, top_idx, rank, pad_start, counts, n_rows):
    n_tok = h_tiles.shape[0] // TOK_ROWS
    tt = DISPATCH_TT
    smem_tok = pl.BlockSpec((TOP_K, tt), lambda i, s, c: (0, i), memory_space=pltpu.SMEM)
    grid_spec = pltpu.PrefetchScalarGridSpec(
        num_scalar_prefetch=2,
        grid=(n_tok // tt,),
        in_specs=[smem_tok, smem_tok,
                  pl.BlockSpec((tt * TOK_ROWS, LANES), lambda i, s, c: (i, 0))],
        out_specs=pl.BlockSpec(memory_space=pl.ANY),
        scratch_shapes=[pltpu.VMEM((EXPERT_TB // 2 * TOK_ROWS, LANES), F32),
                        pltpu.SemaphoreType.DMA(()), pltpu.SemaphoreType.DMA(())],
    )
    return pl.pallas_call(
        functools.partial(_dispatch_body, tt=tt, tb=EXPERT_TB),
        grid_spec=grid_spec,
        out_shape=jax.ShapeDtypeStruct((n_rows * TOK_ROWS, LANES), F32),
        compiler_params=pltpu.CompilerParams(
            dimension_semantics=("arbitrary",), vmem_limit_bytes=VMEM_LIMIT),
        name="moe_dispatch",
    )(pad_start, counts, top_idx, rank, h_tiles)


def _expert_body(be_ref, na_ref, xs_ref, w1_ref, w3_ref, w2_ref, ys_ref, w1b, w3b, w2b, *, tb):
    i = pl.program_id(0)

    @pl.when(i < na_ref[0])
    def _():
        @pl.when((i == 0) | (be_ref[i] != be_ref[jnp.maximum(i - 1, 0)]))
        def _():
            w1b[...] = w1_ref[...].astype(BF16)
            w3b[...] = w3_ref[...].astype(BF16)
            w2b[...] = w2_ref[...].astype(BF16)

        x = _rows_load(xs_ref, tb).astype(BF16)
        mid = (_silu(_bdot(x, w1b[...])) * _bdot(x, w3b[...])).astype(BF16)
        _rows_store(ys_ref, _bdot(mid, w2b[...]), tb)


def _experts(xs, block_expert, n_active, w1, w3, w2, layer):
    tb = EXPERT_TB
    n_blocks = xs.shape[0] // (tb * TOK_ROWS)
    row_block = lambda i, be, na: (jnp.minimum(i, na[0] - 1), 0)
    weight = lambda i, be, na: (layer, be[i], 0, 0)
    grid_spec = pltpu.PrefetchScalarGridSpec(
        num_scalar_prefetch=2,
        grid=(n_blocks,),
        in_specs=[
            pl.BlockSpec((tb * TOK_ROWS, LANES), row_block),
            pl.BlockSpec((None, None, D_MODEL, D_EXPERT), weight),
            pl.BlockSpec((None, None, D_MODEL, D_EXPERT), weight),
            pl.BlockSpec((None, None, D_EXPERT, D_MODEL), weight),
        ],
        out_specs=pl.BlockSpec((tb * TOK_ROWS, LANES), row_block),
        scratch_shapes=[pltpu.VMEM((D_MODEL, D_EXPERT), BF16), pltpu.VMEM((D_MODEL, D_EXPERT), BF16),
                        pltpu.VMEM((D_EXPERT, D_MODEL), BF16)],
    )
    return pl.pallas_call(
        functools.partial(_expert_body, tb=tb),
        grid_spec=grid_spec,
        out_shape=jax.ShapeDtypeStruct(xs.shape, F32),
        compiler_params=pltpu.CompilerParams(
            dimension_semantics=("arbitrary",), vmem_limit_bytes=VMEM_LIMIT),
        name="moe_experts",
    )(block_expert, n_active, xs, w1, w3, w2)


def _combine_body(start_ref, idx_ref, rank_ref, h_ref, gates_ref, ys_ref, ws1_ref, ws3_ref, ws2_ref,
                  g_ref, b_ref, o_ref, gbuf, sem, *, tt, n_tiles, token_tile_out):
    j = pl.program_id(0)

    @pl.when(j < n_tiles)
    def _():
        slot = j % 2
        for t in range(tt):
            for k in range(TOP_K):
                src = start_ref[idx_ref[k, t]] + rank_ref[k, t]
                pltpu.make_async_copy(_token_tile(ys_ref, src), _token_tile(gbuf.at[slot, k], t),
                                      sem.at[slot]).start(priority=k % 2)

    @pl.when(j >= 1)
    def _():
        slot = (j - 1) % 2
        h = _rows_load(h_ref, tt)
        hb = h.astype(BF16)
        mid = (_silu(_bdot(hb, ws1_ref[...])) * _bdot(hb, ws3_ref[...])).astype(BF16)
        shared = _bdot(mid, ws2_ref[...])

        for k in range(TOP_K):
            pltpu.make_async_copy(_token_tiles(ys_ref, 0, tt), gbuf.at[slot, k], sem.at[slot]).wait()
        gates = gates_ref[...]
        routed = gates[:, 0:1] * _rows_load(gbuf.at[slot, 0], tt)
        for k in range(1, TOP_K):
            routed = routed + gates[:, k:k + 1] * _rows_load(gbuf.at[slot, k], tt)
        out = _layer_norm(DEEPNORM_ALPHA * h + (routed + shared), g_ref[...], b_ref[...])
        if token_tile_out:
            _rows_store(o_ref, out, tt)
        else:
            o_ref[...] = out


def _combine(h_tiles, top_idx, rank, gates_tk, pad_start, ys, ws1, ws3, ws2, g, b, token_tile_out):
    n_tok = h_tiles.shape[0] // TOK_ROWS
    tt = COMBINE_TT
    n_tiles = n_tok // tt
    prev_tile = lambda j, s: (jnp.maximum(j - 1, 0), 0)
    next_tok = pl.BlockSpec((TOP_K, tt), lambda j, s: (0, jnp.minimum(j, n_tiles - 1)),
                            memory_space=pltpu.SMEM)
    full = lambda shape: pl.BlockSpec(shape, lambda j, s: (0,) * len(shape))
    if token_tile_out:
        out_spec = pl.BlockSpec((tt * TOK_ROWS, LANES), prev_tile)
        out_shape = jax.ShapeDtypeStruct((n_tok * TOK_ROWS, LANES), F32)
    else:
        out_spec = pl.BlockSpec((tt, D_MODEL), prev_tile)
        out_shape = jax.ShapeDtypeStruct((n_tok, D_MODEL), F32)
    grid_spec = pltpu.PrefetchScalarGridSpec(
        num_scalar_prefetch=1,
        grid=(n_tiles + 1,),
        in_specs=[next_tok, next_tok,
                  pl.BlockSpec((tt * TOK_ROWS, LANES), prev_tile),
                  pl.BlockSpec((tt, TOP_K), prev_tile),
                  pl.BlockSpec(memory_space=pl.ANY),
                  full((D_MODEL, D_EXPERT)), full((D_MODEL, D_EXPERT)), full((D_EXPERT, D_MODEL)),
                  full((1, D_MODEL)), full((1, D_MODEL))],
        out_specs=out_spec,
        scratch_shapes=[pltpu.VMEM((2, TOP_K, tt * TOK_ROWS, LANES), F32),
                        pltpu.SemaphoreType.DMA((2,))],
    )
    return pl.pallas_call(
        functools.partial(_combine_body, tt=tt, n_tiles=n_tiles, token_tile_out=token_tile_out),
        grid_spec=grid_spec,
        out_shape=out_shape,
        compiler_params=pltpu.CompilerParams(
            dimension_semantics=("arbitrary",), vmem_limit_bytes=VMEM_LIMIT),
        name="moe_combine",
    )(pad_start, top_idx, rank, h_tiles, gates_tk, ys, ws1.astype(BF16), ws3.astype(BF16),
      ws2.astype(BF16), g.reshape(1, -1), b.reshape(1, -1))


def _moe_layer(h_tiles, w_router, router_bias, w1, w3, w2, layer, ws1, ws3, ws2, g, b,
               token_tile_out):
    n_tok = h_tiles.shape[0] // TOK_ROWS
    tb = EXPERT_TB
    top_idx, gates, rank, counts = _router(h_tiles, w_router, router_bias)

    n_blocks = -(-(n_tok * TOP_K + N_EXPERTS * (tb - 1)) // tb)
    counts = counts.reshape(N_EXPERTS)
    padded = ((counts + tb - 1) // tb) * tb
    pad_end = jnp.cumsum(padded)
    pad_start = (pad_end - padded).astype(jnp.int32)
    block_first_row = jnp.arange(n_blocks, dtype=jnp.int32) * tb
    block_expert = jnp.minimum(
        jnp.sum((pad_end[None, :] <= block_first_row[:, None]).astype(jnp.int32), axis=1),
        N_EXPERTS - 1)
    n_active = (pad_end[-1:] // tb).astype(jnp.int32)

    xs = _dispatch(h_tiles, top_idx, rank, pad_start, counts, n_blocks * tb)
    ys = _experts(xs, block_expert, n_active, w1, w3, w2, layer)
    return _combine(h_tiles, top_idx, rank, gates.T, pad_start, ys, ws1, ws3, ws2, g, b,
                    token_tile_out)


def _proj_body(h_ref, w_ref, *o_refs, tt, scale):
    y = _bdot(_rows_load(h_ref, tt).astype(BF16), w_ref[...])
    for j, o_ref in enumerate(o_refs):
        o_ref[...] = (y[:, j * O_WIDTH:(j + 1) * O_WIDTH] * scale).astype(BF16)


def _project(h_tiles, w, scale):
    n_tok = h_tiles.shape[0] // TOK_ROWS
    tt = PROJ_TT
    n_out = w.shape[1] // O_WIDTH
    out_spec = pl.BlockSpec((tt, O_WIDTH), lambda i: (i, 0))
    return pl.pallas_call(
        functools.partial(_proj_body, tt=tt, scale=scale),
        grid=(n_tok // tt,),
        in_specs=[pl.BlockSpec((tt * TOK_ROWS, LANES), lambda i: (i, 0)),
                  pl.BlockSpec(w.shape, lambda i: (0, 0))],
        out_specs=[out_spec] * n_out,
        out_shape=[jax.ShapeDtypeStruct((n_tok, O_WIDTH), BF16)] * n_out,
        compiler_params=pltpu.CompilerParams(
            dimension_semantics=("arbitrary",), vmem_limit_bytes=VMEM_LIMIT),
        name="projection",
    )(h_tiles, w.astype(BF16))


def _attn_body(q_ref, kp_ref, kc_ref, vp_ref, vc_ref, o_ref, lse_ref, *, dilation, n_steps):
    t = ATT_BLOCK
    not_first = pl.program_id(2) > 0
    qi = lax.broadcasted_iota(jnp.int32, (t, 2 * t), 0)
    kj = lax.broadcasted_iota(jnp.int32, (t, 2 * t), 1)
    dist = t + qi - kj
    valid = (dist >= 0) & (dist <= n_steps) & (not_first | (kj >= t))
    token_dist = (dilation * dist).astype(F32)
    low_half = lax.broadcasted_iota(jnp.int32, (t, LANES), 1) < HEAD_DIM
    contract_last = (((1,), (1,)), ((), ()))

    for pair in range(ATT_HEADS // 2):
        sl = slice(pair * LANES, (pair + 1) * LANES)
        q2 = q_ref[:, sl]
        k2 = jnp.concatenate([kp_ref[:, sl], kc_ref[:, sl]], axis=0)
        v2 = jnp.concatenate([vp_ref[:, sl], vc_ref[:, sl]], axis=0)
        outs, lses = [], []
        for half in range(2):
            head = 2 * pair + half
            slope = 2.0 ** (-8.0 * (head + 1) / ATT_HEADS)
            keep = low_half if half == 0 else ~low_half
            qh = jnp.where(keep, q2, jnp.zeros_like(q2))
            s = lax.dot_general(qh, k2, contract_last, preferred_element_type=F32)
            s = jnp.where(valid, s - slope * token_dist, MASK_VALUE)
            m = jnp.max(s, axis=-1, keepdims=True)
            p = jnp.exp(s - m)
            l = jnp.sum(p, axis=-1, keepdims=True)
            outs.append(_bdot(p.astype(BF16), v2) / l)
            lses.append(m + jnp.log(l))
        o_ref[:, sl] = jnp.where(low_half, outs[0], outs[1])
        lse_ref[:, sl] = jnp.where(low_half, lses[0], lses[1])


def _attn_branch(q, k, v, bsz, seq, window, dilation):
    t = ATT_BLOCK
    sub_len = seq // dilation
    nb = sub_len // t
    shape3 = (bsz, sub_len, dilation * O_WIDTH)
    q, k, v = (a.reshape(shape3) for a in (q, k, v))
    cur = pl.BlockSpec((None, t, O_WIDTH), lambda bi, r, n: (bi, n, r))
    prev = pl.BlockSpec((None, t, O_WIDTH), lambda bi, r, n: (bi, jnp.maximum(n - 1, 0), r))
    o, lse = pl.pallas_call(
        functools.partial(_attn_body, dilation=dilation, n_steps=window // dilation),
        grid=(bsz, dilation, nb),
        in_specs=[cur, prev, cur, prev, cur],
        out_specs=[cur, cur],
        out_shape=[jax.ShapeDtypeStruct(shape3, F32)] * 2,
        compiler_params=pltpu.CompilerParams(
            dimension_semantics=("arbitrary",) * 3, vmem_limit_bytes=VMEM_LIMIT),
        name=f"dilated_attn_d{dilation}",
    )(q, k, k, v, v)
    return o.reshape(bsz * seq, O_WIDTH), lse.reshape(bsz * seq, O_WIDTH)


def _merge_body(*refs, tt):
    o_refs = refs[:N_GROUPS]
    lse_refs = refs[N_GROUPS:2 * N_GROUPS]
    h_ref, wo_ref, g_ref, b_ref, out_ref = refs[2 * N_GROUPS:]
    lses = [r[...] for r in lse_refs]
    m = functools.reduce(jnp.maximum, lses)
    ws = [jnp.exp(l - m) for l in lses]
    den = functools.reduce(lambda a, c: a + c, ws)
    o = functools.reduce(lambda a, c: a + c, [w * r[...] for w, r in zip(ws, o_refs)]) / den
    mix = _bdot(o.astype(BF16), wo_ref[...])
    z = DEEPNORM_ALPHA * _rows_load(h_ref, tt) + mix
    _rows_store(out_ref, _layer_norm(z, g_ref[...], b_ref[...]), tt)


def _merge(outs, lses, h_tiles, w_o, g, b):
    n_tok = h_tiles.shape[0] // TOK_ROWS
    tt = MERGE_TT
    tok = pl.BlockSpec((tt, O_WIDTH), lambda i: (i, 0))
    tiles = pl.BlockSpec((tt * TOK_ROWS, LANES), lambda i: (i, 0))
    full = lambda shape: pl.BlockSpec(shape, lambda i: (0,) * len(shape))
    return pl.pallas_call(
        functools.partial(_merge_body, tt=tt),
        grid=(n_tok // tt,),
        in_specs=[tok] * (2 * N_GROUPS) + [tiles, full((O_WIDTH, D_MODEL)), full((1, D_MODEL)),
                                           full((1, D_MODEL))],
        out_specs=tiles,
        out_shape=jax.ShapeDtypeStruct(h_tiles.shape, F32),
        compiler_params=pltpu.CompilerParams(
            dimension_semantics=("arbitrary",), vmem_limit_bytes=VMEM_LIMIT),
        name="attn_merge",
    )(*outs, *lses, h_tiles, w_o.astype(BF16), g.reshape(1, -1), b.reshape(1, -1))


def kernel(x, a_w_in, a_conv_w, a_conv_b, a_w_gate_a, a_b_gate_a, a_w_gate_x, a_b_gate_x, a_lambda, a_w_out, w_kv_shared, b_w_q, b_w_o, moe_w_router, moe_router_bias, moe_w1, moe_w3, moe_w2, moe_ws1, moe_ws3, moe_ws2, ln_g, ln_b):
    bsz, seq, d = x.shape
    assert d == D_MODEL and seq % (DILATION_PATTERNS[-1][1] * ATT_BLOCK) == 0

    def moe(h_tiles, layer, token_tile_out):
        return _moe_layer(h_tiles, moe_w_router[layer], moe_router_bias[layer], moe_w1, moe_w3,
                          moe_w2, layer, moe_ws1[layer], moe_ws3[layer], moe_ws2[layer],
                          ln_g[layer, 1], ln_b[layer, 1], token_tile_out)

    h = _rglru_layer(x, a_w_in[0], a_conv_w[0], a_conv_b[0], a_w_gate_a[0], a_b_gate_a[0],
                     a_w_gate_x[0], a_b_gate_x[0], a_lambda[0], a_w_out[0], ln_g[0, 0], ln_b[0, 0])
    h = moe(h, 0, True)
    kv = _project(h, w_kv_shared, 1.0)
    ks, vs = kv[:N_GROUPS], kv[N_GROUPS:]

    qs = _project(h, b_w_q[0], HEAD_DIM ** -0.5)
    outs, lses = [], []
    for grp, (window, dilation) in enumerate(DILATION_PATTERNS):
        o, lse = _attn_branch(qs[grp], ks[grp], vs[grp], bsz, seq, window, dilation)
        outs.append(o)
        lses.append(lse)
    h = _merge(outs, lses, h, b_w_o[0], ln_g[1, 0], ln_b[1, 0])
    h = moe(h, 1, False)
    return h.reshape(bsz, seq, d)
```

```python
import functools
import math

import jax
import jax.numpy as jnp
from jax import lax
from jax.experimental import pallas as pl
from jax.experimental.pallas import tpu as pltpu

F32 = jnp.float32
BF16 = jnp.bfloat16
U32 = jnp.uint32

D_MODEL = 1024
LRU_BLOCKS = 4
CONV_WIDTH = 4
LRU_C = 8.0
ATT_HEADS = 8
HEAD_DIM = 64
DILATION_PATTERNS = ((128, 1), (512, 4), (2048, 16))
N_GROUPS = len(DILATION_PATTERNS)
ATT_BLOCK = 128
O_WIDTH = ATT_HEADS * HEAD_DIM
N_EXPERTS = 256
TOP_K = 8
N_EXPERT_GROUPS = 8
TOPK_GROUPS = 4
D_EXPERT = 256
ROUTED_SCALE = 2.5
DEPTH = 2
DEEPNORM_ALPHA = (2 * DEPTH) ** 0.25
LN_EPS = 1e-5
MASK_VALUE = -1e30

SUBLANES = 8
LANES = 128
TOK_ROWS = D_MODEL // LANES
PK_ROWS = D_MODEL // 2 // LANES
VMEM_LIMIT = 56 * 1024 * 1024

RGLRU_TS = 256
ROUTER_TT = 256
SORTED_ROWS_TL = 2048
DISPATCH_TT = 128
EXPERT_TB = 256
COMBINE_TT = 128
PROJ_TT = 512
MERGE_TT = 256


def _rows_load(ref, n_tok):
    return jnp.concatenate(
        [ref[pl.ds(s, n_tok, stride=TOK_ROWS), :] for s in range(TOK_ROWS)], axis=1)


def _rows_store(ref, val, n_tok):
    for s in range(TOK_ROWS):
        ref[pl.ds(s, n_tok, stride=TOK_ROWS), :] = val[:, s * LANES:(s + 1) * LANES]


def _layer_norm(z, g, b):
    mu = jnp.mean(z, axis=-1, keepdims=True)
    zc = z - mu
    var = jnp.mean(zc * zc, axis=-1, keepdims=True)
    return zc * lax.rsqrt(var + LN_EPS) * g + b


def _silu(x):
    return x * jax.nn.sigmoid(x)


def _gelu_tanh(x):
    c = math.sqrt(2.0 / math.pi)
    return 0.5 * x * (1.0 + jnp.tanh(c * (x + 0.044715 * (x * x * x))))


def _bdot(a, b):
    return jnp.dot(a, b, preferred_element_type=F32)


def _rglru_body(x_ref, win_ref, cw_ref, cb_ref, wga_ref, bga_ref, wgx_ref, bgx_ref, lam_ref,
                wout_ref, g_ref, b_ref, o_ref, hc_ref, tail_ref, *, ts):
    width = D_MODEL
    bw = width // LRU_BLOCKS

    @pl.when(pl.program_id(1) == 0)
    def _():
        hc_ref[...] = jnp.zeros_like(hc_ref)
        tail_ref[...] = jnp.zeros_like(tail_ref)

    x = x_ref[...]
    xz = _bdot(x.astype(BF16), win_ref[...])
    xr = xz[:, :width]
    gate = xz[:, width:]

    tail = tail_ref[...]
    row8 = lax.broadcasted_iota(jnp.int32, (SUBLANES, width), 0)
    cw = cw_ref[...]
    xc = xr * cw[CONV_WIDTH - 1:CONV_WIDTH, :] + cb_ref[...]
    for j in range(1, CONV_WIDTH):
        rx = pltpu.roll(xr, j, 0)
        rp = pltpu.roll(tail, j, 0)
        top = jnp.where(row8 < j, rp, rx[:SUBLANES])
        shifted = jnp.concatenate([top, rx[SUBLANES:]], axis=0)
        xc = xc + shifted * cw[CONV_WIDTH - 1 - j:CONV_WIDTH - j, :]
    tail_ref[...] = xr[ts - SUBLANES:]

    xcb = xc.astype(BF16)

    def block_diag(w_ref):
        return jnp.concatenate(
            [_bdot(xcb[:, n * bw:(n + 1) * bw], w_ref[n]) for n in range(LRU_BLOCKS)], axis=1)

    r = jax.nn.sigmoid(block_diag(wga_ref) + bga_ref[...])
    i = jax.nn.sigmoid(block_diag(wgx_ref) + bgx_ref[...])
    lam = lam_ref[...]
    softplus_neg_lam = jnp.maximum(-lam, 0.0) + jnp.log1p(jnp.exp(-jnp.abs(lam)))
    log_a = (-LRU_C * r) * softplus_neg_lam
    a = jnp.exp(log_a)
    mult = jnp.sqrt(-jnp.tanh(log_a) * (a * a + 1.0))
    u = mult * (i * xc)

    rows = lax.broadcasted_iota(jnp.int32, (ts, width), 0)
    cum_a, cum_u = a, u
    sh = 1
    while sh < ts:
        a_prev = pltpu.roll(cum_a, sh, 0)
        u_prev = pltpu.roll(cum_u, sh, 0)
        live = rows >= sh
        cum_u = jnp.where(live, cum_a * u_prev, 0.0) + cum_u
        cum_a = jnp.where(live, cum_a * a_prev, cum_a)
        sh *= 2
    h = cum_a * hc_ref[...] + cum_u
    hc_ref[...] = h[ts - 1:ts]

    y = (h * _gelu_tanh(gate)).astype(BF16)
    mix = _bdot(y, wout_ref[...])
    z = DEEPNORM_ALPHA * x + mix
    _rows_store(o_ref, _layer_norm(z, g_ref[...], b_ref[...]), ts)


def _rglru_layer(x, w_in, conv_w, conv_b, wga, bga, wgx, bgx, lam, w_out, g, b):
    bsz, seq, d = x.shape
    ts = RGLRU_TS
    ns = seq // ts
    row = lambda v: v.reshape(1, -1)
    full = lambda shape: pl.BlockSpec(shape, lambda bi, si: (0,) * len(shape))
    return pl.pallas_call(
        functools.partial(_rglru_body, ts=ts),
        grid=(bsz, ns),
        in_specs=[
            pl.BlockSpec((None, ts, d), lambda bi, si: (bi, si, 0)),
            full((d, 2 * d)), full((CONV_WIDTH, d)), full((1, d)),
            full((LRU_BLOCKS, d // LRU_BLOCKS, d // LRU_BLOCKS)), full((1, d)),
            full((LRU_BLOCKS, d // LRU_BLOCKS, d // LRU_BLOCKS)), full((1, d)),
            full((1, d)), full((d, d)), full((1, d)), full((1, d)),
        ],
        out_specs=pl.BlockSpec((ts * TOK_ROWS, LANES), lambda bi, si: (bi * ns + si, 0)),
        out_shape=jax.ShapeDtypeStruct((bsz * seq * TOK_ROWS, LANES), F32),
        scratch_shapes=[pltpu.VMEM((1, d), F32), pltpu.VMEM((SUBLANES, d), F32)],
        compiler_params=pltpu.CompilerParams(
            dimension_semantics=("arbitrary", "arbitrary"), vmem_limit_bytes=VMEM_LIMIT),
        name="rglru_layer",
    )(x, w_in.astype(BF16), conv_w, row(conv_b), wga.astype(BF16), row(bga), wgx.astype(BF16),
      row(bgx), row(lam), w_out.astype(BF16), row(g), row(b))


def _router_body(h_ref, wrt_ref, bias_ref, idx_ref, gate_ref, rank_ref, cnt_ref, carry_ref, *, tt):
    n_e = N_EXPERTS
    per_group = n_e // N_EXPERT_GROUPS

    @pl.when(pl.program_id(0) == 0)
    def _():
        carry_ref[...] = jnp.zeros_like(carry_ref)

    h = _rows_load(h_ref, tt)
    logits = lax.dot_general(wrt_ref[...], h, (((1,), (1,)), ((), ())),
                             precision=lax.Precision.HIGHEST, preferred_element_type=F32)
    scores = jax.nn.sigmoid(logits)
    biased = scores + bias_ref[...]

    j_iota = lax.broadcasted_iota(jnp.int32, (per_group, tt), 0)
    group_score = []
    for g in range(N_EXPERT_GROUPS):
        bg = biased[g * per_group:(g + 1) * per_group]
        m1 = jnp.max(bg, axis=0, keepdims=True)
        i1 = jnp.min(jnp.where(bg == m1, j_iota, per_group), axis=0, keepdims=True)
        m2 = jnp.max(jnp.where(j_iota == i1, -jnp.inf, bg), axis=0, keepdims=True)
        group_score.append(m1 + m2)

    masked = []
    for g in range(N_EXPERT_GROUPS):
        beaten_by = jnp.zeros((1, tt), jnp.int32)
        for o in range(N_EXPERT_GROUPS):
            if o == g:
                continue
            wins = group_score[o] > group_score[g]
            if o < g:
                wins = wins | (group_score[o] == group_score[g])
            beaten_by = beaten_by + wins.astype(jnp.int32)
        keep = beaten_by < TOPK_GROUPS
        masked.append(jnp.where(keep, biased[g * per_group:(g + 1) * per_group], MASK_VALUE))
    cur = jnp.concatenate(masked, axis=0)

    e_iota = lax.broadcasted_iota(jnp.int32, (n_e, tt), 0)
    idx_rows, score_rows, sels = [], [], []
    multi_hot = jnp.zeros((n_e, tt), F32)
    for _ in range(TOP_K):
        m = jnp.max(cur, axis=0, keepdims=True)
        ik = jnp.min(jnp.where(cur == m, e_iota, n_e), axis=0, keepdims=True)
        sel = e_iota == ik
        score_rows.append(jnp.sum(jnp.where(sel, scores, 0.0), axis=0, keepdims=True))
        cur = jnp.where(sel, -jnp.inf, cur)
        multi_hot = multi_hot + jnp.where(sel, 1.0, 0.0)
        idx_rows.append(ik)
        sels.append(sel)
    top_s = jnp.concatenate(score_rows, axis=0)
    gate_ref[...] = top_s / jnp.sum(top_s, axis=0, keepdims=True) * ROUTED_SCALE
    idx_ref[...] = jnp.concatenate(idx_rows, axis=0)

    t_row = lax.broadcasted_iota(jnp.int32, (tt, tt), 0)
    t_col = lax.broadcasted_iota(jnp.int32, (tt, tt), 1)
    strict_upper = jnp.where(t_row < t_col, 1.0, 0.0).astype(BF16)
    before = _bdot(multi_hot.astype(BF16), strict_upper) + carry_ref[...]
    rank_rows = [jnp.sum(jnp.where(sel, before, 0.0), axis=0, keepdims=True) for sel in sels]
    rank_ref[...] = jnp.concatenate(rank_rows, axis=0).astype(jnp.int32)
    carry = carry_ref[...] + jnp.sum(multi_hot, axis=1, keepdims=True)
    carry_ref[...] = carry
    cnt_ref[...] = carry.astype(jnp.int32)


def _router(h_tiles, w_router, router_bias):
    n_tok = h_tiles.shape[0] // TOK_ROWS
    tt = ROUTER_TT
    kt = lambda dt: jax.ShapeDtypeStruct((TOP_K, n_tok), dt)
    tok_spec = pl.BlockSpec((TOP_K, tt), lambda i: (0, i))
    return pl.pallas_call(
        functools.partial(_router_body, tt=tt),
        grid=(n_tok // tt,),
        in_specs=[
            pl.BlockSpec((tt * TOK_ROWS, LANES), lambda i: (i, 0)),
            pl.BlockSpec((N_EXPERTS, D_MODEL), lambda i: (0, 0)),
            pl.BlockSpec((N_EXPERTS, 1), lambda i: (0, 0)),
        ],
        out_specs=[tok_spec, tok_spec, tok_spec, pl.BlockSpec((N_EXPERTS, 1), lambda i: (0, 0))],
        out_shape=[kt(jnp.int32), kt(F32), kt(jnp.int32),
                   jax.ShapeDtypeStruct((N_EXPERTS, 1), jnp.int32)],
        scratch_shapes=[pltpu.VMEM((N_EXPERTS, 1), F32)],
        compiler_params=pltpu.CompilerParams(
            dimension_semantics=("arbitrary",), vmem_limit_bytes=VMEM_LIMIT),
        name="moe_router",
    )(h_tiles, w_router.T, router_bias.reshape(N_EXPERTS, 1))


def _pack_rows(x):
    def rounded_bits(v):
        bits = pltpu.bitcast(v, U32)
        return bits + (U32(0x7FFF) + ((bits >> 16) & U32(1)))
    half = D_MODEL // 2
    return (rounded_bits(x[:, :half]) & U32(0xFFFF0000)) | (rounded_bits(x[:, half:]) >> 16)


def _unpack_rows(p):
    return jnp.concatenate([pltpu.bitcast(p & U32(0xFFFF0000), F32), pltpu.bitcast(p << 16, F32)],
                           axis=1)


def _packed_load(ref, n_tok):
    return jnp.concatenate(
        [ref[pl.ds(s, n_tok, stride=PK_ROWS), :] for s in range(PK_ROWS)], axis=1)


def _packed_store(ref, val, n_tok):
    for s in range(PK_ROWS):
        ref[pl.ds(s, n_tok, stride=PK_ROWS), :] = val[:, s * LANES:(s + 1) * LANES]


def _sorted_rows_body(idx_ref, rank_ref, start_ref, o_ref):
    tl = idx_ref.shape[1]
    e_iota = lax.broadcasted_iota(jnp.int32, (N_EXPERTS, tl), 0)
    start = start_ref[...]
    rows = []
    for k in range(TOP_K):
        hit = e_iota == idx_ref[k:k + 1, :]
        base = jnp.sum(jnp.where(hit, start, 0.0), axis=0, keepdims=True)
        rows.append(base.astype(jnp.int32) + rank_ref[k:k + 1, :])
    o_ref[...] = jnp.concatenate(rows, axis=0)


def _sorted_rows(top_idx, rank, pad_start):
    n_tok = top_idx.shape[1]
    tl = SORTED_ROWS_TL
    tok = pl.BlockSpec((TOP_K, tl), lambda i: (0, i))
    return pl.pallas_call(
        _sorted_rows_body,
        grid=(n_tok // tl,),
        in_specs=[tok, tok, pl.BlockSpec((N_EXPERTS, 1), lambda i: (0, 0))],
        out_specs=tok,
        out_shape=jax.ShapeDtypeStruct((TOP_K, n_tok), jnp.int32),
        compiler_params=pltpu.CompilerParams(
            dimension_semantics=("arbitrary",), vmem_limit_bytes=VMEM_LIMIT),
        name="moe_sorted_rows",
    )(top_idx, rank, pad_start.astype(F32).reshape(N_EXPERTS, 1))


def _dispatch_body(start_ref, cnt_ref, dest_ref, h_ref, xs_ref, pk_ref, zero_ref, sem, zero_sem,
                   *, tt, tb):
    @pl.when(pl.program_id(0) == 0)
    def _():
        zero_ref[...] = jnp.zeros_like(zero_ref)

        def pad_copies(e, fn):
            n_pad = ((cnt_ref[e] + (tb - 1)) & (-tb)) - cnt_ref[e]
            row = start_ref[e] + cnt_ref[e]
            for bit in range(tb.bit_length() - 1):
                size = 1 << bit
                has_bit = ((n_pad >> bit) & 1) == 1

                @pl.when(has_bit)
                def _():
                    fn(pltpu.make_async_copy(zero_ref.at[pl.ds(0, size)],
                                             xs_ref.at[pl.ds(row, size)], zero_sem))
                row = row + jnp.where(has_bit, size, 0)

        def start_all(e, c):
            pad_copies(e, lambda cp: cp.start())
            return c

        def wait_all(e, c):
            pad_copies(e, lambda cp: cp.wait())
            return c

        lax.fori_loop(0, N_EXPERTS, start_all, 0)
        lax.fori_loop(0, N_EXPERTS, wait_all, 0)

    packed = _pack_rows(_rows_load(h_ref, tt))
    for s in range(PK_ROWS):
        pk_ref[:, s, :] = packed[:, s * LANES:(s + 1) * LANES]
    for t in range(tt):
        for k in range(TOP_K):
            pltpu.make_async_copy(pk_ref.at[t], xs_ref.at[dest_ref[k, t]], sem).start(priority=k % 2)
    for k in range(TOP_K):
        pltpu.make_async_copy(pk_ref, xs_ref.at[pl.ds(0, tt)], sem).wait()


def _dispatch(h_tiles, dest_rows, pad_start, counts, n_rows):
    n_tok = h_tiles.shape[0] // TOK_ROWS
    tt = DISPATCH_TT
    grid_spec = pltpu.PrefetchScalarGridSpec(
        num_scalar_prefetch=2,
        grid=(n_tok // tt,),
        in_specs=[pl.BlockSpec((TOP_K, tt), lambda i, s, c: (0, i), memory_space=pltpu.SMEM),
                  pl.BlockSpec((tt * TOK_ROWS, LANES), lambda i, s, c: (i, 0))],
        out_specs=pl.BlockSpec(memory_space=pl.ANY),
        scratch_shapes=[pltpu.VMEM((tt, PK_ROWS, LANES), U32),
                        pltpu.VMEM((EXPERT_TB // 2, PK_ROWS, LANES), U32),
                        pltpu.SemaphoreType.DMA(()), pltpu.SemaphoreType.DMA(())],
    )
    return pl.pallas_call(
        functools.partial(_dispatch_body, tt=tt, tb=EXPERT_TB),
        grid_spec=grid_spec,
        out_shape=jax.ShapeDtypeStruct((n_rows, PK_ROWS, LANES), U32),
        compiler_params=pltpu.CompilerParams(
            dimension_semantics=("arbitrary",), vmem_limit_bytes=VMEM_LIMIT),
        name="moe_dispatch",
    )(pad_start, counts, dest_rows, h_tiles)


def _expert_body(be_ref, na_ref, xs_ref, w1_ref, w3_ref, w2_ref, ys_ref, w1b, w3b, w2b, *, tb):
    i = pl.program_id(0)

    @pl.when(i < na_ref[0])
    def _():
        @pl.when((i == 0) | (be_ref[i] != be_ref[jnp.maximum(i - 1, 0)]))
        def _():
            w1b[...] = w1_ref[...].astype(BF16)
            w3b[...] = w3_ref[...].astype(BF16)
            w2b[...] = w2_ref[...].astype(BF16)

        x = _unpack_rows(_packed_load(xs_ref, tb)).astype(BF16)
        mid = (_silu(_bdot(x, w1b[...])) * _bdot(x, w3b[...])).astype(BF16)
        _packed_store(ys_ref, _pack_rows(_bdot(mid, w2b[...])), tb)


def _experts(xs, block_expert, n_active, w1, w3, w2, layer):
    tb = EXPERT_TB
    n_blocks = xs.shape[0] // (tb * PK_ROWS)
    row_block = lambda i, be, na: (jnp.minimum(i, na[0] - 1), 0)
    weight = lambda i, be, na: (layer, be[i], 0, 0)
    grid_spec = pltpu.PrefetchScalarGridSpec(
        num_scalar_prefetch=2,
        grid=(n_blocks,),
        in_specs=[
            pl.BlockSpec((tb * PK_ROWS, LANES), row_block),
            pl.BlockSpec((None, None, D_MODEL, D_EXPERT), weight),
            pl.BlockSpec((None, None, D_MODEL, D_EXPERT), weight),
            pl.BlockSpec((None, None, D_EXPERT, D_MODEL), weight),
        ],
        out_specs=pl.BlockSpec((tb * PK_ROWS, LANES), row_block),
        scratch_shapes=[pltpu.VMEM((D_MODEL, D_EXPERT), BF16), pltpu.VMEM((D_MODEL, D_EXPERT), BF16),
                        pltpu.VMEM((D_EXPERT, D_MODEL), BF16)],
    )
    return pl.pallas_call(
        functools.partial(_expert_body, tb=tb),
        grid_spec=grid_spec,
        out_shape=jax.ShapeDtypeStruct(xs.shape, U32),
        compiler_params=pltpu.CompilerParams(
            dimension_semantics=("arbitrary",), vmem_limit_bytes=VMEM_LIMIT),
        name="moe_experts",
    )(block_expert, n_active, xs, w1, w3, w2)


def _combine_body(src_ref, h_ref, gates_ref, ys_ref, ys_flat_ref, ws1_ref, ws3_ref, ws2_ref,
                  g_ref, b_ref, o_ref, gbuf, sem, *, tt, n_tiles, token_tile_out):
    j = pl.program_id(0)

    @pl.when(j < n_tiles)
    def _():
        slot = j % 2
        for t in range(tt):
            for k in range(TOP_K):
                pltpu.make_async_copy(ys_ref.at[src_ref[k, t]],
                                      gbuf.at[slot, k, pl.ds(t * PK_ROWS, PK_ROWS), :],
                                      sem.at[slot]).start(priority=k % 2)

    @pl.when(j >= 1)
    def _():
        slot = (j - 1) % 2
        h = _rows_load(h_ref, tt)
        hb = h.astype(BF16)
        mid = (_silu(_bdot(hb, ws1_ref[...])) * _bdot(hb, ws3_ref[...])).astype(BF16)
        shared = _bdot(mid, ws2_ref[...])

        for k in range(TOP_K):
            pltpu.make_async_copy(ys_flat_ref.at[pl.ds(0, tt * PK_ROWS), :], gbuf.at[slot, k],
                                  sem.at[slot]).wait()
        gates = gates_ref[...]
        routed = gates[:, 0:1] * _unpack_rows(_packed_load(gbuf.at[slot, 0], tt))
        for k in range(1, TOP_K):
            routed = routed + gates[:, k:k + 1] * _unpack_rows(_packed_load(gbuf.at[slot, k], tt))
        out = _layer_norm(DEEPNORM_ALPHA * h + (routed + shared), g_ref[...], b_ref[...])
        if token_tile_out:
            _rows_store(o_ref, out, tt)
        else:
            o_ref[...] = out


def _combine(h_tiles, src_rows, gates_tk, ys, ws1, ws3, ws2, g, b, token_tile_out):
    n_tok = h_tiles.shape[0] // TOK_ROWS
    tt = COMBINE_TT
    n_tiles = n_tok // tt
    prev_tile = lambda j: (jnp.maximum(j - 1, 0), 0)
    full = lambda shape: pl.BlockSpec(shape, lambda j: (0,) * len(shape))
    if token_tile_out:
        out_spec = pl.BlockSpec((tt * TOK_ROWS, LANES), prev_tile)
        out_shape = jax.ShapeDtypeStruct((n_tok * TOK_ROWS, LANES), F32)
    else:
        out_spec = pl.BlockSpec((tt, D_MODEL), prev_tile)
        out_shape = jax.ShapeDtypeStruct((n_tok, D_MODEL), F32)
    n_rows = ys.shape[0] // PK_ROWS
    return pl.pallas_call(
        functools.partial(_combine_body, tt=tt, n_tiles=n_tiles, token_tile_out=token_tile_out),
        grid=(n_tiles + 1,),
        in_specs=[pl.BlockSpec((TOP_K, tt), lambda j: (0, jnp.minimum(j, n_tiles - 1)),
                               memory_space=pltpu.SMEM),
                  pl.BlockSpec((tt * TOK_ROWS, LANES), prev_tile),
                  pl.BlockSpec((tt, TOP_K), prev_tile),
                  pl.BlockSpec(memory_space=pl.ANY), pl.BlockSpec(memory_space=pl.ANY),
                  full((D_MODEL, D_EXPERT)), full((D_MODEL, D_EXPERT)), full((D_EXPERT, D_MODEL)),
                  full((1, D_MODEL)), full((1, D_MODEL))],
        out_specs=out_spec,
        out_shape=out_shape,
        scratch_shapes=[pltpu.VMEM((2, TOP_K, tt * PK_ROWS, LANES), U32),
                        pltpu.SemaphoreType.DMA((2,))],
        compiler_params=pltpu.CompilerParams(
            dimension_semantics=("arbitrary",), vmem_limit_bytes=VMEM_LIMIT),
        name="moe_combine",
    )(src_rows, h_tiles, gates_tk, ys.reshape(n_rows, PK_ROWS, LANES), ys, ws1.astype(BF16),
      ws3.astype(BF16), ws2.astype(BF16), g.reshape(1, -1), b.reshape(1, -1))


def _moe_layer(h_tiles, w_router, router_bias, w1, w3, w2, layer, ws1, ws3, ws2, g, b,
               token_tile_out):
    n_tok = h_tiles.shape[0] // TOK_ROWS
    tb = EXPERT_TB
    top_idx, gates, rank, counts = _router(h_tiles, w_router, router_bias)

    n_blocks = -(-(n_tok * TOP_K + N_EXPERTS * (tb - 1)) // tb)
    counts = counts.reshape(N_EXPERTS)
    padded = ((counts + tb - 1) // tb) * tb
    pad_end = jnp.cumsum(padded)
    pad_start = (pad_end - padded).astype(jnp.int32)
    block_first_row = jnp.arange(n_blocks, dtype=jnp.int32) * tb
    block_expert = jnp.minimum(
        jnp.sum((pad_end[None, :] <= block_first_row[:, None]).astype(jnp.int32), axis=1),
        N_EXPERTS - 1)
    n_active = (pad_end[-1:] // tb).astype(jnp.int32)

    rows = _sorted_rows(top_idx, rank, pad_start)
    xs = _dispatch(h_tiles, rows, pad_start, counts, n_blocks * tb)
    ys = _experts(xs.reshape(n_blocks * tb * PK_ROWS, LANES), block_expert, n_active, w1, w3, w2,
                  layer)
    return _combine(h_tiles, rows, gates.T, ys, ws1, ws3, ws2, g, b, token_tile_out)


def _proj_body(h_ref, w_ref, *o_refs, tt, scale):
    y = _bdot(_rows_load(h_ref, tt).astype(BF16), w_ref[...])
    for j, o_ref in enumerate(o_refs):
        o_ref[...] = (y[:, j * O_WIDTH:(j + 1) * O_WIDTH] * scale).astype(BF16)


def _project(h_tiles, w, scale):
    n_tok = h_tiles.shape[0] // TOK_ROWS
    tt = PROJ_TT
    n_out = w.shape[1] // O_WIDTH
    out_spec = pl.BlockSpec((tt, O_WIDTH), lambda i: (i, 0))
    return pl.pallas_call(
        functools.partial(_proj_body, tt=tt, scale=scale),
        grid=(n_tok // tt,),
        in_specs=[pl.BlockSpec((tt * TOK_ROWS, LANES), lambda i: (i, 0)),
                  pl.BlockSpec(w.shape, lambda i: (0, 0))],
        out_specs=[out_spec] * n_out,
        out_shape=[jax.ShapeDtypeStruct((n_tok, O_WIDTH), BF16)] * n_out,
        compiler_params=pltpu.CompilerParams(
            dimension_semantics=("arbitrary",), vmem_limit_bytes=VMEM_LIMIT),
        name="projection",
    )(h_tiles, w.astype(BF16))


def _attn_body(q_ref, kp_ref, kc_ref, vp_ref, vc_ref, o_ref, lse_ref, *, dilation, n_steps):
    t = ATT_BLOCK
    not_first = pl.program_id(2) > 0
    qi = lax.broadcasted_iota(jnp.int32, (t, 2 * t), 0)
    kj = lax.broadcasted_iota(jnp.int32, (t, 2 * t), 1)
    dist = t + qi - kj
    valid = (dist >= 0) & (dist <= n_steps) & (not_first | (kj >= t))
    token_dist = (dilation * dist).astype(F32)
    low_half = lax.broadcasted_iota(jnp.int32, (t, LANES), 1) < HEAD_DIM
    contract_last = (((1,), (1,)), ((), ()))

    for pair in range(ATT_HEADS // 2):
        sl = slice(pair * LANES, (pair + 1) * LANES)
        q2 = q_ref[:, sl]
        k2 = jnp.concatenate([kp_ref[:, sl], kc_ref[:, sl]], axis=0)
        v2 = jnp.concatenate([vp_ref[:, sl], vc_ref[:, sl]], axis=0)
        outs, lses = [], []
        for half in range(2):
            head = 2 * pair + half
            slope = 2.0 ** (-8.0 * (head + 1) / ATT_HEADS)
            keep = low_half if half == 0 else ~low_half
            qh = jnp.where(keep, q2, jnp.zeros_like(q2))
            s = lax.dot_general(qh, k2, contract_last, preferred_element_type=F32)
            s = jnp.where(valid, s - slope * token_dist, MASK_VALUE)
            m = jnp.max(s, axis=-1, keepdims=True)
            p = jnp.exp(s - m)
            l = jnp.sum(p, axis=-1, keepdims=True)
            outs.append(_bdot(p.astype(BF16), v2) / l)
            lses.append(m + jnp.log(l))
        o_ref[:, sl] = jnp.where(low_half, outs[0], outs[1])
        lse_ref[:, sl] = jnp.where(low_half, lses[0], lses[1])


def _attn_branch(q, k, v, bsz, seq, window, dilation):
    t = ATT_BLOCK
    sub_len = seq // dilation
    nb = sub_len // t
    shape3 = (bsz, sub_len, dilation * O_WIDTH)
    q, k, v = (a.reshape(shape3) for a in (q, k, v))
    cur = pl.BlockSpec((None, t, O_WIDTH), lambda bi, r, n: (bi, n, r))
    prev = pl.BlockSpec((None, t, O_WIDTH), lambda bi, r, n: (bi, jnp.maximum(n - 1, 0), r))
    o, lse = pl.pallas_call(
        functools.partial(_attn_body, dilation=dilation, n_steps=window // dilation),
        grid=(bsz, dilation, nb),
        in_specs=[cur, prev, cur, prev, cur],
        out_specs=[cur, cur],
        out_shape=[jax.ShapeDtypeStruct(shape3, F32)] * 2,
        compiler_params=pltpu.CompilerParams(
            dimension_semantics=("arbitrary",) * 3, vmem_limit_bytes=VMEM_LIMIT),
        name=f"dilated_attn_d{dilation}",
    )(q, k, k, v, v)
    return o.reshape(bsz * seq, O_WIDTH), lse.reshape(bsz * seq, O_WIDTH)


def _merge_body(*refs, tt):
    o_refs = refs[:N_GROUPS]
    lse_refs = refs[N_GROUPS:2 * N_GROUPS]
    h_ref, wo_ref, g_ref, b_ref, out_ref = refs[2 * N_GROUPS:]
    lses = [r[...] for r in lse_refs]
    m = functools.reduce(jnp.maximum, lses)
    ws = [jnp.exp(l - m) for l in lses]
    den = functools.reduce(lambda a, c: a + c, ws)
    o = functools.reduce(lambda a, c: a + c, [w * r[...] for w, r in zip(ws, o_refs)]) / den
    mix = _bdot(o.astype(BF16), wo_ref[...])
    z = DEEPNORM_ALPHA * _rows_load(h_ref, tt) + mix
    _rows_store(out_ref, _layer_norm(z, g_ref[...], b_ref[...]), tt)


def _merge(outs, lses, h_tiles, w_o, g, b):
    n_tok = h_tiles.shape[0] // TOK_ROWS
    tt = MERGE_TT
    tok = pl.BlockSpec((tt, O_WIDTH), lambda i: (i, 0))
    tiles = pl.BlockSpec((tt * TOK_ROWS, LANES), lambda i: (i, 0))
    full = lambda shape: pl.BlockSpec(shape, lambda i: (0,) * len(shape))
    return pl.pallas_call(
        functools.partial(_merge_body, tt=tt),
        grid=(n_tok // tt,),
        in_specs=[tok] * (2 * N_GROUPS) + [tiles, full((O_WIDTH, D_MODEL)), full((1, D_MODEL)),
                                           full((1, D_MODEL))],
        out_specs=tiles,
        out_shape=jax.ShapeDtypeStruct(h_tiles.shape, F32),
        compiler_params=pltpu.CompilerParams(
            dimension_semantics=("arbitrary",), vmem_limit_bytes=VMEM_LIMIT),
        name="attn_merge",
    )(*outs, *lses, h_tiles, w_o.astype(BF16), g.reshape(1, -1), b.reshape(1, -1))


def kernel(x, a_w_in, a_conv_w, a_conv_b, a_w_gate_a, a_b_gate_a, a_w_gate_x, a_b_gate_x, a_lambda, a_w_out, w_kv_shared, b_w_q, b_w_o, moe_w_router, moe_router_bias, moe_w1, moe_w3, moe_w2, moe_ws1, moe_ws3, moe_ws2, ln_g, ln_b):
    bsz, seq, d = x.shape
    assert d == D_MODEL and seq % (DILATION_PATTERNS[-1][1] * ATT_BLOCK) == 0

    def moe(h_tiles, layer, token_tile_out):
        return _moe_layer(h_tiles, moe_w_router[layer], moe_router_bias[layer], moe_w1, moe_w3,
                          moe_w2, layer, moe_ws1[layer], moe_ws3[layer], moe_ws2[layer],
                          ln_g[layer, 1], ln_b[layer, 1], token_tile_out)

    h = _rglru_layer(x, a_w_in[0], a_conv_w[0], a_conv_b[0], a_w_gate_a[0], a_b_gate_a[0],
                     a_w_gate_x[0], a_b_gate_x[0], a_lambda[0], a_w_out[0], ln_g[0, 0], ln_b[0, 0])
    h = moe(h, 0, True)
    kv = _project(h, w_kv_shared, 1.0)
    ks, vs = kv[:N_GROUPS], kv[N_GROUPS:]

    qs = _project(h, b_w_q[0], HEAD_DIM ** -0.5)
    outs, lses = [], []
    for grp, (window, dilation) in enumerate(DILATION_PATTERNS):
        o, lse = _attn_branch(qs[grp], ks[grp], vs[grp], bsz, seq, window, dilation)
        outs.append(o)
        lses.append(lse)
    h = _merge(outs, lses, h, b_w_o[0], ln_g[1, 0], ln_b[1, 0])
    h = moe(h, 1, False)
    return h.reshape(bsz, seq, d)
```

```python
import functools
import math

import jax
import jax.numpy as jnp
from jax import lax
from jax.experimental import pallas as pl
from jax.experimental.pallas import tpu as pltpu

F32 = jnp.float32
BF16 = jnp.bfloat16
I32 = jnp.int32

D_MODEL = 1024
LRU_BLOCKS = 4
CONV_WIDTH = 4
LRU_C = 8.0
ATT_HEADS = 8
HEAD_DIM = 64
DILATION_PATTERNS = ((128, 1), (512, 4), (2048, 16))
N_GROUPS = len(DILATION_PATTERNS)
ATT_BLOCK = 128
O_WIDTH = ATT_HEADS * HEAD_DIM
N_EXPERTS = 256
TOP_K = 8
N_EXPERT_GROUPS = 8
TOPK_GROUPS = 4
D_EXPERT = 256
ROUTED_SCALE = 2.5
DEPTH = 2
DEEPNORM_ALPHA = (2 * DEPTH) ** 0.25
LN_EPS = 1e-5
MASK_VALUE = -1e30

SUBLANES = 8
LANES = 128
TOK_ROWS = D_MODEL // LANES
PK_ROWS = D_MODEL // 2 // LANES
VMEM_LIMIT = 56 * 1024 * 1024

RGLRU_TS = 256
ROUTER_TT = 256
SORTED_ROWS_TL = 2048
DISPATCH_TT = 128
EXPERT_TB = 256
COMBINE_TT = 128
PROJ_TT = 512
MERGE_TT = 256


def _rows_load(ref, n_tok):
    return jnp.concatenate(
        [ref[pl.ds(s, n_tok, stride=TOK_ROWS), :] for s in range(TOK_ROWS)], axis=1)


def _rows_store(ref, val, n_tok):
    for s in range(TOK_ROWS):
        ref[pl.ds(s, n_tok, stride=TOK_ROWS), :] = val[:, s * LANES:(s + 1) * LANES]


def _layer_norm(z, g, b):
    mu = jnp.mean(z, axis=-1, keepdims=True)
    zc = z - mu
    var = jnp.mean(zc * zc, axis=-1, keepdims=True)
    return zc * lax.rsqrt(var + LN_EPS) * g + b


def _silu(x):
    return x * jax.nn.sigmoid(x)


def _gelu_tanh(x):
    c = math.sqrt(2.0 / math.pi)
    return 0.5 * x * (1.0 + jnp.tanh(c * (x + 0.044715 * (x * x * x))))


def _bdot(a, b):
    return jnp.dot(a, b, preferred_element_type=F32)


def _rglru_body(x_ref, win_ref, cw_ref, cb_ref, wga_ref, bga_ref, wgx_ref, bgx_ref, lam_ref,
                wout_ref, g_ref, b_ref, o_ref, hc_ref, tail_ref, *, ts):
    width = D_MODEL
    bw = width // LRU_BLOCKS

    @pl.when(pl.program_id(1) == 0)
    def _():
        hc_ref[...] = jnp.zeros_like(hc_ref)
        tail_ref[...] = jnp.zeros_like(tail_ref)

    x = x_ref[...]
    xz = _bdot(x.astype(BF16), win_ref[...])
    xr = xz[:, :width]
    gate = xz[:, width:]

    tail = tail_ref[...]
    row8 = lax.broadcasted_iota(jnp.int32, (SUBLANES, width), 0)
    cw = cw_ref[...]
    xc = xr * cw[CONV_WIDTH - 1:CONV_WIDTH, :] + cb_ref[...]
    for j in range(1, CONV_WIDTH):
        rx = pltpu.roll(xr, j, 0)
        rp = pltpu.roll(tail, j, 0)
        top = jnp.where(row8 < j, rp, rx[:SUBLANES])
        shifted = jnp.concatenate([top, rx[SUBLANES:]], axis=0)
        xc = xc + shifted * cw[CONV_WIDTH - 1 - j:CONV_WIDTH - j, :]
    tail_ref[...] = xr[ts - SUBLANES:]

    xcb = xc.astype(BF16)

    def block_diag(w_ref):
        return jnp.concatenate(
            [_bdot(xcb[:, n * bw:(n + 1) * bw], w_ref[n]) for n in range(LRU_BLOCKS)], axis=1)

    r = jax.nn.sigmoid(block_diag(wga_ref) + bga_ref[...])
    i = jax.nn.sigmoid(block_diag(wgx_ref) + bgx_ref[...])
    lam = lam_ref[...]
    softplus_neg_lam = jnp.maximum(-lam, 0.0) + jnp.log1p(jnp.exp(-jnp.abs(lam)))
    log_a = (-LRU_C * r) * softplus_neg_lam
    a = jnp.exp(log_a)
    mult = jnp.sqrt(-jnp.tanh(log_a) * (a * a + 1.0))
    u = mult * (i * xc)

    rows = lax.broadcasted_iota(jnp.int32, (ts, width), 0)
    cum_a, cum_u = a, u
    sh = 1
    while sh < ts:
        a_prev = pltpu.roll(cum_a, sh, 0)
        u_prev = pltpu.roll(cum_u, sh, 0)
        live = rows >= sh
        cum_u = jnp.where(live, cum_a * u_prev, 0.0) + cum_u
        cum_a = jnp.where(live, cum_a * a_prev, cum_a)
        sh *= 2
    h = cum_a * hc_ref[...] + cum_u
    hc_ref[...] = h[ts - 1:ts]

    y = (h * _gelu_tanh(gate)).astype(BF16)
    mix = _bdot(y, wout_ref[...])
    z = DEEPNORM_ALPHA * x + mix
    _rows_store(o_ref, _layer_norm(z, g_ref[...], b_ref[...]), ts)


def _rglru_layer(x, w_in, conv_w, conv_b, wga, bga, wgx, bgx, lam, w_out, g, b):
    bsz, seq, d = x.shape
    ts = RGLRU_TS
    ns = seq // ts
    row = lambda v: v.reshape(1, -1)
    full = lambda shape: pl.BlockSpec(shape, lambda bi, si: (0,) * len(shape))
    return pl.pallas_call(
        functools.partial(_rglru_body, ts=ts),
        grid=(bsz, ns),
        in_specs=[
            pl.BlockSpec((None, ts, d), lambda bi, si: (bi, si, 0)),
            full((d, 2 * d)), full((CONV_WIDTH, d)), full((1, d)),
            full((LRU_BLOCKS, d // LRU_BLOCKS, d // LRU_BLOCKS)), full((1, d)),
            full((LRU_BLOCKS, d // LRU_BLOCKS, d // LRU_BLOCKS)), full((1, d)),
            full((1, d)), full((d, d)), full((1, d)), full((1, d)),
        ],
        out_specs=pl.BlockSpec((ts * TOK_ROWS, LANES), lambda bi, si: (bi * ns + si, 0)),
        out_shape=jax.ShapeDtypeStruct((bsz * seq * TOK_ROWS, LANES), F32),
        scratch_shapes=[pltpu.VMEM((1, d), F32), pltpu.VMEM((SUBLANES, d), F32)],
        compiler_params=pltpu.CompilerParams(
            dimension_semantics=("arbitrary", "arbitrary"), vmem_limit_bytes=VMEM_LIMIT),
        name="rglru_layer",
    )(x, w_in.astype(BF16), conv_w, row(conv_b), wga.astype(BF16), row(bga), wgx.astype(BF16),
      row(bgx), row(lam), w_out.astype(BF16), row(g), row(b))


def _router_body(h_ref, wrt_ref, bias_ref, idx_ref, gate_ref, rank_ref, cnt_ref, carry_ref, *, tt):
    n_e = N_EXPERTS
    per_group = n_e // N_EXPERT_GROUPS

    @pl.when(pl.program_id(0) == 0)
    def _():
        carry_ref[...] = jnp.zeros_like(carry_ref)

    h = _rows_load(h_ref, tt)
    logits = lax.dot_general(wrt_ref[...], h, (((1,), (1,)), ((), ())),
                             precision=lax.Precision.HIGHEST, preferred_element_type=F32)
    scores = jax.nn.sigmoid(logits)
    biased = scores + bias_ref[...]

    j_iota = lax.broadcasted_iota(jnp.int32, (per_group, tt), 0)
    group_score = []
    for g in range(N_EXPERT_GROUPS):
        bg = biased[g * per_group:(g + 1) * per_group]
        m1 = jnp.max(bg, axis=0, keepdims=True)
        i1 = jnp.min(jnp.where(bg == m1, j_iota, per_group), axis=0, keepdims=True)
        m2 = jnp.max(jnp.where(j_iota == i1, -jnp.inf, bg), axis=0, keepdims=True)
        group_score.append(m1 + m2)

    masked = []
    for g in range(N_EXPERT_GROUPS):
        beaten_by = jnp.zeros((1, tt), jnp.int32)
        for o in range(N_EXPERT_GROUPS):
            if o == g:
                continue
            wins = group_score[o] > group_score[g]
            if o < g:
                wins = wins | (group_score[o] == group_score[g])
            beaten_by = beaten_by + wins.astype(jnp.int32)
        keep = beaten_by < TOPK_GROUPS
        masked.append(jnp.where(keep, biased[g * per_group:(g + 1) * per_group], MASK_VALUE))
    cur = jnp.concatenate(masked, axis=0)

    e_iota = lax.broadcasted_iota(jnp.int32, (n_e, tt), 0)
    idx_rows, score_rows, sels = [], [], []
    multi_hot = jnp.zeros((n_e, tt), F32)
    for _ in range(TOP_K):
        m = jnp.max(cur, axis=0, keepdims=True)
        ik = jnp.min(jnp.where(cur == m, e_iota, n_e), axis=0, keepdims=True)
        sel = e_iota == ik
        score_rows.append(jnp.sum(jnp.where(sel, scores, 0.0), axis=0, keepdims=True))
        cur = jnp.where(sel, -jnp.inf, cur)
        multi_hot = multi_hot + jnp.where(sel, 1.0, 0.0)
        idx_rows.append(ik)
        sels.append(sel)
    top_s = jnp.concatenate(score_rows, axis=0)
    gate_ref[...] = top_s / jnp.sum(top_s, axis=0, keepdims=True) * ROUTED_SCALE
    idx_ref[...] = jnp.concatenate(idx_rows, axis=0)

    t_row = lax.broadcasted_iota(jnp.int32, (tt, tt), 0)
    t_col = lax.broadcasted_iota(jnp.int32, (tt, tt), 1)
    strict_upper = jnp.where(t_row < t_col, 1.0, 0.0).astype(BF16)
    before = _bdot(multi_hot.astype(BF16), strict_upper) + carry_ref[...]
    rank_rows = [jnp.sum(jnp.where(sel, before, 0.0), axis=0, keepdims=True) for sel in sels]
    rank_ref[...] = jnp.concatenate(rank_rows, axis=0).astype(jnp.int32)
    carry = carry_ref[...] + jnp.sum(multi_hot, axis=1, keepdims=True)
    carry_ref[...] = carry
    cnt_ref[...] = carry.astype(jnp.int32)


def _router(h_tiles, w_router, router_bias):
    n_tok = h_tiles.shape[0] // TOK_ROWS
    tt = ROUTER_TT
    kt = lambda dt: jax.ShapeDtypeStruct((TOP_K, n_tok), dt)
    tok_spec = pl.BlockSpec((TOP_K, tt), lambda i: (0, i))
    return pl.pallas_call(
        functools.partial(_router_body, tt=tt),
        grid=(n_tok // tt,),
        in_specs=[
            pl.BlockSpec((tt * TOK_ROWS, LANES), lambda i: (i, 0)),
            pl.BlockSpec((N_EXPERTS, D_MODEL), lambda i: (0, 0)),
            pl.BlockSpec((N_EXPERTS, 1), lambda i: (0, 0)),
        ],
        out_specs=[tok_spec, tok_spec, tok_spec, pl.BlockSpec((N_EXPERTS, 1), lambda i: (0, 0))],
        out_shape=[kt(jnp.int32), kt(F32), kt(jnp.int32),
                   jax.ShapeDtypeStruct((N_EXPERTS, 1), jnp.int32)],
        scratch_shapes=[pltpu.VMEM((N_EXPERTS, 1), F32)],
        compiler_params=pltpu.CompilerParams(
            dimension_semantics=("arbitrary",), vmem_limit_bytes=VMEM_LIMIT),
        name="moe_router",
    )(h_tiles, w_router.T, router_bias.reshape(N_EXPERTS, 1))


def _pack_rows(x):
    half = D_MODEL // 2
    return pltpu.pack_elementwise([x[:, :half], x[:, half:]], packed_dtype=BF16)


def _unpack_rows(p):
    return jnp.concatenate(
        [pltpu.unpack_elementwise(p, index=i, packed_dtype=BF16, unpacked_dtype=F32) for i in range(2)],
        axis=1)


def _packed_load(ref, n_tok):
    return jnp.concatenate(
        [ref[pl.ds(s, n_tok, stride=PK_ROWS), :] for s in range(PK_ROWS)], axis=1)


def _packed_store(ref, val, n_tok):
    for s in range(PK_ROWS):
        ref[pl.ds(s, n_tok, stride=PK_ROWS), :] = val[:, s * LANES:(s + 1) * LANES]


def _sorted_rows_body(idx_ref, rank_ref, start_ref, o_ref):
    tl = idx_ref.shape[1]
    e_iota = lax.broadcasted_iota(jnp.int32, (N_EXPERTS, tl), 0)
    start = start_ref[...]
    rows = []
    for k in range(TOP_K):
        hit = e_iota == idx_ref[k:k + 1, :]
        base = jnp.sum(jnp.where(hit, start, 0.0), axis=0, keepdims=True)
        rows.append(base.astype(jnp.int32) + rank_ref[k:k + 1, :])
    o_ref[...] = jnp.concatenate(rows, axis=0)


def _sorted_rows(top_idx, rank, pad_start):
    n_tok = top_idx.shape[1]
    tl = SORTED_ROWS_TL
    tok = pl.BlockSpec((TOP_K, tl), lambda i: (0, i))
    return pl.pallas_call(
        _sorted_rows_body,
        grid=(n_tok // tl,),
        in_specs=[tok, tok, pl.BlockSpec((N_EXPERTS, 1), lambda i: (0, 0))],
        out_specs=tok,
        out_shape=jax.ShapeDtypeStruct((TOP_K, n_tok), jnp.int32),
        compiler_params=pltpu.CompilerParams(
            dimension_semantics=("arbitrary",), vmem_limit_bytes=VMEM_LIMIT),
        name="moe_sorted_rows",
    )(top_idx, rank, pad_start.astype(F32).reshape(N_EXPERTS, 1))


def _dispatch_body(start_ref, cnt_ref, dest_ref, h_ref, xs_ref, pk_ref, zero_ref, sem, zero_sem,
                   *, tt, tb):
    @pl.when(pl.program_id(0) == 0)
    def _():
        zero_ref[...] = jnp.zeros_like(zero_ref)

        def pad_copies(e, fn):
            n_pad = ((cnt_ref[e] + (tb - 1)) & (-tb)) - cnt_ref[e]
            row = start_ref[e] + cnt_ref[e]
            for bit in range(tb.bit_length() - 1):
                size = 1 << bit
                has_bit = ((n_pad >> bit) & 1) == 1

                @pl.when(has_bit)
                def _():
                    fn(pltpu.make_async_copy(zero_ref.at[pl.ds(0, size)],
                                             xs_ref.at[pl.ds(row, size)], zero_sem))
                row = row + jnp.where(has_bit, size, 0)

        def start_all(e, c):
            pad_copies(e, lambda cp: cp.start())
            return c

        def wait_all(e, c):
            pad_copies(e, lambda cp: cp.wait())
            return c

        lax.fori_loop(0, N_EXPERTS, start_all, 0)
        lax.fori_loop(0, N_EXPERTS, wait_all, 0)

    packed = _pack_rows(_rows_load(h_ref, tt))
    for s in range(PK_ROWS):
        pk_ref[:, s, :] = packed[:, s * LANES:(s + 1) * LANES]
    for t in range(tt):
        for k in range(TOP_K):
            pltpu.make_async_copy(pk_ref.at[t], xs_ref.at[dest_ref[k, t]], sem).start(priority=k % 2)
    for k in range(TOP_K):
        pltpu.make_async_copy(pk_ref, xs_ref.at[pl.ds(0, tt)], sem).wait()


def _dispatch(h_tiles, dest_rows, pad_start, counts, n_rows):
    n_tok = h_tiles.shape[0] // TOK_ROWS
    tt = DISPATCH_TT
    grid_spec = pltpu.PrefetchScalarGridSpec(
        num_scalar_prefetch=2,
        grid=(n_tok // tt,),
        in_specs=[pl.BlockSpec((TOP_K, tt), lambda i, s, c: (0, i), memory_space=pltpu.SMEM),
                  pl.BlockSpec((tt * TOK_ROWS, LANES), lambda i, s, c: (i, 0))],
        out_specs=pl.BlockSpec(memory_space=pl.ANY),
        scratch_shapes=[pltpu.VMEM((tt, PK_ROWS, LANES), I32),
                        pltpu.VMEM((EXPERT_TB // 2, PK_ROWS, LANES), I32),
                        pltpu.SemaphoreType.DMA(()), pltpu.SemaphoreType.DMA(())],
    )
    return pl.pallas_call(
        functools.partial(_dispatch_body, tt=tt, tb=EXPERT_TB),
        grid_spec=grid_spec,
        out_shape=jax.ShapeDtypeStruct((n_rows, PK_ROWS, LANES), I32),
        compiler_params=pltpu.CompilerParams(
            dimension_semantics=("arbitrary",), vmem_limit_bytes=VMEM_LIMIT),
        name="moe_dispatch",
    )(pad_start, counts, dest_rows, h_tiles)


def _expert_body(be_ref, first_ref, slot_ref, next_ref, na_ref, xs_ref, w1_ref, w3_ref, w2_ref, ys_ref,
                 w13_buf, w2_buf, w1b, w3b, w2b, sem, *, tb, layer):
    i = pl.program_id(0)

    def weight_copies(expert, slot):
        return (pltpu.make_async_copy(w1_ref.at[layer, expert], w13_buf.at[slot, 0], sem.at[slot]),
                pltpu.make_async_copy(w3_ref.at[layer, expert], w13_buf.at[slot, 1], sem.at[slot]),
                pltpu.make_async_copy(w2_ref.at[layer, expert], w2_buf.at[slot], sem.at[slot]))

    @pl.when(i < na_ref[0])
    def _():
        @pl.when(first_ref[i] == 1)
        def _():
            slot = slot_ref[i]

            @pl.when(i == 0)
            def _():
                for cp in weight_copies(be_ref[0], slot):
                    cp.start()

            for cp in weight_copies(be_ref[i], slot):
                cp.wait()

            @pl.when(next_ref[i] >= 0)
            def _():
                for cp in weight_copies(next_ref[i], 1 - slot):
                    cp.start()

            w1b[...] = w13_buf[slot, 0].astype(BF16)
            w3b[...] = w13_buf[slot, 1].astype(BF16)
            w2b[...] = w2_buf[slot].astype(BF16)

        x = _unpack_rows(_packed_load(xs_ref, tb)).astype(BF16)
        mid = (_silu(_bdot(x, w1b[...])) * _bdot(x, w3b[...])).astype(BF16)
        _packed_store(ys_ref, _pack_rows(_bdot(mid, w2b[...])), tb)


def _experts(xs, block_expert, first, slot, next_expert, n_active, w1, w3, w2, layer):
    tb = EXPERT_TB
    n_blocks = xs.shape[0] // (tb * PK_ROWS)
    row_block = lambda i, be, fi, sl, nx, na: (jnp.minimum(i, na[0] - 1), 0)
    grid_spec = pltpu.PrefetchScalarGridSpec(
        num_scalar_prefetch=5,
        grid=(n_blocks,),
        in_specs=[
            pl.BlockSpec((tb * PK_ROWS, LANES), row_block),
            pl.BlockSpec(memory_space=pl.ANY), pl.BlockSpec(memory_space=pl.ANY),
            pl.BlockSpec(memory_space=pl.ANY),
        ],
        out_specs=pl.BlockSpec((tb * PK_ROWS, LANES), row_block),
        scratch_shapes=[pltpu.VMEM((2, 2, D_MODEL, D_EXPERT), F32),
                        pltpu.VMEM((2, D_EXPERT, D_MODEL), F32),
                        pltpu.VMEM((D_MODEL, D_EXPERT), BF16), pltpu.VMEM((D_MODEL, D_EXPERT), BF16),
                        pltpu.VMEM((D_EXPERT, D_MODEL), BF16), pltpu.SemaphoreType.DMA((2,))],
    )
    return pl.pallas_call(
        functools.partial(_expert_body, tb=tb, layer=layer),
        grid_spec=grid_spec,
        out_shape=jax.ShapeDtypeStruct(xs.shape, I32),
        compiler_params=pltpu.CompilerParams(
            dimension_semantics=("arbitrary",), vmem_limit_bytes=VMEM_LIMIT),
        name="moe_experts",
    )(block_expert, first, slot, next_expert, n_active, xs, w1, w3, w2)


def _combine_body(src_ref, h_ref, gates_ref, ys_ref, ys_flat_ref, ws1_ref, ws3_ref, ws2_ref,
                  g_ref, b_ref, o_ref, gbuf, sem, *, tt, n_tiles, token_tile_out):
    j = pl.program_id(0)

    def step(issue, finish):
        nxt = j % 2
        cur = (j - 1) % 2
        if finish:
            h = _rows_load(h_ref, tt)
            hb = h.astype(BF16)
            mid = (_silu(_bdot(hb, ws1_ref[...])) * _bdot(hb, ws3_ref[...])).astype(BF16)
            routed = _bdot(mid, ws2_ref[...])
            gates = gates_ref[...]
        for k in range(TOP_K):
            if issue:
                for t in range(tt):
                    pltpu.make_async_copy(ys_ref.at[src_ref[k, t]],
                                          gbuf.at[nxt, k, pl.ds(t * PK_ROWS, PK_ROWS), :],
                                          sem.at[nxt]).start(priority=t % 2)
            if finish:
                pltpu.make_async_copy(ys_flat_ref.at[pl.ds(0, tt * PK_ROWS), :], gbuf.at[cur, k],
                                      sem.at[cur]).wait()
                routed = routed + gates[:, k:k + 1] * _unpack_rows(_packed_load(gbuf.at[cur, k], tt))
        if finish:
            out = _layer_norm(DEEPNORM_ALPHA * h + routed, g_ref[...], b_ref[...])
            if token_tile_out:
                _rows_store(o_ref, out, tt)
            else:
                o_ref[...] = out

    @pl.when(j == 0)
    def _():
        step(True, False)

    @pl.when((j >= 1) & (j < n_tiles))
    def _():
        step(True, True)

    @pl.when(j == n_tiles)
    def _():
        step(False, True)


def _combine(h_tiles, src_rows, gates_tk, ys, ws1, ws3, ws2, g, b, token_tile_out):
    n_tok = h_tiles.shape[0] // TOK_ROWS
    tt = COMBINE_TT
    n_tiles = n_tok // tt
    prev_tile = lambda j: (jnp.maximum(j - 1, 0), 0)
    full = lambda shape: pl.BlockSpec(shape, lambda j: (0,) * len(shape))
    if token_tile_out:
        out_spec = pl.BlockSpec((tt * TOK_ROWS, LANES), prev_tile)
        out_shape = jax.ShapeDtypeStruct((n_tok * TOK_ROWS, LANES), F32)
    else:
        out_spec = pl.BlockSpec((tt, D_MODEL), prev_tile)
        out_shape = jax.ShapeDtypeStruct((n_tok, D_MODEL), F32)
    n_rows = ys.shape[0] // PK_ROWS
    return pl.pallas_call(
        functools.partial(_combine_body, tt=tt, n_tiles=n_tiles, token_tile_out=token_tile_out),
        grid=(n_tiles + 1,),
        in_specs=[pl.BlockSpec((TOP_K, tt), lambda j: (0, jnp.minimum(j, n_tiles - 1)),
                               memory_space=pltpu.SMEM),
                  pl.BlockSpec((tt * TOK_ROWS, LANES), prev_tile),
                  pl.BlockSpec((tt, TOP_K), prev_tile),
                  pl.BlockSpec(memory_space=pl.ANY), pl.BlockSpec(memory_space=pl.ANY),
                  full((D_MODEL, D_EXPERT)), full((D_MODEL, D_EXPERT)), full((D_EXPERT, D_MODEL)),
                  full((1, D_MODEL)), full((1, D_MODEL))],
        out_specs=out_spec,
        out_shape=out_shape,
        scratch_shapes=[pltpu.VMEM((2, TOP_K, tt * PK_ROWS, LANES), I32),
                        pltpu.SemaphoreType.DMA((2,))],
        compiler_params=pltpu.CompilerParams(
            dimension_semantics=("arbitrary",), vmem_limit_bytes=VMEM_LIMIT),
        name="moe_combine",
    )(src_rows, h_tiles, gates_tk, ys.reshape(n_rows, PK_ROWS, LANES), ys, ws1.astype(BF16),
      ws3.astype(BF16), ws2.astype(BF16), g.reshape(1, -1), b.reshape(1, -1))


def _moe_layer(h_tiles, w_router, router_bias, w1, w3, w2, layer, ws1, ws3, ws2, g, b,
               token_tile_out):
    n_tok = h_tiles.shape[0] // TOK_ROWS
    tb = EXPERT_TB
    top_idx, gates, rank, counts = _router(h_tiles, w_router, router_bias)

    n_blocks = -(-(n_tok * TOP_K + N_EXPERTS * (tb - 1)) // tb)
    counts = counts.reshape(N_EXPERTS)
    padded = ((counts + tb - 1) // tb) * tb
    pad_end = jnp.cumsum(padded)
    pad_start = (pad_end - padded).astype(jnp.int32)
    block_first_row = jnp.arange(n_blocks, dtype=jnp.int32) * tb
    block_expert = jnp.minimum(
        jnp.sum((pad_end[None, :] <= block_first_row[:, None]).astype(jnp.int32), axis=1),
        N_EXPERTS - 1)
    n_active = (pad_end[-1:] // tb).astype(jnp.int32)
    present = counts > 0
    expert_ids = jnp.arange(N_EXPERTS, dtype=jnp.int32)
    later = lax.cummin(jnp.where(present, expert_ids, N_EXPERTS), reverse=True)
    next_present = jnp.concatenate([later[1:], jnp.full((1,), N_EXPERTS, jnp.int32)])
    next_present = jnp.where(next_present < N_EXPERTS, next_present, -1)
    ordinal = jnp.cumsum(present.astype(jnp.int32)) - 1
    first = jnp.concatenate([jnp.ones((1,), jnp.int32),
                             (block_expert[1:] != block_expert[:-1]).astype(jnp.int32)])
    slot = ordinal[block_expert] % 2
    next_expert = next_present[block_expert]

    rows = _sorted_rows(top_idx, rank, pad_start)
    xs = _dispatch(h_tiles, rows, pad_start, counts, n_blocks * tb)
    ys = _experts(xs.reshape(n_blocks * tb * PK_ROWS, LANES), block_expert, first, slot, next_expert,
                  n_active, w1, w3, w2, layer)
    return _combine(h_tiles, rows, gates.T, ys, ws1, ws3, ws2, g, b, token_tile_out)


def _proj_body(h_ref, w_ref, *o_refs, tt, scale):
    y = _bdot(_rows_load(h_ref, tt).astype(BF16), w_ref[...])
    for j, o_ref in enumerate(o_refs):
        o_ref[...] = (y[:, j * O_WIDTH:(j + 1) * O_WIDTH] * scale).astype(BF16)


def _project(h_tiles, w, scale):
    n_tok = h_tiles.shape[0] // TOK_ROWS
    tt = PROJ_TT
    n_out = w.shape[1] // O_WIDTH
    out_spec = pl.BlockSpec((tt, O_WIDTH), lambda i: (i, 0))
    return pl.pallas_call(
        functools.partial(_proj_body, tt=tt, scale=scale),
        grid=(n_tok // tt,),
        in_specs=[pl.BlockSpec((tt * TOK_ROWS, LANES), lambda i: (i, 0)),
                  pl.BlockSpec(w.shape, lambda i: (0, 0))],
        out_specs=[out_spec] * n_out,
        out_shape=[jax.ShapeDtypeStruct((n_tok, O_WIDTH), BF16)] * n_out,
        compiler_params=pltpu.CompilerParams(
            dimension_semantics=("arbitrary",), vmem_limit_bytes=VMEM_LIMIT),
        name="projection",
    )(h_tiles, w.astype(BF16))


def _attn_body(q_ref, kp_ref, kc_ref, vp_ref, vc_ref, o_ref, lse_ref, *, dilation, n_steps):
    t = ATT_BLOCK
    not_first = pl.program_id(2) > 0
    qi = lax.broadcasted_iota(jnp.int32, (t, 2 * t), 0)
    kj = lax.broadcasted_iota(jnp.int32, (t, 2 * t), 1)
    dist = t + qi - kj
    valid = (dist >= 0) & (dist <= n_steps) & (not_first | (kj >= t))
    token_dist = (dilation * dist).astype(F32)
    low_half = lax.broadcasted_iota(jnp.int32, (t, LANES), 1) < HEAD_DIM
    contract_last = (((1,), (1,)), ((), ()))

    for pair in range(ATT_HEADS // 2):
        sl = slice(pair * LANES, (pair + 1) * LANES)
        q2 = q_ref[:, sl]
        k2 = jnp.concatenate([kp_ref[:, sl], kc_ref[:, sl]], axis=0)
        v2 = jnp.concatenate([vp_ref[:, sl], vc_ref[:, sl]], axis=0)
        outs, lses = [], []
        for half in range(2):
            head = 2 * pair + half
            slope = 2.0 ** (-8.0 * (head + 1) / ATT_HEADS)
            keep = low_half if half == 0 else ~low_half
            qh = jnp.where(keep, q2, jnp.zeros_like(q2))
            s = lax.dot_general(qh, k2, contract_last, preferred_element_type=F32)
            s = jnp.where(valid, s - slope * token_dist, MASK_VALUE)
            m = jnp.max(s, axis=-1, keepdims=True)
            p = jnp.exp(s - m)
            l = jnp.sum(p, axis=-1, keepdims=True)
            outs.append(_bdot(p.astype(BF16), v2) / l)
            lses.append(m + jnp.log(l))
        o_ref[:, sl] = jnp.where(low_half, outs[0], outs[1])
        lse_ref[:, sl] = jnp.where(low_half, lses[0], lses[1])


def _attn_branch(q, k, v, bsz, seq, window, dilation):
    t = ATT_BLOCK
    sub_len = seq // dilation
    nb = sub_len // t
    shape3 = (bsz, sub_len, dilation * O_WIDTH)
    q, k, v = (a.reshape(shape3) for a in (q, k, v))
    cur = pl.BlockSpec((None, t, O_WIDTH), lambda bi, r, n: (bi, n, r))
    prev = pl.BlockSpec((None, t, O_WIDTH), lambda bi, r, n: (bi, jnp.maximum(n - 1, 0), r))
    o, lse = pl.pallas_call(
        functools.partial(_attn_body, dilation=dilation, n_steps=window // dilation),
        grid=(bsz, dilation, nb),
        in_specs=[cur, prev, cur, prev, cur],
        out_specs=[cur, cur],
        out_shape=[jax.ShapeDtypeStruct(shape3, F32)] * 2,
        compiler_params=pltpu.CompilerParams(
            dimension_semantics=("arbitrary",) * 3, vmem_limit_bytes=VMEM_LIMIT),
        name=f"dilated_attn_d{dilation}",
    )(q, k, k, v, v)
    return o.reshape(bsz * seq, O_WIDTH), lse.reshape(bsz * seq, O_WIDTH)


def _merge_body(*refs, tt):
    o_refs = refs[:N_GROUPS]
    lse_refs = refs[N_GROUPS:2 * N_GROUPS]
    h_ref, wo_ref, g_ref, b_ref, out_ref = refs[2 * N_GROUPS:]
    lses = [r[...] for r in lse_refs]
    m = functools.reduce(jnp.maximum, lses)
    ws = [jnp.exp(l - m) for l in lses]
    den = functools.reduce(lambda a, c: a + c, ws)
    o = functools.reduce(lambda a, c: a + c, [w * r[...] for w, r in zip(ws, o_refs)]) / den
    mix = _bdot(o.astype(BF16), wo_ref[...])
    z = DEEPNORM_ALPHA * _rows_load(h_ref, tt) + mix
    _rows_store(out_ref, _layer_norm(z, g_ref[...], b_ref[...]), tt)


def _merge(outs, lses, h_tiles, w_o, g, b):
    n_tok = h_tiles.shape[0] // TOK_ROWS
    tt = MERGE_TT
    tok = pl.BlockSpec((tt, O_WIDTH), lambda i: (i, 0))
    tiles = pl.BlockSpec((tt * TOK_ROWS, LANES), lambda i: (i, 0))
    full = lambda shape: pl.BlockSpec(shape, lambda i: (0,) * len(shape))
    return pl.pallas_call(
        functools.partial(_merge_body, tt=tt),
        grid=(n_tok // tt,),
        in_specs=[tok] * (2 * N_GROUPS) + [tiles, full((O_WIDTH, D_MODEL)), full((1, D_MODEL)),
                                           full((1, D_MODEL))],
        out_specs=tiles,
        out_shape=jax.ShapeDtypeStruct(h_tiles.shape, F32),
        compiler_params=pltpu.CompilerParams(
            dimension_semantics=("arbitrary",), vmem_limit_bytes=VMEM_LIMIT),
        name="attn_merge",
    )(*outs, *lses, h_tiles, w_o.astype(BF16), g.reshape(1, -1), b.reshape(1, -1))


def kernel(x, a_w_in, a_conv_w, a_conv_b, a_w_gate_a, a_b_gate_a, a_w_gate_x, a_b_gate_x, a_lambda, a_w_out, w_kv_shared, b_w_q, b_w_o, moe_w_router, moe_router_bias, moe_w1, moe_w3, moe_w2, moe_ws1, moe_ws3, moe_ws2, ln_g, ln_b):
    bsz, seq, d = x.shape
    assert d == D_MODEL and seq % (DILATION_PATTERNS[-1][1] * ATT_BLOCK) == 0

    def moe(h_tiles, layer, token_tile_out):
        return _moe_layer(h_tiles, moe_w_router[layer], moe_router_bias[layer], moe_w1, moe_w3,
                          moe_w2, layer, moe_ws1[layer], moe_ws3[layer], moe_ws2[layer],
                          ln_g[layer, 1], ln_b[layer, 1], token_tile_out)

    h = _rglru_layer(x, a_w_in[0], a_conv_w[0], a_conv_b[0], a_w_gate_a[0], a_b_gate_a[0],
                     a_w_gate_x[0], a_b_gate_x[0], a_lambda[0], a_w_out[0], ln_g[0, 0], ln_b[0, 0])
    h = moe(h, 0, True)
    kv = _project(h, w_kv_shared, 1.0)
    ks, vs = kv[:N_GROUPS], kv[N_GROUPS:]

    qs = _project(h, b_w_q[0], HEAD_DIM ** -0.5)
    outs, lses = [], []
    for grp, (window, dilation) in enumerate(DILATION_PATTERNS):
        o, lse = _attn_branch(qs[grp], ks[grp], vs[grp], bsz, seq, window, dilation)
        outs.append(o)
        lses.append(lse)
    h = _merge(outs, lses, h, b_w_o[0], ln_g[1, 0], ln_b[1, 0])
    h = moe(h, 1, False)
    return h.reshape(bsz, seq, d)
```

```python
import functools
import math

import jax
import jax.numpy as jnp
from jax import lax
from jax.experimental import pallas as pl
from jax.experimental.pallas import tpu as pltpu

F32 = jnp.float32
BF16 = jnp.bfloat16
I32 = jnp.int32

D_MODEL = 1024
LRU_BLOCKS = 4
CONV_WIDTH = 4
LRU_C = 8.0
ATT_HEADS = 8
HEAD_DIM = 64
DILATION_PATTERNS = ((128, 1), (512, 4), (2048, 16))
N_GROUPS = len(DILATION_PATTERNS)
ATT_BLOCK = 128
O_WIDTH = ATT_HEADS * HEAD_DIM
N_EXPERTS = 256
TOP_K = 8
N_EXPERT_GROUPS = 8
TOPK_GROUPS = 4
D_EXPERT = 256
ROUTED_SCALE = 2.5
DEPTH = 2
DEEPNORM_ALPHA = (2 * DEPTH) ** 0.25
LN_EPS = 1e-5
MASK_VALUE = -1e30

SUBLANES = 8
LANES = 128
TOK_ROWS = D_MODEL // LANES
PK_ROWS = D_MODEL // 2 // LANES
PAIR_SLABS = ATT_HEADS * HEAD_DIM // LANES // 2
VMEM_LIMIT = 56 * 1024 * 1024

RGLRU_TS = 256
ROUTER_TT = 256
SORTED_ROWS_TL = 2048
DISPATCH_TT = 128
EXPERT_TB = 256
COMBINE_TT = 128
PROJ_TT = 512
MERGE_TT = 256


def _rows_load(ref, n_tok):
    return jnp.concatenate(
        [ref[pl.ds(s, n_tok, stride=TOK_ROWS), :] for s in range(TOK_ROWS)], axis=1)


def _rows_store(ref, val, n_tok):
    for s in range(TOK_ROWS):
        ref[pl.ds(s, n_tok, stride=TOK_ROWS), :] = val[:, s * LANES:(s + 1) * LANES]


def _layer_norm(z, g, b):
    mu = jnp.mean(z, axis=-1, keepdims=True)
    zc = z - mu
    var = jnp.mean(zc * zc, axis=-1, keepdims=True)
    return zc * lax.rsqrt(var + LN_EPS) * g + b


def _silu(x):
    return x * jax.nn.sigmoid(x)


def _gelu_tanh(x):
    c = math.sqrt(2.0 / math.pi)
    return 0.5 * x * (1.0 + jnp.tanh(c * (x + 0.044715 * (x * x * x))))


def _bdot(a, b):
    return jnp.dot(a, b, preferred_element_type=F32)


def _rglru_body(x_ref, win_ref, cw_ref, cb_ref, wga_ref, bga_ref, wgx_ref, bgx_ref, lam_ref,
                wout_ref, g_ref, b_ref, o_ref, hc_ref, tail_ref, *, ts):
    width = D_MODEL
    bw = width // LRU_BLOCKS

    @pl.when(pl.program_id(1) == 0)
    def _():
        hc_ref[...] = jnp.zeros_like(hc_ref)
        tail_ref[...] = jnp.zeros_like(tail_ref)

    x = x_ref[...]
    xz = _bdot(x.astype(BF16), win_ref[...])
    xr = xz[:, :width]
    gate = xz[:, width:]

    tail = tail_ref[...]
    row8 = lax.broadcasted_iota(jnp.int32, (SUBLANES, width), 0)
    cw = cw_ref[...]
    xc = xr * cw[CONV_WIDTH - 1:CONV_WIDTH, :] + cb_ref[...]
    for j in range(1, CONV_WIDTH):
        rx = pltpu.roll(xr, j, 0)
        rp = pltpu.roll(tail, j, 0)
        top = jnp.where(row8 < j, rp, rx[:SUBLANES])
        shifted = jnp.concatenate([top, rx[SUBLANES:]], axis=0)
        xc = xc + shifted * cw[CONV_WIDTH - 1 - j:CONV_WIDTH - j, :]
    tail_ref[...] = xr[ts - SUBLANES:]

    xcb = xc.astype(BF16)

    def block_diag(w_ref):
        return jnp.concatenate(
            [_bdot(xcb[:, n * bw:(n + 1) * bw], w_ref[n]) for n in range(LRU_BLOCKS)], axis=1)

    r = jax.nn.sigmoid(block_diag(wga_ref) + bga_ref[...])
    i = jax.nn.sigmoid(block_diag(wgx_ref) + bgx_ref[...])
    lam = lam_ref[...]
    softplus_neg_lam = jnp.maximum(-lam, 0.0) + jnp.log1p(jnp.exp(-jnp.abs(lam)))
    log_a = (-LRU_C * r) * softplus_neg_lam
    a = jnp.exp(log_a)
    mult = jnp.sqrt(-jnp.tanh(log_a) * (a * a + 1.0))
    u = mult * (i * xc)

    groups = ts // SUBLANES
    a3 = a.reshape(groups, SUBLANES, width)
    u3 = u.reshape(groups, SUBLANES, width)
    sub = lax.broadcasted_iota(jnp.int32, (groups, SUBLANES, width), 1)
    sh = 1
    while sh < SUBLANES:
        a_prev = pltpu.roll(a3, sh, 1)
        u_prev = pltpu.roll(u3, sh, 1)
        live = sub >= sh
        u3 = jnp.where(live, a3 * u_prev, 0.0) + u3
        a3 = jnp.where(live, a3 * a_prev, a3)
        sh *= 2
    carry = hc_ref[...]
    h_groups = []
    for grp in range(groups):
        h_grp = a3[grp] * carry + u3[grp]
        h_groups.append(h_grp)
        carry = h_grp[SUBLANES - 1:SUBLANES]
    h = jnp.concatenate(h_groups, axis=0)
    hc_ref[...] = carry

    y = (h * _gelu_tanh(gate)).astype(BF16)
    mix = _bdot(y, wout_ref[...])
    z = DEEPNORM_ALPHA * x + mix
    _rows_store(o_ref, _layer_norm(z, g_ref[...], b_ref[...]), ts)


def _rglru_layer(x, w_in, conv_w, conv_b, wga, bga, wgx, bgx, lam, w_out, g, b):
    bsz, seq, d = x.shape
    ts = RGLRU_TS
    ns = seq // ts
    row = lambda v: v.reshape(1, -1)
    full = lambda shape: pl.BlockSpec(shape, lambda bi, si: (0,) * len(shape))
    return pl.pallas_call(
        functools.partial(_rglru_body, ts=ts),
        grid=(bsz, ns),
        in_specs=[
            pl.BlockSpec((None, ts, d), lambda bi, si: (bi, si, 0)),
            full((d, 2 * d)), full((CONV_WIDTH, d)), full((1, d)),
            full((LRU_BLOCKS, d // LRU_BLOCKS, d // LRU_BLOCKS)), full((1, d)),
            full((LRU_BLOCKS, d // LRU_BLOCKS, d // LRU_BLOCKS)), full((1, d)),
            full((1, d)), full((d, d)), full((1, d)), full((1, d)),
        ],
        out_specs=pl.BlockSpec((ts * TOK_ROWS, LANES), lambda bi, si: (bi * ns + si, 0)),
        out_shape=jax.ShapeDtypeStruct((bsz * seq * TOK_ROWS, LANES), F32),
        scratch_shapes=[pltpu.VMEM((1, d), F32), pltpu.VMEM((SUBLANES, d), F32)],
        compiler_params=pltpu.CompilerParams(
            dimension_semantics=("arbitrary", "arbitrary"), vmem_limit_bytes=VMEM_LIMIT),
        name="rglru_layer",
    )(x, w_in.astype(BF16), conv_w, row(conv_b), wga.astype(BF16), row(bga), wgx.astype(BF16),
      row(bgx), row(lam), w_out.astype(BF16), row(g), row(b))


def _router_body(h_ref, wrt_ref, bias_ref, idx_ref, gate_ref, rank_ref, cnt_ref, carry_ref, *, tt):
    n_e = N_EXPERTS
    per_group = n_e // N_EXPERT_GROUPS

    @pl.when(pl.program_id(0) == 0)
    def _():
        carry_ref[...] = jnp.zeros_like(carry_ref)

    h = _rows_load(h_ref, tt)
    logits = lax.dot_general(wrt_ref[...], h, (((1,), (1,)), ((), ())),
                             precision=lax.Precision.HIGHEST, preferred_element_type=F32)
    scores = jax.nn.sigmoid(logits)
    biased = scores + bias_ref[...]

    j_iota = lax.broadcasted_iota(jnp.int32, (per_group, tt), 0)
    group_score = []
    for g in range(N_EXPERT_GROUPS):
        bg = biased[g * per_group:(g + 1) * per_group]
        m1 = jnp.max(bg, axis=0, keepdims=True)
        i1 = jnp.min(jnp.where(bg == m1, j_iota, per_group), axis=0, keepdims=True)
        m2 = jnp.max(jnp.where(j_iota == i1, -jnp.inf, bg), axis=0, keepdims=True)
        group_score.append(m1 + m2)

    masked = []
    for g in range(N_EXPERT_GROUPS):
        beaten_by = jnp.zeros((1, tt), jnp.int32)
        for o in range(N_EXPERT_GROUPS):
            if o == g:
                continue
            wins = group_score[o] > group_score[g]
            if o < g:
                wins = wins | (group_score[o] == group_score[g])
            beaten_by = beaten_by + wins.astype(jnp.int32)
        keep = beaten_by < TOPK_GROUPS
        masked.append(jnp.where(keep, biased[g * per_group:(g + 1) * per_group], MASK_VALUE))
    cur = jnp.concatenate(masked, axis=0)

    e_iota = lax.broadcasted_iota(jnp.int32, (n_e, tt), 0)
    idx_rows, score_rows, sels = [], [], []
    multi_hot = jnp.zeros((n_e, tt), F32)
    for _ in range(TOP_K):
        m = jnp.max(cur, axis=0, keepdims=True)
        ik = jnp.min(jnp.where(cur == m, e_iota, n_e), axis=0, keepdims=True)
        sel = e_iota == ik
        score_rows.append(jnp.sum(jnp.where(sel, scores, 0.0), axis=0, keepdims=True))
        cur = jnp.where(sel, -jnp.inf, cur)
        multi_hot = multi_hot + jnp.where(sel, 1.0, 0.0)
        idx_rows.append(ik)
        sels.append(sel)
    top_s = jnp.concatenate(score_rows, axis=0)
    gate_ref[...] = top_s / jnp.sum(top_s, axis=0, keepdims=True) * ROUTED_SCALE
    idx_ref[...] = jnp.concatenate(idx_rows, axis=0)

    t_row = lax.broadcasted_iota(jnp.int32, (tt, tt), 0)
    t_col = lax.broadcasted_iota(jnp.int32, (tt, tt), 1)
    strict_upper = jnp.where(t_row < t_col, 1.0, 0.0).astype(BF16)
    before = _bdot(multi_hot.astype(BF16), strict_upper) + carry_ref[...]
    rank_rows = [jnp.sum(jnp.where(sel, before, 0.0), axis=0, keepdims=True) for sel in sels]
    rank_ref[...] = jnp.concatenate(rank_rows, axis=0).astype(jnp.int32)
    carry = carry_ref[...] + jnp.sum(multi_hot, axis=1, keepdims=True)
    carry_ref[...] = carry
    cnt_ref[...] = carry.astype(jnp.int32)


def _router(h_tiles, w_router, router_bias):
    n_tok = h_tiles.shape[0] // TOK_ROWS
    tt = ROUTER_TT
    kt = lambda dt: jax.ShapeDtypeStruct((TOP_K, n_tok), dt)
    tok_spec = pl.BlockSpec((TOP_K, tt), lambda i: (0, i))
    return pl.pallas_call(
        functools.partial(_router_body, tt=tt),
        grid=(n_tok // tt,),
        in_specs=[
            pl.BlockSpec((tt * TOK_ROWS, LANES), lambda i: (i, 0)),
            pl.BlockSpec((N_EXPERTS, D_MODEL), lambda i: (0, 0)),
            pl.BlockSpec((N_EXPERTS, 1), lambda i: (0, 0)),
        ],
        out_specs=[tok_spec, tok_spec, tok_spec, pl.BlockSpec((N_EXPERTS, 1), lambda i: (0, 0))],
        out_shape=[kt(jnp.int32), kt(F32), kt(jnp.int32),
                   jax.ShapeDtypeStruct((N_EXPERTS, 1), jnp.int32)],
        scratch_shapes=[pltpu.VMEM((N_EXPERTS, 1), F32)],
        compiler_params=pltpu.CompilerParams(
            dimension_semantics=("arbitrary",), vmem_limit_bytes=VMEM_LIMIT),
        name="moe_router",
    )(h_tiles, w_router.T, router_bias.reshape(N_EXPERTS, 1))


def _pack_rows(x):
    half = D_MODEL // 2
    return pltpu.pack_elementwise([x[:, :half], x[:, half:]], packed_dtype=BF16)


def _unpack_rows(p):
    return jnp.concatenate(
        [pltpu.unpack_elementwise(p, index=i, packed_dtype=BF16, unpacked_dtype=F32) for i in range(2)],
        axis=1)


def _packed_load(ref, n_tok):
    return jnp.concatenate(
        [ref[pl.ds(s, n_tok, stride=PK_ROWS), :] for s in range(PK_ROWS)], axis=1)


def _packed_store(ref, val, n_tok):
    for s in range(PK_ROWS):
        ref[pl.ds(s, n_tok, stride=PK_ROWS), :] = val[:, s * LANES:(s + 1) * LANES]


def _sorted_rows_body(idx_ref, rank_ref, start_ref, o_ref):
    tl = idx_ref.shape[1]
    e_iota = lax.broadcasted_iota(jnp.int32, (N_EXPERTS, tl), 0)
    start = start_ref[...]
    rows = []
    for k in range(TOP_K):
        hit = e_iota == idx_ref[k:k + 1, :]
        base = jnp.sum(jnp.where(hit, start, 0.0), axis=0, keepdims=True)
        rows.append(base.astype(jnp.int32) + rank_ref[k:k + 1, :])
    o_ref[...] = jnp.concatenate(rows, axis=0)


def _sorted_rows(top_idx, rank, pad_start):
    n_tok = top_idx.shape[1]
    tl = SORTED_ROWS_TL
    tok = pl.BlockSpec((TOP_K, tl), lambda i: (0, i))
    return pl.pallas_call(
        _sorted_rows_body,
        grid=(n_tok // tl,),
        in_specs=[tok, tok, pl.BlockSpec((N_EXPERTS, 1), lambda i: (0, 0))],
        out_specs=tok,
        out_shape=jax.ShapeDtypeStruct((TOP_K, n_tok), jnp.int32),
        compiler_params=pltpu.CompilerParams(
            dimension_semantics=("arbitrary",), vmem_limit_bytes=VMEM_LIMIT),
        name="moe_sorted_rows",
    )(top_idx, rank, pad_start.astype(F32).reshape(N_EXPERTS, 1))


def _dispatch_body(start_ref, cnt_ref, dest_ref, h_ref, xs_ref, pk_ref, zero_ref, sem, zero_sem,
                   *, tt, tb):
    @pl.when(pl.program_id(0) == 0)
    def _():
        zero_ref[...] = jnp.zeros_like(zero_ref)

        def pad_copies(e, fn):
            n_pad = ((cnt_ref[e] + (tb - 1)) & (-tb)) - cnt_ref[e]
            row = start_ref[e] + cnt_ref[e]
            for bit in range(tb.bit_length() - 1):
                size = 1 << bit
                has_bit = ((n_pad >> bit) & 1) == 1

                @pl.when(has_bit)
                def _():
                    fn(pltpu.make_async_copy(zero_ref.at[pl.ds(0, size)],
                                             xs_ref.at[pl.ds(row, size)], zero_sem))
                row = row + jnp.where(has_bit, size, 0)

        def start_all(e, c):
            pad_copies(e, lambda cp: cp.start())
            return c

        def wait_all(e, c):
            pad_copies(e, lambda cp: cp.wait())
            return c

        lax.fori_loop(0, N_EXPERTS, start_all, 0)
        lax.fori_loop(0, N_EXPERTS, wait_all, 0)

    packed = _pack_rows(_rows_load(h_ref, tt))
    for s in range(PK_ROWS):
        pk_ref[:, s, :] = packed[:, s * LANES:(s + 1) * LANES]
    for t in range(tt):
        for k in range(TOP_K):
            pltpu.make_async_copy(pk_ref.at[t], xs_ref.at[dest_ref[k, t]], sem).start(priority=k % 2)
    for k in range(TOP_K):
        pltpu.make_async_copy(pk_ref, xs_ref.at[pl.ds(0, tt)], sem).wait()


def _dispatch(h_tiles, dest_rows, pad_start, counts, n_rows):
    n_tok = h_tiles.shape[0] // TOK_ROWS
    tt = DISPATCH_TT
    grid_spec = pltpu.PrefetchScalarGridSpec(
        num_scalar_prefetch=2,
        grid=(n_tok // tt,),
        in_specs=[pl.BlockSpec((TOP_K, tt), lambda i, s, c: (0, i), memory_space=pltpu.SMEM),
                  pl.BlockSpec((tt * TOK_ROWS, LANES), lambda i, s, c: (i, 0))],
        out_specs=pl.BlockSpec(memory_space=pl.ANY),
        scratch_shapes=[pltpu.VMEM((tt, PK_ROWS, LANES), I32),
                        pltpu.VMEM((EXPERT_TB // 2, PK_ROWS, LANES), I32),
                        pltpu.SemaphoreType.DMA(()), pltpu.SemaphoreType.DMA(())],
    )
    return pl.pallas_call(
        functools.partial(_dispatch_body, tt=tt, tb=EXPERT_TB),
        grid_spec=grid_spec,
        out_shape=jax.ShapeDtypeStruct((n_rows, PK_ROWS, LANES), I32),
        compiler_params=pltpu.CompilerParams(
            dimension_semantics=("arbitrary",), vmem_limit_bytes=VMEM_LIMIT),
        name="moe_dispatch",
    )(pad_start, counts, dest_rows, h_tiles)


def _expert_body(be_ref, first_ref, slot_ref, next_ref, na_ref, xs_ref, w1_ref, w3_ref, w2_ref, ys_ref,
                 w13_buf, w2_buf, w1b, w3b, w2b, sem, *, tb, layer):
    i = pl.program_id(0)

    def weight_copies(expert, slot):
        return (pltpu.make_async_copy(w1_ref.at[layer, expert], w13_buf.at[slot, 0], sem.at[slot]),
                pltpu.make_async_copy(w3_ref.at[layer, expert], w13_buf.at[slot, 1], sem.at[slot]),
                pltpu.make_async_copy(w2_ref.at[layer, expert], w2_buf.at[slot], sem.at[slot]))

    @pl.when(i < na_ref[0])
    def _():
        @pl.when(first_ref[i] == 1)
        def _():
            slot = slot_ref[i]

            @pl.when(i == 0)
            def _():
                for cp in weight_copies(be_ref[0], slot):
                    cp.start()

            for cp in weight_copies(be_ref[i], slot):
                cp.wait()

            @pl.when(next_ref[i] >= 0)
            def _():
                for cp in weight_copies(next_ref[i], 1 - slot):
                    cp.start()

            w1b[...] = w13_buf[slot, 0].astype(BF16)
            w3b[...] = w13_buf[slot, 1].astype(BF16)
            w2b[...] = w2_buf[slot].astype(BF16)

        x = _unpack_rows(_packed_load(xs_ref, tb)).astype(BF16)
        mid = (_silu(_bdot(x, w1b[...])) * _bdot(x, w3b[...])).astype(BF16)
        _packed_store(ys_ref, _pack_rows(_bdot(mid, w2b[...])), tb)


def _experts(xs, block_expert, first, slot, next_expert, n_active, w1, w3, w2, layer):
    tb = EXPERT_TB
    n_blocks = xs.shape[0] // (tb * PK_ROWS)
    row_block = lambda i, be, fi, sl, nx, na: (jnp.minimum(i, na[0] - 1), 0)
    grid_spec = pltpu.PrefetchScalarGridSpec(
        num_scalar_prefetch=5,
        grid=(n_blocks,),
        in_specs=[
            pl.BlockSpec((tb * PK_ROWS, LANES), row_block),
            pl.BlockSpec(memory_space=pl.ANY), pl.BlockSpec(memory_space=pl.ANY),
            pl.BlockSpec(memory_space=pl.ANY),
        ],
        out_specs=pl.BlockSpec((tb * PK_ROWS, LANES), row_block),
        scratch_shapes=[pltpu.VMEM((2, 2, D_MODEL, D_EXPERT), F32),
                        pltpu.VMEM((2, D_EXPERT, D_MODEL), F32),
                        pltpu.VMEM((D_MODEL, D_EXPERT), BF16), pltpu.VMEM((D_MODEL, D_EXPERT), BF16),
                        pltpu.VMEM((D_EXPERT, D_MODEL), BF16), pltpu.SemaphoreType.DMA((2,))],
    )
    return pl.pallas_call(
        functools.partial(_expert_body, tb=tb, layer=layer),
        grid_spec=grid_spec,
        out_shape=jax.ShapeDtypeStruct(xs.shape, I32),
        compiler_params=pltpu.CompilerParams(
            dimension_semantics=("arbitrary",), vmem_limit_bytes=VMEM_LIMIT),
        name="moe_experts",
    )(block_expert, first, slot, next_expert, n_active, xs, w1, w3, w2)


def _combine_body(src_ref, h_ref, gates_ref, ys_ref, ys_flat_ref, ws1_ref, ws3_ref, ws2_ref,
                  g_ref, b_ref, o_ref, gbuf, sem, *, tt, n_tiles, token_tile_out):
    j = pl.program_id(0)

    def step(issue, finish):
        nxt = j % 2
        cur = (j - 1) % 2
        if finish:
            h = _rows_load(h_ref, tt)
            hb = h.astype(BF16)
            mid = (_silu(_bdot(hb, ws1_ref[...])) * _bdot(hb, ws3_ref[...])).astype(BF16)
            routed = _bdot(mid, ws2_ref[...])
            gates = gates_ref[...]
        for k in range(TOP_K):
            if issue:
                for t in range(tt):
                    pltpu.make_async_copy(ys_ref.at[src_ref[k, t]],
                                          gbuf.at[nxt, k, pl.ds(t * PK_ROWS, PK_ROWS), :],
                                          sem.at[nxt]).start(priority=t % 2)
            if finish:
                pltpu.make_async_copy(ys_flat_ref.at[pl.ds(0, tt * PK_ROWS), :], gbuf.at[cur, k],
                                      sem.at[cur]).wait()
                routed = routed + gates[:, k:k + 1] * _unpack_rows(_packed_load(gbuf.at[cur, k], tt))
        if finish:
            out = _layer_norm(DEEPNORM_ALPHA * h + routed, g_ref[...], b_ref[...])
            if token_tile_out:
                _rows_store(o_ref, out, tt)
            else:
                o_ref[...] = out

    @pl.when(j == 0)
    def _():
        step(True, False)

    @pl.when((j >= 1) & (j < n_tiles))
    def _():
        step(True, True)

    @pl.when(j == n_tiles)
    def _():
        step(False, True)


def _combine(h_tiles, src_rows, gates_tk, ys, ws1, ws3, ws2, g, b, token_tile_out):
    n_tok = h_tiles.shape[0] // TOK_ROWS
    tt = COMBINE_TT
    n_tiles = n_tok // tt
    prev_tile = lambda j: (jnp.maximum(j - 1, 0), 0)
    full = lambda shape: pl.BlockSpec(shape, lambda j: (0,) * len(shape))
    if token_tile_out:
        out_spec = pl.BlockSpec((tt * TOK_ROWS, LANES), prev_tile)
        out_shape = jax.ShapeDtypeStruct((n_tok * TOK_ROWS, LANES), F32)
    else:
        out_spec = pl.BlockSpec((tt, D_MODEL), prev_tile)
        out_shape = jax.ShapeDtypeStruct((n_tok, D_MODEL), F32)
    n_rows = ys.shape[0] // PK_ROWS
    return pl.pallas_call(
        functools.partial(_combine_body, tt=tt, n_tiles=n_tiles, token_tile_out=token_tile_out),
        grid=(n_tiles + 1,),
        in_specs=[pl.BlockSpec((TOP_K, tt), lambda j: (0, jnp.minimum(j, n_tiles - 1)),
                               memory_space=pltpu.SMEM),
                  pl.BlockSpec((tt * TOK_ROWS, LANES), prev_tile),
                  pl.BlockSpec((tt, TOP_K), prev_tile),
                  pl.BlockSpec(memory_space=pl.ANY), pl.BlockSpec(memory_space=pl.ANY),
                  full((D_MODEL, D_EXPERT)), full((D_MODEL, D_EXPERT)), full((D_EXPERT, D_MODEL)),
                  full((1, D_MODEL)), full((1, D_MODEL))],
        out_specs=out_spec,
        out_shape=out_shape,
        scratch_shapes=[pltpu.VMEM((2, TOP_K, tt * PK_ROWS, LANES), I32),
                        pltpu.SemaphoreType.DMA((2,))],
        compiler_params=pltpu.CompilerParams(
            dimension_semantics=("arbitrary",), vmem_limit_bytes=VMEM_LIMIT),
        name="moe_combine",
    )(src_rows, h_tiles, gates_tk, ys.reshape(n_rows, PK_ROWS, LANES), ys, ws1.astype(BF16),
      ws3.astype(BF16), ws2.astype(BF16), g.reshape(1, -1), b.reshape(1, -1))


def _moe_layer(h_tiles, w_router, router_bias, w1, w3, w2, layer, ws1, ws3, ws2, g, b,
               token_tile_out):
    n_tok = h_tiles.shape[0] // TOK_ROWS
    tb = EXPERT_TB
    top_idx, gates, rank, counts = _router(h_tiles, w_router, router_bias)

    n_blocks = -(-(n_tok * TOP_K + N_EXPERTS * (tb - 1)) // tb)
    counts = counts.reshape(N_EXPERTS)
    padded = ((counts + tb - 1) // tb) * tb
    pad_end = jnp.cumsum(padded)
    pad_start = (pad_end - padded).astype(jnp.int32)
    block_first_row = jnp.arange(n_blocks, dtype=jnp.int32) * tb
    block_expert = jnp.minimum(
        jnp.sum((pad_end[None, :] <= block_first_row[:, None]).astype(jnp.int32), axis=1),
        N_EXPERTS - 1)
    n_active = (pad_end[-1:] // tb).astype(jnp.int32)
    present = counts > 0
    expert_ids = jnp.arange(N_EXPERTS, dtype=jnp.int32)
    later = lax.cummin(jnp.where(present, expert_ids, N_EXPERTS), reverse=True)
    next_present = jnp.concatenate([later[1:], jnp.full((1,), N_EXPERTS, jnp.int32)])
    next_present = jnp.where(next_present < N_EXPERTS, next_present, -1)
    ordinal = jnp.cumsum(present.astype(jnp.int32)) - 1
    first = jnp.concatenate([jnp.ones((1,), jnp.int32),
                             (block_expert[1:] != block_expert[:-1]).astype(jnp.int32)])
    slot = ordinal[block_expert] % 2
    next_expert = next_present[block_expert]

    rows = _sorted_rows(top_idx, rank, pad_start)
    xs = _dispatch(h_tiles, rows, pad_start, counts, n_blocks * tb)
    ys = _experts(xs.reshape(n_blocks * tb * PK_ROWS, LANES), block_expert, first, slot, next_expert,
                  n_active, w1, w3, w2, layer)
    return _combine(h_tiles, rows, gates.T, ys, ws1, ws3, ws2, g, b, token_tile_out)


def _proj_body(h_ref, w_ref, *o_refs, tt, scale):
    y = _bdot(_rows_load(h_ref, tt).astype(BF16), w_ref[...])
    for j, o_ref in enumerate(o_refs):
        for s in range(PAIR_SLABS):
            lo = j * O_WIDTH + 2 * s * LANES
            o_ref[s] = pltpu.pack_elementwise(
                [y[:, lo:lo + LANES] * scale, y[:, lo + LANES:lo + 2 * LANES] * scale],
                packed_dtype=BF16)


def _project(h_tiles, w, scale):
    n_tok = h_tiles.shape[0] // TOK_ROWS
    tt = PROJ_TT
    n_out = w.shape[1] // O_WIDTH
    out_spec = pl.BlockSpec((PAIR_SLABS, tt, LANES), lambda i: (0, i, 0))
    return pl.pallas_call(
        functools.partial(_proj_body, tt=tt, scale=scale),
        grid=(n_tok // tt,),
        in_specs=[pl.BlockSpec((tt * TOK_ROWS, LANES), lambda i: (i, 0)),
                  pl.BlockSpec(w.shape, lambda i: (0, 0))],
        out_specs=[out_spec] * n_out,
        out_shape=[jax.ShapeDtypeStruct((PAIR_SLABS, n_tok, LANES), I32)] * n_out,
        compiler_params=pltpu.CompilerParams(
            dimension_semantics=("arbitrary",), vmem_limit_bytes=VMEM_LIMIT),
        name="projection",
    )(h_tiles, w.astype(BF16))


def _attn_body(q_ref, kp_ref, kc_ref, vp_ref, vc_ref, o_ref, lse_ref, *, dilation, n_steps):
    t = ATT_BLOCK
    not_first = pl.program_id(1) > 0
    qi = lax.broadcasted_iota(jnp.int32, (t, 2 * t), 0)
    kj = lax.broadcasted_iota(jnp.int32, (t, 2 * t), 1)
    dist = t + qi - kj
    valid = (dist >= 0) & (dist <= n_steps) & (not_first | (kj >= t))
    token_dist = (dilation * dist).astype(F32)
    low_half = lax.broadcasted_iota(jnp.int32, (t, LANES), 1) < HEAD_DIM
    contract_last = (((1,), (1,)), ((), ()))

    def unpack(words, which):
        return pltpu.unpack_elementwise(words, index=which, packed_dtype=BF16,
                                        unpacked_dtype=F32).astype(BF16)

    def residue(r, carry):
        rows = pl.ds(r, t) if dilation == 1 else pl.ds(r, t, stride=dilation)
        for slab in range(PAIR_SLABS):
            q_words = q_ref.at[slab][rows, :]
            k_words = jnp.concatenate([kp_ref.at[slab][rows, :], kc_ref.at[slab][rows, :]], axis=0)
            v_words = jnp.concatenate([vp_ref.at[slab][rows, :], vc_ref.at[slab][rows, :]], axis=0)
            for which in range(2):
                pair = 2 * slab + which
                q2, k2, v2 = unpack(q_words, which), unpack(k_words, which), unpack(v_words, which)
                outs, lses = [], []
                for half in range(2):
                    head = 2 * pair + half
                    slope = 2.0 ** (-8.0 * (head + 1) / ATT_HEADS)
                    keep = low_half if half == 0 else ~low_half
                    qh = jnp.where(keep, q2, jnp.zeros_like(q2))
                    s = lax.dot_general(qh, k2, contract_last, preferred_element_type=F32)
                    s = jnp.where(valid, s - slope * token_dist, MASK_VALUE)
                    m = jnp.max(s, axis=-1, keepdims=True)
                    p = jnp.exp(s - m)
                    l = jnp.sum(p, axis=-1, keepdims=True)
                    outs.append(_bdot(p.astype(BF16), v2) / l)
                    lses.append(m + jnp.log(l))
                o_ref.at[pair][rows, :] = jnp.where(low_half, outs[0], outs[1])
                lse_ref.at[pair][rows, :] = jnp.where(low_half, lses[0], lses[1])
        return carry

    if dilation == 1:
        residue(0, 0)
    else:
        lax.fori_loop(0, dilation, residue, 0)


def _attn_branch(q, k, v, bsz, seq, window, dilation):
    span = ATT_BLOCK * dilation
    n_spans = seq // span
    pairs = ATT_HEADS // 2
    cur = lambda slabs: pl.BlockSpec((slabs, span, LANES), lambda bi, n: (0, bi * n_spans + n, 0))
    prev = pl.BlockSpec((PAIR_SLABS, span, LANES),
                        lambda bi, n: (0, bi * n_spans + jnp.maximum(n - 1, 0), 0))
    return pl.pallas_call(
        functools.partial(_attn_body, dilation=dilation, n_steps=window // dilation),
        grid=(bsz, n_spans),
        in_specs=[cur(PAIR_SLABS), prev, cur(PAIR_SLABS), prev, cur(PAIR_SLABS)],
        out_specs=[cur(pairs), cur(pairs)],
        out_shape=[jax.ShapeDtypeStruct((pairs, bsz * seq, LANES), F32)] * 2,
        compiler_params=pltpu.CompilerParams(
            dimension_semantics=("arbitrary",) * 2, vmem_limit_bytes=VMEM_LIMIT),
        name=f"dilated_attn_d{dilation}",
    )(q, k, k, v, v)


def _merge_body(*refs, tt):
    o_refs = refs[:N_GROUPS]
    lse_refs = refs[N_GROUPS:2 * N_GROUPS]
    h_ref, wo_ref, g_ref, b_ref, out_ref = refs[2 * N_GROUPS:]
    wide = lambda r: jnp.concatenate([r[c] for c in range(ATT_HEADS // 2)], axis=1)
    lses = [wide(r) for r in lse_refs]
    m = functools.reduce(jnp.maximum, lses)
    ws = [jnp.exp(l - m) for l in lses]
    den = functools.reduce(lambda a, c: a + c, ws)
    o = functools.reduce(lambda a, c: a + c, [w * wide(r) for w, r in zip(ws, o_refs)]) / den
    mix = _bdot(o.astype(BF16), wo_ref[...])
    z = DEEPNORM_ALPHA * _rows_load(h_ref, tt) + mix
    _rows_store(out_ref, _layer_norm(z, g_ref[...], b_ref[...]), tt)


def _merge(outs, lses, h_tiles, w_o, g, b):
    n_tok = h_tiles.shape[0] // TOK_ROWS
    tt = MERGE_TT
    tok = pl.BlockSpec((ATT_HEADS // 2, tt, LANES), lambda i: (0, i, 0))
    tiles = pl.BlockSpec((tt * TOK_ROWS, LANES), lambda i: (i, 0))
    full = lambda shape: pl.BlockSpec(shape, lambda i: (0,) * len(shape))
    return pl.pallas_call(
        functools.partial(_merge_body, tt=tt),
        grid=(n_tok // tt,),
        in_specs=[tok] * (2 * N_GROUPS) + [tiles, full((O_WIDTH, D_MODEL)), full((1, D_MODEL)),
                                           full((1, D_MODEL))],
        out_specs=tiles,
        out_shape=jax.ShapeDtypeStruct(h_tiles.shape, F32),
        compiler_params=pltpu.CompilerParams(
            dimension_semantics=("arbitrary",), vmem_limit_bytes=VMEM_LIMIT),
        name="attn_merge",
    )(*outs, *lses, h_tiles, w_o.astype(BF16), g.reshape(1, -1), b.reshape(1, -1))


def kernel(x, a_w_in, a_conv_w, a_conv_b, a_w_gate_a, a_b_gate_a, a_w_gate_x, a_b_gate_x, a_lambda, a_w_out, w_kv_shared, b_w_q, b_w_o, moe_w_router, moe_router_bias, moe_w1, moe_w3, moe_w2, moe_ws1, moe_ws3, moe_ws2, ln_g, ln_b):
    bsz, seq, d = x.shape
    assert d == D_MODEL and seq % (DILATION_PATTERNS[-1][1] * ATT_BLOCK) == 0

    def moe(h_tiles, layer, token_tile_out):
        return _moe_layer(h_tiles, moe_w_router[layer], moe_router_bias[layer], moe_w1, moe_w3,
                          moe_w2, layer, moe_ws1[layer], moe_ws3[layer], moe_ws2[layer],
                          ln_g[layer, 1], ln_b[layer, 1], token_tile_out)

    h = _rglru_layer(x, a_w_in[0], a_conv_w[0], a_conv_b[0], a_w_gate_a[0], a_b_gate_a[0],
                     a_w_gate_x[0], a_b_gate_x[0], a_lambda[0], a_w_out[0], ln_g[0, 0], ln_b[0, 0])
    h = moe(h, 0, True)
    kv = _project(h, w_kv_shared, 1.0)
    ks, vs = kv[:N_GROUPS], kv[N_GROUPS:]

    qs = _project(h, b_w_q[0], HEAD_DIM ** -0.5)
    outs, lses = [], []
    for grp, (window, dilation) in enumerate(DILATION_PATTERNS):
        o, lse = _attn_branch(qs[grp], ks[grp], vs[grp], bsz, seq, window, dilation)
        outs.append(o)
        lses.append(lse)
    h = _merge(outs, lses, h, b_w_o[0], ln_g[1, 0], ln_b[1, 0])
    h = moe(h, 1, False)
    return h.reshape(bsz, seq, d)
```

```python
import functools
import math

import jax
import jax.numpy as jnp
from jax import lax
from jax.experimental import pallas as pl
from jax.experimental.pallas import tpu as pltpu

F32 = jnp.float32
BF16 = jnp.bfloat16
I32 = jnp.int32

D_MODEL = 1024
LRU_BLOCKS = 4
CONV_WIDTH = 4
LRU_C = 8.0
ATT_HEADS = 8
HEAD_DIM = 64
DILATION_PATTERNS = ((128, 1), (512, 4), (2048, 16))
N_GROUPS = len(DILATION_PATTERNS)
ATT_BLOCK = 128
O_WIDTH = ATT_HEADS * HEAD_DIM
N_EXPERTS = 256
TOP_K = 8
N_EXPERT_GROUPS = 8
TOPK_GROUPS = 4
D_EXPERT = 256
ROUTED_SCALE = 2.5
DEPTH = 2
DEEPNORM_ALPHA = (2 * DEPTH) ** 0.25
LN_EPS = 1e-5
MASK_VALUE = -1e30

SUBLANES = 8
LANES = 128
TOK_ROWS = D_MODEL // LANES
PK_ROWS = D_MODEL // 2 // LANES
PAIR_SLABS = ATT_HEADS * HEAD_DIM // LANES // 2
VMEM_LIMIT = 56 * 1024 * 1024

RGLRU_TS = 256
ROUTER_TT = 256
SORTED_ROWS_TL = 2048
DISPATCH_TT = 128
EXPERT_TB = 512
COMBINE_TT = 128
PROJ_TT = 512
MERGE_TT = 256


def _rows_load(ref, n_tok):
    return jnp.concatenate(
        [ref[pl.ds(s, n_tok, stride=TOK_ROWS), :] for s in range(TOK_ROWS)], axis=1)


def _rows_store(ref, val, n_tok):
    for s in range(TOK_ROWS):
        ref[pl.ds(s, n_tok, stride=TOK_ROWS), :] = val[:, s * LANES:(s + 1) * LANES]


def _layer_norm(z, g, b):
    mu = jnp.mean(z, axis=-1, keepdims=True)
    zc = z - mu
    var = jnp.mean(zc * zc, axis=-1, keepdims=True)
    return zc * lax.rsqrt(var + LN_EPS) * g + b


def _silu(x):
    return x * jax.nn.sigmoid(x)


def _gelu_tanh(x):
    c = math.sqrt(2.0 / math.pi)
    return 0.5 * x * (1.0 + jnp.tanh(c * (x + 0.044715 * (x * x * x))))


def _bdot(a, b):
    return jnp.dot(a, b, preferred_element_type=F32)


def _rglru_body(x_ref, win_ref, cw_ref, cb_ref, wga_ref, bga_ref, wgx_ref, bgx_ref, lam_ref,
                wout_ref, g_ref, b_ref, o_ref, hc_ref, tail_ref, *, ts):
    width = D_MODEL
    bw = width // LRU_BLOCKS

    @pl.when(pl.program_id(1) == 0)
    def _():
        hc_ref[...] = jnp.zeros_like(hc_ref)
        tail_ref[...] = jnp.zeros_like(tail_ref)

    x = x_ref[...]
    xz = _bdot(x.astype(BF16), win_ref[...])
    xr = xz[:, :width]
    gate = xz[:, width:]

    tail = tail_ref[...]
    row8 = lax.broadcasted_iota(jnp.int32, (SUBLANES, width), 0)
    cw = cw_ref[...]
    xc = xr * cw[CONV_WIDTH - 1:CONV_WIDTH, :] + cb_ref[...]
    for j in range(1, CONV_WIDTH):
        rx = pltpu.roll(xr, j, 0)
        rp = pltpu.roll(tail, j, 0)
        top = jnp.where(row8 < j, rp, rx[:SUBLANES])
        shifted = jnp.concatenate([top, rx[SUBLANES:]], axis=0)
        xc = xc + shifted * cw[CONV_WIDTH - 1 - j:CONV_WIDTH - j, :]
    tail_ref[...] = xr[ts - SUBLANES:]

    xcb = xc.astype(BF16)

    def block_diag(w_ref):
        return jnp.concatenate(
            [_bdot(xcb[:, n * bw:(n + 1) * bw], w_ref[n]) for n in range(LRU_BLOCKS)], axis=1)

    r = jax.nn.sigmoid(block_diag(wga_ref) + bga_ref[...])
    i = jax.nn.sigmoid(block_diag(wgx_ref) + bgx_ref[...])
    lam = lam_ref[...]
    softplus_neg_lam = jnp.maximum(-lam, 0.0) + jnp.log1p(jnp.exp(-jnp.abs(lam)))
    log_a = (-LRU_C * r) * softplus_neg_lam
    a = jnp.exp(log_a)
    mult = jnp.sqrt(-jnp.tanh(log_a) * (a * a + 1.0))
    u = mult * (i * xc)

    groups = ts // SUBLANES
    a3 = a.reshape(groups, SUBLANES, width)
    u3 = u.reshape(groups, SUBLANES, width)
    sub = lax.broadcasted_iota(jnp.int32, (groups, SUBLANES, width), 1)
    sh = 1
    while sh < SUBLANES:
        a_prev = pltpu.roll(a3, sh, 1)
        u_prev = pltpu.roll(u3, sh, 1)
        live = sub >= sh
        u3 = jnp.where(live, a3 * u_prev, 0.0) + u3
        a3 = jnp.where(live, a3 * a_prev, a3)
        sh *= 2
    carry = hc_ref[...]
    h_groups = []
    for grp in range(groups):
        h_grp = a3[grp] * carry + u3[grp]
        h_groups.append(h_grp)
        carry = h_grp[SUBLANES - 1:SUBLANES]
    h = jnp.concatenate(h_groups, axis=0)
    hc_ref[...] = carry

    y = (h * _gelu_tanh(gate)).astype(BF16)
    mix = _bdot(y, wout_ref[...])
    z = DEEPNORM_ALPHA * x + mix
    _rows_store(o_ref, _layer_norm(z, g_ref[...], b_ref[...]), ts)


def _rglru_layer(x, w_in, conv_w, conv_b, wga, bga, wgx, bgx, lam, w_out, g, b):
    bsz, seq, d = x.shape
    ts = RGLRU_TS
    ns = seq // ts
    row = lambda v: v.reshape(1, -1)
    full = lambda shape: pl.BlockSpec(shape, lambda bi, si: (0,) * len(shape))
    return pl.pallas_call(
        functools.partial(_rglru_body, ts=ts),
        grid=(bsz, ns),
        in_specs=[
            pl.BlockSpec((None, ts, d), lambda bi, si: (bi, si, 0)),
            full((d, 2 * d)), full((CONV_WIDTH, d)), full((1, d)),
            full((LRU_BLOCKS, d // LRU_BLOCKS, d // LRU_BLOCKS)), full((1, d)),
            full((LRU_BLOCKS, d // LRU_BLOCKS, d // LRU_BLOCKS)), full((1, d)),
            full((1, d)), full((d, d)), full((1, d)), full((1, d)),
        ],
        out_specs=pl.BlockSpec((ts * TOK_ROWS, LANES), lambda bi, si: (bi * ns + si, 0)),
        out_shape=jax.ShapeDtypeStruct((bsz * seq * TOK_ROWS, LANES), F32),
        scratch_shapes=[pltpu.VMEM((1, d), F32), pltpu.VMEM((SUBLANES, d), F32)],
        compiler_params=pltpu.CompilerParams(
            dimension_semantics=("arbitrary", "arbitrary"), vmem_limit_bytes=VMEM_LIMIT),
        name="rglru_layer",
    )(x, w_in.astype(BF16), conv_w, row(conv_b), wga.astype(BF16), row(bga), wgx.astype(BF16),
      row(bgx), row(lam), w_out.astype(BF16), row(g), row(b))


def _router_body(h_ref, wrt_ref, bias_ref, idx_ref, gate_ref, rank_ref, cnt_ref, carry_ref, *, tt):
    n_e = N_EXPERTS
    per_group = n_e // N_EXPERT_GROUPS

    @pl.when(pl.program_id(0) == 0)
    def _():
        carry_ref[...] = jnp.zeros_like(carry_ref)

    h = _rows_load(h_ref, tt)
    logits = lax.dot_general(wrt_ref[...], h, (((1,), (1,)), ((), ())),
                             precision=lax.Precision.HIGHEST, preferred_element_type=F32)
    scores = jax.nn.sigmoid(logits)
    biased = scores + bias_ref[...]

    j_iota = lax.broadcasted_iota(jnp.int32, (per_group, tt), 0)
    group_score = []
    for g in range(N_EXPERT_GROUPS):
        bg = biased[g * per_group:(g + 1) * per_group]
        m1 = jnp.max(bg, axis=0, keepdims=True)
        i1 = jnp.min(jnp.where(bg == m1, j_iota, per_group), axis=0, keepdims=True)
        m2 = jnp.max(jnp.where(j_iota == i1, -jnp.inf, bg), axis=0, keepdims=True)
        group_score.append(m1 + m2)

    masked = []
    for g in range(N_EXPERT_GROUPS):
        beaten_by = jnp.zeros((1, tt), jnp.int32)
        for o in range(N_EXPERT_GROUPS):
            if o == g:
                continue
            wins = group_score[o] > group_score[g]
            if o < g:
                wins = wins | (group_score[o] == group_score[g])
            beaten_by = beaten_by + wins.astype(jnp.int32)
        keep = beaten_by < TOPK_GROUPS
        masked.append(jnp.where(keep, biased[g * per_group:(g + 1) * per_group], MASK_VALUE))
    cur = jnp.concatenate(masked, axis=0)

    e_iota = lax.broadcasted_iota(jnp.int32, (n_e, tt), 0)
    idx_rows, score_rows, sels = [], [], []
    for _ in range(TOP_K):
        m = jnp.max(cur, axis=0, keepdims=True)
        ik = jnp.min(jnp.where(cur == m, e_iota, n_e), axis=0, keepdims=True)
        sel = e_iota == ik
        score_rows.append(jnp.sum(jnp.where(sel, scores, 0.0), axis=0, keepdims=True))
        cur = jnp.where(sel, -jnp.inf, cur)
        idx_rows.append(ik)
        sels.append(sel)
    top_s = jnp.concatenate(score_rows, axis=0)
    gate_ref[...] = top_s / jnp.sum(top_s, axis=0, keepdims=True) * ROUTED_SCALE
    idx_ref[...] = jnp.concatenate(idx_rows, axis=0)
    multi_hot = jnp.where(cur == -jnp.inf, 1.0, 0.0)

    t_row = lax.broadcasted_iota(jnp.int32, (tt, tt), 0)
    t_col = lax.broadcasted_iota(jnp.int32, (tt, tt), 1)
    strict_upper = jnp.where(t_row < t_col, 1.0, 0.0).astype(BF16)
    before = _bdot(multi_hot.astype(BF16), strict_upper) + carry_ref[...]
    rank_rows = [jnp.sum(jnp.where(sel, before, 0.0), axis=0, keepdims=True) for sel in sels]
    rank_ref[...] = jnp.concatenate(rank_rows, axis=0).astype(jnp.int32)
    carry = carry_ref[...] + jnp.sum(multi_hot, axis=1, keepdims=True)
    carry_ref[...] = carry
    cnt_ref[...] = carry.astype(jnp.int32)


def _router(h_tiles, w_router, router_bias):
    n_tok = h_tiles.shape[0] // TOK_ROWS
    tt = ROUTER_TT
    kt = lambda dt: jax.ShapeDtypeStruct((TOP_K, n_tok), dt)
    tok_spec = pl.BlockSpec((TOP_K, tt), lambda i: (0, i))
    return pl.pallas_call(
        functools.partial(_router_body, tt=tt),
        grid=(n_tok // tt,),
        in_specs=[
            pl.BlockSpec((tt * TOK_ROWS, LANES), lambda i: (i, 0)),
            pl.BlockSpec((N_EXPERTS, D_MODEL), lambda i: (0, 0)),
            pl.BlockSpec((N_EXPERTS, 1), lambda i: (0, 0)),
        ],
        out_specs=[tok_spec, tok_spec, tok_spec, pl.BlockSpec((N_EXPERTS, 1), lambda i: (0, 0))],
        out_shape=[kt(jnp.int32), kt(F32), kt(jnp.int32),
                   jax.ShapeDtypeStruct((N_EXPERTS, 1), jnp.int32)],
        scratch_shapes=[pltpu.VMEM((N_EXPERTS, 1), F32)],
        compiler_params=pltpu.CompilerParams(
            dimension_semantics=("arbitrary",), vmem_limit_bytes=VMEM_LIMIT),
        name="moe_router",
    )(h_tiles, w_router.T, router_bias.reshape(N_EXPERTS, 1))


def _pack_rows(x):
    half = D_MODEL // 2
    return pltpu.pack_elementwise([x[:, :half], x[:, half:]], packed_dtype=BF16)


def _unpack_rows(p):
    return jnp.concatenate(
        [pltpu.unpack_elementwise(p, index=i, packed_dtype=BF16, unpacked_dtype=F32) for i in range(2)],
        axis=1)


def _packed_load(ref, n_tok):
    return jnp.concatenate(
        [ref[pl.ds(s, n_tok, stride=PK_ROWS), :] for s in range(PK_ROWS)], axis=1)


def _packed_store(ref, val, n_tok):
    for s in range(PK_ROWS):
        ref[pl.ds(s, n_tok, stride=PK_ROWS), :] = val[:, s * LANES:(s + 1) * LANES]


def _sorted_rows_body(idx_ref, rank_ref, start_ref, o_ref):
    tl = idx_ref.shape[1]
    e_iota = lax.broadcasted_iota(jnp.int32, (N_EXPERTS, tl), 0)
    start = start_ref[...]
    rows = []
    for k in range(TOP_K):
        hit = e_iota == idx_ref[k:k + 1, :]
        base = jnp.sum(jnp.where(hit, start, 0.0), axis=0, keepdims=True)
        rows.append(base.astype(jnp.int32) + rank_ref[k:k + 1, :])
    o_ref[...] = jnp.concatenate(rows, axis=0)


def _sorted_rows(top_idx, rank, pad_start):
    n_tok = top_idx.shape[1]
    tl = SORTED_ROWS_TL
    tok = pl.BlockSpec((TOP_K, tl), lambda i: (0, i))
    return pl.pallas_call(
        _sorted_rows_body,
        grid=(n_tok // tl,),
        in_specs=[tok, tok, pl.BlockSpec((N_EXPERTS, 1), lambda i: (0, 0))],
        out_specs=tok,
        out_shape=jax.ShapeDtypeStruct((TOP_K, n_tok), jnp.int32),
        compiler_params=pltpu.CompilerParams(
            dimension_semantics=("arbitrary",), vmem_limit_bytes=VMEM_LIMIT),
        name="moe_sorted_rows",
    )(top_idx, rank, pad_start.astype(F32).reshape(N_EXPERTS, 1))


def _dispatch_body(start_ref, cnt_ref, dest_ref, h_ref, xs_ref, pk_ref, zero_ref, sem, zero_sem,
                   *, tt, tb):
    @pl.when(pl.program_id(0) == 0)
    def _():
        zero_ref[...] = jnp.zeros_like(zero_ref)

        def pad_copies(e, fn):
            n_pad = ((cnt_ref[e] + (tb - 1)) & (-tb)) - cnt_ref[e]
            row = start_ref[e] + cnt_ref[e]
            for bit in range(tb.bit_length() - 1):
                size = 1 << bit
                has_bit = ((n_pad >> bit) & 1) == 1

                @pl.when(has_bit)
                def _():
                    fn(pltpu.make_async_copy(zero_ref.at[pl.ds(0, size)],
                                             xs_ref.at[pl.ds(row, size)], zero_sem))
                row = row + jnp.where(has_bit, size, 0)

        def start_all(e, c):
            pad_copies(e, lambda cp: cp.start())
            return c

        def wait_all(e, c):
            pad_copies(e, lambda cp: cp.wait())
            return c

        lax.fori_loop(0, N_EXPERTS, start_all, 0)
        lax.fori_loop(0, N_EXPERTS, wait_all, 0)

    packed = _pack_rows(_rows_load(h_ref, tt))
    for s in range(PK_ROWS):
        pk_ref[:, s, :] = packed[:, s * LANES:(s + 1) * LANES]
    for t in range(tt):
        for k in range(TOP_K):
            pltpu.make_async_copy(pk_ref.at[t], xs_ref.at[dest_ref[k, t]], sem).start(priority=k % 2)
    for k in range(TOP_K):
        pltpu.make_async_copy(pk_ref, xs_ref.at[pl.ds(0, tt)], sem).wait()


def _dispatch(h_tiles, dest_rows, pad_start, counts, n_rows):
    n_tok = h_tiles.shape[0] // TOK_ROWS
    tt = DISPATCH_TT
    grid_spec = pltpu.PrefetchScalarGridSpec(
        num_scalar_prefetch=2,
        grid=(n_tok // tt,),
        in_specs=[pl.BlockSpec((TOP_K, tt), lambda i, s, c: (0, i), memory_space=pltpu.SMEM),
                  pl.BlockSpec((tt * TOK_ROWS, LANES), lambda i, s, c: (i, 0))],
        out_specs=pl.BlockSpec(memory_space=pl.ANY),
        scratch_shapes=[pltpu.VMEM((tt, PK_ROWS, LANES), I32),
                        pltpu.VMEM((EXPERT_TB // 2, PK_ROWS, LANES), I32),
                        pltpu.SemaphoreType.DMA(()), pltpu.SemaphoreType.DMA(())],
    )
    return pl.pallas_call(
        functools.partial(_dispatch_body, tt=tt, tb=EXPERT_TB),
        grid_spec=grid_spec,
        out_shape=jax.ShapeDtypeStruct((n_rows, PK_ROWS, LANES), I32),
        compiler_params=pltpu.CompilerParams(
            dimension_semantics=("arbitrary",), vmem_limit_bytes=VMEM_LIMIT),
        name="moe_dispatch",
    )(pad_start, counts, dest_rows, h_tiles)


def _expert_body(be_ref, first_ref, slot_ref, next_ref, na_ref, xs_ref, w1_ref, w3_ref, w2_ref, ys_ref,
                 w13_buf, w2_buf, w1b, w3b, w2b, sem, *, tb, layer):
    i = pl.program_id(0)

    def weight_copies(expert, slot):
        return (pltpu.make_async_copy(w1_ref.at[layer, expert], w13_buf.at[slot, 0], sem.at[slot]),
                pltpu.make_async_copy(w3_ref.at[layer, expert], w13_buf.at[slot, 1], sem.at[slot]),
                pltpu.make_async_copy(w2_ref.at[layer, expert], w2_buf.at[slot], sem.at[slot]))

    @pl.when(i < na_ref[0])
    def _():
        @pl.when(first_ref[i] == 1)
        def _():
            slot = slot_ref[i]

            @pl.when(i == 0)
            def _():
                for cp in weight_copies(be_ref[0], slot):
                    cp.start()

            for cp in weight_copies(be_ref[i], slot):
                cp.wait()

            @pl.when(next_ref[i] >= 0)
            def _():
                for cp in weight_copies(next_ref[i], 1 - slot):
                    cp.start()

            w1b[...] = w13_buf[slot, 0].astype(BF16)
            w3b[...] = w13_buf[slot, 1].astype(BF16)
            w2b[...] = w2_buf[slot].astype(BF16)

        x = _unpack_rows(_packed_load(xs_ref, tb)).astype(BF16)
        mid = (_silu(_bdot(x, w1b[...])) * _bdot(x, w3b[...])).astype(BF16)
        _packed_store(ys_ref, _pack_rows(_bdot(mid, w2b[...])), tb)


def _experts(xs, block_expert, first, slot, next_expert, n_active, w1, w3, w2, layer):
    tb = EXPERT_TB
    n_blocks = xs.shape[0] // (tb * PK_ROWS)
    row_block = lambda i, be, fi, sl, nx, na: (jnp.minimum(i, na[0] - 1), 0)
    grid_spec = pltpu.PrefetchScalarGridSpec(
        num_scalar_prefetch=5,
        grid=(n_blocks,),
        in_specs=[
            pl.BlockSpec((tb * PK_ROWS, LANES), row_block),
            pl.BlockSpec(memory_space=pl.ANY), pl.BlockSpec(memory_space=pl.ANY),
            pl.BlockSpec(memory_space=pl.ANY),
        ],
        out_specs=pl.BlockSpec((tb * PK_ROWS, LANES), row_block),
        scratch_shapes=[pltpu.VMEM((2, 2, D_MODEL, D_EXPERT), F32),
                        pltpu.VMEM((2, D_EXPERT, D_MODEL), F32),
                        pltpu.VMEM((D_MODEL, D_EXPERT), BF16), pltpu.VMEM((D_MODEL, D_EXPERT), BF16),
                        pltpu.VMEM((D_EXPERT, D_MODEL), BF16), pltpu.SemaphoreType.DMA((2,))],
    )
    return pl.pallas_call(
        functools.partial(_expert_body, tb=tb, layer=layer),
        grid_spec=grid_spec,
        out_shape=jax.ShapeDtypeStruct(xs.shape, I32),
        compiler_params=pltpu.CompilerParams(
            dimension_semantics=("arbitrary",), vmem_limit_bytes=VMEM_LIMIT),
        name="moe_experts",
    )(block_expert, first, slot, next_expert, n_active, xs, w1, w3, w2)


def _combine_body(src_ref, h_ref, gates_ref, ys_ref, ys_flat_ref, ws1_ref, ws3_ref, ws2_ref,
                  g_ref, b_ref, o_ref, gbuf, sem, *, tt, n_tiles, token_tile_out):
    j = pl.program_id(0)

    def step(issue, finish):
        nxt = j % 2
        cur = (j - 1) % 2
        if finish:
            h = _rows_load(h_ref, tt)
            hb = h.astype(BF16)
            mid = (_silu(_bdot(hb, ws1_ref[...])) * _bdot(hb, ws3_ref[...])).astype(BF16)
            routed = _bdot(mid, ws2_ref[...])
            gates = gates_ref[...]
        for k in range(TOP_K):
            if issue:
                for t in range(tt):
                    pltpu.make_async_copy(ys_ref.at[src_ref[k, t]],
                                          gbuf.at[nxt, k, pl.ds(t * PK_ROWS, PK_ROWS), :],
                                          sem.at[nxt]).start(priority=t % 2)
            if finish:
                pltpu.make_async_copy(ys_flat_ref.at[pl.ds(0, tt * PK_ROWS), :], gbuf.at[cur, k],
                                      sem.at[cur]).wait()
                routed = routed + gates[:, k:k + 1] * _unpack_rows(_packed_load(gbuf.at[cur, k], tt))
        if finish:
            out = _layer_norm(DEEPNORM_ALPHA * h + routed, g_ref[...], b_ref[...])
            if token_tile_out:
                _rows_store(o_ref, out, tt)
            else:
                o_ref[...] = out

    @pl.when(j == 0)
    def _():
        step(True, False)

    @pl.when((j >= 1) & (j < n_tiles))
    def _():
        step(True, True)

    @pl.when(j == n_tiles)
    def _():
        step(False, True)


def _combine(h_tiles, src_rows, gates_tk, ys, ws1, ws3, ws2, g, b, token_tile_out):
    n_tok = h_tiles.shape[0] // TOK_ROWS
    tt = COMBINE_TT
    n_tiles = n_tok // tt
    prev_tile = lambda j: (jnp.maximum(j - 1, 0), 0)
    full = lambda shape: pl.BlockSpec(shape, lambda j: (0,) * len(shape))
    if token_tile_out:
        out_spec = pl.BlockSpec((tt * TOK_ROWS, LANES), prev_tile)
        out_shape = jax.ShapeDtypeStruct((n_tok * TOK_ROWS, LANES), F32)
    else:
        out_spec = pl.BlockSpec((tt, D_MODEL), prev_tile)
        out_shape = jax.ShapeDtypeStruct((n_tok, D_MODEL), F32)
    n_rows = ys.shape[0] // PK_ROWS
    return pl.pallas_call(
        functools.partial(_combine_body, tt=tt, n_tiles=n_tiles, token_tile_out=token_tile_out),
        grid=(n_tiles + 1,),
        in_specs=[pl.BlockSpec((TOP_K, tt), lambda j: (0, jnp.minimum(j, n_tiles - 1)),
                               memory_space=pltpu.SMEM),
                  pl.BlockSpec((tt * TOK_ROWS, LANES), prev_tile),
                  pl.BlockSpec((tt, TOP_K), prev_tile),
                  pl.BlockSpec(memory_space=pl.ANY), pl.BlockSpec(memory_space=pl.ANY),
                  full((D_MODEL, D_EXPERT)), full((D_MODEL, D_EXPERT)), full((D_EXPERT, D_MODEL)),
                  full((1, D_MODEL)), full((1, D_MODEL))],
        out_specs=out_spec,
        out_shape=out_shape,
        scratch_shapes=[pltpu.VMEM((2, TOP_K, tt * PK_ROWS, LANES), I32),
                        pltpu.SemaphoreType.DMA((2,))],
        compiler_params=pltpu.CompilerParams(
            dimension_semantics=("arbitrary",), vmem_limit_bytes=VMEM_LIMIT),
        name="moe_combine",
    )(src_rows, h_tiles, gates_tk, ys.reshape(n_rows, PK_ROWS, LANES), ys, ws1.astype(BF16),
      ws3.astype(BF16), ws2.astype(BF16), g.reshape(1, -1), b.reshape(1, -1))


def _moe_layer(h_tiles, w_router, router_bias, w1, w3, w2, layer, ws1, ws3, ws2, g, b,
               token_tile_out):
    n_tok = h_tiles.shape[0] // TOK_ROWS
    tb = EXPERT_TB
    top_idx, gates, rank, counts = _router(h_tiles, w_router, router_bias)

    n_blocks = -(-(n_tok * TOP_K + N_EXPERTS * (tb - 1)) // tb)
    counts = counts.reshape(N_EXPERTS)
    padded = ((counts + tb - 1) // tb) * tb
    pad_end = jnp.cumsum(padded)
    pad_start = (pad_end - padded).astype(jnp.int32)
    block_first_row = jnp.arange(n_blocks, dtype=jnp.int32) * tb
    block_expert = jnp.minimum(
        jnp.sum((pad_end[None, :] <= block_first_row[:, None]).astype(jnp.int32), axis=1),
        N_EXPERTS - 1)
    n_active = (pad_end[-1:] // tb).astype(jnp.int32)
    present = counts > 0
    expert_ids = jnp.arange(N_EXPERTS, dtype=jnp.int32)
    later = lax.cummin(jnp.where(present, expert_ids, N_EXPERTS), reverse=True)
    next_present = jnp.concatenate([later[1:], jnp.full((1,), N_EXPERTS, jnp.int32)])
    next_present = jnp.where(next_present < N_EXPERTS, next_present, -1)
    ordinal = jnp.cumsum(present.astype(jnp.int32)) - 1
    first = jnp.concatenate([jnp.ones((1,), jnp.int32),
                             (block_expert[1:] != block_expert[:-1]).astype(jnp.int32)])
    slot = ordinal[block_expert] % 2
    next_expert = next_present[block_expert]

    rows = _sorted_rows(top_idx, rank, pad_start)
    xs = _dispatch(h_tiles, rows, pad_start, counts, n_blocks * tb)
    ys = _experts(xs.reshape(n_blocks * tb * PK_ROWS, LANES), block_expert, first, slot, next_expert,
                  n_active, w1, w3, w2, layer)
    return _combine(h_tiles, rows, gates.T, ys, ws1, ws3, ws2, g, b, token_tile_out)


def _proj_body(h_ref, w_ref, *o_refs, tt, scale):
    y = _bdot(_rows_load(h_ref, tt).astype(BF16), w_ref[...])
    for j, o_ref in enumerate(o_refs):
        for s in range(PAIR_SLABS):
            lo = j * O_WIDTH + 2 * s * LANES
            o_ref[s] = pltpu.pack_elementwise(
                [y[:, lo:lo + LANES] * scale, y[:, lo + LANES:lo + 2 * LANES] * scale],
                packed_dtype=BF16)


def _project(h_tiles, w, scale):
    n_tok = h_tiles.shape[0] // TOK_ROWS
    tt = PROJ_TT
    n_out = w.shape[1] // O_WIDTH
    out_spec = pl.BlockSpec((PAIR_SLABS, tt, LANES), lambda i: (0, i, 0))
    return pl.pallas_call(
        functools.partial(_proj_body, tt=tt, scale=scale),
        grid=(n_tok // tt,),
        in_specs=[pl.BlockSpec((tt * TOK_ROWS, LANES), lambda i: (i, 0)),
                  pl.BlockSpec(w.shape, lambda i: (0, 0))],
        out_specs=[out_spec] * n_out,
        out_shape=[jax.ShapeDtypeStruct((PAIR_SLABS, n_tok, LANES), I32)] * n_out,
        compiler_params=pltpu.CompilerParams(
            dimension_semantics=("arbitrary",), vmem_limit_bytes=VMEM_LIMIT),
        name="projection",
    )(h_tiles, w.astype(BF16))


def _attn_body(q_ref, kp_ref, kc_ref, vp_ref, vc_ref, o_ref, lse_ref, *, dilation, n_steps):
    t = ATT_BLOCK
    not_first = pl.program_id(1) > 0
    qi = lax.broadcasted_iota(jnp.int32, (t, 2 * t), 0)
    kj = lax.broadcasted_iota(jnp.int32, (t, 2 * t), 1)
    dist = t + qi - kj
    valid = (dist >= 0) & (dist <= n_steps) & (not_first | (kj >= t))
    token_dist = (dilation * dist).astype(F32)
    low_half = lax.broadcasted_iota(jnp.int32, (t, LANES), 1) < HEAD_DIM
    contract_last = (((1,), (1,)), ((), ()))

    def unpack(words, which):
        return pltpu.unpack_elementwise(words, index=which, packed_dtype=BF16,
                                        unpacked_dtype=F32).astype(BF16)

    def residue(r, carry):
        rows = pl.ds(r, t) if dilation == 1 else pl.ds(r, t, stride=dilation)
        for slab in range(PAIR_SLABS):
            q_words = q_ref.at[slab][rows, :]
            k_words = jnp.concatenate([kp_ref.at[slab][rows, :], kc_ref.at[slab][rows, :]], axis=0)
            v_words = jnp.concatenate([vp_ref.at[slab][rows, :], vc_ref.at[slab][rows, :]], axis=0)
            for which in range(2):
                pair = 2 * slab + which
                q2, k2, v2 = unpack(q_words, which), unpack(k_words, which), unpack(v_words, which)
                outs, lses = [], []
                for half in range(2):
                    head = 2 * pair + half
                    slope = 2.0 ** (-8.0 * (head + 1) / ATT_HEADS)
                    keep = low_half if half == 0 else ~low_half
                    qh = jnp.where(keep, q2, jnp.zeros_like(q2))
                    s = lax.dot_general(qh, k2, contract_last, preferred_element_type=F32)
                    s = jnp.where(valid, s - slope * token_dist, MASK_VALUE)
                    m = jnp.max(s, axis=-1, keepdims=True)
                    p = jnp.exp(s - m)
                    l = jnp.sum(p, axis=-1, keepdims=True)
                    outs.append(_bdot(p.astype(BF16), v2) / l)
                    lses.append(m + jnp.log(l))
                o_ref.at[pair][rows, :] = jnp.where(low_half, outs[0], outs[1])
                lse_ref.at[pair][rows, :] = jnp.where(low_half, lses[0], lses[1])
        return carry

    if dilation == 1:
        residue(0, 0)
    else:
        lax.fori_loop(0, dilation, residue, 0)


def _attn_branch(q, k, v, bsz, seq, window, dilation):
    span = ATT_BLOCK * dilation
    n_spans = seq // span
    pairs = ATT_HEADS // 2
    cur = lambda slabs: pl.BlockSpec((slabs, span, LANES), lambda bi, n: (0, bi * n_spans + n, 0))
    prev = pl.BlockSpec((PAIR_SLABS, span, LANES),
                        lambda bi, n: (0, bi * n_spans + jnp.maximum(n - 1, 0), 0))
    return pl.pallas_call(
        functools.partial(_attn_body, dilation=dilation, n_steps=window // dilation),
        grid=(bsz, n_spans),
        in_specs=[cur(PAIR_SLABS), prev, cur(PAIR_SLABS), prev, cur(PAIR_SLABS)],
        out_specs=[cur(pairs), cur(pairs)],
        out_shape=[jax.ShapeDtypeStruct((pairs, bsz * seq, LANES), F32)] * 2,
        compiler_params=pltpu.CompilerParams(
            dimension_semantics=("arbitrary",) * 2, vmem_limit_bytes=VMEM_LIMIT),
        name=f"dilated_attn_d{dilation}",
    )(q, k, k, v, v)


def _merge_body(*refs, tt):
    o_refs = refs[:N_GROUPS]
    lse_refs = refs[N_GROUPS:2 * N_GROUPS]
    h_ref, wo_ref, g_ref, b_ref, out_ref = refs[2 * N_GROUPS:]
    wide = lambda r: jnp.concatenate([r[c] for c in range(ATT_HEADS // 2)], axis=1)
    lses = [wide(r) for r in lse_refs]
    m = functools.reduce(jnp.maximum, lses)
    ws = [jnp.exp(l - m) for l in lses]
    den = functools.reduce(lambda a, c: a + c, ws)
    o = functools.reduce(lambda a, c: a + c, [w * wide(r) for w, r in zip(ws, o_refs)]) / den
    mix = _bdot(o.astype(BF16), wo_ref[...])
    z = DEEPNORM_ALPHA * _rows_load(h_ref, tt) + mix
    _rows_store(out_ref, _layer_norm(z, g_ref[...], b_ref[...]), tt)


def _merge(outs, lses, h_tiles, w_o, g, b):
    n_tok = h_tiles.shape[0] // TOK_ROWS
    tt = MERGE_TT
    tok = pl.BlockSpec((ATT_HEADS // 2, tt, LANES), lambda i: (0, i, 0))
    tiles = pl.BlockSpec((tt * TOK_ROWS, LANES), lambda i: (i, 0))
    full = lambda shape: pl.BlockSpec(shape, lambda i: (0,) * len(shape))
    return pl.pallas_call(
        functools.partial(_merge_body, tt=tt),
        grid=(n_tok // tt,),
        in_specs=[tok] * (2 * N_GROUPS) + [tiles, full((O_WIDTH, D_MODEL)), full((1, D_MODEL)),
                                           full((1, D_MODEL))],
        out_specs=tiles,
        out_shape=jax.ShapeDtypeStruct(h_tiles.shape, F32),
        compiler_params=pltpu.CompilerParams(
            dimension_semantics=("arbitrary",), vmem_limit_bytes=VMEM_LIMIT),
        name="attn_merge",
    )(*outs, *lses, h_tiles, w_o.astype(BF16), g.reshape(1, -1), b.reshape(1, -1))


def kernel(x, a_w_in, a_conv_w, a_conv_b, a_w_gate_a, a_b_gate_a, a_w_gate_x, a_b_gate_x, a_lambda, a_w_out, w_kv_shared, b_w_q, b_w_o, moe_w_router, moe_router_bias, moe_w1, moe_w3, moe_w2, moe_ws1, moe_ws3, moe_ws2, ln_g, ln_b):
    bsz, seq, d = x.shape
    assert d == D_MODEL and seq % (DILATION_PATTERNS[-1][1] * ATT_BLOCK) == 0

    def moe(h_tiles, layer, token_tile_out):
        return _moe_layer(h_tiles, moe_w_router[layer], moe_router_bias[layer], moe_w1, moe_w3,
                          moe_w2, layer, moe_ws1[layer], moe_ws3[layer], moe_ws2[layer],
                          ln_g[layer, 1], ln_b[layer, 1], token_tile_out)

    h = _rglru_layer(x, a_w_in[0], a_conv_w[0], a_conv_b[0], a_w_gate_a[0], a_b_gate_a[0],
                     a_w_gate_x[0], a_b_gate_x[0], a_lambda[0], a_w_out[0], ln_g[0, 0], ln_b[0, 0])
    h = moe(h, 0, True)
    kv = _project(h, w_kv_shared, 1.0)
    ks, vs = kv[:N_GROUPS], kv[N_GROUPS:]

    qs = _project(h, b_w_q[0], HEAD_DIM ** -0.5)
    outs, lses = [], []
    for grp, (window, dilation) in enumerate(DILATION_PATTERNS):
        o, lse = _attn_branch(qs[grp], ks[grp], vs[grp], bsz, seq, window, dilation)
        outs.append(o)
        lses.append(lse)
    h = _merge(outs, lses, h, b_w_o[0], ln_g[1, 0], ln_b[1, 0])
    h = moe(h, 1, False)
    return h.reshape(bsz, seq, d)
```

```python
import functools
import math

import jax
import jax.numpy as jnp
from jax import lax
from jax.experimental import pallas as pl
from jax.experimental.pallas import tpu as pltpu

F32 = jnp.float32
BF16 = jnp.bfloat16
I32 = jnp.int32

D_MODEL = 1024
LRU_BLOCKS = 4
CONV_WIDTH = 4
LRU_C = 8.0
ATT_HEADS = 8
HEAD_DIM = 64
DILATION_PATTERNS = ((128, 1), (512, 4), (2048, 16))
N_GROUPS = len(DILATION_PATTERNS)
ATT_BLOCK = 128
O_WIDTH = ATT_HEADS * HEAD_DIM
N_EXPERTS = 256
TOP_K = 8
N_EXPERT_GROUPS = 8
TOPK_GROUPS = 4
D_EXPERT = 256
ROUTED_SCALE = 2.5
DEPTH = 2
DEEPNORM_ALPHA = (2 * DEPTH) ** 0.25
LN_EPS = 1e-5
MASK_VALUE = -1e30

SUBLANES = 8
LANES = 128
TOK_ROWS = D_MODEL // LANES
PK_ROWS = D_MODEL // 2 // LANES
PAIR_SLABS = ATT_HEADS * HEAD_DIM // LANES // 2
VMEM_LIMIT = 56 * 1024 * 1024

RGLRU_TS = 256
ROUTER_TT = 256
SORTED_ROWS_TL = 2048
DISPATCH_TT = 128
EXPERT_TB = 512
COMBINE_TT = 128
MERGE_TT = 256


def _rows_load(ref, n_tok):
    return jnp.concatenate(
        [ref[pl.ds(s, n_tok, stride=TOK_ROWS), :] for s in range(TOK_ROWS)], axis=1)


def _rows_store(ref, val, n_tok):
    for s in range(TOK_ROWS):
        ref[pl.ds(s, n_tok, stride=TOK_ROWS), :] = val[:, s * LANES:(s + 1) * LANES]


def _layer_norm(z, g, b):
    mu = jnp.mean(z, axis=-1, keepdims=True)
    zc = z - mu
    var = jnp.mean(zc * zc, axis=-1, keepdims=True)
    return zc * lax.rsqrt(var + LN_EPS) * g + b


def _silu(x):
    return x * jax.nn.sigmoid(x)


def _gelu_tanh(x):
    c = math.sqrt(2.0 / math.pi)
    return 0.5 * x * (1.0 + jnp.tanh(c * (x + 0.044715 * (x * x * x))))


def _bdot(a, b):
    return jnp.dot(a, b, preferred_element_type=F32)


def _rglru_body(x_ref, win_ref, cw_ref, cb_ref, wga_ref, bga_ref, wgx_ref, bgx_ref, lam_ref,
                wout_ref, g_ref, b_ref, o_ref, hc_ref, tail_ref, *, ts):
    width = D_MODEL
    bw = width // LRU_BLOCKS

    @pl.when(pl.program_id(1) == 0)
    def _():
        hc_ref[...] = jnp.zeros_like(hc_ref)
        tail_ref[...] = jnp.zeros_like(tail_ref)

    x = x_ref[...]
    xz = _bdot(x.astype(BF16), win_ref[...])
    xr = xz[:, :width]
    gate = xz[:, width:]

    tail = tail_ref[...]
    row8 = lax.broadcasted_iota(jnp.int32, (SUBLANES, width), 0)
    cw = cw_ref[...]
    xc = xr * cw[CONV_WIDTH - 1:CONV_WIDTH, :] + cb_ref[...]
    for j in range(1, CONV_WIDTH):
        rx = pltpu.roll(xr, j, 0)
        rp = pltpu.roll(tail, j, 0)
        top = jnp.where(row8 < j, rp, rx[:SUBLANES])
        shifted = jnp.concatenate([top, rx[SUBLANES:]], axis=0)
        xc = xc + shifted * cw[CONV_WIDTH - 1 - j:CONV_WIDTH - j, :]
    tail_ref[...] = xr[ts - SUBLANES:]

    xcb = xc.astype(BF16)

    def block_diag(w_ref):
        return jnp.concatenate(
            [_bdot(xcb[:, n * bw:(n + 1) * bw], w_ref[n]) for n in range(LRU_BLOCKS)], axis=1)

    r = jax.nn.sigmoid(block_diag(wga_ref) + bga_ref[...])
    i = jax.nn.sigmoid(block_diag(wgx_ref) + bgx_ref[...])
    lam = lam_ref[...]
    softplus_neg_lam = jnp.maximum(-lam, 0.0) + jnp.log1p(jnp.exp(-jnp.abs(lam)))
    log_a = (-LRU_C * r) * softplus_neg_lam
    a = jnp.exp(log_a)
    mult = jnp.sqrt(-jnp.tanh(log_a) * (a * a + 1.0))
    u = mult * (i * xc)

    groups = ts // SUBLANES
    a3 = a.reshape(groups, SUBLANES, width)
    u3 = u.reshape(groups, SUBLANES, width)
    sub = lax.broadcasted_iota(jnp.int32, (groups, SUBLANES, width), 1)
    sh = 1
    while sh < SUBLANES:
        a_prev = pltpu.roll(a3, sh, 1)
        u_prev = pltpu.roll(u3, sh, 1)
        live = sub >= sh
        u3 = jnp.where(live, a3 * u_prev, 0.0) + u3
        a3 = jnp.where(live, a3 * a_prev, a3)
        sh *= 2
    carry = hc_ref[...]
    h_groups = []
    for grp in range(groups):
        h_grp = a3[grp] * carry + u3[grp]
        h_groups.append(h_grp)
        carry = h_grp[SUBLANES - 1:SUBLANES]
    h = jnp.concatenate(h_groups, axis=0)
    hc_ref[...] = carry

    y = (h * _gelu_tanh(gate)).astype(BF16)
    mix = _bdot(y, wout_ref[...])
    z = DEEPNORM_ALPHA * x + mix
    _rows_store(o_ref, _layer_norm(z, g_ref[...], b_ref[...]), ts)


def _rglru_layer(x, w_in, conv_w, conv_b, wga, bga, wgx, bgx, lam, w_out, g, b):
    bsz, seq, d = x.shape
    ts = RGLRU_TS
    ns = seq // ts
    row = lambda v: v.reshape(1, -1)
    full = lambda shape: pl.BlockSpec(shape, lambda bi, si: (0,) * len(shape))
    return pl.pallas_call(
        functools.partial(_rglru_body, ts=ts),
        grid=(bsz, ns),
        in_specs=[
            pl.BlockSpec((None, ts, d), lambda bi, si: (bi, si, 0)),
            full((d, 2 * d)), full((CONV_WIDTH, d)), full((1, d)),
            full((LRU_BLOCKS, d // LRU_BLOCKS, d // LRU_BLOCKS)), full((1, d)),
            full((LRU_BLOCKS, d // LRU_BLOCKS, d // LRU_BLOCKS)), full((1, d)),
            full((1, d)), full((d, d)), full((1, d)), full((1, d)),
        ],
        out_specs=pl.BlockSpec((ts * TOK_ROWS, LANES), lambda bi, si: (bi * ns + si, 0)),
        out_shape=jax.ShapeDtypeStruct((bsz * seq * TOK_ROWS, LANES), F32),
        scratch_shapes=[pltpu.VMEM((1, d), F32), pltpu.VMEM((SUBLANES, d), F32)],
        compiler_params=pltpu.CompilerParams(
            dimension_semantics=("arbitrary", "arbitrary"), vmem_limit_bytes=VMEM_LIMIT),
        name="rglru_layer",
    )(x, w_in.astype(BF16), conv_w, row(conv_b), wga.astype(BF16), row(bga), wgx.astype(BF16),
      row(bgx), row(lam), w_out.astype(BF16), row(g), row(b))


def _router_body(h_ref, wrt_ref, bias_ref, idx_ref, gate_ref, rank_ref, cnt_ref, carry_ref, *, tt):
    n_e = N_EXPERTS
    per_group = n_e // N_EXPERT_GROUPS

    @pl.when(pl.program_id(0) == 0)
    def _():
        carry_ref[...] = jnp.zeros_like(carry_ref)

    h = _rows_load(h_ref, tt)
    logits = lax.dot_general(wrt_ref[...], h, (((1,), (1,)), ((), ())),
                             precision=lax.Precision.HIGHEST, preferred_element_type=F32)
    scores = jax.nn.sigmoid(logits)
    biased = scores + bias_ref[...]

    j_iota = lax.broadcasted_iota(jnp.int32, (per_group, tt), 0)
    group_score = []
    for g in range(N_EXPERT_GROUPS):
        bg = biased[g * per_group:(g + 1) * per_group]
        m1 = jnp.max(bg, axis=0, keepdims=True)
        i1 = jnp.min(jnp.where(bg == m1, j_iota, per_group), axis=0, keepdims=True)
        m2 = jnp.max(jnp.where(j_iota == i1, -jnp.inf, bg), axis=0, keepdims=True)
        group_score.append(m1 + m2)

    masked = []
    for g in range(N_EXPERT_GROUPS):
        beaten_by = jnp.zeros((1, tt), jnp.int32)
        for o in range(N_EXPERT_GROUPS):
            if o == g:
                continue
            wins = group_score[o] > group_score[g]
            if o < g:
                wins = wins | (group_score[o] == group_score[g])
            beaten_by = beaten_by + wins.astype(jnp.int32)
        keep = beaten_by < TOPK_GROUPS
        masked.append(jnp.where(keep, biased[g * per_group:(g + 1) * per_group], MASK_VALUE))
    cur = jnp.concatenate(masked, axis=0)

    e_iota = lax.broadcasted_iota(jnp.int32, (n_e, tt), 0)
    idx_rows, score_rows, sels = [], [], []
    for _ in range(TOP_K):
        m = jnp.max(cur, axis=0, keepdims=True)
        ik = jnp.min(jnp.where(cur == m, e_iota, n_e), axis=0, keepdims=True)
        sel = e_iota == ik
        score_rows.append(jnp.sum(jnp.where(sel, scores, 0.0), axis=0, keepdims=True))
        cur = jnp.where(sel, -jnp.inf, cur)
        idx_rows.append(ik)
        sels.append(sel)
    top_s = jnp.concatenate(score_rows, axis=0)
    gate_ref[...] = top_s / jnp.sum(top_s, axis=0, keepdims=True) * ROUTED_SCALE
    idx_ref[...] = jnp.concatenate(idx_rows, axis=0)
    multi_hot = jnp.where(cur == -jnp.inf, 1.0, 0.0)

    t_row = lax.broadcasted_iota(jnp.int32, (tt, tt), 0)
    t_col = lax.broadcasted_iota(jnp.int32, (tt, tt), 1)
    strict_upper = jnp.where(t_row < t_col, 1.0, 0.0).astype(BF16)
    before = _bdot(multi_hot.astype(BF16), strict_upper) + carry_ref[...]
    rank_rows = [jnp.sum(jnp.where(sel, before, 0.0), axis=0, keepdims=True) for sel in sels]
    rank_ref[...] = jnp.concatenate(rank_rows, axis=0).astype(jnp.int32)
    carry = carry_ref[...] + jnp.sum(multi_hot, axis=1, keepdims=True)
    carry_ref[...] = carry
    cnt_ref[...] = carry.astype(jnp.int32)


def _router(h_tiles, w_router, router_bias):
    n_tok = h_tiles.shape[0] // TOK_ROWS
    tt = ROUTER_TT
    kt = lambda dt: jax.ShapeDtypeStruct((TOP_K, n_tok), dt)
    tok_spec = pl.BlockSpec((TOP_K, tt), lambda i: (0, i))
    return pl.pallas_call(
        functools.partial(_router_body, tt=tt),
        grid=(n_tok // tt,),
        in_specs=[
            pl.BlockSpec((tt * TOK_ROWS, LANES), lambda i: (i, 0)),
            pl.BlockSpec((N_EXPERTS, D_MODEL), lambda i: (0, 0)),
            pl.BlockSpec((N_EXPERTS, 1), lambda i: (0, 0)),
        ],
        out_specs=[tok_spec, tok_spec, tok_spec, pl.BlockSpec((N_EXPERTS, 1), lambda i: (0, 0))],
        out_shape=[kt(jnp.int32), kt(F32), kt(jnp.int32),
                   jax.ShapeDtypeStruct((N_EXPERTS, 1), jnp.int32)],
        scratch_shapes=[pltpu.VMEM((N_EXPERTS, 1), F32)],
        compiler_params=pltpu.CompilerParams(
            dimension_semantics=("arbitrary",), vmem_limit_bytes=VMEM_LIMIT),
        name="moe_router",
    )(h_tiles, w_router.T, router_bias.reshape(N_EXPERTS, 1))


def _pack_rows(x):
    half = D_MODEL // 2
    return pltpu.pack_elementwise([x[:, :half], x[:, half:]], packed_dtype=BF16)


def _unpack_rows(p):
    return jnp.concatenate(
        [pltpu.unpack_elementwise(p, index=i, packed_dtype=BF16, unpacked_dtype=F32) for i in range(2)],
        axis=1)


def _packed_load(ref, n_tok):
    return jnp.concatenate(
        [ref[pl.ds(s, n_tok, stride=PK_ROWS), :] for s in range(PK_ROWS)], axis=1)


def _packed_store(ref, val, n_tok):
    for s in range(PK_ROWS):
        ref[pl.ds(s, n_tok, stride=PK_ROWS), :] = val[:, s * LANES:(s + 1) * LANES]


def _sorted_rows_body(idx_ref, rank_ref, start_ref, o_ref):
    tl = idx_ref.shape[1]
    e_iota = lax.broadcasted_iota(jnp.int32, (N_EXPERTS, tl), 0)
    start = start_ref[...]
    rows = []
    for k in range(TOP_K):
        hit = e_iota == idx_ref[k:k + 1, :]
        base = jnp.sum(jnp.where(hit, start, 0.0), axis=0, keepdims=True)
        rows.append(base.astype(jnp.int32) + rank_ref[k:k + 1, :])
    o_ref[...] = jnp.concatenate(rows, axis=0)


def _sorted_rows(top_idx, rank, pad_start):
    n_tok = top_idx.shape[1]
    tl = SORTED_ROWS_TL
    tok = pl.BlockSpec((TOP_K, tl), lambda i: (0, i))
    return pl.pallas_call(
        _sorted_rows_body,
        grid=(n_tok // tl,),
        in_specs=[tok, tok, pl.BlockSpec((N_EXPERTS, 1), lambda i: (0, 0))],
        out_specs=tok,
        out_shape=jax.ShapeDtypeStruct((TOP_K, n_tok), jnp.int32),
        compiler_params=pltpu.CompilerParams(
            dimension_semantics=("arbitrary",), vmem_limit_bytes=VMEM_LIMIT),
        name="moe_sorted_rows",
    )(top_idx, rank, pad_start.astype(F32).reshape(N_EXPERTS, 1))


def _dispatch_body(start_ref, cnt_ref, dest_ref, h_ref, xs_ref, pk_ref, zero_ref, sem, zero_sem,
                   *, tt, tb):
    @pl.when(pl.program_id(0) == 0)
    def _():
        zero_ref[...] = jnp.zeros_like(zero_ref)

        def pad_copies(e, fn):
            n_pad = ((cnt_ref[e] + (tb - 1)) & (-tb)) - cnt_ref[e]
            row = start_ref[e] + cnt_ref[e]
            for bit in range(tb.bit_length() - 1):
                size = 1 << bit
                has_bit = ((n_pad >> bit) & 1) == 1

                @pl.when(has_bit)
                def _():
                    fn(pltpu.make_async_copy(zero_ref.at[pl.ds(0, size)],
                                             xs_ref.at[pl.ds(row, size)], zero_sem))
                row = row + jnp.where(has_bit, size, 0)

        def start_all(e, c):
            pad_copies(e, lambda cp: cp.start())
            return c

        def wait_all(e, c):
            pad_copies(e, lambda cp: cp.wait())
            return c

        lax.fori_loop(0, N_EXPERTS, start_all, 0)
        lax.fori_loop(0, N_EXPERTS, wait_all, 0)

    packed = _pack_rows(_rows_load(h_ref, tt))
    for s in range(PK_ROWS):
        pk_ref[:, s, :] = packed[:, s * LANES:(s + 1) * LANES]
    for t in range(tt):
        for k in range(TOP_K):
            pltpu.make_async_copy(pk_ref.at[t], xs_ref.at[dest_ref[k, t]], sem).start(priority=k % 2)
    for k in range(TOP_K):
        pltpu.make_async_copy(pk_ref, xs_ref.at[pl.ds(0, tt)], sem).wait()


def _dispatch(h_tiles, dest_rows, pad_start, counts, n_rows):
    n_tok = h_tiles.shape[0] // TOK_ROWS
    tt = DISPATCH_TT
    grid_spec = pltpu.PrefetchScalarGridSpec(
        num_scalar_prefetch=2,
        grid=(n_tok // tt,),
        in_specs=[pl.BlockSpec((TOP_K, tt), lambda i, s, c: (0, i), memory_space=pltpu.SMEM),
                  pl.BlockSpec((tt * TOK_ROWS, LANES), lambda i, s, c: (i, 0))],
        out_specs=pl.BlockSpec(memory_space=pl.ANY),
        scratch_shapes=[pltpu.VMEM((tt, PK_ROWS, LANES), I32),
                        pltpu.VMEM((EXPERT_TB // 2, PK_ROWS, LANES), I32),
                        pltpu.SemaphoreType.DMA(()), pltpu.SemaphoreType.DMA(())],
    )
    return pl.pallas_call(
        functools.partial(_dispatch_body, tt=tt, tb=EXPERT_TB),
        grid_spec=grid_spec,
        out_shape=jax.ShapeDtypeStruct((n_rows, PK_ROWS, LANES), I32),
        compiler_params=pltpu.CompilerParams(
            dimension_semantics=("arbitrary",), vmem_limit_bytes=VMEM_LIMIT),
        name="moe_dispatch",
    )(pad_start, counts, dest_rows, h_tiles)


def _expert_body(be_ref, first_ref, slot_ref, next_ref, na_ref, xs_ref, w1_ref, w3_ref, w2_ref, ys_ref,
                 w13_buf, w2_buf, w1b, w3b, w2b, sem, *, tb, layer):
    i = pl.program_id(0)

    def weight_copies(expert, slot):
        return (pltpu.make_async_copy(w1_ref.at[layer, expert], w13_buf.at[slot, 0], sem.at[slot]),
                pltpu.make_async_copy(w3_ref.at[layer, expert], w13_buf.at[slot, 1], sem.at[slot]),
                pltpu.make_async_copy(w2_ref.at[layer, expert], w2_buf.at[slot], sem.at[slot]))

    @pl.when(i < na_ref[0])
    def _():
        @pl.when(first_ref[i] == 1)
        def _():
            slot = slot_ref[i]

            @pl.when(i == 0)
            def _():
                for cp in weight_copies(be_ref[0], slot):
                    cp.start()

            for cp in weight_copies(be_ref[i], slot):
                cp.wait()

            @pl.when(next_ref[i] >= 0)
            def _():
                for cp in weight_copies(next_ref[i], 1 - slot):
                    cp.start()

            w1b[...] = w13_buf[slot, 0].astype(BF16)
            w3b[...] = w13_buf[slot, 1].astype(BF16)
            w2b[...] = w2_buf[slot].astype(BF16)

        x = _unpack_rows(_packed_load(xs_ref, tb)).astype(BF16)
        mid = (_silu(_bdot(x, w1b[...])) * _bdot(x, w3b[...])).astype(BF16)
        _packed_store(ys_ref, _pack_rows(_bdot(mid, w2b[...])), tb)


def _experts(xs, block_expert, first, slot, next_expert, n_active, w1, w3, w2, layer):
    tb = EXPERT_TB
    n_blocks = xs.shape[0] // (tb * PK_ROWS)
    row_block = lambda i, be, fi, sl, nx, na: (jnp.minimum(i, na[0] - 1), 0)
    grid_spec = pltpu.PrefetchScalarGridSpec(
        num_scalar_prefetch=5,
        grid=(n_blocks,),
        in_specs=[
            pl.BlockSpec((tb * PK_ROWS, LANES), row_block),
            pl.BlockSpec(memory_space=pl.ANY), pl.BlockSpec(memory_space=pl.ANY),
            pl.BlockSpec(memory_space=pl.ANY),
        ],
        out_specs=pl.BlockSpec((tb * PK_ROWS, LANES), row_block),
        scratch_shapes=[pltpu.VMEM((2, 2, D_MODEL, D_EXPERT), F32),
                        pltpu.VMEM((2, D_EXPERT, D_MODEL), F32),
                        pltpu.VMEM((D_MODEL, D_EXPERT), BF16), pltpu.VMEM((D_MODEL, D_EXPERT), BF16),
                        pltpu.VMEM((D_EXPERT, D_MODEL), BF16), pltpu.SemaphoreType.DMA((2,))],
    )
    return pl.pallas_call(
        functools.partial(_expert_body, tb=tb, layer=layer),
        grid_spec=grid_spec,
        out_shape=jax.ShapeDtypeStruct(xs.shape, I32),
        compiler_params=pltpu.CompilerParams(
            dimension_semantics=("arbitrary",), vmem_limit_bytes=VMEM_LIMIT),
        name="moe_experts",
    )(block_expert, first, slot, next_expert, n_active, xs, w1, w3, w2)


def _store_head_slabs(y, o_refs, scales):
    for j, (o_ref, scale) in enumerate(zip(o_refs, scales)):
        for s in range(PAIR_SLABS):
            lo = j * O_WIDTH + 2 * s * LANES
            o_ref[s] = pltpu.pack_elementwise(
                [y[:, lo:lo + LANES] * scale, y[:, lo + LANES:lo + 2 * LANES] * scale],
                packed_dtype=BF16)


def _combine_body(src_ref, h_ref, gates_ref, ys_ref, ys_flat_ref, ws1_ref, ws3_ref, ws2_ref,
                  g_ref, b_ref, *rest, tt, n_tiles, token_tile_out, proj_scales):
    if proj_scales:
        wp_ref, o_ref, *proj_refs, gbuf, sem, pbuf = rest
    else:
        o_ref, gbuf, sem = rest
    j = pl.program_id(0)

    def project_chunk(c):
        lo = c * O_WIDTH
        y = _bdot(pbuf[j % 2], wp_ref[:, lo:lo + O_WIDTH])
        _store_head_slabs(y, proj_refs[c:c + 1], proj_scales[c:c + 1])

    def step(issue, finish, project):
        nxt = j % 2
        cur = (j - 1) % 2
        if finish:
            h = _rows_load(h_ref, tt)
            hb = h.astype(BF16)
            mid = (_silu(_bdot(hb, ws1_ref[...])) * _bdot(hb, ws3_ref[...])).astype(BF16)
            routed = _bdot(mid, ws2_ref[...])
            gates = gates_ref[...]
        for k in range(TOP_K):
            if issue:
                for t in range(tt):
                    pltpu.make_async_copy(ys_ref.at[src_ref[k, t]],
                                          gbuf.at[nxt, k, pl.ds(t * PK_ROWS, PK_ROWS), :],
                                          sem.at[nxt]).start(priority=t % 2)
            if project:
                project_chunk(k)
            if finish:
                pltpu.make_async_copy(ys_flat_ref.at[pl.ds(0, tt * PK_ROWS), :], gbuf.at[cur, k],
                                      sem.at[cur]).wait()
                routed = routed + gates[:, k:k + 1] * _unpack_rows(_packed_load(gbuf.at[cur, k], tt))
        if project:
            for c in range(TOP_K, len(proj_scales)):
                project_chunk(c)
        if finish:
            out = _layer_norm(DEEPNORM_ALPHA * h + routed, g_ref[...], b_ref[...])
            if token_tile_out:
                _rows_store(o_ref, out, tt)
            else:
                o_ref[...] = out
            if proj_scales:
                pbuf[(j - 1) % 2] = out.astype(BF16)

    if proj_scales:
        pl.when(j == 0)(lambda: step(True, False, False))
        pl.when(j == 1)(lambda: step(True, True, False))
        pl.when((j >= 2) & (j < n_tiles))(lambda: step(True, True, True))
        pl.when(j == n_tiles)(lambda: step(False, True, True))
        pl.when(j == n_tiles + 1)(lambda: step(False, False, True))
    else:
        pl.when(j == 0)(lambda: step(True, False, False))
        pl.when((j >= 1) & (j < n_tiles))(lambda: step(True, True, False))
        pl.when(j == n_tiles)(lambda: step(False, True, False))


def _combine(h_tiles, src_rows, gates_tk, ys, ws1, ws3, ws2, g, b, token_tile_out, proj_w=None,
             proj_scales=()):
    n_tok = h_tiles.shape[0] // TOK_ROWS
    tt = COMBINE_TT
    n_tiles = n_tok // tt
    n_steps = n_tiles + (2 if proj_scales else 1)
    tile_before = lambda back: (lambda j: (jnp.clip(j - back, 0, n_tiles - 1), 0))
    full = lambda shape: pl.BlockSpec(shape, lambda j: (0,) * len(shape))
    if token_tile_out:
        out_spec = pl.BlockSpec((tt * TOK_ROWS, LANES), tile_before(1))
        out_shape = jax.ShapeDtypeStruct((n_tok * TOK_ROWS, LANES), F32)
    else:
        out_spec = pl.BlockSpec((tt, D_MODEL), tile_before(1))
        out_shape = jax.ShapeDtypeStruct((n_tok, D_MODEL), F32)
    n_rows = ys.shape[0] // PK_ROWS
    proj_in = [proj_w.astype(BF16)] if proj_scales else []
    proj_scratch = [pltpu.VMEM((2, tt, D_MODEL), BF16)] if proj_scales else []
    slab_spec = pl.BlockSpec((PAIR_SLABS, tt, LANES),
                             lambda j: (0, jnp.clip(j - 2, 0, n_tiles - 1), 0))
    slab_shape = jax.ShapeDtypeStruct((PAIR_SLABS, n_tok, LANES), I32)
    return pl.pallas_call(
        functools.partial(_combine_body, tt=tt, n_tiles=n_tiles, token_tile_out=token_tile_out,
                          proj_scales=tuple(proj_scales)),
        grid=(n_steps,),
        in_specs=[pl.BlockSpec((TOP_K, tt), lambda j: (0, jnp.minimum(j, n_tiles - 1)),
                               memory_space=pltpu.SMEM),
                  pl.BlockSpec((tt * TOK_ROWS, LANES), tile_before(1)),
                  pl.BlockSpec((tt, TOP_K), tile_before(1)),
                  pl.BlockSpec(memory_space=pl.ANY), pl.BlockSpec(memory_space=pl.ANY),
                  full((D_MODEL, D_EXPERT)), full((D_MODEL, D_EXPERT)), full((D_EXPERT, D_MODEL)),
                  full((1, D_MODEL)), full((1, D_MODEL))] + [full(w.shape) for w in proj_in],
        out_specs=[out_spec] + [slab_spec] * len(proj_scales),
        out_shape=[out_shape] + [slab_shape] * len(proj_scales),
        scratch_shapes=[pltpu.VMEM((2, TOP_K, tt * PK_ROWS, LANES), I32),
                        pltpu.SemaphoreType.DMA((2,))] + proj_scratch,
        compiler_params=pltpu.CompilerParams(
            dimension_semantics=("arbitrary",), vmem_limit_bytes=VMEM_LIMIT),
        name="moe_combine",
    )(src_rows, h_tiles, gates_tk, ys.reshape(n_rows, PK_ROWS, LANES), ys, ws1.astype(BF16),
      ws3.astype(BF16), ws2.astype(BF16), g.reshape(1, -1), b.reshape(1, -1), *proj_in)


def _moe_layer(h_tiles, w_router, router_bias, w1, w3, w2, layer, ws1, ws3, ws2, g, b,
               token_tile_out, proj_w=None, proj_scales=()):
    n_tok = h_tiles.shape[0] // TOK_ROWS
    tb = EXPERT_TB
    top_idx, gates, rank, counts = _router(h_tiles, w_router, router_bias)

    n_blocks = -(-(n_tok * TOP_K + N_EXPERTS * (tb - 1)) // tb)
    counts = counts.reshape(N_EXPERTS)
    padded = ((counts + tb - 1) // tb) * tb
    pad_end = jnp.cumsum(padded)
    pad_start = (pad_end - padded).astype(jnp.int32)
    block_first_row = jnp.arange(n_blocks, dtype=jnp.int32) * tb
    block_expert = jnp.minimum(
        jnp.sum((pad_end[None, :] <= block_first_row[:, None]).astype(jnp.int32), axis=1),
        N_EXPERTS - 1)
    n_active = (pad_end[-1:] // tb).astype(jnp.int32)
    present = counts > 0
    expert_ids = jnp.arange(N_EXPERTS, dtype=jnp.int32)
    later = lax.cummin(jnp.where(present, expert_ids, N_EXPERTS), reverse=True)
    next_present = jnp.concatenate([later[1:], jnp.full((1,), N_EXPERTS, jnp.int32)])
    next_present = jnp.where(next_present < N_EXPERTS, next_present, -1)
    ordinal = jnp.cumsum(present.astype(jnp.int32)) - 1
    first = jnp.concatenate([jnp.ones((1,), jnp.int32),
                             (block_expert[1:] != block_expert[:-1]).astype(jnp.int32)])
    slot = ordinal[block_expert] % 2
    next_expert = next_present[block_expert]

    rows = _sorted_rows(top_idx, rank, pad_start)
    xs = _dispatch(h_tiles, rows, pad_start, counts, n_blocks * tb)
    ys = _experts(xs.reshape(n_blocks * tb * PK_ROWS, LANES), block_expert, first, slot, next_expert,
                  n_active, w1, w3, w2, layer)
    return _combine(h_tiles, rows, gates.T, ys, ws1, ws3, ws2, g, b, token_tile_out, proj_w,
                    proj_scales)


def _attn_body(q_ref, kp_ref, kc_ref, vp_ref, vc_ref, o_ref, lse_ref, *, dilation, n_steps):
    t = ATT_BLOCK
    not_first = pl.program_id(1) > 0
    qi = lax.broadcasted_iota(jnp.int32, (t, 2 * t), 0)
    kj = lax.broadcasted_iota(jnp.int32, (t, 2 * t), 1)
    dist = t + qi - kj
    valid = (dist >= 0) & (dist <= n_steps) & (not_first | (kj >= t))
    token_dist = (dilation * dist).astype(F32)
    low_half = lax.broadcasted_iota(jnp.int32, (t, LANES), 1) < HEAD_DIM
    contract_last = (((1,), (1,)), ((), ()))

    def unpack(words, which):
        return pltpu.unpack_elementwise(words, index=which, packed_dtype=BF16,
                                        unpacked_dtype=F32).astype(BF16)

    def residue(r, carry):
        rows = pl.ds(r, t) if dilation == 1 else pl.ds(r, t, stride=dilation)
        for slab in range(PAIR_SLABS):
            q_words = q_ref.at[slab][rows, :]
            k_words = jnp.concatenate([kp_ref.at[slab][rows, :], kc_ref.at[slab][rows, :]], axis=0)
            v_words = jnp.concatenate([vp_ref.at[slab][rows, :], vc_ref.at[slab][rows, :]], axis=0)
            for which in range(2):
                pair = 2 * slab + which
                q2, k2, v2 = unpack(q_words, which), unpack(k_words, which), unpack(v_words, which)
                outs, lses = [], []
                for half in range(2):
                    head = 2 * pair + half
                    slope = 2.0 ** (-8.0 * (head + 1) / ATT_HEADS)
                    keep = low_half if half == 0 else ~low_half
                    qh = jnp.where(keep, q2, jnp.zeros_like(q2))
                    s = lax.dot_general(qh, k2, contract_last, preferred_element_type=F32)
                    s = jnp.where(valid, s - slope * token_dist, MASK_VALUE)
                    m = jnp.max(s, axis=-1, keepdims=True)
                    p = jnp.exp(s - m)
                    l = jnp.sum(p, axis=-1, keepdims=True)
                    outs.append(_bdot(p.astype(BF16), v2) / l)
                    lses.append(m + jnp.log(l))
                o_ref.at[pair][rows, :] = jnp.where(low_half, outs[0], outs[1])
                lse_ref.at[pair][rows, :] = jnp.where(low_half, lses[0], lses[1])
        return carry

    if dilation == 1:
        residue(0, 0)
    else:
        lax.fori_loop(0, dilation, residue, 0)


def _attn_branch(q, k, v, bsz, seq, window, dilation):
    span = ATT_BLOCK * dilation
    n_spans = seq // span
    pairs = ATT_HEADS // 2
    cur = lambda slabs: pl.BlockSpec((slabs, span, LANES), lambda bi, n: (0, bi * n_spans + n, 0))
    prev = pl.BlockSpec((PAIR_SLABS, span, LANES),
                        lambda bi, n: (0, bi * n_spans + jnp.maximum(n - 1, 0), 0))
    return pl.pallas_call(
        functools.partial(_attn_body, dilation=dilation, n_steps=window // dilation),
        grid=(bsz, n_spans),
        in_specs=[cur(PAIR_SLABS), prev, cur(PAIR_SLABS), prev, cur(PAIR_SLABS)],
        out_specs=[cur(pairs), cur(pairs)],
        out_shape=[jax.ShapeDtypeStruct((pairs, bsz * seq, LANES), F32)] * 2,
        compiler_params=pltpu.CompilerParams(
            dimension_semantics=("arbitrary",) * 2, vmem_limit_bytes=VMEM_LIMIT),
        name=f"dilated_attn_d{dilation}",
    )(q, k, k, v, v)


def _merge_body(*refs, tt):
    o_refs = refs[:N_GROUPS]
    lse_refs = refs[N_GROUPS:2 * N_GROUPS]
    h_ref, wo_ref, g_ref, b_ref, out_ref = refs[2 * N_GROUPS:]
    wide = lambda r: jnp.concatenate([r[c] for c in range(ATT_HEADS // 2)], axis=1)
    lses = [wide(r) for r in lse_refs]
    m = functools.reduce(jnp.maximum, lses)
    ws = [jnp.exp(l - m) for l in lses]
    den = functools.reduce(lambda a, c: a + c, ws)
    o = functools.reduce(lambda a, c: a + c, [w * wide(r) for w, r in zip(ws, o_refs)]) / den
    mix = _bdot(o.astype(BF16), wo_ref[...])
    z = DEEPNORM_ALPHA * _rows_load(h_ref, tt) + mix
    _rows_store(out_ref, _layer_norm(z, g_ref[...], b_ref[...]), tt)


def _merge(outs, lses, h_tiles, w_o, g, b):
    n_tok = h_tiles.shape[0] // TOK_ROWS
    tt = MERGE_TT
    tok = pl.BlockSpec((ATT_HEADS // 2, tt, LANES), lambda i: (0, i, 0))
    tiles = pl.BlockSpec((tt * TOK_ROWS, LANES), lambda i: (i, 0))
    full = lambda shape: pl.BlockSpec(shape, lambda i: (0,) * len(shape))
    return pl.pallas_call(
        functools.partial(_merge_body, tt=tt),
        grid=(n_tok // tt,),
        in_specs=[tok] * (2 * N_GROUPS) + [tiles, full((O_WIDTH, D_MODEL)), full((1, D_MODEL)),
                                           full((1, D_MODEL))],
        out_specs=tiles,
        out_shape=jax.ShapeDtypeStruct(h_tiles.shape, F32),
        compiler_params=pltpu.CompilerParams(
            dimension_semantics=("arbitrary",), vmem_limit_bytes=VMEM_LIMIT),
        name="attn_merge",
    )(*outs, *lses, h_tiles, w_o.astype(BF16), g.reshape(1, -1), b.reshape(1, -1))


def kernel(x, a_w_in, a_conv_w, a_conv_b, a_w_gate_a, a_b_gate_a, a_w_gate_x, a_b_gate_x, a_lambda, a_w_out, w_kv_shared, b_w_q, b_w_o, moe_w_router, moe_router_bias, moe_w1, moe_w3, moe_w2, moe_ws1, moe_ws3, moe_ws2, ln_g, ln_b):
    bsz, seq, d = x.shape
    assert d == D_MODEL and seq % (DILATION_PATTERNS[-1][1] * ATT_BLOCK) == 0

    def moe(h_tiles, layer, token_tile_out, proj_w=None, proj_scales=()):
        return _moe_layer(h_tiles, moe_w_router[layer], moe_router_bias[layer], moe_w1, moe_w3,
                          moe_w2, layer, moe_ws1[layer], moe_ws3[layer], moe_ws2[layer],
                          ln_g[layer, 1], ln_b[layer, 1], token_tile_out, proj_w, proj_scales)

    h = _rglru_layer(x, a_w_in[0], a_conv_w[0], a_conv_b[0], a_w_gate_a[0], a_b_gate_a[0],
                     a_w_gate_x[0], a_b_gate_x[0], a_lambda[0], a_w_out[0], ln_g[0, 0], ln_b[0, 0])
    proj_w = jnp.concatenate([w_kv_shared, b_w_q[0]], axis=1)
    proj_scales = (1.0,) * (2 * N_GROUPS) + (HEAD_DIM ** -0.5,) * N_GROUPS
    h, *kvq = moe(h, 0, True, proj_w, proj_scales)
    ks, vs, qs = kvq[:N_GROUPS], kvq[N_GROUPS:2 * N_GROUPS], kvq[2 * N_GROUPS:]

    outs, lses = [], []
    for grp, (window, dilation) in enumerate(DILATION_PATTERNS):
        o, lse = _attn_branch(qs[grp], ks[grp], vs[grp], bsz, seq, window, dilation)
        outs.append(o)
        lses.append(lse)
    h = _merge(outs, lses, h, b_w_o[0], ln_g[1, 0], ln_b[1, 0])
    h, = moe(h, 1, False)
    return h.reshape(bsz, seq, d)
```

```python
import functools
import math

import jax
import jax.numpy as jnp
from jax import lax
from jax.experimental import pallas as pl
from jax.experimental.pallas import tpu as pltpu

F32 = jnp.float32
BF16 = jnp.bfloat16
I32 = jnp.int32

D_MODEL = 1024
LRU_BLOCKS = 4
CONV_WIDTH = 4
LRU_C = 8.0
ATT_HEADS = 8
HEAD_DIM = 64
DILATION_PATTERNS = ((128, 1), (512, 4), (2048, 16))
N_GROUPS = len(DILATION_PATTERNS)
ATT_BLOCK = 128
O_WIDTH = ATT_HEADS * HEAD_DIM
N_EXPERTS = 256
TOP_K = 8
N_EXPERT_GROUPS = 8
TOPK_GROUPS = 4
D_EXPERT = 256
ROUTED_SCALE = 2.5
DEPTH = 2
DEEPNORM_ALPHA = (2 * DEPTH) ** 0.25
LN_EPS = 1e-5
MASK_VALUE = -1e30

SUBLANES = 8
LANES = 128
TOK_ROWS = D_MODEL // LANES
PK_ROWS = D_MODEL // 2 // LANES
PAIR_SLABS = ATT_HEADS * HEAD_DIM // LANES // 2
VMEM_LIMIT = 56 * 1024 * 1024

RGLRU_TS = 256
ROUTER_TT = 256
SORTED_ROWS_TL = 2048
DISPATCH_TT = 256
EXPERT_TB = 512
COMBINE_TT = 256
PROJ_TT = 512
MERGE_TT = 256


def _rows_load(ref, n_tok):
    return jnp.concatenate(
        [ref[pl.ds(s, n_tok, stride=TOK_ROWS), :] for s in range(TOK_ROWS)], axis=1)


def _rows_store(ref, val, n_tok):
    for s in range(TOK_ROWS):
        ref[pl.ds(s, n_tok, stride=TOK_ROWS), :] = val[:, s * LANES:(s + 1) * LANES]


def _layer_norm(z, g, b):
    mu = jnp.mean(z, axis=-1, keepdims=True)
    zc = z - mu
    var = jnp.mean(zc * zc, axis=-1, keepdims=True)
    return zc * lax.rsqrt(var + LN_EPS) * g + b


def _silu(x):
    return x * jax.nn.sigmoid(x)


def _gelu_tanh(x):
    c = math.sqrt(2.0 / math.pi)
    return 0.5 * x * (1.0 + jnp.tanh(c * (x + 0.044715 * (x * x * x))))


def _bdot(a, b):
    return jnp.dot(a, b, preferred_element_type=F32)


def _rglru_body(x_ref, win_ref, cw_ref, cb_ref, wga_ref, bga_ref, wgx_ref, bgx_ref, lam_ref,
                wout_ref, g_ref, b_ref, o_ref, hc_ref, tail_ref, *, ts):
    width = D_MODEL
    bw = width // LRU_BLOCKS

    @pl.when(pl.program_id(1) == 0)
    def _():
        hc_ref[...] = jnp.zeros_like(hc_ref)
        tail_ref[...] = jnp.zeros_like(tail_ref)

    x = x_ref[...]
    xz = _bdot(x.astype(BF16), win_ref[...])
    xr = xz[:, :width]
    gate = xz[:, width:]

    tail = tail_ref[...]
    row8 = lax.broadcasted_iota(jnp.int32, (SUBLANES, width), 0)
    cw = cw_ref[...]
    xc = xr * cw[CONV_WIDTH - 1:CONV_WIDTH, :] + cb_ref[...]
    for j in range(1, CONV_WIDTH):
        rx = pltpu.roll(xr, j, 0)
        rp = pltpu.roll(tail, j, 0)
        top = jnp.where(row8 < j, rp, rx[:SUBLANES])
        shifted = jnp.concatenate([top, rx[SUBLANES:]], axis=0)
        xc = xc + shifted * cw[CONV_WIDTH - 1 - j:CONV_WIDTH - j, :]
    tail_ref[...] = xr[ts - SUBLANES:]

    xcb = xc.astype(BF16)

    def block_diag(w_ref):
        return jnp.concatenate(
            [_bdot(xcb[:, n * bw:(n + 1) * bw], w_ref[n]) for n in range(LRU_BLOCKS)], axis=1)

    r = jax.nn.sigmoid(block_diag(wga_ref) + bga_ref[...])
    i = jax.nn.sigmoid(block_diag(wgx_ref) + bgx_ref[...])
    lam = lam_ref[...]
    softplus_neg_lam = jnp.maximum(-lam, 0.0) + jnp.log1p(jnp.exp(-jnp.abs(lam)))
    log_a = (-LRU_C * r) * softplus_neg_lam
    a = jnp.exp(log_a)
    mult = jnp.sqrt(-jnp.tanh(log_a) * (a * a + 1.0))
    u = mult * (i * xc)

    groups = ts // SUBLANES
    a3 = a.reshape(groups, SUBLANES, width)
    u3 = u.reshape(groups, SUBLANES, width)
    sub = lax.broadcasted_iota(jnp.int32, (groups, SUBLANES, width), 1)
    sh = 1
    while sh < SUBLANES:
        a_prev = pltpu.roll(a3, sh, 1)
        u_prev = pltpu.roll(u3, sh, 1)
        live = sub >= sh
        u3 = jnp.where(live, a3 * u_prev, 0.0) + u3
        a3 = jnp.where(live, a3 * a_prev, a3)
        sh *= 2
    carry = hc_ref[...]
    h_groups = []
    for grp in range(groups):
        h_grp = a3[grp] * carry + u3[grp]
        h_groups.append(h_grp)
        carry = h_grp[SUBLANES - 1:SUBLANES]
    h = jnp.concatenate(h_groups, axis=0)
    hc_ref[...] = carry

    y = (h * _gelu_tanh(gate)).astype(BF16)
    mix = _bdot(y, wout_ref[...])
    z = DEEPNORM_ALPHA * x + mix
    _rows_store(o_ref, _layer_norm(z, g_ref[...], b_ref[...]), ts)


def _rglru_layer(x, w_in, conv_w, conv_b, wga, bga, wgx, bgx, lam, w_out, g, b):
    bsz, seq, d = x.shape
    ts = RGLRU_TS
    ns = seq // ts
    row = lambda v: v.reshape(1, -1)
    full = lambda shape: pl.BlockSpec(shape, lambda bi, si: (0,) * len(shape))
    return pl.pallas_call(
        functools.partial(_rglru_body, ts=ts),
        grid=(bsz, ns),
        in_specs=[
            pl.BlockSpec((None, ts, d), lambda bi, si: (bi, si, 0)),
            full((d, 2 * d)), full((CONV_WIDTH, d)), full((1, d)),
            full((LRU_BLOCKS, d // LRU_BLOCKS, d // LRU_BLOCKS)), full((1, d)),
            full((LRU_BLOCKS, d // LRU_BLOCKS, d // LRU_BLOCKS)), full((1, d)),
            full((1, d)), full((d, d)), full((1, d)), full((1, d)),
        ],
        out_specs=pl.BlockSpec((ts * TOK_ROWS, LANES), lambda bi, si: (bi * ns + si, 0)),
        out_shape=jax.ShapeDtypeStruct((bsz * seq * TOK_ROWS, LANES), F32),
        scratch_shapes=[pltpu.VMEM((1, d), F32), pltpu.VMEM((SUBLANES, d), F32)],
        compiler_params=pltpu.CompilerParams(
            dimension_semantics=("arbitrary", "arbitrary"), vmem_limit_bytes=VMEM_LIMIT),
        name="rglru_layer",
    )(x, w_in.astype(BF16), conv_w, row(conv_b), wga.astype(BF16), row(bga), wgx.astype(BF16),
      row(bgx), row(lam), w_out.astype(BF16), row(g), row(b))


def _router_body(h_ref, wrt_ref, bias_ref, idx_ref, gate_ref, rank_ref, cnt_ref, carry_ref, *, tt):
    n_e = N_EXPERTS
    per_group = n_e // N_EXPERT_GROUPS

    @pl.when(pl.program_id(0) == 0)
    def _():
        carry_ref[...] = jnp.zeros_like(carry_ref)

    h = _rows_load(h_ref, tt)
    logits = lax.dot_general(wrt_ref[...], h, (((1,), (1,)), ((), ())),
                             precision=lax.Precision.HIGHEST, preferred_element_type=F32)
    scores = jax.nn.sigmoid(logits)
    biased = scores + bias_ref[...]

    j_iota = lax.broadcasted_iota(jnp.int32, (per_group, tt), 0)
    group_score = []
    for g in range(N_EXPERT_GROUPS):
        bg = biased[g * per_group:(g + 1) * per_group]
        m1 = jnp.max(bg, axis=0, keepdims=True)
        i1 = jnp.min(jnp.where(bg == m1, j_iota, per_group), axis=0, keepdims=True)
        m2 = jnp.max(jnp.where(j_iota == i1, -jnp.inf, bg), axis=0, keepdims=True)
        group_score.append(m1 + m2)

    masked = []
    for g in range(N_EXPERT_GROUPS):
        beaten_by = jnp.zeros((1, tt), jnp.int32)
        for o in range(N_EXPERT_GROUPS):
            if o == g:
                continue
            wins = group_score[o] > group_score[g]
            if o < g:
                wins = wins | (group_score[o] == group_score[g])
            beaten_by = beaten_by + wins.astype(jnp.int32)
        keep = beaten_by < TOPK_GROUPS
        masked.append(jnp.where(keep, biased[g * per_group:(g + 1) * per_group], MASK_VALUE))
    cur = jnp.concatenate(masked, axis=0)

    e_iota = lax.broadcasted_iota(jnp.int32, (n_e, tt), 0)
    idx_rows, score_rows, sels = [], [], []
    for _ in range(TOP_K):
        m = jnp.max(cur, axis=0, keepdims=True)
        ik = jnp.min(jnp.where(cur == m, e_iota, n_e), axis=0, keepdims=True)
        sel = e_iota == ik
        score_rows.append(jnp.sum(jnp.where(sel, scores, 0.0), axis=0, keepdims=True))
        cur = jnp.where(sel, -jnp.inf, cur)
        idx_rows.append(ik)
        sels.append(sel)
    top_s = jnp.concatenate(score_rows, axis=0)
    gate_ref[...] = top_s / jnp.sum(top_s, axis=0, keepdims=True) * ROUTED_SCALE
    idx_ref[...] = jnp.concatenate(idx_rows, axis=0)
    multi_hot = jnp.where(cur == -jnp.inf, 1.0, 0.0)

    t_row = lax.broadcasted_iota(jnp.int32, (tt, tt), 0)
    t_col = lax.broadcasted_iota(jnp.int32, (tt, tt), 1)
    strict_upper = jnp.where(t_row < t_col, 1.0, 0.0).astype(BF16)
    before = _bdot(multi_hot.astype(BF16), strict_upper) + carry_ref[...]
    rank_rows = [jnp.sum(jnp.where(sel, before, 0.0), axis=0, keepdims=True) for sel in sels]
    rank_ref[...] = jnp.concatenate(rank_rows, axis=0).astype(jnp.int32)
    carry = carry_ref[...] + jnp.sum(multi_hot, axis=1, keepdims=True)
    carry_ref[...] = carry
    cnt_ref[...] = carry.astype(jnp.int32)


def _router(h_tiles, w_router, router_bias):
    n_tok = h_tiles.shape[0] // TOK_ROWS
    tt = ROUTER_TT
    kt = lambda dt: jax.ShapeDtypeStruct((TOP_K, n_tok), dt)
    tok_spec = pl.BlockSpec((TOP_K, tt), lambda i: (0, i))
    return pl.pallas_call(
        functools.partial(_router_body, tt=tt),
        grid=(n_tok // tt,),
        in_specs=[
            pl.BlockSpec((tt * TOK_ROWS, LANES), lambda i: (i, 0)),
            pl.BlockSpec((N_EXPERTS, D_MODEL), lambda i: (0, 0)),
            pl.BlockSpec((N_EXPERTS, 1), lambda i: (0, 0)),
        ],
        out_specs=[tok_spec, tok_spec, tok_spec, pl.BlockSpec((N_EXPERTS, 1), lambda i: (0, 0))],
        out_shape=[kt(jnp.int32), kt(F32), kt(jnp.int32),
                   jax.ShapeDtypeStruct((N_EXPERTS, 1), jnp.int32)],
        scratch_shapes=[pltpu.VMEM((N_EXPERTS, 1), F32)],
        compiler_params=pltpu.CompilerParams(
            dimension_semantics=("arbitrary",), vmem_limit_bytes=VMEM_LIMIT),
        name="moe_router",
    )(h_tiles, w_router.T, router_bias.reshape(N_EXPERTS, 1))


def _pack_rows(x):
    half = D_MODEL // 2
    return pltpu.pack_elementwise([x[:, :half], x[:, half:]], packed_dtype=BF16)


def _unpack_rows(p):
    return jnp.concatenate(
        [pltpu.unpack_elementwise(p, index=i, packed_dtype=BF16, unpacked_dtype=F32) for i in range(2)],
        axis=1)


def _packed_load(ref, n_tok):
    return jnp.concatenate(
        [ref[pl.ds(s, n_tok, stride=PK_ROWS), :] for s in range(PK_ROWS)], axis=1)


def _packed_store(ref, val, n_tok):
    for s in range(PK_ROWS):
        ref[pl.ds(s, n_tok, stride=PK_ROWS), :] = val[:, s * LANES:(s + 1) * LANES]


def _sorted_rows_body(idx_ref, rank_ref, start_ref, o_ref):
    tl = idx_ref.shape[1]
    e_iota = lax.broadcasted_iota(jnp.int32, (N_EXPERTS, tl), 0)
    start = start_ref[...]
    rows = []
    for k in range(TOP_K):
        hit = e_iota == idx_ref[k:k + 1, :]
        base = jnp.sum(jnp.where(hit, start, 0.0), axis=0, keepdims=True)
        rows.append(base.astype(jnp.int32) + rank_ref[k:k + 1, :])
    o_ref[...] = jnp.concatenate(rows, axis=0)


def _sorted_rows(top_idx, rank, pad_start):
    n_tok = top_idx.shape[1]
    tl = SORTED_ROWS_TL
    tok = pl.BlockSpec((TOP_K, tl), lambda i: (0, i))
    return pl.pallas_call(
        _sorted_rows_body,
        grid=(n_tok // tl,),
        in_specs=[tok, tok, pl.BlockSpec((N_EXPERTS, 1), lambda i: (0, 0))],
        out_specs=tok,
        out_shape=jax.ShapeDtypeStruct((TOP_K, n_tok), jnp.int32),
        compiler_params=pltpu.CompilerParams(
            dimension_semantics=("arbitrary",), vmem_limit_bytes=VMEM_LIMIT),
        name="moe_sorted_rows",
    )(top_idx, rank, pad_start.astype(F32).reshape(N_EXPERTS, 1))


def _dispatch_body(start_ref, cnt_ref, dest_ref, h_ref, xs_ref, pk_ref, zero_ref, sem, zero_sem,
                   *, tt, tb):
    @pl.when(pl.program_id(0) == 0)
    def _():
        zero_ref[...] = jnp.zeros_like(zero_ref)

        def pad_copies(e, fn):
            n_pad = ((cnt_ref[e] + (tb - 1)) & (-tb)) - cnt_ref[e]
            row = start_ref[e] + cnt_ref[e]
            for bit in range(tb.bit_length() - 1):
                size = 1 << bit
                has_bit = ((n_pad >> bit) & 1) == 1

                @pl.when(has_bit)
                def _():
                    fn(pltpu.make_async_copy(zero_ref.at[pl.ds(0, size)],
                                             xs_ref.at[pl.ds(row, size)], zero_sem))
                row = row + jnp.where(has_bit, size, 0)

        def start_all(e, c):
            pad_copies(e, lambda cp: cp.start())
            return c

        def wait_all(e, c):
            pad_copies(e, lambda cp: cp.wait())
            return c

        lax.fori_loop(0, N_EXPERTS, start_all, 0)
        lax.fori_loop(0, N_EXPERTS, wait_all, 0)

    packed = _pack_rows(_rows_load(h_ref, tt))
    for s in range(PK_ROWS):
        pk_ref[:, s, :] = packed[:, s * LANES:(s + 1) * LANES]
    for t in range(tt):
        for k in range(TOP_K):
            pltpu.make_async_copy(pk_ref.at[t], xs_ref.at[dest_ref[k, t]], sem).start(priority=k % 2)
    for k in range(TOP_K):
        pltpu.make_async_copy(pk_ref, xs_ref.at[pl.ds(0, tt)], sem).wait()


def _dispatch(h_tiles, dest_rows, pad_start, counts, n_rows):
    n_tok = h_tiles.shape[0] // TOK_ROWS
    tt = DISPATCH_TT
    grid_spec = pltpu.PrefetchScalarGridSpec(
        num_scalar_prefetch=2,
        grid=(n_tok // tt,),
        in_specs=[pl.BlockSpec((TOP_K, tt), lambda i, s, c: (0, i), memory_space=pltpu.SMEM),
                  pl.BlockSpec((tt * TOK_ROWS, LANES), lambda i, s, c: (i, 0))],
        out_specs=pl.BlockSpec(memory_space=pl.ANY),
        scratch_shapes=[pltpu.VMEM((tt, PK_ROWS, LANES), I32),
                        pltpu.VMEM((EXPERT_TB // 2, PK_ROWS, LANES), I32),
                        pltpu.SemaphoreType.DMA(()), pltpu.SemaphoreType.DMA(())],
    )
    return pl.pallas_call(
        functools.partial(_dispatch_body, tt=tt, tb=EXPERT_TB),
        grid_spec=grid_spec,
        out_shape=jax.ShapeDtypeStruct((n_rows, PK_ROWS, LANES), I32),
        compiler_params=pltpu.CompilerParams(
            dimension_semantics=("arbitrary",), vmem_limit_bytes=VMEM_LIMIT),
        name="moe_dispatch",
    )(pad_start, counts, dest_rows, h_tiles)


def _expert_body(be_ref, first_ref, slot_ref, next_ref, na_ref, xs_ref, w1_ref, w3_ref, w2_ref, ys_ref,
                 w13_buf, w2_buf, w1b, w3b, w2b, sem, *, tb, layer):
    i = pl.program_id(0)

    def weight_copies(expert, slot):
        return (pltpu.make_async_copy(w1_ref.at[layer, expert], w13_buf.at[slot, 0], sem.at[slot]),
                pltpu.make_async_copy(w3_ref.at[layer, expert], w13_buf.at[slot, 1], sem.at[slot]),
                pltpu.make_async_copy(w2_ref.at[layer, expert], w2_buf.at[slot], sem.at[slot]))

    @pl.when(i < na_ref[0])
    def _():
        @pl.when(first_ref[i] == 1)
        def _():
            slot = slot_ref[i]

            @pl.when(i == 0)
            def _():
                for cp in weight_copies(be_ref[0], slot):
                    cp.start()

            for cp in weight_copies(be_ref[i], slot):
                cp.wait()

            @pl.when(next_ref[i] >= 0)
            def _():
                for cp in weight_copies(next_ref[i], 1 - slot):
                    cp.start()

            w1b[...] = w13_buf[slot, 0].astype(BF16)
            w3b[...] = w13_buf[slot, 1].astype(BF16)
            w2b[...] = w2_buf[slot].astype(BF16)

        x = _unpack_rows(_packed_load(xs_ref, tb)).astype(BF16)
        mid = (_silu(_bdot(x, w1b[...])) * _bdot(x, w3b[...])).astype(BF16)
        _packed_store(ys_ref, _pack_rows(_bdot(mid, w2b[...])), tb)


def _experts(xs, block_expert, first, slot, next_expert, n_active, w1, w3, w2, layer):
    tb = EXPERT_TB
    n_blocks = xs.shape[0] // (tb * PK_ROWS)
    row_block = lambda i, be, fi, sl, nx, na: (jnp.minimum(i, na[0] - 1), 0)
    grid_spec = pltpu.PrefetchScalarGridSpec(
        num_scalar_prefetch=5,
        grid=(n_blocks,),
        in_specs=[
            pl.BlockSpec((tb * PK_ROWS, LANES), row_block),
            pl.BlockSpec(memory_space=pl.ANY), pl.BlockSpec(memory_space=pl.ANY),
            pl.BlockSpec(memory_space=pl.ANY),
        ],
        out_specs=pl.BlockSpec((tb * PK_ROWS, LANES), row_block),
        scratch_shapes=[pltpu.VMEM((2, 2, D_MODEL, D_EXPERT), F32),
                        pltpu.VMEM((2, D_EXPERT, D_MODEL), F32),
                        pltpu.VMEM((D_MODEL, D_EXPERT), BF16), pltpu.VMEM((D_MODEL, D_EXPERT), BF16),
                        pltpu.VMEM((D_EXPERT, D_MODEL), BF16), pltpu.SemaphoreType.DMA((2,))],
    )
    return pl.pallas_call(
        functools.partial(_expert_body, tb=tb, layer=layer),
        grid_spec=grid_spec,
        out_shape=jax.ShapeDtypeStruct(xs.shape, I32),
        compiler_params=pltpu.CompilerParams(
            dimension_semantics=("arbitrary",), vmem_limit_bytes=VMEM_LIMIT),
        name="moe_experts",
    )(block_expert, first, slot, next_expert, n_active, xs, w1, w3, w2)


def _store_head_slabs(y, o_refs, scales):
    for j, (o_ref, scale) in enumerate(zip(o_refs, scales)):
        for s in range(PAIR_SLABS):
            lo = j * O_WIDTH + 2 * s * LANES
            o_ref[s] = pltpu.pack_elementwise(
                [y[:, lo:lo + LANES] * scale, y[:, lo + LANES:lo + 2 * LANES] * scale],
                packed_dtype=BF16)


def _combine_body(src_ref, h_ref, gates_ref, ys_ref, ys_flat_ref, ws1_ref, ws3_ref, ws2_ref,
                  g_ref, b_ref, o_ref, gbuf, sem, *, tt, n_tiles, token_tile_out):
    j = pl.program_id(0)

    def step(issue, finish):
        nxt = j % 2
        cur = (j - 1) % 2
        if finish:
            h = _rows_load(h_ref, tt)
            hb = h.astype(BF16)
            mid = (_silu(_bdot(hb, ws1_ref[...])) * _bdot(hb, ws3_ref[...])).astype(BF16)
            routed = _bdot(mid, ws2_ref[...])
            gates = gates_ref[...]
        for k in range(TOP_K):
            if issue:
                for t in range(tt):
                    pltpu.make_async_copy(ys_ref.at[src_ref[k, t]],
                                          gbuf.at[nxt, k, pl.ds(t * PK_ROWS, PK_ROWS), :],
                                          sem.at[nxt]).start(priority=t % 2)
            if finish:
                pltpu.make_async_copy(ys_flat_ref.at[pl.ds(0, tt * PK_ROWS), :], gbuf.at[cur, k],
                                      sem.at[cur]).wait()
                routed = routed + gates[:, k:k + 1] * _unpack_rows(_packed_load(gbuf.at[cur, k], tt))
        if finish:
            out = _layer_norm(DEEPNORM_ALPHA * h + routed, g_ref[...], b_ref[...])
            if token_tile_out:
                _rows_store(o_ref, out, tt)
            else:
                o_ref[...] = out

    pl.when(j == 0)(lambda: step(True, False))
    pl.when((j >= 1) & (j < n_tiles))(lambda: step(True, True))
    pl.when(j == n_tiles)(lambda: step(False, True))


def _combine(h_tiles, src_rows, gates_tk, ys, ws1, ws3, ws2, g, b, token_tile_out):
    n_tok = h_tiles.shape[0] // TOK_ROWS
    tt = COMBINE_TT
    n_tiles = n_tok // tt
    prev_tile = lambda j: (jnp.maximum(j - 1, 0), 0)
    full = lambda shape: pl.BlockSpec(shape, lambda j: (0,) * len(shape))
    if token_tile_out:
        out_spec = pl.BlockSpec((tt * TOK_ROWS, LANES), prev_tile)
        out_shape = jax.ShapeDtypeStruct((n_tok * TOK_ROWS, LANES), F32)
    else:
        out_spec = pl.BlockSpec((tt, D_MODEL), prev_tile)
        out_shape = jax.ShapeDtypeStruct((n_tok, D_MODEL), F32)
    n_rows = ys.shape[0] // PK_ROWS
    return pl.pallas_call(
        functools.partial(_combine_body, tt=tt, n_tiles=n_tiles, token_tile_out=token_tile_out),
        grid=(n_tiles + 1,),
        in_specs=[pl.BlockSpec((TOP_K, tt), lambda j: (0, jnp.minimum(j, n_tiles - 1)),
                               memory_space=pltpu.SMEM),
                  pl.BlockSpec((tt * TOK_ROWS, LANES), prev_tile),
                  pl.BlockSpec((tt, TOP_K), prev_tile),
                  pl.BlockSpec(memory_space=pl.ANY), pl.BlockSpec(memory_space=pl.ANY),
                  full((D_MODEL, D_EXPERT)), full((D_MODEL, D_EXPERT)), full((D_EXPERT, D_MODEL)),
                  full((1, D_MODEL)), full((1, D_MODEL))],
        out_specs=out_spec,
        out_shape=out_shape,
        scratch_shapes=[pltpu.VMEM((2, TOP_K, tt * PK_ROWS, LANES), I32),
                        pltpu.SemaphoreType.DMA((2,))],
        compiler_params=pltpu.CompilerParams(
            dimension_semantics=("arbitrary",), vmem_limit_bytes=VMEM_LIMIT),
        name="moe_combine",
    )(src_rows, h_tiles, gates_tk, ys.reshape(n_rows, PK_ROWS, LANES), ys, ws1.astype(BF16),
      ws3.astype(BF16), ws2.astype(BF16), g.reshape(1, -1), b.reshape(1, -1))


def _moe_layer(h_tiles, w_router, router_bias, w1, w3, w2, layer, ws1, ws3, ws2, g, b,
               token_tile_out):
    n_tok = h_tiles.shape[0] // TOK_ROWS
    tb = EXPERT_TB
    top_idx, gates, rank, counts = _router(h_tiles, w_router, router_bias)

    n_blocks = -(-(n_tok * TOP_K + N_EXPERTS * (tb - 1)) // tb)
    counts = counts.reshape(N_EXPERTS)
    padded = ((counts + tb - 1) // tb) * tb
    pad_end = jnp.cumsum(padded)
    pad_start = (pad_end - padded).astype(jnp.int32)
    block_first_row = jnp.arange(n_blocks, dtype=jnp.int32) * tb
    block_expert = jnp.minimum(
        jnp.sum((pad_end[None, :] <= block_first_row[:, None]).astype(jnp.int32), axis=1),
        N_EXPERTS - 1)
    n_active = (pad_end[-1:] // tb).astype(jnp.int32)
    present = counts > 0
    expert_ids = jnp.arange(N_EXPERTS, dtype=jnp.int32)
    later = lax.cummin(jnp.where(present, expert_ids, N_EXPERTS), reverse=True)
    next_present = jnp.concatenate([later[1:], jnp.full((1,), N_EXPERTS, jnp.int32)])
    next_present = jnp.where(next_present < N_EXPERTS, next_present, -1)
    ordinal = jnp.cumsum(present.astype(jnp.int32)) - 1
    first = jnp.concatenate([jnp.ones((1,), jnp.int32),
                             (block_expert[1:] != block_expert[:-1]).astype(jnp.int32)])
    slot = ordinal[block_expert] % 2
    next_expert = next_present[block_expert]

    rows = _sorted_rows(top_idx, rank, pad_start)
    xs = _dispatch(h_tiles, rows, pad_start, counts, n_blocks * tb)
    ys = _experts(xs.reshape(n_blocks * tb * PK_ROWS, LANES), block_expert, first, slot, next_expert,
                  n_active, w1, w3, w2, layer)
    return _combine(h_tiles, rows, gates.T, ys, ws1, ws3, ws2, g, b, token_tile_out)


def _proj_body(h_ref, w_ref, *o_refs, tt, scale):
    y = _bdot(_rows_load(h_ref, tt).astype(BF16), w_ref[...])
    _store_head_slabs(y, o_refs, (scale,) * len(o_refs))


def _project(h_tiles, w, scale):
    n_tok = h_tiles.shape[0] // TOK_ROWS
    tt = PROJ_TT
    n_out = w.shape[1] // O_WIDTH
    out_spec = pl.BlockSpec((PAIR_SLABS, tt, LANES), lambda i: (0, i, 0))
    return pl.pallas_call(
        functools.partial(_proj_body, tt=tt, scale=scale),
        grid=(n_tok // tt,),
        in_specs=[pl.BlockSpec((tt * TOK_ROWS, LANES), lambda i: (i, 0)),
                  pl.BlockSpec(w.shape, lambda i: (0, 0))],
        out_specs=[out_spec] * n_out,
        out_shape=[jax.ShapeDtypeStruct((PAIR_SLABS, n_tok, LANES), I32)] * n_out,
        compiler_params=pltpu.CompilerParams(
            dimension_semantics=("arbitrary",), vmem_limit_bytes=VMEM_LIMIT),
        name="projection",
    )(h_tiles, w.astype(BF16))


def _attn_body(q_ref, kp_ref, kc_ref, vp_ref, vc_ref, o_ref, lse_ref, *, dilation, n_steps):
    t = ATT_BLOCK
    not_first = pl.program_id(1) > 0
    qi = lax.broadcasted_iota(jnp.int32, (t, 2 * t), 0)
    kj = lax.broadcasted_iota(jnp.int32, (t, 2 * t), 1)
    dist = t + qi - kj
    valid = (dist >= 0) & (dist <= n_steps) & (not_first | (kj >= t))
    token_dist = (dilation * dist).astype(F32)
    low_half = lax.broadcasted_iota(jnp.int32, (t, LANES), 1) < HEAD_DIM
    contract_last = (((1,), (1,)), ((), ()))

    def unpack(words, which):
        return pltpu.unpack_elementwise(words, index=which, packed_dtype=BF16,
                                        unpacked_dtype=F32).astype(BF16)

    def residue(r, carry):
        rows = pl.ds(r, t) if dilation == 1 else pl.ds(r, t, stride=dilation)
        for slab in range(PAIR_SLABS):
            q_words = q_ref.at[slab][rows, :]
            k_words = jnp.concatenate([kp_ref.at[slab][rows, :], kc_ref.at[slab][rows, :]], axis=0)
            v_words = jnp.concatenate([vp_ref.at[slab][rows, :], vc_ref.at[slab][rows, :]], axis=0)
            for which in range(2):
                pair = 2 * slab + which
                q2, k2, v2 = unpack(q_words, which), unpack(k_words, which), unpack(v_words, which)
                outs, lses = [], []
                for half in range(2):
                    head = 2 * pair + half
                    slope = 2.0 ** (-8.0 * (head + 1) / ATT_HEADS)
                    keep = low_half if half == 0 else ~low_half
                    qh = jnp.where(keep, q2, jnp.zeros_like(q2))
                    s = lax.dot_general(qh, k2, contract_last, preferred_element_type=F32)
                    s = jnp.where(valid, s - slope * token_dist, MASK_VALUE)
                    m = jnp.max(s, axis=-1, keepdims=True)
                    p = jnp.exp(s - m)
                    l = jnp.sum(p, axis=-1, keepdims=True)
                    outs.append(_bdot(p.astype(BF16), v2) / l)
                    lses.append(m + jnp.log(l))
                o_ref.at[pair][rows, :] = jnp.where(low_half, outs[0], outs[1])
                lse_ref.at[pair][rows, :] = jnp.where(low_half, lses[0], lses[1])
        return carry

    if dilation == 1:
        residue(0, 0)
    else:
        lax.fori_loop(0, dilation, residue, 0)


def _attn_branch(q, k, v, bsz, seq, window, dilation):
    span = ATT_BLOCK * dilation
    n_spans = seq // span
    pairs = ATT_HEADS // 2
    cur = lambda slabs: pl.BlockSpec((slabs, span, LANES), lambda bi, n: (0, bi * n_spans + n, 0))
    prev = pl.BlockSpec((PAIR_SLABS, span, LANES),
                        lambda bi, n: (0, bi * n_spans + jnp.maximum(n - 1, 0), 0))
    return pl.pallas_call(
        functools.partial(_attn_body, dilation=dilation, n_steps=window // dilation),
        grid=(bsz, n_spans),
        in_specs=[cur(PAIR_SLABS), prev, cur(PAIR_SLABS), prev, cur(PAIR_SLABS)],
        out_specs=[cur(pairs), cur(pairs)],
        out_shape=[jax.ShapeDtypeStruct((pairs, bsz * seq, LANES), F32)] * 2,
        compiler_params=pltpu.CompilerParams(
            dimension_semantics=("arbitrary",) * 2, vmem_limit_bytes=VMEM_LIMIT),
        name=f"dilated_attn_d{dilation}",
    )(q, k, k, v, v)


def _merge_body(*refs, tt):
    o_refs = refs[:N_GROUPS]
    lse_refs = refs[N_GROUPS:2 * N_GROUPS]
    h_ref, wo_ref, g_ref, b_ref, out_ref = refs[2 * N_GROUPS:]
    wide = lambda r: jnp.concatenate([r[c] for c in range(ATT_HEADS // 2)], axis=1)
    lses = [wide(r) for r in lse_refs]
    m = functools.reduce(jnp.maximum, lses)
    ws = [jnp.exp(l - m) for l in lses]
    den = functools.reduce(lambda a, c: a + c, ws)
    o = functools.reduce(lambda a, c: a + c, [w * wide(r) for w, r in zip(ws, o_refs)]) / den
    mix = _bdot(o.astype(BF16), wo_ref[...])
    z = DEEPNORM_ALPHA * _rows_load(h_ref, tt) + mix
    _rows_store(out_ref, _layer_norm(z, g_ref[...], b_ref[...]), tt)


def _merge(outs, lses, h_tiles, w_o, g, b):
    n_tok = h_tiles.shape[0] // TOK_ROWS
    tt = MERGE_TT
    tok = pl.BlockSpec((ATT_HEADS // 2, tt, LANES), lambda i: (0, i, 0))
    tiles = pl.BlockSpec((tt * TOK_ROWS, LANES), lambda i: (i, 0))
    full = lambda shape: pl.BlockSpec(shape, lambda i: (0,) * len(shape))
    return pl.pallas_call(
        functools.partial(_merge_body, tt=tt),
        grid=(n_tok // tt,),
        in_specs=[tok] * (2 * N_GROUPS) + [tiles, full((O_WIDTH, D_MODEL)), full((1, D_MODEL)),
                                           full((1, D_MODEL))],
        out_specs=tiles,
        out_shape=jax.ShapeDtypeStruct(h_tiles.shape, F32),
        compiler_params=pltpu.CompilerParams(
            dimension_semantics=("arbitrary",), vmem_limit_bytes=VMEM_LIMIT),
        name="attn_merge",
    )(*outs, *lses, h_tiles, w_o.astype(BF16), g.reshape(1, -1), b.reshape(1, -1))


def kernel(x, a_w_in, a_conv_w, a_conv_b, a_w_gate_a, a_b_gate_a, a_w_gate_x, a_b_gate_x, a_lambda, a_w_out, w_kv_shared, b_w_q, b_w_o, moe_w_router, moe_router_bias, moe_w1, moe_w3, moe_w2, moe_ws1, moe_ws3, moe_ws2, ln_g, ln_b):
    bsz, seq, d = x.shape
    assert d == D_MODEL and seq % (DILATION_PATTERNS[-1][1] * ATT_BLOCK) == 0

    def moe(h_tiles, layer, token_tile_out):
        return _moe_layer(h_tiles, moe_w_router[layer], moe_router_bias[layer], moe_w1, moe_w3,
                          moe_w2, layer, moe_ws1[layer], moe_ws3[layer], moe_ws2[layer],
                          ln_g[layer, 1], ln_b[layer, 1], token_tile_out)

    h = _rglru_layer(x, a_w_in[0], a_conv_w[0], a_conv_b[0], a_w_gate_a[0], a_b_gate_a[0],
                     a_w_gate_x[0], a_b_gate_x[0], a_lambda[0], a_w_out[0], ln_g[0, 0], ln_b[0, 0])
    h = moe(h, 0, True)
    kv = _project(h, w_kv_shared, 1.0)
    ks, vs = kv[:N_GROUPS], kv[N_GROUPS:]

    qs = _project(h, b_w_q[0], HEAD_DIM ** -0.5)
    outs, lses = [], []
    for grp, (window, dilation) in enumerate(DILATION_PATTERNS):
        o, lse = _attn_branch(qs[grp], ks[grp], vs[grp], bsz, seq, window, dilation)
        outs.append(o)
        lses.append(lse)
    h = _merge(outs, lses, h, b_w_o[0], ln_g[1, 0], ln_b[1, 0])
    h = moe(h, 1, False)
    return h.reshape(bsz, seq, d)
```

```python
import functools
import math

import jax
import jax.numpy as jnp
from jax import lax
from jax.experimental import pallas as pl
from jax.experimental.pallas import tpu as pltpu

F32 = jnp.float32
BF16 = jnp.bfloat16
I32 = jnp.int32

D_MODEL = 1024
LRU_BLOCKS = 4
CONV_WIDTH = 4
LRU_C = 8.0
ATT_HEADS = 8
HEAD_DIM = 64
DILATION_PATTERNS = ((128, 1), (512, 4), (2048, 16))
N_GROUPS = len(DILATION_PATTERNS)
ATT_BLOCK = 128
O_WIDTH = ATT_HEADS * HEAD_DIM
N_EXPERTS = 256
TOP_K = 8
N_EXPERT_GROUPS = 8
TOPK_GROUPS = 4
D_EXPERT = 256
ROUTED_SCALE = 2.5
DEPTH = 2
DEEPNORM_ALPHA = (2 * DEPTH) ** 0.25
LN_EPS = 1e-5
MASK_VALUE = -1e30

SUBLANES = 8
LANES = 128
TOK_ROWS = D_MODEL // LANES
PK_ROWS = D_MODEL // 2 // LANES
PAIR_SLABS = ATT_HEADS * HEAD_DIM // LANES // 2
VMEM_LIMIT = 56 * 1024 * 1024

RGLRU_TS = 256
ROUTER_TT = 256
SORTED_ROWS_TL = 2048
DISPATCH_TT = 256
EXPERT_TB = 512
COMBINE_TT = 256
PROJ_TT = 512
MERGE_TT = 256


def _rows_load(ref, n_tok):
    return jnp.concatenate(
        [ref[pl.ds(s, n_tok, stride=TOK_ROWS), :] for s in range(TOK_ROWS)], axis=1)


def _rows_store(ref, val, n_tok):
    for s in range(TOK_ROWS):
        ref[pl.ds(s, n_tok, stride=TOK_ROWS), :] = val[:, s * LANES:(s + 1) * LANES]


def _layer_norm(z, g, b):
    mu = jnp.mean(z, axis=-1, keepdims=True)
    zc = z - mu
    var = jnp.mean(zc * zc, axis=-1, keepdims=True)
    return zc * lax.rsqrt(var + LN_EPS) * g + b


def _silu(x):
    return x * jax.nn.sigmoid(x)


def _gelu_tanh(x):
    c = math.sqrt(2.0 / math.pi)
    return 0.5 * x * (1.0 + jnp.tanh(c * (x + 0.044715 * (x * x * x))))


def _bdot(a, b):
    return jnp.dot(a, b, preferred_element_type=F32)


def _rglru_body(x_ref, win_ref, cw_ref, cb_ref, wga_ref, bga_ref, wgx_ref, bgx_ref, lam_ref,
                wout_ref, g_ref, b_ref, o_ref, hc_ref, tail_ref, *, ts):
    width = D_MODEL
    bw = width // LRU_BLOCKS

    @pl.when(pl.program_id(1) == 0)
    def _():
        hc_ref[...] = jnp.zeros_like(hc_ref)
        tail_ref[...] = jnp.zeros_like(tail_ref)

    x = x_ref[...]
    xz = _bdot(x.astype(BF16), win_ref[...])
    xr = xz[:, :width]
    gate = xz[:, width:]

    tail = tail_ref[...]
    row8 = lax.broadcasted_iota(jnp.int32, (SUBLANES, width), 0)
    cw = cw_ref[...]
    xc = xr * cw[CONV_WIDTH - 1:CONV_WIDTH, :] + cb_ref[...]
    for j in range(1, CONV_WIDTH):
        rx = pltpu.roll(xr, j, 0)
        rp = pltpu.roll(tail, j, 0)
        top = jnp.where(row8 < j, rp, rx[:SUBLANES])
        shifted = jnp.concatenate([top, rx[SUBLANES:]], axis=0)
        xc = xc + shifted * cw[CONV_WIDTH - 1 - j:CONV_WIDTH - j, :]
    tail_ref[...] = xr[ts - SUBLANES:]

    xcb = xc.astype(BF16)

    def block_diag(w_ref):
        return jnp.concatenate(
            [_bdot(xcb[:, n * bw:(n + 1) * bw], w_ref[n]) for n in range(LRU_BLOCKS)], axis=1)

    r = jax.nn.sigmoid(block_diag(wga_ref) + bga_ref[...])
    i = jax.nn.sigmoid(block_diag(wgx_ref) + bgx_ref[...])
    lam = lam_ref[...]
    softplus_neg_lam = jnp.maximum(-lam, 0.0) + jnp.log1p(jnp.exp(-jnp.abs(lam)))
    log_a = (-LRU_C * r) * softplus_neg_lam
    a = jnp.exp(log_a)
    mult = jnp.sqrt(-jnp.tanh(log_a) * (a * a + 1.0))
    u = mult * (i * xc)

    groups = ts // SUBLANES
    a3 = a.reshape(groups, SUBLANES, width)
    u3 = u.reshape(groups, SUBLANES, width)
    sub = lax.broadcasted_iota(jnp.int32, (groups, SUBLANES, width), 1)
    sh = 1
    while sh < SUBLANES:
        a_prev = pltpu.roll(a3, sh, 1)
        u_prev = pltpu.roll(u3, sh, 1)
        live = sub >= sh
        u3 = jnp.where(live, a3 * u_prev, 0.0) + u3
        a3 = jnp.where(live, a3 * a_prev, a3)
        sh *= 2
    carry = hc_ref[...]
    h_groups = []
    for grp in range(groups):
        h_grp = a3[grp] * carry + u3[grp]
        h_groups.append(h_grp)
        carry = h_grp[SUBLANES - 1:SUBLANES]
    h = jnp.concatenate(h_groups, axis=0)
    hc_ref[...] = carry

    y = (h * _gelu_tanh(gate)).astype(BF16)
    mix = _bdot(y, wout_ref[...])
    z = DEEPNORM_ALPHA * x + mix
    _rows_store(o_ref, _layer_norm(z, g_ref[...], b_ref[...]), ts)


def _rglru_layer(x, w_in, conv_w, conv_b, wga, bga, wgx, bgx, lam, w_out, g, b):
    bsz, seq, d = x.shape
    ts = RGLRU_TS
    ns = seq // ts
    row = lambda v: v.reshape(1, -1)
    full = lambda shape: pl.BlockSpec(shape, lambda bi, si: (0,) * len(shape))
    return pl.pallas_call(
        functools.partial(_rglru_body, ts=ts),
        grid=(bsz, ns),
        in_specs=[
            pl.BlockSpec((None, ts, d), lambda bi, si: (bi, si, 0)),
            full((d, 2 * d)), full((CONV_WIDTH, d)), full((1, d)),
            full((LRU_BLOCKS, d // LRU_BLOCKS, d // LRU_BLOCKS)), full((1, d)),
            full((LRU_BLOCKS, d // LRU_BLOCKS, d // LRU_BLOCKS)), full((1, d)),
            full((1, d)), full((d, d)), full((1, d)), full((1, d)),
        ],
        out_specs=pl.BlockSpec((ts * TOK_ROWS, LANES), lambda bi, si: (bi * ns + si, 0)),
        out_shape=jax.ShapeDtypeStruct((bsz * seq * TOK_ROWS, LANES), F32),
        scratch_shapes=[pltpu.VMEM((1, d), F32), pltpu.VMEM((SUBLANES, d), F32)],
        compiler_params=pltpu.CompilerParams(
            dimension_semantics=("arbitrary", "arbitrary"), vmem_limit_bytes=VMEM_LIMIT),
        name="rglru_layer",
    )(x, w_in.astype(BF16), conv_w, row(conv_b), wga.astype(BF16), row(bga), wgx.astype(BF16),
      row(bgx), row(lam), w_out.astype(BF16), row(g), row(b))


def _router_body(h_ref, wrt_ref, bias_ref, idx_ref, gate_ref, rank_ref, cnt_ref, carry_ref, *, tt):
    n_e = N_EXPERTS
    per_group = n_e // N_EXPERT_GROUPS

    @pl.when(pl.program_id(0) == 0)
    def _():
        carry_ref[...] = jnp.zeros_like(carry_ref)

    h = _rows_load(h_ref, tt)
    logits = lax.dot_general(wrt_ref[...], h.astype(BF16), (((1,), (1,)), ((), ())),
                             preferred_element_type=F32)
    scores = jax.nn.sigmoid(logits)
    biased = scores + bias_ref[...]

    j_iota = lax.broadcasted_iota(jnp.int32, (per_group, tt), 0)
    group_score = []
    for g in range(N_EXPERT_GROUPS):
        bg = biased[g * per_group:(g + 1) * per_group]
        m1 = jnp.max(bg, axis=0, keepdims=True)
        i1 = jnp.min(jnp.where(bg == m1, j_iota, per_group), axis=0, keepdims=True)
        m2 = jnp.max(jnp.where(j_iota == i1, -jnp.inf, bg), axis=0, keepdims=True)
        group_score.append(m1 + m2)

    masked = []
    for g in range(N_EXPERT_GROUPS):
        beaten_by = jnp.zeros((1, tt), jnp.int32)
        for o in range(N_EXPERT_GROUPS):
            if o == g:
                continue
            wins = group_score[o] > group_score[g]
            if o < g:
                wins = wins | (group_score[o] == group_score[g])
            beaten_by = beaten_by + wins.astype(jnp.int32)
        keep = beaten_by < TOPK_GROUPS
        masked.append(jnp.where(keep, biased[g * per_group:(g + 1) * per_group], MASK_VALUE))
    cur = jnp.concatenate(masked, axis=0)

    e_iota = lax.broadcasted_iota(jnp.int32, (n_e, tt), 0)
    idx_rows, score_rows, sels = [], [], []
    for _ in range(TOP_K):
        m = jnp.max(cur, axis=0, keepdims=True)
        ik = jnp.min(jnp.where(cur == m, e_iota, n_e), axis=0, keepdims=True)
        sel = e_iota == ik
        score_rows.append(jnp.sum(jnp.where(sel, scores, 0.0), axis=0, keepdims=True))
        cur = jnp.where(sel, -jnp.inf, cur)
        idx_rows.append(ik)
        sels.append(sel)
    top_s = jnp.concatenate(score_rows, axis=0)
    gate_ref[...] = top_s / jnp.sum(top_s, axis=0, keepdims=True) * ROUTED_SCALE
    idx_ref[...] = jnp.concatenate(idx_rows, axis=0)
    multi_hot = jnp.where(cur == -jnp.inf, 1.0, 0.0)

    t_row = lax.broadcasted_iota(jnp.int32, (tt, tt), 0)
    t_col = lax.broadcasted_iota(jnp.int32, (tt, tt), 1)
    strict_upper = jnp.where(t_row < t_col, 1.0, 0.0).astype(BF16)
    before = _bdot(multi_hot.astype(BF16), strict_upper) + carry_ref[...]
    rank_rows = [jnp.sum(jnp.where(sel, before, 0.0), axis=0, keepdims=True) for sel in sels]
    rank_ref[...] = jnp.concatenate(rank_rows, axis=0).astype(jnp.int32)
    carry = carry_ref[...] + jnp.sum(multi_hot, axis=1, keepdims=True)
    carry_ref[...] = carry
    cnt_ref[...] = carry.astype(jnp.int32)


def _router(h_tiles, w_router, router_bias):
    n_tok = h_tiles.shape[0] // TOK_ROWS
    tt = ROUTER_TT
    kt = lambda dt: jax.ShapeDtypeStruct((TOP_K, n_tok), dt)
    tok_spec = pl.BlockSpec((TOP_K, tt), lambda i: (0, i))
    return pl.pallas_call(
        functools.partial(_router_body, tt=tt),
        grid=(n_tok // tt,),
        in_specs=[
            pl.BlockSpec((tt * TOK_ROWS, LANES), lambda i: (i, 0)),
            pl.BlockSpec((N_EXPERTS, D_MODEL), lambda i: (0, 0)),
            pl.BlockSpec((N_EXPERTS, 1), lambda i: (0, 0)),
        ],
        out_specs=[tok_spec, tok_spec, tok_spec, pl.BlockSpec((N_EXPERTS, 1), lambda i: (0, 0))],
        out_shape=[kt(jnp.int32), kt(F32), kt(jnp.int32),
                   jax.ShapeDtypeStruct((N_EXPERTS, 1), jnp.int32)],
        scratch_shapes=[pltpu.VMEM((N_EXPERTS, 1), F32)],
        compiler_params=pltpu.CompilerParams(
            dimension_semantics=("arbitrary",), vmem_limit_bytes=VMEM_LIMIT),
        name="moe_router",
    )(h_tiles, w_router.T.astype(BF16), router_bias.reshape(N_EXPERTS, 1))


def _pack_rows(x):
    half = D_MODEL // 2
    return pltpu.pack_elementwise([x[:, :half], x[:, half:]], packed_dtype=BF16)


def _unpack_rows(p):
    return jnp.concatenate(
        [pltpu.unpack_elementwise(p, index=i, packed_dtype=BF16, unpacked_dtype=F32) for i in range(2)],
        axis=1)


def _packed_load(ref, n_tok):
    return jnp.concatenate(
        [ref[pl.ds(s, n_tok, stride=PK_ROWS), :] for s in range(PK_ROWS)], axis=1)


def _packed_store(ref, val, n_tok):
    for s in range(PK_ROWS):
        ref[pl.ds(s, n_tok, stride=PK_ROWS), :] = val[:, s * LANES:(s + 1) * LANES]


def _sorted_rows_body(idx_ref, rank_ref, start_ref, o_ref):
    tl = idx_ref.shape[1]
    e_iota = lax.broadcasted_iota(jnp.int32, (N_EXPERTS, tl), 0)
    start = start_ref[...]
    rows = []
    for k in range(TOP_K):
        hit = e_iota == idx_ref[k:k + 1, :]
        base = jnp.sum(jnp.where(hit, start, 0.0), axis=0, keepdims=True)
        rows.append(base.astype(jnp.int32) + rank_ref[k:k + 1, :])
    o_ref[...] = jnp.concatenate(rows, axis=0)


def _sorted_rows(top_idx, rank, pad_start):
    n_tok = top_idx.shape[1]
    tl = SORTED_ROWS_TL
    tok = pl.BlockSpec((TOP_K, tl), lambda i: (0, i))
    return pl.pallas_call(
        _sorted_rows_body,
        grid=(n_tok // tl,),
        in_specs=[tok, tok, pl.BlockSpec((N_EXPERTS, 1), lambda i: (0, 0))],
        out_specs=tok,
        out_shape=jax.ShapeDtypeStruct((TOP_K, n_tok), jnp.int32),
        compiler_params=pltpu.CompilerParams(
            dimension_semantics=("arbitrary",), vmem_limit_bytes=VMEM_LIMIT),
        name="moe_sorted_rows",
    )(top_idx, rank, pad_start.astype(F32).reshape(N_EXPERTS, 1))


def _dispatch_body(start_ref, cnt_ref, dest_ref, h_ref, xs_ref, pk_ref, zero_ref, sem, zero_sem,
                   *, tt, tb):
    @pl.when(pl.program_id(0) == 0)
    def _():
        zero_ref[...] = jnp.zeros_like(zero_ref)

        def pad_copies(e, fn):
            n_pad = ((cnt_ref[e] + (tb - 1)) & (-tb)) - cnt_ref[e]
            row = start_ref[e] + cnt_ref[e]
            for bit in range(tb.bit_length() - 1):
                size = 1 << bit
                has_bit = ((n_pad >> bit) & 1) == 1

                @pl.when(has_bit)
                def _():
                    fn(pltpu.make_async_copy(zero_ref.at[pl.ds(0, size)],
                                             xs_ref.at[pl.ds(row, size)], zero_sem))
                row = row + jnp.where(has_bit, size, 0)

        def start_all(e, c):
            pad_copies(e, lambda cp: cp.start())
            return c

        def wait_all(e, c):
            pad_copies(e, lambda cp: cp.wait())
            return c

        lax.fori_loop(0, N_EXPERTS, start_all, 0)
        lax.fori_loop(0, N_EXPERTS, wait_all, 0)

    packed = _pack_rows(_rows_load(h_ref, tt))
    for s in range(PK_ROWS):
        pk_ref[:, s, :] = packed[:, s * LANES:(s + 1) * LANES]
    for t in range(tt):
        for k in range(TOP_K):
            pltpu.make_async_copy(pk_ref.at[t], xs_ref.at[dest_ref[k, t]], sem).start(priority=k % 2)
    for k in range(TOP_K):
        pltpu.make_async_copy(pk_ref, xs_ref.at[pl.ds(0, tt)], sem).wait()


def _dispatch(h_tiles, dest_rows, pad_start, counts, n_rows):
    n_tok = h_tiles.shape[0] // TOK_ROWS
    tt = DISPATCH_TT
    grid_spec = pltpu.PrefetchScalarGridSpec(
        num_scalar_prefetch=2,
        grid=(n_tok // tt,),
        in_specs=[pl.BlockSpec((TOP_K, tt), lambda i, s, c: (0, i), memory_space=pltpu.SMEM),
                  pl.BlockSpec((tt * TOK_ROWS, LANES), lambda i, s, c: (i, 0))],
        out_specs=pl.BlockSpec(memory_space=pl.ANY),
        scratch_shapes=[pltpu.VMEM((tt, PK_ROWS, LANES), I32),
                        pltpu.VMEM((EXPERT_TB // 2, PK_ROWS, LANES), I32),
                        pltpu.SemaphoreType.DMA(()), pltpu.SemaphoreType.DMA(())],
    )
    return pl.pallas_call(
        functools.partial(_dispatch_body, tt=tt, tb=EXPERT_TB),
        grid_spec=grid_spec,
        out_shape=jax.ShapeDtypeStruct((n_rows, PK_ROWS, LANES), I32),
        compiler_params=pltpu.CompilerParams(
            dimension_semantics=("arbitrary",), vmem_limit_bytes=VMEM_LIMIT),
        name="moe_dispatch",
    )(pad_start, counts, dest_rows, h_tiles)


def _expert_body(be_ref, first_ref, slot_ref, next_ref, na_ref, xs_ref, w1_ref, w3_ref, w2_ref, ys_ref,
                 w13_buf, w2_buf, w1b, w3b, w2b, sem, *, tb, layer):
    i = pl.program_id(0)

    def weight_copies(expert, slot):
        return (pltpu.make_async_copy(w1_ref.at[layer, expert], w13_buf.at[slot, 0], sem.at[slot]),
                pltpu.make_async_copy(w3_ref.at[layer, expert], w13_buf.at[slot, 1], sem.at[slot]),
                pltpu.make_async_copy(w2_ref.at[layer, expert], w2_buf.at[slot], sem.at[slot]))

    @pl.when(i < na_ref[0])
    def _():
        @pl.when(first_ref[i] == 1)
        def _():
            slot = slot_ref[i]

            @pl.when(i == 0)
            def _():
                for cp in weight_copies(be_ref[0], slot):
                    cp.start()

            for cp in weight_copies(be_ref[i], slot):
                cp.wait()

            @pl.when(next_ref[i] >= 0)
            def _():
                for cp in weight_copies(next_ref[i], 1 - slot):
                    cp.start()

            w1b[...] = w13_buf[slot, 0].astype(BF16)
            w3b[...] = w13_buf[slot, 1].astype(BF16)
            w2b[...] = w2_buf[slot].astype(BF16)

        x = _unpack_rows(_packed_load(xs_ref, tb)).astype(BF16)
        mid = (_silu(_bdot(x, w1b[...])) * _bdot(x, w3b[...])).astype(BF16)
        _packed_store(ys_ref, _pack_rows(_bdot(mid, w2b[...])), tb)


def _experts(xs, block_expert, first, slot, next_expert, n_active, w1, w3, w2, layer):
    tb = EXPERT_TB
    n_blocks = xs.shape[0] // (tb * PK_ROWS)
    row_block = lambda i, be, fi, sl, nx, na: (jnp.minimum(i, na[0] - 1), 0)
    grid_spec = pltpu.PrefetchScalarGridSpec(
        num_scalar_prefetch=5,
        grid=(n_blocks,),
        in_specs=[
            pl.BlockSpec((tb * PK_ROWS, LANES), row_block),
            pl.BlockSpec(memory_space=pl.ANY), pl.BlockSpec(memory_space=pl.ANY),
            pl.BlockSpec(memory_space=pl.ANY),
        ],
        out_specs=pl.BlockSpec((tb * PK_ROWS, LANES), row_block),
        scratch_shapes=[pltpu.VMEM((2, 2, D_MODEL, D_EXPERT), F32),
                        pltpu.VMEM((2, D_EXPERT, D_MODEL), F32),
                        pltpu.VMEM((D_MODEL, D_EXPERT), BF16), pltpu.VMEM((D_MODEL, D_EXPERT), BF16),
                        pltpu.VMEM((D_EXPERT, D_MODEL), BF16), pltpu.SemaphoreType.DMA((2,))],
    )
    return pl.pallas_call(
        functools.partial(_expert_body, tb=tb, layer=layer),
        grid_spec=grid_spec,
        out_shape=jax.ShapeDtypeStruct(xs.shape, I32),
        compiler_params=pltpu.CompilerParams(
            dimension_semantics=("arbitrary",), vmem_limit_bytes=VMEM_LIMIT),
        name="moe_experts",
    )(block_expert, first, slot, next_expert, n_active, xs, w1, w3, w2)


def _store_head_slabs(y, o_refs, scales):
    for j, (o_ref, scale) in enumerate(zip(o_refs, scales)):
        for s in range(PAIR_SLABS):
            lo = j * O_WIDTH + 2 * s * LANES
            o_ref[s] = pltpu.pack_elementwise(
                [y[:, lo:lo + LANES] * scale, y[:, lo + LANES:lo + 2 * LANES] * scale],
                packed_dtype=BF16)


def _combine_body(src_ref, h_ref, gates_ref, ys_ref, ys_flat_ref, ws1_ref, ws3_ref, ws2_ref,
                  g_ref, b_ref, o_ref, gbuf, sem, *, tt, n_tiles, token_tile_out):
    j = pl.program_id(0)

    def step(issue, finish):
        nxt = j % 2
        cur = (j - 1) % 2
        if finish:
            h = _rows_load(h_ref, tt)
            hb = h.astype(BF16)
            mid = (_silu(_bdot(hb, ws1_ref[...])) * _bdot(hb, ws3_ref[...])).astype(BF16)
            routed = _bdot(mid, ws2_ref[...])
            gates = gates_ref[...]
        for k in range(TOP_K):
            if issue:
                for t in range(tt):
                    pltpu.make_async_copy(ys_ref.at[src_ref[k, t]],
                                          gbuf.at[nxt, k, pl.ds(t * PK_ROWS, PK_ROWS), :],
                                          sem.at[nxt]).start(priority=t % 2)
            if finish:
                pltpu.make_async_copy(ys_flat_ref.at[pl.ds(0, tt * PK_ROWS), :], gbuf.at[cur, k],
                                      sem.at[cur]).wait()
                routed = routed + gates[:, k:k + 1] * _unpack_rows(_packed_load(gbuf.at[cur, k], tt))
        if finish:
            out = _layer_norm(DEEPNORM_ALPHA * h + routed, g_ref[...], b_ref[...])
            if token_tile_out:
                _rows_store(o_ref, out, tt)
            else:
                o_ref[...] = out

    pl.when(j == 0)(lambda: step(True, False))
    pl.when((j >= 1) & (j < n_tiles))(lambda: step(True, True))
    pl.when(j == n_tiles)(lambda: step(False, True))


def _combine(h_tiles, src_rows, gates_tk, ys, ws1, ws3, ws2, g, b, token_tile_out):
    n_tok = h_tiles.shape[0] // TOK_ROWS
    tt = COMBINE_TT
    n_tiles = n_tok // tt
    prev_tile = lambda j: (jnp.maximum(j - 1, 0), 0)
    full = lambda shape: pl.BlockSpec(shape, lambda j: (0,) * len(shape))
    if token_tile_out:
        out_spec = pl.BlockSpec((tt * TOK_ROWS, LANES), prev_tile)
        out_shape = jax.ShapeDtypeStruct((n_tok * TOK_ROWS, LANES), F32)
    else:
        out_spec = pl.BlockSpec((tt, D_MODEL), prev_tile)
        out_shape = jax.ShapeDtypeStruct((n_tok, D_MODEL), F32)
    n_rows = ys.shape[0] // PK_ROWS
    return pl.pallas_call(
        functools.partial(_combine_body, tt=tt, n_tiles=n_tiles, token_tile_out=token_tile_out),
        grid=(n_tiles + 1,),
        in_specs=[pl.BlockSpec((TOP_K, tt), lambda j: (0, jnp.minimum(j, n_tiles - 1)),
                               memory_space=pltpu.SMEM),
                  pl.BlockSpec((tt * TOK_ROWS, LANES), prev_tile),
                  pl.BlockSpec((tt, TOP_K), prev_tile),
                  pl.BlockSpec(memory_space=pl.ANY), pl.BlockSpec(memory_space=pl.ANY),
                  full((D_MODEL, D_EXPERT)), full((D_MODEL, D_EXPERT)), full((D_EXPERT, D_MODEL)),
                  full((1, D_MODEL)), full((1, D_MODEL))],
        out_specs=out_spec,
        out_shape=out_shape,
        scratch_shapes=[pltpu.VMEM((2, TOP_K, tt * PK_ROWS, LANES), I32),
                        pltpu.SemaphoreType.DMA((2,))],
        compiler_params=pltpu.CompilerParams(
            dimension_semantics=("arbitrary",), vmem_limit_bytes=VMEM_LIMIT),
        name="moe_combine",
    )(src_rows, h_tiles, gates_tk, ys.reshape(n_rows, PK_ROWS, LANES), ys, ws1.astype(BF16),
      ws3.astype(BF16), ws2.astype(BF16), g.reshape(1, -1), b.reshape(1, -1))


def _moe_layer(h_tiles, w_router, router_bias, w1, w3, w2, layer, ws1, ws3, ws2, g, b,
               token_tile_out):
    n_tok = h_tiles.shape[0] // TOK_ROWS
    tb = EXPERT_TB
    top_idx, gates, rank, counts = _router(h_tiles, w_router, router_bias)

    n_blocks = -(-(n_tok * TOP_K + N_EXPERTS * (tb - 1)) // tb)
    counts = counts.reshape(N_EXPERTS)
    padded = ((counts + tb - 1) // tb) * tb
    pad_end = jnp.cumsum(padded)
    pad_start = (pad_end - padded).astype(jnp.int32)
    block_first_row = jnp.arange(n_blocks, dtype=jnp.int32) * tb
    block_expert = jnp.minimum(
        jnp.sum((pad_end[None, :] <= block_first_row[:, None]).astype(jnp.int32), axis=1),
        N_EXPERTS - 1)
    n_active = (pad_end[-1:] // tb).astype(jnp.int32)
    present = counts > 0
    expert_ids = jnp.arange(N_EXPERTS, dtype=jnp.int32)
    later = lax.cummin(jnp.where(present, expert_ids, N_EXPERTS), reverse=True)
    next_present = jnp.concatenate([later[1:], jnp.full((1,), N_EXPERTS, jnp.int32)])
    next_present = jnp.where(next_present < N_EXPERTS, next_present, -1)
    ordinal = jnp.cumsum(present.astype(jnp.int32)) - 1
    first = jnp.concatenate([jnp.ones((1,), jnp.int32),
                             (block_expert[1:] != block_expert[:-1]).astype(jnp.int32)])
    slot = ordinal[block_expert] % 2
    next_expert = next_present[block_expert]

    rows = _sorted_rows(top_idx, rank, pad_start)
    xs = _dispatch(h_tiles, rows, pad_start, counts, n_blocks * tb)
    ys = _experts(xs.reshape(n_blocks * tb * PK_ROWS, LANES), block_expert, first, slot, next_expert,
                  n_active, w1, w3, w2, layer)
    return _combine(h_tiles, rows, gates.T, ys, ws1, ws3, ws2, g, b, token_tile_out)


def _proj_body(h_ref, w_ref, *o_refs, tt, scale):
    y = _bdot(_rows_load(h_ref, tt).astype(BF16), w_ref[...])
    _store_head_slabs(y, o_refs, (scale,) * len(o_refs))


def _project(h_tiles, w, scale):
    n_tok = h_tiles.shape[0] // TOK_ROWS
    tt = PROJ_TT
    n_out = w.shape[1] // O_WIDTH
    out_spec = pl.BlockSpec((PAIR_SLABS, tt, LANES), lambda i: (0, i, 0))
    return pl.pallas_call(
        functools.partial(_proj_body, tt=tt, scale=scale),
        grid=(n_tok // tt,),
        in_specs=[pl.BlockSpec((tt * TOK_ROWS, LANES), lambda i: (i, 0)),
                  pl.BlockSpec(w.shape, lambda i: (0, 0))],
        out_specs=[out_spec] * n_out,
        out_shape=[jax.ShapeDtypeStruct((PAIR_SLABS, n_tok, LANES), I32)] * n_out,
        compiler_params=pltpu.CompilerParams(
            dimension_semantics=("arbitrary",), vmem_limit_bytes=VMEM_LIMIT),
        name="projection",
    )(h_tiles, w.astype(BF16))


def _attn_body(q_ref, kp_ref, kc_ref, vp_ref, vc_ref, o_ref, lse_ref, *, dilation, n_steps):
    t = ATT_BLOCK
    not_first = pl.program_id(1) > 0
    qi = lax.broadcasted_iota(jnp.int32, (t, 2 * t), 0)
    kj = lax.broadcasted_iota(jnp.int32, (t, 2 * t), 1)
    dist = t + qi - kj
    valid = (dist >= 0) & (dist <= n_steps) & (not_first | (kj >= t))
    token_dist = (dilation * dist).astype(F32)
    low_half = lax.broadcasted_iota(jnp.int32, (t, LANES), 1) < HEAD_DIM
    contract_last = (((1,), (1,)), ((), ()))

    def unpack(words, which):
        return pltpu.unpack_elementwise(words, index=which, packed_dtype=BF16,
                                        unpacked_dtype=F32).astype(BF16)

    def residue(r, carry):
        rows = pl.ds(r, t) if dilation == 1 else pl.ds(r, t, stride=dilation)
        for slab in range(PAIR_SLABS):
            q_words = q_ref.at[slab][rows, :]
            k_words = jnp.concatenate([kp_ref.at[slab][rows, :], kc_ref.at[slab][rows, :]], axis=0)
            v_words = jnp.concatenate([vp_ref.at[slab][rows, :], vc_ref.at[slab][rows, :]], axis=0)
            for which in range(2):
                pair = 2 * slab + which
                q2, k2, v2 = unpack(q_words, which), unpack(k_words, which), unpack(v_words, which)
                outs, lses = [], []
                for half in range(2):
                    head = 2 * pair + half
                    slope = 2.0 ** (-8.0 * (head + 1) / ATT_HEADS)
                    keep = low_half if half == 0 else ~low_half
                    qh = jnp.where(keep, q2, jnp.zeros_like(q2))
                    s = lax.dot_general(qh, k2, contract_last, preferred_element_type=F32)
                    s = jnp.where(valid, s - slope * token_dist, MASK_VALUE)
                    m = jnp.max(s, axis=-1, keepdims=True)
                    p = jnp.exp(s - m)
                    l = jnp.sum(p, axis=-1, keepdims=True)
                    outs.append(_bdot(p.astype(BF16), v2) / l)
                    lses.append(m + jnp.log(l))
                o_ref.at[pair][rows, :] = jnp.where(low_half, outs[0], outs[1])
                lse_ref.at[pair][rows, :] = jnp.where(low_half, lses[0], lses[1])
        return carry

    if dilation == 1:
        residue(0, 0)
    else:
        lax.fori_loop(0, dilation, residue, 0)


def _attn_branch(q, k, v, bsz, seq, window, dilation):
    span = ATT_BLOCK * dilation
    n_spans = seq // span
    pairs = ATT_HEADS // 2
    cur = lambda slabs: pl.BlockSpec((slabs, span, LANES), lambda bi, n: (0, bi * n_spans + n, 0))
    prev = pl.BlockSpec((PAIR_SLABS, span, LANES),
                        lambda bi, n: (0, bi * n_spans + jnp.maximum(n - 1, 0), 0))
    return pl.pallas_call(
        functools.partial(_attn_body, dilation=dilation, n_steps=window // dilation),
        grid=(bsz, n_spans),
        in_specs=[cur(PAIR_SLABS), prev, cur(PAIR_SLABS), prev, cur(PAIR_SLABS)],
        out_specs=[cur(pairs), cur(pairs)],
        out_shape=[jax.ShapeDtypeStruct((pairs, bsz * seq, LANES), F32)] * 2,
        compiler_params=pltpu.CompilerParams(
            dimension_semantics=("arbitrary",) * 2, vmem_limit_bytes=VMEM_LIMIT),
        name=f"dilated_attn_d{dilation}",
    )(q, k, k, v, v)


def _merge_body(*refs, tt):
    o_refs = refs[:N_GROUPS]
    lse_refs = refs[N_GROUPS:2 * N_GROUPS]
    h_ref, wo_ref, g_ref, b_ref, out_ref = refs[2 * N_GROUPS:]
    wide = lambda r: jnp.concatenate([r[c] for c in range(ATT_HEADS // 2)], axis=1)
    lses = [wide(r) for r in lse_refs]
    m = functools.reduce(jnp.maximum, lses)
    ws = [jnp.exp(l - m) for l in lses]
    den = functools.reduce(lambda a, c: a + c, ws)
    o = functools.reduce(lambda a, c: a + c, [w * wide(r) for w, r in zip(ws, o_refs)]) / den
    mix = _bdot(o.astype(BF16), wo_ref[...])
    z = DEEPNORM_ALPHA * _rows_load(h_ref, tt) + mix
    _rows_store(out_ref, _layer_norm(z, g_ref[...], b_ref[...]), tt)


def _merge(outs, lses, h_tiles, w_o, g, b):
    n_tok = h_tiles.shape[0] // TOK_ROWS
    tt = MERGE_TT
    tok = pl.BlockSpec((ATT_HEADS // 2, tt, LANES), lambda i: (0, i, 0))
    tiles = pl.BlockSpec((tt * TOK_ROWS, LANES), lambda i: (i, 0))
    full = lambda shape: pl.BlockSpec(shape, lambda i: (0,) * len(shape))
    return pl.pallas_call(
        functools.partial(_merge_body, tt=tt),
        grid=(n_tok // tt,),
        in_specs=[tok] * (2 * N_GROUPS) + [tiles, full((O_WIDTH, D_MODEL)), full((1, D_MODEL)),
                                           full((1, D_MODEL))],
        out_specs=tiles,
        out_shape=jax.ShapeDtypeStruct(h_tiles.shape, F32),
        compiler_params=pltpu.CompilerParams(
            dimension_semantics=("arbitrary",), vmem_limit_bytes=VMEM_LIMIT),
        name="attn_merge",
    )(*outs, *lses, h_tiles, w_o.astype(BF16), g.reshape(1, -1), b.reshape(1, -1))


def kernel(x, a_w_in, a_conv_w, a_conv_b, a_w_gate_a, a_b_gate_a, a_w_gate_x, a_b_gate_x, a_lambda, a_w_out, w_kv_shared, b_w_q, b_w_o, moe_w_router, moe_router_bias, moe_w1, moe_w3, moe_w2, moe_ws1, moe_ws3, moe_ws2, ln_g, ln_b):
    bsz, seq, d = x.shape
    assert d == D_MODEL and seq % (DILATION_PATTERNS[-1][1] * ATT_BLOCK) == 0

    def moe(h_tiles, layer, token_tile_out):
        return _moe_layer(h_tiles, moe_w_router[layer], moe_router_bias[layer], moe_w1, moe_w3,
                          moe_w2, layer, moe_ws1[layer], moe_ws3[layer], moe_ws2[layer],
                          ln_g[layer, 1], ln_b[layer, 1], token_tile_out)

    h = _rglru_layer(x, a_w_in[0], a_conv_w[0], a_conv_b[0], a_w_gate_a[0], a_b_gate_a[0],
                     a_w_gate_x[0], a_b_gate_x[0], a_lambda[0], a_w_out[0], ln_g[0, 0], ln_b[0, 0])
    h = moe(h, 0, True)
    kv = _project(h, w_kv_shared, 1.0)
    ks, vs = kv[:N_GROUPS], kv[N_GROUPS:]

    qs = _project(h, b_w_q[0], HEAD_DIM ** -0.5)
    outs, lses = [], []
    for grp, (window, dilation) in enumerate(DILATION_PATTERNS):
        o, lse = _attn_branch(qs[grp], ks[grp], vs[grp], bsz, seq, window, dilation)
        outs.append(o)
        lses.append(lse)
    h = _merge(outs, lses, h, b_w_o[0], ln_g[1, 0], ln_b[1, 0])
    h = moe(h, 1, False)
    return h.reshape(bsz, seq, d)
```

```python
import functools
import math

import jax
import jax.numpy as jnp
from jax import lax
from jax.experimental import pallas as pl
from jax.experimental.pallas import tpu as pltpu

F32 = jnp.float32
BF16 = jnp.bfloat16
I32 = jnp.int32

D_MODEL = 1024
LRU_BLOCKS = 4
CONV_WIDTH = 4
LRU_C = 8.0
ATT_HEADS = 8
HEAD_DIM = 64
DILATION_PATTERNS = ((128, 1), (512, 4), (2048, 16))
N_GROUPS = len(DILATION_PATTERNS)
ATT_BLOCK = 128
O_WIDTH = ATT_HEADS * HEAD_DIM
N_EXPERTS = 256
TOP_K = 8
N_EXPERT_GROUPS = 8
TOPK_GROUPS = 4
D_EXPERT = 256
ROUTED_SCALE = 2.5
DEPTH = 2
DEEPNORM_ALPHA = (2 * DEPTH) ** 0.25
LN_EPS = 1e-5
MASK_VALUE = -1e30

SUBLANES = 8
LANES = 128
TOK_ROWS = D_MODEL // LANES
PK_ROWS = D_MODEL // 2 // LANES
PAIR_SLABS = ATT_HEADS * HEAD_DIM // LANES // 2
VMEM_LIMIT = 56 * 1024 * 1024

RGLRU_TS = 256
ROUTER_TT = 256
SORTED_ROWS_TL = 2048
DISPATCH_TT = 512
EXPERT_TB = 512
COMBINE_TT = 512
PROJ_TT = 512
MERGE_TT = 256


def _rows_load(ref, n_tok):
    return jnp.concatenate(
        [ref[pl.ds(s, n_tok, stride=TOK_ROWS), :] for s in range(TOK_ROWS)], axis=1)


def _rows_store(ref, val, n_tok):
    for s in range(TOK_ROWS):
        ref[pl.ds(s, n_tok, stride=TOK_ROWS), :] = val[:, s * LANES:(s + 1) * LANES]


def _layer_norm(z, g, b):
    mu = jnp.mean(z, axis=-1, keepdims=True)
    zc = z - mu
    var = jnp.mean(zc * zc, axis=-1, keepdims=True)
    return zc * lax.rsqrt(var + LN_EPS) * g + b


def _silu(x):
    return x * jax.nn.sigmoid(x)


def _gelu_tanh(x):
    c = math.sqrt(2.0 / math.pi)
    return 0.5 * x * (1.0 + jnp.tanh(c * (x + 0.044715 * (x * x * x))))


def _bdot(a, b):
    return jnp.dot(a, b, preferred_element_type=F32)


def _rglru_body(x_ref, win_ref, cw_ref, cb_ref, wga_ref, bga_ref, wgx_ref, bgx_ref, lam_ref,
                wout_ref, g_ref, b_ref, o_ref, hc_ref, tail_ref, *, ts):
    width = D_MODEL
    bw = width // LRU_BLOCKS

    @pl.when(pl.program_id(1) == 0)
    def _():
        hc_ref[...] = jnp.zeros_like(hc_ref)
        tail_ref[...] = jnp.zeros_like(tail_ref)

    x = x_ref[...]
    xz = _bdot(x.astype(BF16), win_ref[...])
    xr = xz[:, :width]
    gate = xz[:, width:]

    tail = tail_ref[...]
    row8 = lax.broadcasted_iota(jnp.int32, (SUBLANES, width), 0)
    cw = cw_ref[...]
    xc = xr * cw[CONV_WIDTH - 1:CONV_WIDTH, :] + cb_ref[...]
    for j in range(1, CONV_WIDTH):
        rx = pltpu.roll(xr, j, 0)
        rp = pltpu.roll(tail, j, 0)
        top = jnp.where(row8 < j, rp, rx[:SUBLANES])
        shifted = jnp.concatenate([top, rx[SUBLANES:]], axis=0)
        xc = xc + shifted * cw[CONV_WIDTH - 1 - j:CONV_WIDTH - j, :]
    tail_ref[...] = xr[ts - SUBLANES:]

    xcb = xc.astype(BF16)

    def block_diag(w_ref):
        return jnp.concatenate(
            [_bdot(xcb[:, n * bw:(n + 1) * bw], w_ref[n]) for n in range(LRU_BLOCKS)], axis=1)

    r = jax.nn.sigmoid(block_diag(wga_ref) + bga_ref[...])
    i = jax.nn.sigmoid(block_diag(wgx_ref) + bgx_ref[...])
    lam = lam_ref[...]
    softplus_neg_lam = jnp.maximum(-lam, 0.0) + jnp.log1p(jnp.exp(-jnp.abs(lam)))
    log_a = (-LRU_C * r) * softplus_neg_lam
    a = jnp.exp(log_a)
    mult = jnp.sqrt(-jnp.tanh(log_a) * (a * a + 1.0))
    u = mult * (i * xc)

    groups = ts // SUBLANES
    a3 = a.reshape(groups, SUBLANES, width)
    u3 = u.reshape(groups, SUBLANES, width)
    sub = lax.broadcasted_iota(jnp.int32, (groups, SUBLANES, width), 1)
    sh = 1
    while sh < SUBLANES:
        a_prev = pltpu.roll(a3, sh, 1)
        u_prev = pltpu.roll(u3, sh, 1)
        live = sub >= sh
        u3 = jnp.where(live, a3 * u_prev, 0.0) + u3
        a3 = jnp.where(live, a3 * a_prev, a3)
        sh *= 2
    carry = hc_ref[...]
    h_groups = []
    for grp in range(groups):
        h_grp = a3[grp] * carry + u3[grp]
        h_groups.append(h_grp)
        carry = h_grp[SUBLANES - 1:SUBLANES]
    h = jnp.concatenate(h_groups, axis=0)
    hc_ref[...] = carry

    y = (h * _gelu_tanh(gate)).astype(BF16)
    mix = _bdot(y, wout_ref[...])
    z = DEEPNORM_ALPHA * x + mix
    _rows_store(o_ref, _layer_norm(z, g_ref[...], b_ref[...]), ts)


def _rglru_layer(x, w_in, conv_w, conv_b, wga, bga, wgx, bgx, lam, w_out, g, b):
    bsz, seq, d = x.shape
    ts = RGLRU_TS
    ns = seq // ts
    row = lambda v: v.reshape(1, -1)
    full = lambda shape: pl.BlockSpec(shape, lambda bi, si: (0,) * len(shape))
    return pl.pallas_call(
        functools.partial(_rglru_body, ts=ts),
        grid=(bsz, ns),
        in_specs=[
            pl.BlockSpec((None, ts, d), lambda bi, si: (bi, si, 0)),
            full((d, 2 * d)), full((CONV_WIDTH, d)), full((1, d)),
            full((LRU_BLOCKS, d // LRU_BLOCKS, d // LRU_BLOCKS)), full((1, d)),
            full((LRU_BLOCKS, d // LRU_BLOCKS, d // LRU_BLOCKS)), full((1, d)),
            full((1, d)), full((d, d)), full((1, d)), full((1, d)),
        ],
        out_specs=pl.BlockSpec((ts * TOK_ROWS, LANES), lambda bi, si: (bi * ns + si, 0)),
        out_shape=jax.ShapeDtypeStruct((bsz * seq * TOK_ROWS, LANES), F32),
        scratch_shapes=[pltpu.VMEM((1, d), F32), pltpu.VMEM((SUBLANES, d), F32)],
        compiler_params=pltpu.CompilerParams(
            dimension_semantics=("arbitrary", "arbitrary"), vmem_limit_bytes=VMEM_LIMIT),
        name="rglru_layer",
    )(x, w_in.astype(BF16), conv_w, row(conv_b), wga.astype(BF16), row(bga), wgx.astype(BF16),
      row(bgx), row(lam), w_out.astype(BF16), row(g), row(b))


def _router_body(h_ref, wrt_ref, bias_ref, idx_ref, gate_ref, rank_ref, cnt_ref, carry_ref, *, tt):
    n_e = N_EXPERTS
    per_group = n_e // N_EXPERT_GROUPS

    @pl.when(pl.program_id(0) == 0)
    def _():
        carry_ref[...] = jnp.zeros_like(carry_ref)

    h = _rows_load(h_ref, tt)
    logits = lax.dot_general(wrt_ref[...], h.astype(BF16), (((1,), (1,)), ((), ())),
                             preferred_element_type=F32)
    scores = jax.nn.sigmoid(logits)
    biased = scores + bias_ref[...]

    j_iota = lax.broadcasted_iota(jnp.int32, (per_group, tt), 0)
    group_score = []
    for g in range(N_EXPERT_GROUPS):
        bg = biased[g * per_group:(g + 1) * per_group]
        m1 = jnp.max(bg, axis=0, keepdims=True)
        i1 = jnp.min(jnp.where(bg == m1, j_iota, per_group), axis=0, keepdims=True)
        m2 = jnp.max(jnp.where(j_iota == i1, -jnp.inf, bg), axis=0, keepdims=True)
        group_score.append(m1 + m2)

    masked = []
    for g in range(N_EXPERT_GROUPS):
        beaten_by = jnp.zeros((1, tt), jnp.int32)
        for o in range(N_EXPERT_GROUPS):
            if o == g:
                continue
            wins = group_score[o] > group_score[g]
            if o < g:
                wins = wins | (group_score[o] == group_score[g])
            beaten_by = beaten_by + wins.astype(jnp.int32)
        keep = beaten_by < TOPK_GROUPS
        masked.append(jnp.where(keep, biased[g * per_group:(g + 1) * per_group], MASK_VALUE))
    cur = jnp.concatenate(masked, axis=0)

    e_iota = lax.broadcasted_iota(jnp.int32, (n_e, tt), 0)
    idx_rows, score_rows, sels = [], [], []
    for _ in range(TOP_K):
        m = jnp.max(cur, axis=0, keepdims=True)
        ik = jnp.min(jnp.where(cur == m, e_iota, n_e), axis=0, keepdims=True)
        sel = e_iota == ik
        score_rows.append(jnp.sum(jnp.where(sel, scores, 0.0), axis=0, keepdims=True))
        cur = jnp.where(sel, -jnp.inf, cur)
        idx_rows.append(ik)
        sels.append(sel)
    top_s = jnp.concatenate(score_rows, axis=0)
    gate_ref[...] = top_s / jnp.sum(top_s, axis=0, keepdims=True) * ROUTED_SCALE
    idx_ref[...] = jnp.concatenate(idx_rows, axis=0)
    multi_hot = jnp.where(cur == -jnp.inf, 1.0, 0.0)

    t_row = lax.broadcasted_iota(jnp.int32, (tt, tt), 0)
    t_col = lax.broadcasted_iota(jnp.int32, (tt, tt), 1)
    strict_upper = jnp.where(t_row < t_col, 1.0, 0.0).astype(BF16)
    before = _bdot(multi_hot.astype(BF16), strict_upper) + carry_ref[...]
    rank_rows = [jnp.sum(jnp.where(sel, before, 0.0), axis=0, keepdims=True) for sel in sels]
    rank_ref[...] = jnp.concatenate(rank_rows, axis=0).astype(jnp.int32)
    carry = carry_ref[...] + jnp.sum(multi_hot, axis=1, keepdims=True)
    carry_ref[...] = carry
    cnt_ref[...] = carry.astype(jnp.int32)


def _router(h_tiles, w_router, router_bias):
    n_tok = h_tiles.shape[0] // TOK_ROWS
    tt = ROUTER_TT
    kt = lambda dt: jax.ShapeDtypeStruct((TOP_K, n_tok), dt)
    tok_spec = pl.BlockSpec((TOP_K, tt), lambda i: (0, i))
    return pl.pallas_call(
        functools.partial(_router_body, tt=tt),
        grid=(n_tok // tt,),
        in_specs=[
            pl.BlockSpec((tt * TOK_ROWS, LANES), lambda i: (i, 0)),
            pl.BlockSpec((N_EXPERTS, D_MODEL), lambda i: (0, 0)),
            pl.BlockSpec((N_EXPERTS, 1), lambda i: (0, 0)),
        ],
        out_specs=[tok_spec, tok_spec, tok_spec, pl.BlockSpec((N_EXPERTS, 1), lambda i: (0, 0))],
        out_shape=[kt(jnp.int32), kt(F32), kt(jnp.int32),
                   jax.ShapeDtypeStruct((N_EXPERTS, 1), jnp.int32)],
        scratch_shapes=[pltpu.VMEM((N_EXPERTS, 1), F32)],
        compiler_params=pltpu.CompilerParams(
            dimension_semantics=("arbitrary",), vmem_limit_bytes=VMEM_LIMIT),
        name="moe_router",
    )(h_tiles, w_router.T.astype(BF16), router_bias.reshape(N_EXPERTS, 1))


def _pack_rows(x):
    half = D_MODEL // 2
    return pltpu.pack_elementwise([x[:, :half], x[:, half:]], packed_dtype=BF16)


def _unpack_rows(p):
    return jnp.concatenate(
        [pltpu.unpack_elementwise(p, index=i, packed_dtype=BF16, unpacked_dtype=F32) for i in range(2)],
        axis=1)


def _packed_load(ref, n_tok):
    return jnp.concatenate(
        [ref[pl.ds(s, n_tok, stride=PK_ROWS), :] for s in range(PK_ROWS)], axis=1)


def _packed_store(ref, val, n_tok):
    for s in range(PK_ROWS):
        ref[pl.ds(s, n_tok, stride=PK_ROWS), :] = val[:, s * LANES:(s + 1) * LANES]


def _sorted_rows_body(idx_ref, rank_ref, start_ref, o_ref):
    tl = idx_ref.shape[1]
    e_iota = lax.broadcasted_iota(jnp.int32, (N_EXPERTS, tl), 0)
    start = start_ref[...]
    rows = []
    for k in range(TOP_K):
        hit = e_iota == idx_ref[k:k + 1, :]
        base = jnp.sum(jnp.where(hit, start, 0.0), axis=0, keepdims=True)
        rows.append(base.astype(jnp.int32) + rank_ref[k:k + 1, :])
    o_ref[...] = jnp.concatenate(rows, axis=0)


def _sorted_rows(top_idx, rank, pad_start):
    n_tok = top_idx.shape[1]
    tl = SORTED_ROWS_TL
    tok = pl.BlockSpec((TOP_K, tl), lambda i: (0, i))
    return pl.pallas_call(
        _sorted_rows_body,
        grid=(n_tok // tl,),
        in_specs=[tok, tok, pl.BlockSpec((N_EXPERTS, 1), lambda i: (0, 0))],
        out_specs=tok,
        out_shape=jax.ShapeDtypeStruct((TOP_K, n_tok), jnp.int32),
        compiler_params=pltpu.CompilerParams(
            dimension_semantics=("arbitrary",), vmem_limit_bytes=VMEM_LIMIT),
        name="moe_sorted_rows",
    )(top_idx, rank, pad_start.astype(F32).reshape(N_EXPERTS, 1))


def _dispatch_body(start_ref, cnt_ref, dest_ref, h_ref, xs_ref, pk_ref, zero_ref, sem, zero_sem,
                   *, tt, tb):
    @pl.when(pl.program_id(0) == 0)
    def _():
        zero_ref[...] = jnp.zeros_like(zero_ref)

        def pad_copies(e, fn):
            n_pad = ((cnt_ref[e] + (tb - 1)) & (-tb)) - cnt_ref[e]
            row = start_ref[e] + cnt_ref[e]
            for bit in range(tb.bit_length() - 1):
                size = 1 << bit
                has_bit = ((n_pad >> bit) & 1) == 1

                @pl.when(has_bit)
                def _():
                    fn(pltpu.make_async_copy(zero_ref.at[pl.ds(0, size)],
                                             xs_ref.at[pl.ds(row, size)], zero_sem))
                row = row + jnp.where(has_bit, size, 0)

        def start_all(e, c):
            pad_copies(e, lambda cp: cp.start())
            return c

        def wait_all(e, c):
            pad_copies(e, lambda cp: cp.wait())
            return c

        lax.fori_loop(0, N_EXPERTS, start_all, 0)
        lax.fori_loop(0, N_EXPERTS, wait_all, 0)

    packed = _pack_rows(_rows_load(h_ref, tt))
    for s in range(PK_ROWS):
        pk_ref[:, s, :] = packed[:, s * LANES:(s + 1) * LANES]
    for t in range(tt):
        for k in range(TOP_K):
            pltpu.make_async_copy(pk_ref.at[t], xs_ref.at[dest_ref[k, t]], sem).start(priority=k % 2)
    for k in range(TOP_K):
        pltpu.make_async_copy(pk_ref, xs_ref.at[pl.ds(0, tt)], sem).wait()


def _dispatch(h_tiles, dest_rows, pad_start, counts, n_rows):
    n_tok = h_tiles.shape[0] // TOK_ROWS
    tt = DISPATCH_TT
    grid_spec = pltpu.PrefetchScalarGridSpec(
        num_scalar_prefetch=2,
        grid=(n_tok // tt,),
        in_specs=[pl.BlockSpec((TOP_K, tt), lambda i, s, c: (0, i), memory_space=pltpu.SMEM),
                  pl.BlockSpec((tt * TOK_ROWS, LANES), lambda i, s, c: (i, 0))],
        out_specs=pl.BlockSpec(memory_space=pl.ANY),
        scratch_shapes=[pltpu.VMEM((tt, PK_ROWS, LANES), I32),
                        pltpu.VMEM((EXPERT_TB // 2, PK_ROWS, LANES), I32),
                        pltpu.SemaphoreType.DMA(()), pltpu.SemaphoreType.DMA(())],
    )
    return pl.pallas_call(
        functools.partial(_dispatch_body, tt=tt, tb=EXPERT_TB),
        grid_spec=grid_spec,
        out_shape=jax.ShapeDtypeStruct((n_rows, PK_ROWS, LANES), I32),
        compiler_params=pltpu.CompilerParams(
            dimension_semantics=("arbitrary",), vmem_limit_bytes=VMEM_LIMIT),
        name="moe_dispatch",
    )(pad_start, counts, dest_rows, h_tiles)


def _expert_body(be_ref, first_ref, slot_ref, next_ref, na_ref, xs_ref, w1_ref, w3_ref, w2_ref, ys_ref,
                 w13_buf, w2_buf, w1b, w3b, w2b, sem, *, tb, layer):
    i = pl.program_id(0)

    def weight_copies(expert, slot):
        return (pltpu.make_async_copy(w1_ref.at[layer, expert], w13_buf.at[slot, 0], sem.at[slot]),
                pltpu.make_async_copy(w3_ref.at[layer, expert], w13_buf.at[slot, 1], sem.at[slot]),
                pltpu.make_async_copy(w2_ref.at[layer, expert], w2_buf.at[slot], sem.at[slot]))

    @pl.when(i < na_ref[0])
    def _():
        @pl.when(first_ref[i] == 1)
        def _():
            slot = slot_ref[i]

            @pl.when(i == 0)
            def _():
                for cp in weight_copies(be_ref[0], slot):
                    cp.start()

            for cp in weight_copies(be_ref[i], slot):
                cp.wait()

            @pl.when(next_ref[i] >= 0)
            def _():
                for cp in weight_copies(next_ref[i], 1 - slot):
                    cp.start()

            w1b[...] = w13_buf[slot, 0].astype(BF16)
            w3b[...] = w13_buf[slot, 1].astype(BF16)
            w2b[...] = w2_buf[slot].astype(BF16)

        x = _unpack_rows(_packed_load(xs_ref, tb)).astype(BF16)
        mid = (_silu(_bdot(x, w1b[...])) * _bdot(x, w3b[...])).astype(BF16)
        _packed_store(ys_ref, _pack_rows(_bdot(mid, w2b[...])), tb)


def _experts(xs, block_expert, first, slot, next_expert, n_active, w1, w3, w2, layer):
    tb = EXPERT_TB
    n_blocks = xs.shape[0] // (tb * PK_ROWS)
    row_block = lambda i, be, fi, sl, nx, na: (jnp.minimum(i, na[0] - 1), 0)
    grid_spec = pltpu.PrefetchScalarGridSpec(
        num_scalar_prefetch=5,
        grid=(n_blocks,),
        in_specs=[
            pl.BlockSpec((tb * PK_ROWS, LANES), row_block),
            pl.BlockSpec(memory_space=pl.ANY), pl.BlockSpec(memory_space=pl.ANY),
            pl.BlockSpec(memory_space=pl.ANY),
        ],
        out_specs=pl.BlockSpec((tb * PK_ROWS, LANES), row_block),
        scratch_shapes=[pltpu.VMEM((2, 2, D_MODEL, D_EXPERT), F32),
                        pltpu.VMEM((2, D_EXPERT, D_MODEL), F32),
                        pltpu.VMEM((D_MODEL, D_EXPERT), BF16), pltpu.VMEM((D_MODEL, D_EXPERT), BF16),
                        pltpu.VMEM((D_EXPERT, D_MODEL), BF16), pltpu.SemaphoreType.DMA((2,))],
    )
    return pl.pallas_call(
        functools.partial(_expert_body, tb=tb, layer=layer),
        grid_spec=grid_spec,
        out_shape=jax.ShapeDtypeStruct(xs.shape, I32),
        compiler_params=pltpu.CompilerParams(
            dimension_semantics=("arbitrary",), vmem_limit_bytes=VMEM_LIMIT),
        name="moe_experts",
    )(block_expert, first, slot, next_expert, n_active, xs, w1, w3, w2)


def _store_head_slabs(y, o_refs, scales):
    for j, (o_ref, scale) in enumerate(zip(o_refs, scales)):
        for s in range(PAIR_SLABS):
            lo = j * O_WIDTH + 2 * s * LANES
            o_ref[s] = pltpu.pack_elementwise(
                [y[:, lo:lo + LANES] * scale, y[:, lo + LANES:lo + 2 * LANES] * scale],
                packed_dtype=BF16)


def _combine_body(src_ref, h_ref, gates_ref, ys_ref, ys_flat_ref, ws1_ref, ws3_ref, ws2_ref,
                  g_ref, b_ref, o_ref, gbuf, sem, *, tt, n_tiles, token_tile_out):
    j = pl.program_id(0)

    def step(issue, finish):
        nxt = j % 2
        cur = (j - 1) % 2
        if finish:
            h = _rows_load(h_ref, tt)
            hb = h.astype(BF16)
            mid = (_silu(_bdot(hb, ws1_ref[...])) * _bdot(hb, ws3_ref[...])).astype(BF16)
            routed = _bdot(mid, ws2_ref[...])
            gates = gates_ref[...]
        for k in range(TOP_K):
            if issue:
                for t in range(tt):
                    pltpu.make_async_copy(ys_ref.at[src_ref[k, t]],
                                          gbuf.at[nxt, k, pl.ds(t * PK_ROWS, PK_ROWS), :],
                                          sem.at[nxt]).start(priority=t % 2)
            if finish:
                pltpu.make_async_copy(ys_flat_ref.at[pl.ds(0, tt * PK_ROWS), :], gbuf.at[cur, k],
                                      sem.at[cur]).wait()
                routed = routed + gates[:, k:k + 1] * _unpack_rows(_packed_load(gbuf.at[cur, k], tt))
        if finish:
            out = _layer_norm(DEEPNORM_ALPHA * h + routed, g_ref[...], b_ref[...])
            if token_tile_out:
                _rows_store(o_ref, out, tt)
            else:
                o_ref[...] = out

    pl.when(j == 0)(lambda: step(True, False))
    pl.when((j >= 1) & (j < n_tiles))(lambda: step(True, True))
    pl.when(j == n_tiles)(lambda: step(False, True))


def _combine(h_tiles, src_rows, gates_tk, ys, ws1, ws3, ws2, g, b, token_tile_out):
    n_tok = h_tiles.shape[0] // TOK_ROWS
    tt = COMBINE_TT
    n_tiles = n_tok // tt
    prev_tile = lambda j: (jnp.maximum(j - 1, 0), 0)
    full = lambda shape: pl.BlockSpec(shape, lambda j: (0,) * len(shape))
    if token_tile_out:
        out_spec = pl.BlockSpec((tt * TOK_ROWS, LANES), prev_tile)
        out_shape = jax.ShapeDtypeStruct((n_tok * TOK_ROWS, LANES), F32)
    else:
        out_spec = pl.BlockSpec((tt, D_MODEL), prev_tile)
        out_shape = jax.ShapeDtypeStruct((n_tok, D_MODEL), F32)
    n_rows = ys.shape[0] // PK_ROWS
    return pl.pallas_call(
        functools.partial(_combine_body, tt=tt, n_tiles=n_tiles, token_tile_out=token_tile_out),
        grid=(n_tiles + 1,),
        in_specs=[pl.BlockSpec((TOP_K, tt), lambda j: (0, jnp.minimum(j, n_tiles - 1)),
                               memory_space=pltpu.SMEM),
                  pl.BlockSpec((tt * TOK_ROWS, LANES), prev_tile),
                  pl.BlockSpec((tt, TOP_K), prev_tile),
                  pl.BlockSpec(memory_space=pl.ANY), pl.BlockSpec(memory_space=pl.ANY),
                  full((D_MODEL, D_EXPERT)), full((D_MODEL, D_EXPERT)), full((D_EXPERT, D_MODEL)),
                  full((1, D_MODEL)), full((1, D_MODEL))],
        out_specs=out_spec,
        out_shape=out_shape,
        scratch_shapes=[pltpu.VMEM((2, TOP_K, tt * PK_ROWS, LANES), I32),
                        pltpu.SemaphoreType.DMA((2,))],
        compiler_params=pltpu.CompilerParams(
            dimension_semantics=("arbitrary",), vmem_limit_bytes=VMEM_LIMIT),
        name="moe_combine",
    )(src_rows, h_tiles, gates_tk, ys.reshape(n_rows, PK_ROWS, LANES), ys, ws1.astype(BF16),
      ws3.astype(BF16), ws2.astype(BF16), g.reshape(1, -1), b.reshape(1, -1))


def _moe_layer(h_tiles, w_router, router_bias, w1, w3, w2, layer, ws1, ws3, ws2, g, b,
               token_tile_out):
    n_tok = h_tiles.shape[0] // TOK_ROWS
    tb = EXPERT_TB
    top_idx, gates, rank, counts = _router(h_tiles, w_router, router_bias)

    n_blocks = -(-(n_tok * TOP_K + N_EXPERTS * (tb - 1)) // tb)
    counts = counts.reshape(N_EXPERTS)
    padded = ((counts + tb - 1) // tb) * tb
    pad_end = jnp.cumsum(padded)
    pad_start = (pad_end - padded).astype(jnp.int32)
    block_first_row = jnp.arange(n_blocks, dtype=jnp.int32) * tb
    block_expert = jnp.minimum(
        jnp.sum((pad_end[None, :] <= block_first_row[:, None]).astype(jnp.int32), axis=1),
        N_EXPERTS - 1)
    n_active = (pad_end[-1:] // tb).astype(jnp.int32)
    present = counts > 0
    expert_ids = jnp.arange(N_EXPERTS, dtype=jnp.int32)
    later = lax.cummin(jnp.where(present, expert_ids, N_EXPERTS), reverse=True)
    next_present = jnp.concatenate([later[1:], jnp.full((1,), N_EXPERTS, jnp.int32)])
    next_present = jnp.where(next_present < N_EXPERTS, next_present, -1)
    ordinal = jnp.cumsum(present.astype(jnp.int32)) - 1
    first = jnp.concatenate([jnp.ones((1,), jnp.int32),
                             (block_expert[1:] != block_expert[:-1]).astype(jnp.int32)])
    slot = ordinal[block_expert] % 2
    next_expert = next_present[block_expert]

    rows = _sorted_rows(top_idx, rank, pad_start)
    xs = _dispatch(h_tiles, rows, pad_start, counts, n_blocks * tb)
    ys = _experts(xs.reshape(n_blocks * tb * PK_ROWS, LANES), block_expert, first, slot, next_expert,
                  n_active, w1, w3, w2, layer)
    return _combine(h_tiles, rows, gates.T, ys, ws1, ws3, ws2, g, b, token_tile_out)


def _proj_body(h_ref, w_ref, *o_refs, tt, scale):
    y = _bdot(_rows_load(h_ref, tt).astype(BF16), w_ref[...])
    _store_head_slabs(y, o_refs, (scale,) * len(o_refs))


def _project(h_tiles, w, scale):
    n_tok = h_tiles.shape[0] // TOK_ROWS
    tt = PROJ_TT
    n_out = w.shape[1] // O_WIDTH
    out_spec = pl.BlockSpec((PAIR_SLABS, tt, LANES), lambda i: (0, i, 0))
    return pl.pallas_call(
        functools.partial(_proj_body, tt=tt, scale=scale),
        grid=(n_tok // tt,),
        in_specs=[pl.BlockSpec((tt * TOK_ROWS, LANES), lambda i: (i, 0)),
                  pl.BlockSpec(w.shape, lambda i: (0, 0))],
        out_specs=[out_spec] * n_out,
        out_shape=[jax.ShapeDtypeStruct((PAIR_SLABS, n_tok, LANES), I32)] * n_out,
        compiler_params=pltpu.CompilerParams(
            dimension_semantics=("arbitrary",), vmem_limit_bytes=VMEM_LIMIT),
        name="projection",
    )(h_tiles, w.astype(BF16))


def _attn_body(q_ref, kp_ref, kc_ref, vp_ref, vc_ref, o_ref, lse_ref, *, dilation, n_steps):
    t = ATT_BLOCK
    not_first = pl.program_id(1) > 0
    qi = lax.broadcasted_iota(jnp.int32, (t, 2 * t), 0)
    kj = lax.broadcasted_iota(jnp.int32, (t, 2 * t), 1)
    dist = t + qi - kj
    valid = (dist >= 0) & (dist <= n_steps) & (not_first | (kj >= t))
    token_dist = (dilation * dist).astype(F32)
    low_half = lax.broadcasted_iota(jnp.int32, (t, LANES), 1) < HEAD_DIM
    contract_last = (((1,), (1,)), ((), ()))

    def unpack(words, which):
        return pltpu.unpack_elementwise(words, index=which, packed_dtype=BF16,
                                        unpacked_dtype=F32).astype(BF16)

    def residue(r, carry):
        rows = pl.ds(r, t) if dilation == 1 else pl.ds(r, t, stride=dilation)
        for slab in range(PAIR_SLABS):
            q_words = q_ref.at[slab][rows, :]
            k_words = jnp.concatenate([kp_ref.at[slab][rows, :], kc_ref.at[slab][rows, :]], axis=0)
            v_words = jnp.concatenate([vp_ref.at[slab][rows, :], vc_ref.at[slab][rows, :]], axis=0)
            for which in range(2):
                pair = 2 * slab + which
                q2, k2, v2 = unpack(q_words, which), unpack(k_words, which), unpack(v_words, which)
                outs, lses = [], []
                for half in range(2):
                    head = 2 * pair + half
                    slope = 2.0 ** (-8.0 * (head + 1) / ATT_HEADS)
                    keep = low_half if half == 0 else ~low_half
                    qh = jnp.where(keep, q2, jnp.zeros_like(q2))
                    s = lax.dot_general(qh, k2, contract_last, preferred_element_type=F32)
                    s = jnp.where(valid, s - slope * token_dist, MASK_VALUE)
                    m = jnp.max(s, axis=-1, keepdims=True)
                    p = jnp.exp(s - m)
                    l = jnp.sum(p, axis=-1, keepdims=True)
                    outs.append(_bdot(p.astype(BF16), v2) / l)
                    lses.append(m + jnp.log(l))
                o_ref.at[pair][rows, :] = jnp.where(low_half, outs[0], outs[1])
                lse_ref.at[pair][rows, :] = jnp.where(low_half, lses[0], lses[1])
        return carry

    if dilation == 1:
        residue(0, 0)
    else:
        lax.fori_loop(0, dilation, residue, 0)


def _attn_branch(q, k, v, bsz, seq, window, dilation):
    span = ATT_BLOCK * dilation
    n_spans = seq // span
    pairs = ATT_HEADS // 2
    cur = lambda slabs: pl.BlockSpec((slabs, span, LANES), lambda bi, n: (0, bi * n_spans + n, 0))
    prev = pl.BlockSpec((PAIR_SLABS, span, LANES),
                        lambda bi, n: (0, bi * n_spans + jnp.maximum(n - 1, 0), 0))
    return pl.pallas_call(
        functools.partial(_attn_body, dilation=dilation, n_steps=window // dilation),
        grid=(bsz, n_spans),
        in_specs=[cur(PAIR_SLABS), prev, cur(PAIR_SLABS), prev, cur(PAIR_SLABS)],
        out_specs=[cur(pairs), cur(pairs)],
        out_shape=[jax.ShapeDtypeStruct((pairs, bsz * seq, LANES), F32)] * 2,
        compiler_params=pltpu.CompilerParams(
            dimension_semantics=("arbitrary",) * 2, vmem_limit_bytes=VMEM_LIMIT),
        name=f"dilated_attn_d{dilation}",
    )(q, k, k, v, v)


def _merge_body(*refs, tt):
    o_refs = refs[:N_GROUPS]
    lse_refs = refs[N_GROUPS:2 * N_GROUPS]
    h_ref, wo_ref, g_ref, b_ref, out_ref = refs[2 * N_GROUPS:]
    wide = lambda r: jnp.concatenate([r[c] for c in range(ATT_HEADS // 2)], axis=1)
    lses = [wide(r) for r in lse_refs]
    m = functools.reduce(jnp.maximum, lses)
    ws = [jnp.exp(l - m) for l in lses]
    den = functools.reduce(lambda a, c: a + c, ws)
    o = functools.reduce(lambda a, c: a + c, [w * wide(r) for w, r in zip(ws, o_refs)]) / den
    mix = _bdot(o.astype(BF16), wo_ref[...])
    z = DEEPNORM_ALPHA * _rows_load(h_ref, tt) + mix
    _rows_store(out_ref, _layer_norm(z, g_ref[...], b_ref[...]), tt)


def _merge(outs, lses, h_tiles, w_o, g, b):
    n_tok = h_tiles.shape[0] // TOK_ROWS
    tt = MERGE_TT
    tok = pl.BlockSpec((ATT_HEADS // 2, tt, LANES), lambda i: (0, i, 0))
    tiles = pl.BlockSpec((tt * TOK_ROWS, LANES), lambda i: (i, 0))
    full = lambda shape: pl.BlockSpec(shape, lambda i: (0,) * len(shape))
    return pl.pallas_call(
        functools.partial(_merge_body, tt=tt),
        grid=(n_tok // tt,),
        in_specs=[tok] * (2 * N_GROUPS) + [tiles, full((O_WIDTH, D_MODEL)), full((1, D_MODEL)),
                                           full((1, D_MODEL))],
        out_specs=tiles,
        out_shape=jax.ShapeDtypeStruct(h_tiles.shape, F32),
        compiler_params=pltpu.CompilerParams(
            dimension_semantics=("arbitrary",), vmem_limit_bytes=VMEM_LIMIT),
        name="attn_merge",
    )(*outs, *lses, h_tiles, w_o.astype(BF16), g.reshape(1, -1), b.reshape(1, -1))


def kernel(x, a_w_in, a_conv_w, a_conv_b, a_w_gate_a, a_b_gate_a, a_w_gate_x, a_b_gate_x, a_lambda, a_w_out, w_kv_shared, b_w_q, b_w_o, moe_w_router, moe_router_bias, moe_w1, moe_w3, moe_w2, moe_ws1, moe_ws3, moe_ws2, ln_g, ln_b):
    bsz, seq, d = x.shape
    assert d == D_MODEL and seq % (DILATION_PATTERNS[-1][1] * ATT_BLOCK) == 0

    def moe(h_tiles, layer, token_tile_out):
        return _moe_layer(h_tiles, moe_w_router[layer], moe_router_bias[layer], moe_w1, moe_w3,
                          moe_w2, layer, moe_ws1[layer], moe_ws3[layer], moe_ws2[layer],
                          ln_g[layer, 1], ln_b[layer, 1], token_tile_out)

    h = _rglru_layer(x, a_w_in[0], a_conv_w[0], a_conv_b[0], a_w_gate_a[0], a_b_gate_a[0],
                     a_w_gate_x[0], a_b_gate_x[0], a_lambda[0], a_w_out[0], ln_g[0, 0], ln_b[0, 0])
    h = moe(h, 0, True)
    kv = _project(h, w_kv_shared, 1.0)
    ks, vs = kv[:N_GROUPS], kv[N_GROUPS:]

    qs = _project(h, b_w_q[0], HEAD_DIM ** -0.5)
    outs, lses = [], []
    for grp, (window, dilation) in enumerate(DILATION_PATTERNS):
        o, lse = _attn_branch(qs[grp], ks[grp], vs[grp], bsz, seq, window, dilation)
        outs.append(o)
        lses.append(lse)
    h = _merge(outs, lses, h, b_w_o[0], ln_g[1, 0], ln_b[1, 0])
    h = moe(h, 1, False)
    return h.reshape(bsz, seq, d)
```

```python
import functools
import math

import jax
import jax.numpy as jnp
from jax import lax
from jax.experimental import pallas as pl
from jax.experimental.pallas import tpu as pltpu

F32 = jnp.float32
BF16 = jnp.bfloat16
I32 = jnp.int32

D_MODEL = 1024
LRU_BLOCKS = 4
CONV_WIDTH = 4
LRU_C = 8.0
ATT_HEADS = 8
HEAD_DIM = 64
DILATION_PATTERNS = ((128, 1), (512, 4), (2048, 16))
N_GROUPS = len(DILATION_PATTERNS)
ATT_BLOCK = 128
O_WIDTH = ATT_HEADS * HEAD_DIM
N_EXPERTS = 256
TOP_K = 8
N_EXPERT_GROUPS = 8
TOPK_GROUPS = 4
D_EXPERT = 256
ROUTED_SCALE = 2.5
DEPTH = 2
DEEPNORM_ALPHA = (2 * DEPTH) ** 0.25
LN_EPS = 1e-5
MASK_VALUE = -1e30

SUBLANES = 8
LANES = 128
TOK_ROWS = D_MODEL // LANES
PK_ROWS = D_MODEL // 2 // LANES
PAIR_SLABS = ATT_HEADS * HEAD_DIM // LANES // 2
LSE_LANES = LANES // ATT_HEADS
VMEM_LIMIT = 56 * 1024 * 1024

RGLRU_TS = 256
ROUTER_TT = 256
SORTED_ROWS_TL = 2048
DISPATCH_TT = 512
EXPERT_TB = 512
COMBINE_TT = 512
PROJ_TT = 512
MERGE_TT = 256


def _rows_load(ref, n_tok):
    return jnp.concatenate(
        [ref[pl.ds(s, n_tok, stride=TOK_ROWS), :] for s in range(TOK_ROWS)], axis=1)


def _rows_store(ref, val, n_tok):
    for s in range(TOK_ROWS):
        ref[pl.ds(s, n_tok, stride=TOK_ROWS), :] = val[:, s * LANES:(s + 1) * LANES]


def _layer_norm(z, g, b):
    mu = jnp.mean(z, axis=-1, keepdims=True)
    zc = z - mu
    var = jnp.mean(zc * zc, axis=-1, keepdims=True)
    return zc * lax.rsqrt(var + LN_EPS) * g + b


def _silu(x):
    return x * jax.nn.sigmoid(x)


def _gelu_tanh(x):
    c = math.sqrt(2.0 / math.pi)
    return 0.5 * x * (1.0 + jnp.tanh(c * (x + 0.044715 * (x * x * x))))


def _bdot(a, b):
    return jnp.dot(a, b, preferred_element_type=F32)


def _rglru_body(x_ref, win_ref, cw_ref, cb_ref, wga_ref, bga_ref, wgx_ref, bgx_ref, lam_ref,
                wout_ref, g_ref, b_ref, o_ref, hc_ref, tail_ref, *, ts):
    width = D_MODEL
    bw = width // LRU_BLOCKS

    @pl.when(pl.program_id(1) == 0)
    def _():
        hc_ref[...] = jnp.zeros_like(hc_ref)
        tail_ref[...] = jnp.zeros_like(tail_ref)

    x = x_ref[...]
    xz = _bdot(x.astype(BF16), win_ref[...])
    xr = xz[:, :width]
    gate = xz[:, width:]

    tail = tail_ref[...]
    row8 = lax.broadcasted_iota(jnp.int32, (SUBLANES, width), 0)
    cw = cw_ref[...]
    xc = xr * cw[CONV_WIDTH - 1:CONV_WIDTH, :] + cb_ref[...]
    for j in range(1, CONV_WIDTH):
        rx = pltpu.roll(xr, j, 0)
        rp = pltpu.roll(tail, j, 0)
        top = jnp.where(row8 < j, rp, rx[:SUBLANES])
        shifted = jnp.concatenate([top, rx[SUBLANES:]], axis=0)
        xc = xc + shifted * cw[CONV_WIDTH - 1 - j:CONV_WIDTH - j, :]
    tail_ref[...] = xr[ts - SUBLANES:]

    xcb = xc.astype(BF16)

    def block_diag(w_ref):
        return jnp.concatenate(
            [_bdot(xcb[:, n * bw:(n + 1) * bw], w_ref[n]) for n in range(LRU_BLOCKS)], axis=1)

    r = jax.nn.sigmoid(block_diag(wga_ref) + bga_ref[...])
    i = jax.nn.sigmoid(block_diag(wgx_ref) + bgx_ref[...])
    lam = lam_ref[...]
    softplus_neg_lam = jnp.maximum(-lam, 0.0) + jnp.log1p(jnp.exp(-jnp.abs(lam)))
    log_a = (-LRU_C * r) * softplus_neg_lam
    a = jnp.exp(log_a)
    mult = jnp.sqrt(-jnp.tanh(log_a) * (a * a + 1.0))
    u = mult * (i * xc)

    groups = ts // SUBLANES
    a3 = a.reshape(groups, SUBLANES, width)
    u3 = u.reshape(groups, SUBLANES, width)
    sub = lax.broadcasted_iota(jnp.int32, (groups, SUBLANES, width), 1)
    sh = 1
    while sh < SUBLANES:
        a_prev = pltpu.roll(a3, sh, 1)
        u_prev = pltpu.roll(u3, sh, 1)
        live = sub >= sh
        u3 = jnp.where(live, a3 * u_prev, 0.0) + u3
        a3 = jnp.where(live, a3 * a_prev, a3)
        sh *= 2
    carry = hc_ref[...]
    h_groups = []
    for grp in range(groups):
        h_grp = a3[grp] * carry + u3[grp]
        h_groups.append(h_grp)
        carry = h_grp[SUBLANES - 1:SUBLANES]
    h = jnp.concatenate(h_groups, axis=0)
    hc_ref[...] = carry

    y = (h * _gelu_tanh(gate)).astype(BF16)
    mix = _bdot(y, wout_ref[...])
    z = DEEPNORM_ALPHA * x + mix
    _rows_store(o_ref, _layer_norm(z, g_ref[...], b_ref[...]), ts)


def _rglru_layer(x, w_in, conv_w, conv_b, wga, bga, wgx, bgx, lam, w_out, g, b):
    bsz, seq, d = x.shape
    ts = RGLRU_TS
    ns = seq // ts
    row = lambda v: v.reshape(1, -1)
    full = lambda shape: pl.BlockSpec(shape, lambda bi, si: (0,) * len(shape))
    return pl.pallas_call(
        functools.partial(_rglru_body, ts=ts),
        grid=(bsz, ns),
        in_specs=[
            pl.BlockSpec((None, ts, d), lambda bi, si: (bi, si, 0)),
            full((d, 2 * d)), full((CONV_WIDTH, d)), full((1, d)),
            full((LRU_BLOCKS, d // LRU_BLOCKS, d // LRU_BLOCKS)), full((1, d)),
            full((LRU_BLOCKS, d // LRU_BLOCKS, d // LRU_BLOCKS)), full((1, d)),
            full((1, d)), full((d, d)), full((1, d)), full((1, d)),
        ],
        out_specs=pl.BlockSpec((ts * TOK_ROWS, LANES), lambda bi, si: (bi * ns + si, 0)),
        out_shape=jax.ShapeDtypeStruct((bsz * seq * TOK_ROWS, LANES), F32),
        scratch_shapes=[pltpu.VMEM((1, d), F32), pltpu.VMEM((SUBLANES, d), F32)],
        compiler_params=pltpu.CompilerParams(
            dimension_semantics=("arbitrary", "arbitrary"), vmem_limit_bytes=VMEM_LIMIT),
        name="rglru_layer",
    )(x, w_in.astype(BF16), conv_w, row(conv_b), wga.astype(BF16), row(bga), wgx.astype(BF16),
      row(bgx), row(lam), w_out.astype(BF16), row(g), row(b))


def _router_body(h_ref, wrt_ref, bias_ref, idx_ref, gate_ref, rank_ref, cnt_ref, carry_ref, *, tt):
    n_e = N_EXPERTS
    per_group = n_e // N_EXPERT_GROUPS

    @pl.when(pl.program_id(0) == 0)
    def _():
        carry_ref[...] = jnp.zeros_like(carry_ref)

    h = _rows_load(h_ref, tt)
    logits = lax.dot_general(wrt_ref[...], h.astype(BF16), (((1,), (1,)), ((), ())),
                             preferred_element_type=F32)
    scores = jax.nn.sigmoid(logits)
    biased = scores + bias_ref[...]

    j_iota = lax.broadcasted_iota(jnp.int32, (per_group, tt), 0)
    group_score = []
    for g in range(N_EXPERT_GROUPS):
        bg = biased[g * per_group:(g + 1) * per_group]
        m1 = jnp.max(bg, axis=0, keepdims=True)
        i1 = jnp.min(jnp.where(bg == m1, j_iota, per_group), axis=0, keepdims=True)
        m2 = jnp.max(jnp.where(j_iota == i1, -jnp.inf, bg), axis=0, keepdims=True)
        group_score.append(m1 + m2)

    masked = []
    for g in range(N_EXPERT_GROUPS):
        beaten_by = jnp.zeros((1, tt), jnp.int32)
        for o in range(N_EXPERT_GROUPS):
            if o == g:
                continue
            wins = group_score[o] > group_score[g]
            if o < g:
                wins = wins | (group_score[o] == group_score[g])
            beaten_by = beaten_by + wins.astype(jnp.int32)
        keep = beaten_by < TOPK_GROUPS
        masked.append(jnp.where(keep, biased[g * per_group:(g + 1) * per_group], MASK_VALUE))
    cur = jnp.concatenate(masked, axis=0)

    e_iota = lax.broadcasted_iota(jnp.int32, (n_e, tt), 0)
    idx_rows, score_rows, sels = [], [], []
    for _ in range(TOP_K):
        m = jnp.max(cur, axis=0, keepdims=True)
        ik = jnp.min(jnp.where(cur == m, e_iota, n_e), axis=0, keepdims=True)
        sel = e_iota == ik
        score_rows.append(jnp.sum(jnp.where(sel, scores, 0.0), axis=0, keepdims=True))
        cur = jnp.where(sel, -jnp.inf, cur)
        idx_rows.append(ik)
        sels.append(sel)
    top_s = jnp.concatenate(score_rows, axis=0)
    gate_ref[...] = top_s / jnp.sum(top_s, axis=0, keepdims=True) * ROUTED_SCALE
    idx_ref[...] = jnp.concatenate(idx_rows, axis=0)
    multi_hot = jnp.where(cur == -jnp.inf, 1.0, 0.0)

    t_row = lax.broadcasted_iota(jnp.int32, (tt, tt), 0)
    t_col = lax.broadcasted_iota(jnp.int32, (tt, tt), 1)
    strict_upper = jnp.where(t_row < t_col, 1.0, 0.0).astype(BF16)
    before = _bdot(multi_hot.astype(BF16), strict_upper) + carry_ref[...]
    rank_rows = [jnp.sum(jnp.where(sel, before, 0.0), axis=0, keepdims=True) for sel in sels]
    rank_ref[...] = jnp.concatenate(rank_rows, axis=0).astype(jnp.int32)
    carry = carry_ref[...] + jnp.sum(multi_hot, axis=1, keepdims=True)
    carry_ref[...] = carry
    cnt_ref[...] = carry.astype(jnp.int32)


def _router(h_tiles, w_router, router_bias):
    n_tok = h_tiles.shape[0] // TOK_ROWS
    tt = ROUTER_TT
    kt = lambda dt: jax.ShapeDtypeStruct((TOP_K, n_tok), dt)
    tok_spec = pl.BlockSpec((TOP_K, tt), lambda i: (0, i))
    return pl.pallas_call(
        functools.partial(_router_body, tt=tt),
        grid=(n_tok // tt,),
        in_specs=[
            pl.BlockSpec((tt * TOK_ROWS, LANES), lambda i: (i, 0)),
            pl.BlockSpec((N_EXPERTS, D_MODEL), lambda i: (0, 0)),
            pl.BlockSpec((N_EXPERTS, 1), lambda i: (0, 0)),
        ],
        out_specs=[tok_spec, tok_spec, tok_spec, pl.BlockSpec((N_EXPERTS, 1), lambda i: (0, 0))],
        out_shape=[kt(jnp.int32), kt(F32), kt(jnp.int32),
                   jax.ShapeDtypeStruct((N_EXPERTS, 1), jnp.int32)],
        scratch_shapes=[pltpu.VMEM((N_EXPERTS, 1), F32)],
        compiler_params=pltpu.CompilerParams(
            dimension_semantics=("arbitrary",), vmem_limit_bytes=VMEM_LIMIT),
        name="moe_router",
    )(h_tiles, w_router.T.astype(BF16), router_bias.reshape(N_EXPERTS, 1))


def _pack_rows(x):
    half = D_MODEL // 2
    return pltpu.pack_elementwise([x[:, :half], x[:, half:]], packed_dtype=BF16)


def _unpack_rows(p):
    return jnp.concatenate(
        [pltpu.unpack_elementwise(p, index=i, packed_dtype=BF16, unpacked_dtype=F32) for i in range(2)],
        axis=1)


def _packed_load(ref, n_tok):
    return jnp.concatenate(
        [ref[pl.ds(s, n_tok, stride=PK_ROWS), :] for s in range(PK_ROWS)], axis=1)


def _packed_store(ref, val, n_tok):
    for s in range(PK_ROWS):
        ref[pl.ds(s, n_tok, stride=PK_ROWS), :] = val[:, s * LANES:(s + 1) * LANES]


def _sorted_rows_body(idx_ref, rank_ref, start_ref, o_ref):
    tl = idx_ref.shape[1]
    e_iota = lax.broadcasted_iota(jnp.int32, (N_EXPERTS, tl), 0)
    start = start_ref[...]
    rows = []
    for k in range(TOP_K):
        hit = e_iota == idx_ref[k:k + 1, :]
        base = jnp.sum(jnp.where(hit, start, 0.0), axis=0, keepdims=True)
        rows.append(base.astype(jnp.int32) + rank_ref[k:k + 1, :])
    o_ref[...] = jnp.concatenate(rows, axis=0)


def _sorted_rows(top_idx, rank, pad_start):
    n_tok = top_idx.shape[1]
    tl = SORTED_ROWS_TL
    tok = pl.BlockSpec((TOP_K, tl), lambda i: (0, i))
    return pl.pallas_call(
        _sorted_rows_body,
        grid=(n_tok // tl,),
        in_specs=[tok, tok, pl.BlockSpec((N_EXPERTS, 1), lambda i: (0, 0))],
        out_specs=tok,
        out_shape=jax.ShapeDtypeStruct((TOP_K, n_tok), jnp.int32),
        compiler_params=pltpu.CompilerParams(
            dimension_semantics=("arbitrary",), vmem_limit_bytes=VMEM_LIMIT),
        name="moe_sorted_rows",
    )(top_idx, rank, pad_start.astype(F32).reshape(N_EXPERTS, 1))


def _dispatch_body(start_ref, cnt_ref, dest_ref, h_ref, xs_ref, pk_ref, zero_ref, sem, zero_sem,
                   *, tt, tb):
    @pl.when(pl.program_id(0) == 0)
    def _():
        zero_ref[...] = jnp.zeros_like(zero_ref)

        def pad_copies(e, fn):
            n_pad = ((cnt_ref[e] + (tb - 1)) & (-tb)) - cnt_ref[e]
            row = start_ref[e] + cnt_ref[e]
            for bit in range(tb.bit_length() - 1):
                size = 1 << bit
                has_bit = ((n_pad >> bit) & 1) == 1

                @pl.when(has_bit)
                def _():
                    fn(pltpu.make_async_copy(zero_ref.at[pl.ds(0, size)],
                                             xs_ref.at[pl.ds(row, size)], zero_sem))
                row = row + jnp.where(has_bit, size, 0)

        def start_all(e, c):
            pad_copies(e, lambda cp: cp.start())
            return c

        def wait_all(e, c):
            pad_copies(e, lambda cp: cp.wait())
            return c

        lax.fori_loop(0, N_EXPERTS, start_all, 0)
        lax.fori_loop(0, N_EXPERTS, wait_all, 0)

    packed = _pack_rows(_rows_load(h_ref, tt))
    for s in range(PK_ROWS):
        pk_ref[:, s, :] = packed[:, s * LANES:(s + 1) * LANES]
    for t in range(tt):
        for k in range(TOP_K):
            pltpu.make_async_copy(pk_ref.at[t], xs_ref.at[dest_ref[k, t]], sem).start(priority=k % 2)
    for k in range(TOP_K):
        pltpu.make_async_copy(pk_ref, xs_ref.at[pl.ds(0, tt)], sem).wait()


def _dispatch(h_tiles, dest_rows, pad_start, counts, n_rows):
    n_tok = h_tiles.shape[0] // TOK_ROWS
    tt = DISPATCH_TT
    grid_spec = pltpu.PrefetchScalarGridSpec(
        num_scalar_prefetch=2,
        grid=(n_tok // tt,),
        in_specs=[pl.BlockSpec((TOP_K, tt), lambda i, s, c: (0, i), memory_space=pltpu.SMEM),
                  pl.BlockSpec((tt * TOK_ROWS, LANES), lambda i, s, c: (i, 0))],
        out_specs=pl.BlockSpec(memory_space=pl.ANY),
        scratch_shapes=[pltpu.VMEM((tt, PK_ROWS, LANES), I32),
                        pltpu.VMEM((EXPERT_TB // 2, PK_ROWS, LANES), I32),
                        pltpu.SemaphoreType.DMA(()), pltpu.SemaphoreType.DMA(())],
    )
    return pl.pallas_call(
        functools.partial(_dispatch_body, tt=tt, tb=EXPERT_TB),
        grid_spec=grid_spec,
        out_shape=jax.ShapeDtypeStruct((n_rows, PK_ROWS, LANES), I32),
        compiler_params=pltpu.CompilerParams(
            dimension_semantics=("arbitrary",), vmem_limit_bytes=VMEM_LIMIT),
        name="moe_dispatch",
    )(pad_start, counts, dest_rows, h_tiles)


def _expert_body(be_ref, first_ref, slot_ref, next_ref, na_ref, xs_ref, w1_ref, w3_ref, w2_ref, ys_ref,
                 w13_buf, w2_buf, w1b, w3b, w2b, sem, *, tb, layer):
    i = pl.program_id(0)

    def weight_copies(expert, slot):
        return (pltpu.make_async_copy(w1_ref.at[layer, expert], w13_buf.at[slot, 0], sem.at[slot]),
                pltpu.make_async_copy(w3_ref.at[layer, expert], w13_buf.at[slot, 1], sem.at[slot]),
                pltpu.make_async_copy(w2_ref.at[layer, expert], w2_buf.at[slot], sem.at[slot]))

    @pl.when(i < na_ref[0])
    def _():
        @pl.when(first_ref[i] == 1)
        def _():
            slot = slot_ref[i]

            @pl.when(i == 0)
            def _():
                for cp in weight_copies(be_ref[0], slot):
                    cp.start()

            for cp in weight_copies(be_ref[i], slot):
                cp.wait()

            @pl.when(next_ref[i] >= 0)
            def _():
                for cp in weight_copies(next_ref[i], 1 - slot):
                    cp.start()

            w1b[...] = w13_buf[slot, 0].astype(BF16)
            w3b[...] = w13_buf[slot, 1].astype(BF16)
            w2b[...] = w2_buf[slot].astype(BF16)

        x = _unpack_rows(_packed_load(xs_ref, tb)).astype(BF16)
        mid = (_silu(_bdot(x, w1b[...])) * _bdot(x, w3b[...])).astype(BF16)
        _packed_store(ys_ref, _pack_rows(_bdot(mid, w2b[...])), tb)


def _experts(xs, block_expert, first, slot, next_expert, n_active, w1, w3, w2, layer):
    tb = EXPERT_TB
    n_blocks = xs.shape[0] // (tb * PK_ROWS)
    row_block = lambda i, be, fi, sl, nx, na: (jnp.minimum(i, na[0] - 1), 0)
    grid_spec = pltpu.PrefetchScalarGridSpec(
        num_scalar_prefetch=5,
        grid=(n_blocks,),
        in_specs=[
            pl.BlockSpec((tb * PK_ROWS, LANES), row_block),
            pl.BlockSpec(memory_space=pl.ANY), pl.BlockSpec(memory_space=pl.ANY),
            pl.BlockSpec(memory_space=pl.ANY),
        ],
        out_specs=pl.BlockSpec((tb * PK_ROWS, LANES), row_block),
        scratch_shapes=[pltpu.VMEM((2, 2, D_MODEL, D_EXPERT), F32),
                        pltpu.VMEM((2, D_EXPERT, D_MODEL), F32),
                        pltpu.VMEM((D_MODEL, D_EXPERT), BF16), pltpu.VMEM((D_MODEL, D_EXPERT), BF16),
                        pltpu.VMEM((D_EXPERT, D_MODEL), BF16), pltpu.SemaphoreType.DMA((2,))],
    )
    return pl.pallas_call(
        functools.partial(_expert_body, tb=tb, layer=layer),
        grid_spec=grid_spec,
        out_shape=jax.ShapeDtypeStruct(xs.shape, I32),
        compiler_params=pltpu.CompilerParams(
            dimension_semantics=("arbitrary",), vmem_limit_bytes=VMEM_LIMIT),
        name="moe_experts",
    )(block_expert, first, slot, next_expert, n_active, xs, w1, w3, w2)


def _store_head_slabs(y, o_refs, scales):
    for j, (o_ref, scale) in enumerate(zip(o_refs, scales)):
        for s in range(PAIR_SLABS):
            lo = j * O_WIDTH + 2 * s * LANES
            o_ref[s] = pltpu.pack_elementwise(
                [y[:, lo:lo + LANES] * scale, y[:, lo + LANES:lo + 2 * LANES] * scale],
                packed_dtype=BF16)


def _combine_body(src_ref, h_ref, gates_ref, ys_ref, ys_flat_ref, ws1_ref, ws3_ref, ws2_ref,
                  g_ref, b_ref, o_ref, gbuf, sem, *, tt, n_tiles, token_tile_out):
    j = pl.program_id(0)

    def step(issue, finish):
        nxt = j % 2
        cur = (j - 1) % 2
        if finish:
            h = _rows_load(h_ref, tt)
            hb = h.astype(BF16)
            mid = (_silu(_bdot(hb, ws1_ref[...])) * _bdot(hb, ws3_ref[...])).astype(BF16)
            routed = _bdot(mid, ws2_ref[...])
            gates = gates_ref[...]
        for k in range(TOP_K):
            if issue:
                for t in range(tt):
                    pltpu.make_async_copy(ys_ref.at[src_ref[k, t]],
                                          gbuf.at[nxt, k, pl.ds(t * PK_ROWS, PK_ROWS), :],
                                          sem.at[nxt]).start(priority=t % 2)
            if finish:
                pltpu.make_async_copy(ys_flat_ref.at[pl.ds(0, tt * PK_ROWS), :], gbuf.at[cur, k],
                                      sem.at[cur]).wait()
                routed = routed + gates[:, k:k + 1] * _unpack_rows(_packed_load(gbuf.at[cur, k], tt))
        if finish:
            out = _layer_norm(DEEPNORM_ALPHA * h + routed, g_ref[...], b_ref[...])
            if token_tile_out:
                _rows_store(o_ref, out, tt)
            else:
                o_ref[...] = out

    pl.when(j == 0)(lambda: step(True, False))
    pl.when((j >= 1) & (j < n_tiles))(lambda: step(True, True))
    pl.when(j == n_tiles)(lambda: step(False, True))


def _combine(h_tiles, src_rows, gates_tk, ys, ws1, ws3, ws2, g, b, token_tile_out):
    n_tok = h_tiles.shape[0] // TOK_ROWS
    tt = COMBINE_TT
    n_tiles = n_tok // tt
    prev_tile = lambda j: (jnp.maximum(j - 1, 0), 0)
    full = lambda shape: pl.BlockSpec(shape, lambda j: (0,) * len(shape))
    if token_tile_out:
        out_spec = pl.BlockSpec((tt * TOK_ROWS, LANES), prev_tile)
        out_shape = jax.ShapeDtypeStruct((n_tok * TOK_ROWS, LANES), F32)
    else:
        out_spec = pl.BlockSpec((tt, D_MODEL), prev_tile)
        out_shape = jax.ShapeDtypeStruct((n_tok, D_MODEL), F32)
    n_rows = ys.shape[0] // PK_ROWS
    return pl.pallas_call(
        functools.partial(_combine_body, tt=tt, n_tiles=n_tiles, token_tile_out=token_tile_out),
        grid=(n_tiles + 1,),
        in_specs=[pl.BlockSpec((TOP_K, tt), lambda j: (0, jnp.minimum(j, n_tiles - 1)),
                               memory_space=pltpu.SMEM),
                  pl.BlockSpec((tt * TOK_ROWS, LANES), prev_tile),
                  pl.BlockSpec((tt, TOP_K), prev_tile),
                  pl.BlockSpec(memory_space=pl.ANY), pl.BlockSpec(memory_space=pl.ANY),
                  full((D_MODEL, D_EXPERT)), full((D_MODEL, D_EXPERT)), full((D_EXPERT, D_MODEL)),
                  full((1, D_MODEL)), full((1, D_MODEL))],
        out_specs=out_spec,
        out_shape=out_shape,
        scratch_shapes=[pltpu.VMEM((2, TOP_K, tt * PK_ROWS, LANES), I32),
                        pltpu.SemaphoreType.DMA((2,))],
        compiler_params=pltpu.CompilerParams(
            dimension_semantics=("arbitrary",), vmem_limit_bytes=VMEM_LIMIT),
        name="moe_combine",
    )(src_rows, h_tiles, gates_tk, ys.reshape(n_rows, PK_ROWS, LANES), ys, ws1.astype(BF16),
      ws3.astype(BF16), ws2.astype(BF16), g.reshape(1, -1), b.reshape(1, -1))


def _moe_layer(h_tiles, w_router, router_bias, w1, w3, w2, layer, ws1, ws3, ws2, g, b,
               token_tile_out):
    n_tok = h_tiles.shape[0] // TOK_ROWS
    tb = EXPERT_TB
    top_idx, gates, rank, counts = _router(h_tiles, w_router, router_bias)

    n_blocks = -(-(n_tok * TOP_K + N_EXPERTS * (tb - 1)) // tb)
    counts = counts.reshape(N_EXPERTS)
    padded = ((counts + tb - 1) // tb) * tb
    pad_end = jnp.cumsum(padded)
    pad_start = (pad_end - padded).astype(jnp.int32)
    block_first_row = jnp.arange(n_blocks, dtype=jnp.int32) * tb
    block_expert = jnp.minimum(
        jnp.sum((pad_end[None, :] <= block_first_row[:, None]).astype(jnp.int32), axis=1),
        N_EXPERTS - 1)
    n_active = (pad_end[-1:] // tb).astype(jnp.int32)
    present = counts > 0
    expert_ids = jnp.arange(N_EXPERTS, dtype=jnp.int32)
    later = lax.cummin(jnp.where(present, expert_ids, N_EXPERTS), reverse=True)
    next_present = jnp.concatenate([later[1:], jnp.full((1,), N_EXPERTS, jnp.int32)])
    next_present = jnp.where(next_present < N_EXPERTS, next_present, -1)
    ordinal = jnp.cumsum(present.astype(jnp.int32)) - 1
    first = jnp.concatenate([jnp.ones((1,), jnp.int32),
                             (block_expert[1:] != block_expert[:-1]).astype(jnp.int32)])
    slot = ordinal[block_expert] % 2
    next_expert = next_present[block_expert]

    rows = _sorted_rows(top_idx, rank, pad_start)
    xs = _dispatch(h_tiles, rows, pad_start, counts, n_blocks * tb)
    ys = _experts(xs.reshape(n_blocks * tb * PK_ROWS, LANES), block_expert, first, slot, next_expert,
                  n_active, w1, w3, w2, layer)
    return _combine(h_tiles, rows, gates.T, ys, ws1, ws3, ws2, g, b, token_tile_out)


def _proj_body(h_ref, w_ref, *o_refs, tt, scales):
    y = _bdot(_rows_load(h_ref, tt).astype(BF16), w_ref[...])
    _store_head_slabs(y, o_refs, scales)


def _project(h_tiles, w, scales):
    n_tok = h_tiles.shape[0] // TOK_ROWS
    tt = PROJ_TT
    n_out = len(scales)
    out_spec = pl.BlockSpec((PAIR_SLABS, tt, LANES), lambda i: (0, i, 0))
    return pl.pallas_call(
        functools.partial(_proj_body, tt=tt, scales=tuple(scales)),
        grid=(n_tok // tt,),
        in_specs=[pl.BlockSpec((tt * TOK_ROWS, LANES), lambda i: (i, 0)),
                  pl.BlockSpec(w.shape, lambda i: (0, 0))],
        out_specs=[out_spec] * n_out,
        out_shape=[jax.ShapeDtypeStruct((PAIR_SLABS, n_tok, LANES), I32)] * n_out,
        compiler_params=pltpu.CompilerParams(
            dimension_semantics=("arbitrary",), vmem_limit_bytes=VMEM_LIMIT),
        name="projection",
    )(h_tiles, w.astype(BF16))


def _attn_body(q_ref, kp_ref, kc_ref, vp_ref, vc_ref, o_ref, lse_ref, *, dilation, n_steps):
    t = ATT_BLOCK
    not_first = pl.program_id(1) > 0
    qi = lax.broadcasted_iota(jnp.int32, (t, 2 * t), 0)
    kj = lax.broadcasted_iota(jnp.int32, (t, 2 * t), 1)
    dist = t + qi - kj
    valid = (dist >= 0) & (dist <= n_steps) & (not_first | (kj >= t))
    token_dist = (dilation * dist).astype(F32)
    lane = lax.broadcasted_iota(jnp.int32, (t, LANES), 1)
    low_half = lane < HEAD_DIM
    lane_head = lane // LSE_LANES
    contract_last = (((1,), (1,)), ((), ()))

    def unpack(words, which):
        return pltpu.unpack_elementwise(words, index=which, packed_dtype=BF16,
                                        unpacked_dtype=F32).astype(BF16)

    def residue(r, carry):
        rows = pl.ds(r, t) if dilation == 1 else pl.ds(r, t, stride=dilation)
        lse_all = jnp.zeros((t, LANES), F32)
        for slab in range(PAIR_SLABS):
            q_words = q_ref.at[slab][rows, :]
            k_words = jnp.concatenate([kp_ref.at[slab][rows, :], kc_ref.at[slab][rows, :]], axis=0)
            v_words = jnp.concatenate([vp_ref.at[slab][rows, :], vc_ref.at[slab][rows, :]], axis=0)
            pair_outs = []
            for which in range(2):
                pair = 2 * slab + which
                q2, k2, v2 = unpack(q_words, which), unpack(k_words, which), unpack(v_words, which)
                outs, lses = [], []
                for half in range(2):
                    head = 2 * pair + half
                    slope = 2.0 ** (-8.0 * (head + 1) / ATT_HEADS)
                    keep = low_half if half == 0 else ~low_half
                    qh = jnp.where(keep, q2, jnp.zeros_like(q2))
                    s = lax.dot_general(qh, k2, contract_last, preferred_element_type=F32)
                    s = jnp.where(valid, s - slope * token_dist, MASK_VALUE)
                    m = jnp.max(s, axis=-1, keepdims=True)
                    p = jnp.exp(s - m)
                    l = jnp.sum(p, axis=-1, keepdims=True)
                    outs.append(_bdot(p.astype(BF16), v2) / l)
                    lse_all = jnp.where(lane_head == head, m + jnp.log(l), lse_all)
                pair_outs.append(jnp.where(low_half, outs[0], outs[1]))
            o_ref.at[slab][rows, :] = pltpu.pack_elementwise(pair_outs, packed_dtype=BF16)
        lse_ref[rows, :] = lse_all
        return carry

    if dilation == 1:
        residue(0, 0)
    else:
        lax.fori_loop(0, dilation, residue, 0)


def _attn_branch(q, k, v, bsz, seq, window, dilation):
    span = ATT_BLOCK * dilation
    n_spans = seq // span
    cur = pl.BlockSpec((PAIR_SLABS, span, LANES), lambda bi, n: (0, bi * n_spans + n, 0))
    prev = pl.BlockSpec((PAIR_SLABS, span, LANES),
                        lambda bi, n: (0, bi * n_spans + jnp.maximum(n - 1, 0), 0))
    return pl.pallas_call(
        functools.partial(_attn_body, dilation=dilation, n_steps=window // dilation),
        grid=(bsz, n_spans),
        in_specs=[cur, prev, cur, prev, cur],
        out_specs=[cur, pl.BlockSpec((span, LANES), lambda bi, n: (bi * n_spans + n, 0))],
        out_shape=[jax.ShapeDtypeStruct((PAIR_SLABS, bsz * seq, LANES), I32),
                   jax.ShapeDtypeStruct((bsz * seq, LANES), F32)],
        compiler_params=pltpu.CompilerParams(
            dimension_semantics=("arbitrary",) * 2, vmem_limit_bytes=VMEM_LIMIT),
        name=f"dilated_attn_d{dilation}",
    )(q, k, k, v, v)


def _merge_body(*refs, tt):
    o_refs = refs[:N_GROUPS]
    lse_refs = refs[N_GROUPS:2 * N_GROUPS]
    h_ref, wo_ref, g_ref, b_ref, out_ref = refs[2 * N_GROUPS:]
    lses = [r[...] for r in lse_refs]
    m = functools.reduce(jnp.maximum, lses)
    es = [jnp.exp(l - m) for l in lses]
    den = functools.reduce(lambda a, c: a + c, es)
    row = lax.broadcasted_iota(jnp.int32, (LANES, O_WIDTH), 0)
    col = lax.broadcasted_iota(jnp.int32, (LANES, O_WIDTH), 1)
    select = jnp.where(row == (col // HEAD_DIM) * LSE_LANES, 1.0, 0.0).astype(BF16)

    def widen(w):
        hi = w.astype(BF16)
        lo = (w - hi.astype(F32)).astype(BF16)
        return _bdot(hi, select) + _bdot(lo, select)

    def branch_out(o_ref):
        return jnp.concatenate(
            [pltpu.unpack_elementwise(o_ref[pair // 2], index=pair % 2, packed_dtype=BF16,
                                      unpacked_dtype=F32) for pair in range(ATT_HEADS // 2)], axis=1)

    o = functools.reduce(lambda a, c: a + c,
                         [widen(e / den) * branch_out(r) for e, r in zip(es, o_refs)])
    mix = _bdot(o.astype(BF16), wo_ref[...])
    z = DEEPNORM_ALPHA * _rows_load(h_ref, tt) + mix
    _rows_store(out_ref, _layer_norm(z, g_ref[...], b_ref[...]), tt)


def _merge(outs, lses, h_tiles, w_o, g, b):
    n_tok = h_tiles.shape[0] // TOK_ROWS
    tt = MERGE_TT
    o_spec = pl.BlockSpec((PAIR_SLABS, tt, LANES), lambda i: (0, i, 0))
    lse_spec = pl.BlockSpec((tt, LANES), lambda i: (i, 0))
    tiles = pl.BlockSpec((tt * TOK_ROWS, LANES), lambda i: (i, 0))
    full = lambda shape: pl.BlockSpec(shape, lambda i: (0,) * len(shape))
    return pl.pallas_call(
        functools.partial(_merge_body, tt=tt),
        grid=(n_tok // tt,),
        in_specs=[o_spec] * N_GROUPS + [lse_spec] * N_GROUPS + [
            tiles, full((O_WIDTH, D_MODEL)), full((1, D_MODEL)), full((1, D_MODEL))],
        out_specs=tiles,
        out_shape=jax.ShapeDtypeStruct(h_tiles.shape, F32),
        compiler_params=pltpu.CompilerParams(
            dimension_semantics=("arbitrary",), vmem_limit_bytes=VMEM_LIMIT),
        name="attn_merge",
    )(*outs, *lses, h_tiles, w_o.astype(BF16), g.reshape(1, -1), b.reshape(1, -1))


def kernel(x, a_w_in, a_conv_w, a_conv_b, a_w_gate_a, a_b_gate_a, a_w_gate_x, a_b_gate_x, a_lambda, a_w_out, w_kv_shared, b_w_q, b_w_o, moe_w_router, moe_router_bias, moe_w1, moe_w3, moe_w2, moe_ws1, moe_ws3, moe_ws2, ln_g, ln_b):
    bsz, seq, d = x.shape
    assert d == D_MODEL and seq % (DILATION_PATTERNS[-1][1] * ATT_BLOCK) == 0

    def moe(h_tiles, layer, token_tile_out):
        return _moe_layer(h_tiles, moe_w_router[layer], moe_router_bias[layer], moe_w1, moe_w3,
                          moe_w2, layer, moe_ws1[layer], moe_ws3[layer], moe_ws2[layer],
                          ln_g[layer, 1], ln_b[layer, 1], token_tile_out)

    h = _rglru_layer(x, a_w_in[0], a_conv_w[0], a_conv_b[0], a_w_gate_a[0], a_b_gate_a[0],
                     a_w_gate_x[0], a_b_gate_x[0], a_lambda[0], a_w_out[0], ln_g[0, 0], ln_b[0, 0])
    h = moe(h, 0, True)
    kvq = _project(h, jnp.concatenate([w_kv_shared, b_w_q[0]], axis=1),
                   (1.0,) * (2 * N_GROUPS) + (HEAD_DIM ** -0.5,) * N_GROUPS)
    ks, vs, qs = kvq[:N_GROUPS], kvq[N_GROUPS:2 * N_GROUPS], kvq[2 * N_GROUPS:]

    outs, lses = [], []
    for grp, (window, dilation) in enumerate(DILATION_PATTERNS):
        o, lse = _attn_branch(qs[grp], ks[grp], vs[grp], bsz, seq, window, dilation)
        outs.append(o)
        lses.append(lse)
    h = _merge(outs, lses, h, b_w_o[0], ln_g[1, 0], ln_b[1, 0])
    h = moe(h, 1, False)
    return h.reshape(bsz, seq, d)
```

```python
import functools
import math

import jax
import jax.numpy as jnp
from jax import lax
from jax.experimental import pallas as pl
from jax.experimental.pallas import tpu as pltpu

F32 = jnp.float32
BF16 = jnp.bfloat16
I32 = jnp.int32

D_MODEL = 1024
LRU_BLOCKS = 4
CONV_WIDTH = 4
LRU_C = 8.0
ATT_HEADS = 8
HEAD_DIM = 64
DILATION_PATTERNS = ((128, 1), (512, 4), (2048, 16))
N_GROUPS = len(DILATION_PATTERNS)
ATT_BLOCK = 128
O_WIDTH = ATT_HEADS * HEAD_DIM
N_EXPERTS = 256
TOP_K = 8
N_EXPERT_GROUPS = 8
TOPK_GROUPS = 4
D_EXPERT = 256
ROUTED_SCALE = 2.5
DEPTH = 2
DEEPNORM_ALPHA = (2 * DEPTH) ** 0.25
LN_EPS = 1e-5
MASK_VALUE = -1e30

SUBLANES = 8
LANES = 128
TOK_ROWS = D_MODEL // LANES
PK_ROWS = D_MODEL // 2 // LANES
PAIR_SLABS = ATT_HEADS * HEAD_DIM // LANES // 2
LSE_LANES = LANES // ATT_HEADS
VMEM_LIMIT = 56 * 1024 * 1024

RGLRU_TS = 512
ROUTER_TT = 512
SORTED_ROWS_TL = 2048
DISPATCH_TT = 512
EXPERT_TB = 512
COMBINE_TT = 512
PROJ_TT = 512
MERGE_TT = 256


def _rows_load(ref, n_tok):
    return jnp.concatenate(
        [ref[pl.ds(s, n_tok, stride=TOK_ROWS), :] for s in range(TOK_ROWS)], axis=1)


def _rows_store(ref, val, n_tok):
    for s in range(TOK_ROWS):
        ref[pl.ds(s, n_tok, stride=TOK_ROWS), :] = val[:, s * LANES:(s + 1) * LANES]


def _layer_norm(z, g, b):
    mu = jnp.mean(z, axis=-1, keepdims=True)
    zc = z - mu
    var = jnp.mean(zc * zc, axis=-1, keepdims=True)
    return zc * lax.rsqrt(var + LN_EPS) * g + b


def _silu(x):
    return x * jax.nn.sigmoid(x)


def _gelu_tanh(x):
    c = math.sqrt(2.0 / math.pi)
    return 0.5 * x * (1.0 + jnp.tanh(c * (x + 0.044715 * (x * x * x))))


def _bdot(a, b):
    return jnp.dot(a, b, preferred_element_type=F32)


def _rglru_body(x_ref, win_ref, cw_ref, cb_ref, wga_ref, bga_ref, wgx_ref, bgx_ref, lam_ref,
                wout_ref, g_ref, b_ref, o_ref, hc_ref, tail_ref, *, ts):
    width = D_MODEL
    bw = width // LRU_BLOCKS

    @pl.when(pl.program_id(1) == 0)
    def _():
        hc_ref[...] = jnp.zeros_like(hc_ref)
        tail_ref[...] = jnp.zeros_like(tail_ref)

    x = x_ref[...]
    xz = _bdot(x.astype(BF16), win_ref[...])
    xr = xz[:, :width]
    gate = xz[:, width:]

    tail = tail_ref[...]
    row8 = lax.broadcasted_iota(jnp.int32, (SUBLANES, width), 0)
    cw = cw_ref[...]
    xc = xr * cw[CONV_WIDTH - 1:CONV_WIDTH, :] + cb_ref[...]
    for j in range(1, CONV_WIDTH):
        rx = pltpu.roll(xr, j, 0)
        rp = pltpu.roll(tail, j, 0)
        top = jnp.where(row8 < j, rp, rx[:SUBLANES])
        shifted = jnp.concatenate([top, rx[SUBLANES:]], axis=0)
        xc = xc + shifted * cw[CONV_WIDTH - 1 - j:CONV_WIDTH - j, :]
    tail_ref[...] = xr[ts - SUBLANES:]

    xcb = xc.astype(BF16)

    def block_diag(w_ref):
        return jnp.concatenate(
            [_bdot(xcb[:, n * bw:(n + 1) * bw], w_ref[n]) for n in range(LRU_BLOCKS)], axis=1)

    r = jax.nn.sigmoid(block_diag(wga_ref) + bga_ref[...])
    i = jax.nn.sigmoid(block_diag(wgx_ref) + bgx_ref[...])
    lam = lam_ref[...]
    softplus_neg_lam = jnp.maximum(-lam, 0.0) + jnp.log1p(jnp.exp(-jnp.abs(lam)))
    log_a = (-LRU_C * r) * softplus_neg_lam
    a = jnp.exp(log_a)
    mult = jnp.sqrt(-jnp.tanh(log_a) * (a * a + 1.0))
    u = mult * (i * xc)

    groups = ts // SUBLANES
    a3 = a.reshape(groups, SUBLANES, width)
    u3 = u.reshape(groups, SUBLANES, width)
    sub = lax.broadcasted_iota(jnp.int32, (groups, SUBLANES, width), 1)
    sh = 1
    while sh < SUBLANES:
        a_prev = pltpu.roll(a3, sh, 1)
        u_prev = pltpu.roll(u3, sh, 1)
        live = sub >= sh
        u3 = jnp.where(live, a3 * u_prev, 0.0) + u3
        a3 = jnp.where(live, a3 * a_prev, a3)
        sh *= 2
    carry = hc_ref[...]
    h_groups = []
    for grp in range(groups):
        h_grp = a3[grp] * carry + u3[grp]
        h_groups.append(h_grp)
        carry = h_grp[SUBLANES - 1:SUBLANES]
    h = jnp.concatenate(h_groups, axis=0)
    hc_ref[...] = carry

    y = (h * _gelu_tanh(gate)).astype(BF16)
    mix = _bdot(y, wout_ref[...])
    z = DEEPNORM_ALPHA * x + mix
    _rows_store(o_ref, _layer_norm(z, g_ref[...], b_ref[...]), ts)


def _rglru_layer(x, w_in, conv_w, conv_b, wga, bga, wgx, bgx, lam, w_out, g, b):
    bsz, seq, d = x.shape
    ts = RGLRU_TS
    ns = seq // ts
    row = lambda v: v.reshape(1, -1)
    full = lambda shape: pl.BlockSpec(shape, lambda bi, si: (0,) * len(shape))
    return pl.pallas_call(
        functools.partial(_rglru_body, ts=ts),
        grid=(bsz, ns),
        in_specs=[
            pl.BlockSpec((None, ts, d), lambda bi, si: (bi, si, 0)),
            full((d, 2 * d)), full((CONV_WIDTH, d)), full((1, d)),
            full((LRU_BLOCKS, d // LRU_BLOCKS, d // LRU_BLOCKS)), full((1, d)),
            full((LRU_BLOCKS, d // LRU_BLOCKS, d // LRU_BLOCKS)), full((1, d)),
            full((1, d)), full((d, d)), full((1, d)), full((1, d)),
        ],
        out_specs=pl.BlockSpec((ts * TOK_ROWS, LANES), lambda bi, si: (bi * ns + si, 0)),
        out_shape=jax.ShapeDtypeStruct((bsz * seq * TOK_ROWS, LANES), F32),
        scratch_shapes=[pltpu.VMEM((1, d), F32), pltpu.VMEM((SUBLANES, d), F32)],
        compiler_params=pltpu.CompilerParams(
            dimension_semantics=("arbitrary", "arbitrary"), vmem_limit_bytes=VMEM_LIMIT),
        name="rglru_layer",
    )(x, w_in.astype(BF16), conv_w, row(conv_b), wga.astype(BF16), row(bga), wgx.astype(BF16),
      row(bgx), row(lam), w_out.astype(BF16), row(g), row(b))


def _router_body(h_ref, wrt_ref, bias_ref, idx_ref, gate_ref, rank_ref, cnt_ref, carry_ref, *, tt):
    n_e = N_EXPERTS
    per_group = n_e // N_EXPERT_GROUPS

    @pl.when(pl.program_id(0) == 0)
    def _():
        carry_ref[...] = jnp.zeros_like(carry_ref)

    h = _rows_load(h_ref, tt)
    logits = lax.dot_general(wrt_ref[...], h.astype(BF16), (((1,), (1,)), ((), ())),
                             preferred_element_type=F32)
    scores = jax.nn.sigmoid(logits)
    biased = scores + bias_ref[...]

    j_iota = lax.broadcasted_iota(jnp.int32, (per_group, tt), 0)
    group_score = []
    for g in range(N_EXPERT_GROUPS):
        bg = biased[g * per_group:(g + 1) * per_group]
        m1 = jnp.max(bg, axis=0, keepdims=True)
        i1 = jnp.min(jnp.where(bg == m1, j_iota, per_group), axis=0, keepdims=True)
        m2 = jnp.max(jnp.where(j_iota == i1, -jnp.inf, bg), axis=0, keepdims=True)
        group_score.append(m1 + m2)

    masked = []
    for g in range(N_EXPERT_GROUPS):
        beaten_by = jnp.zeros((1, tt), jnp.int32)
        for o in range(N_EXPERT_GROUPS):
            if o == g:
                continue
            wins = group_score[o] > group_score[g]
            if o < g:
                wins = wins | (group_score[o] == group_score[g])
            beaten_by = beaten_by + wins.astype(jnp.int32)
        keep = beaten_by < TOPK_GROUPS
        masked.append(jnp.where(keep, biased[g * per_group:(g + 1) * per_group], MASK_VALUE))
    cur = jnp.concatenate(masked, axis=0)

    e_iota = lax.broadcasted_iota(jnp.int32, (n_e, tt), 0)
    idx_rows, score_rows, sels = [], [], []
    for _ in range(TOP_K):
        m = jnp.max(cur, axis=0, keepdims=True)
        ik = jnp.min(jnp.where(cur == m, e_iota, n_e), axis=0, keepdims=True)
        sel = e_iota == ik
        score_rows.append(jnp.sum(jnp.where(sel, scores, 0.0), axis=0, keepdims=True))
        cur = jnp.where(sel, -jnp.inf, cur)
        idx_rows.append(ik)
        sels.append(sel)
    top_s = jnp.concatenate(score_rows, axis=0)
    gate_ref[...] = top_s / jnp.sum(top_s, axis=0, keepdims=True) * ROUTED_SCALE
    idx_ref[...] = jnp.concatenate(idx_rows, axis=0)
    multi_hot = jnp.where(cur == -jnp.inf, 1.0, 0.0)

    t_row = lax.broadcasted_iota(jnp.int32, (tt, tt), 0)
    t_col = lax.broadcasted_iota(jnp.int32, (tt, tt), 1)
    strict_upper = jnp.where(t_row < t_col, 1.0, 0.0).astype(BF16)
    before = _bdot(multi_hot.astype(BF16), strict_upper) + carry_ref[...]
    rank_rows = [jnp.sum(jnp.where(sel, before, 0.0), axis=0, keepdims=True) for sel in sels]
    rank_ref[...] = jnp.concatenate(rank_rows, axis=0).astype(jnp.int32)
    carry = carry_ref[...] + jnp.sum(multi_hot, axis=1, keepdims=True)
    carry_ref[...] = carry
    cnt_ref[...] = carry.astype(jnp.int32)


def _router(h_tiles, w_router, router_bias):
    n_tok = h_tiles.shape[0] // TOK_ROWS
    tt = ROUTER_TT
    kt = lambda dt: jax.ShapeDtypeStruct((TOP_K, n_tok), dt)
    tok_spec = pl.BlockSpec((TOP_K, tt), lambda i: (0, i))
    return pl.pallas_call(
        functools.partial(_router_body, tt=tt),
        grid=(n_tok // tt,),
        in_specs=[
            pl.BlockSpec((tt * TOK_ROWS, LANES), lambda i: (i, 0)),
            pl.BlockSpec((N_EXPERTS, D_MODEL), lambda i: (0, 0)),
            pl.BlockSpec((N_EXPERTS, 1), lambda i: (0, 0)),
        ],
        out_specs=[tok_spec, tok_spec, tok_spec, pl.BlockSpec((N_EXPERTS, 1), lambda i: (0, 0))],
        out_shape=[kt(jnp.int32), kt(F32), kt(jnp.int32),
                   jax.ShapeDtypeStruct((N_EXPERTS, 1), jnp.int32)],
        scratch_shapes=[pltpu.VMEM((N_EXPERTS, 1), F32)],
        compiler_params=pltpu.CompilerParams(
            dimension_semantics=("arbitrary",), vmem_limit_bytes=VMEM_LIMIT),
        name="moe_router",
    )(h_tiles, w_router.T.astype(BF16), router_bias.reshape(N_EXPERTS, 1))


def _pack_rows(x):
    half = D_MODEL // 2
    return pltpu.pack_elementwise([x[:, :half], x[:, half:]], packed_dtype=BF16)


def _unpack_rows(p):
    return jnp.concatenate(
        [pltpu.unpack_elementwise(p, index=i, packed_dtype=BF16, unpacked_dtype=F32) for i in range(2)],
        axis=1)


def _packed_load(ref, n_tok):
    return jnp.concatenate(
        [ref[pl.ds(s, n_tok, stride=PK_ROWS), :] for s in range(PK_ROWS)], axis=1)


def _packed_store(ref, val, n_tok):
    for s in range(PK_ROWS):
        ref[pl.ds(s, n_tok, stride=PK_ROWS), :] = val[:, s * LANES:(s + 1) * LANES]


def _sorted_rows_body(idx_ref, rank_ref, start_ref, o_ref):
    tl = idx_ref.shape[1]
    e_iota = lax.broadcasted_iota(jnp.int32, (N_EXPERTS, tl), 0)
    start = start_ref[...]
    rows = []
    for k in range(TOP_K):
        hit = e_iota == idx_ref[k:k + 1, :]
        base = jnp.sum(jnp.where(hit, start, 0.0), axis=0, keepdims=True)
        rows.append(base.astype(jnp.int32) + rank_ref[k:k + 1, :])
    o_ref[...] = jnp.concatenate(rows, axis=0)


def _sorted_rows(top_idx, rank, pad_start):
    n_tok = top_idx.shape[1]
    tl = SORTED_ROWS_TL
    tok = pl.BlockSpec((TOP_K, tl), lambda i: (0, i))
    return pl.pallas_call(
        _sorted_rows_body,
        grid=(n_tok // tl,),
        in_specs=[tok, tok, pl.BlockSpec((N_EXPERTS, 1), lambda i: (0, 0))],
        out_specs=tok,
        out_shape=jax.ShapeDtypeStruct((TOP_K, n_tok), jnp.int32),
        compiler_params=pltpu.CompilerParams(
            dimension_semantics=("arbitrary",), vmem_limit_bytes=VMEM_LIMIT),
        name="moe_sorted_rows",
    )(top_idx, rank, pad_start.astype(F32).reshape(N_EXPERTS, 1))


def _dispatch_body(start_ref, cnt_ref, dest_ref, h_ref, xs_ref, pk_ref, zero_ref, sem, zero_sem,
                   *, tt, tb):
    @pl.when(pl.program_id(0) == 0)
    def _():
        zero_ref[...] = jnp.zeros_like(zero_ref)

        def pad_copies(e, fn):
            n_pad = ((cnt_ref[e] + (tb - 1)) & (-tb)) - cnt_ref[e]
            row = start_ref[e] + cnt_ref[e]
            for bit in range(tb.bit_length() - 1):
                size = 1 << bit
                has_bit = ((n_pad >> bit) & 1) == 1

                @pl.when(has_bit)
                def _():
                    fn(pltpu.make_async_copy(zero_ref.at[pl.ds(0, size)],
                                             xs_ref.at[pl.ds(row, size)], zero_sem))
                row = row + jnp.where(has_bit, size, 0)

        def start_all(e, c):
            pad_copies(e, lambda cp: cp.start())
            return c

        def wait_all(e, c):
            pad_copies(e, lambda cp: cp.wait())
            return c

        lax.fori_loop(0, N_EXPERTS, start_all, 0)
        lax.fori_loop(0, N_EXPERTS, wait_all, 0)

    packed = _pack_rows(_rows_load(h_ref, tt))
    for s in range(PK_ROWS):
        pk_ref[:, s, :] = packed[:, s * LANES:(s + 1) * LANES]
    for t in range(tt):
        for k in range(TOP_K):
            pltpu.make_async_copy(pk_ref.at[t], xs_ref.at[dest_ref[k, t]], sem).start(priority=k % 2)
    for k in range(TOP_K):
        pltpu.make_async_copy(pk_ref, xs_ref.at[pl.ds(0, tt)], sem).wait()


def _dispatch(h_tiles, dest_rows, pad_start, counts, n_rows):
    n_tok = h_tiles.shape[0] // TOK_ROWS
    tt = DISPATCH_TT
    grid_spec = pltpu.PrefetchScalarGridSpec(
        num_scalar_prefetch=2,
        grid=(n_tok // tt,),
        in_specs=[pl.BlockSpec((TOP_K, tt), lambda i, s, c: (0, i), memory_space=pltpu.SMEM),
                  pl.BlockSpec((tt * TOK_ROWS, LANES), lambda i, s, c: (i, 0))],
        out_specs=pl.BlockSpec(memory_space=pl.ANY),
        scratch_shapes=[pltpu.VMEM((tt, PK_ROWS, LANES), I32),
                        pltpu.VMEM((EXPERT_TB // 2, PK_ROWS, LANES), I32),
                        pltpu.SemaphoreType.DMA(()), pltpu.SemaphoreType.DMA(())],
    )
    return pl.pallas_call(
        functools.partial(_dispatch_body, tt=tt, tb=EXPERT_TB),
        grid_spec=grid_spec,
        out_shape=jax.ShapeDtypeStruct((n_rows, PK_ROWS, LANES), I32),
        compiler_params=pltpu.CompilerParams(
            dimension_semantics=("arbitrary",), vmem_limit_bytes=VMEM_LIMIT),
        name="moe_dispatch",
    )(pad_start, counts, dest_rows, h_tiles)


def _expert_body(be_ref, first_ref, slot_ref, next_ref, na_ref, xs_ref, w1_ref, w3_ref, w2_ref, ys_ref,
                 w13_buf, w2_buf, w1b, w3b, w2b, sem, *, tb, layer):
    i = pl.program_id(0)

    def weight_copies(expert, slot):
        return (pltpu.make_async_copy(w1_ref.at[layer, expert], w13_buf.at[slot, 0], sem.at[slot]),
                pltpu.make_async_copy(w3_ref.at[layer, expert], w13_buf.at[slot, 1], sem.at[slot]),
                pltpu.make_async_copy(w2_ref.at[layer, expert], w2_buf.at[slot], sem.at[slot]))

    @pl.when(i < na_ref[0])
    def _():
        @pl.when(first_ref[i] == 1)
        def _():
            slot = slot_ref[i]

            @pl.when(i == 0)
            def _():
                for cp in weight_copies(be_ref[0], slot):
                    cp.start()

            for cp in weight_copies(be_ref[i], slot):
                cp.wait()

            @pl.when(next_ref[i] >= 0)
            def _():
                for cp in weight_copies(next_ref[i], 1 - slot):
                    cp.start()

            w1b[...] = w13_buf[slot, 0].astype(BF16)
            w3b[...] = w13_buf[slot, 1].astype(BF16)
            w2b[...] = w2_buf[slot].astype(BF16)

        x = _unpack_rows(_packed_load(xs_ref, tb)).astype(BF16)
        mid = (_silu(_bdot(x, w1b[...])) * _bdot(x, w3b[...])).astype(BF16)
        _packed_store(ys_ref, _pack_rows(_bdot(mid, w2b[...])), tb)


def _experts(xs, block_expert, first, slot, next_expert, n_active, w1, w3, w2, layer):
    tb = EXPERT_TB
    n_blocks = xs.shape[0] // (tb * PK_ROWS)
    row_block = lambda i, be, fi, sl, nx, na: (jnp.minimum(i, na[0] - 1), 0)
    grid_spec = pltpu.PrefetchScalarGridSpec(
        num_scalar_prefetch=5,
        grid=(n_blocks,),
        in_specs=[
            pl.BlockSpec((tb * PK_ROWS, LANES), row_block),
            pl.BlockSpec(memory_space=pl.ANY), pl.BlockSpec(memory_space=pl.ANY),
            pl.BlockSpec(memory_space=pl.ANY),
        ],
        out_specs=pl.BlockSpec((tb * PK_ROWS, LANES), row_block),
        scratch_shapes=[pltpu.VMEM((2, 2, D_MODEL, D_EXPERT), F32),
                        pltpu.VMEM((2, D_EXPERT, D_MODEL), F32),
                        pltpu.VMEM((D_MODEL, D_EXPERT), BF16), pltpu.VMEM((D_MODEL, D_EXPERT), BF16),
                        pltpu.VMEM((D_EXPERT, D_MODEL), BF16), pltpu.SemaphoreType.DMA((2,))],
    )
    return pl.pallas_call(
        functools.partial(_expert_body, tb=tb, layer=layer),
        grid_spec=grid_spec,
        out_shape=jax.ShapeDtypeStruct(xs.shape, I32),
        compiler_params=pltpu.CompilerParams(
            dimension_semantics=("arbitrary",), vmem_limit_bytes=VMEM_LIMIT),
        name="moe_experts",
    )(block_expert, first, slot, next_expert, n_active, xs, w1, w3, w2)


def _store_head_slabs(y, o_refs, scales):
    for j, (o_ref, scale) in enumerate(zip(o_refs, scales)):
        for s in range(PAIR_SLABS):
            lo = j * O_WIDTH + 2 * s * LANES
            o_ref[s] = pltpu.pack_elementwise(
                [y[:, lo:lo + LANES] * scale, y[:, lo + LANES:lo + 2 * LANES] * scale],
                packed_dtype=BF16)


def _combine_body(src_ref, h_ref, gates_ref, ys_ref, ys_flat_ref, ws1_ref, ws3_ref, ws2_ref,
                  g_ref, b_ref, o_ref, gbuf, sem, *, tt, n_tiles, token_tile_out):
    j = pl.program_id(0)

    def step(issue, finish):
        nxt = j % 2
        cur = (j - 1) % 2
        if finish:
            h = _rows_load(h_ref, tt)
            hb = h.astype(BF16)
            mid = (_silu(_bdot(hb, ws1_ref[...])) * _bdot(hb, ws3_ref[...])).astype(BF16)
            routed = _bdot(mid, ws2_ref[...])
            gates = gates_ref[...]
        for k in range(TOP_K):
            if issue:
                for t in range(tt):
                    pltpu.make_async_copy(ys_ref.at[src_ref[k, t]],
                                          gbuf.at[nxt, k, pl.ds(t * PK_ROWS, PK_ROWS), :],
                                          sem.at[nxt]).start(priority=t % 2)
            if finish:
                pltpu.make_async_copy(ys_flat_ref.at[pl.ds(0, tt * PK_ROWS), :], gbuf.at[cur, k],
                                      sem.at[cur]).wait()
                routed = routed + gates[:, k:k + 1] * _unpack_rows(_packed_load(gbuf.at[cur, k], tt))
        if finish:
            out = _layer_norm(DEEPNORM_ALPHA * h + routed, g_ref[...], b_ref[...])
            if token_tile_out:
                _rows_store(o_ref, out, tt)
            else:
                o_ref[...] = out

    pl.when(j == 0)(lambda: step(True, False))
    pl.when((j >= 1) & (j < n_tiles))(lambda: step(True, True))
    pl.when(j == n_tiles)(lambda: step(False, True))


def _combine(h_tiles, src_rows, gates_tk, ys, ws1, ws3, ws2, g, b, token_tile_out):
    n_tok = h_tiles.shape[0] // TOK_ROWS
    tt = COMBINE_TT
    n_tiles = n_tok // tt
    prev_tile = lambda j: (jnp.maximum(j - 1, 0), 0)
    full = lambda shape: pl.BlockSpec(shape, lambda j: (0,) * len(shape))
    if token_tile_out:
        out_spec = pl.BlockSpec((tt * TOK_ROWS, LANES), prev_tile)
        out_shape = jax.ShapeDtypeStruct((n_tok * TOK_ROWS, LANES), F32)
    else:
        out_spec = pl.BlockSpec((tt, D_MODEL), prev_tile)
        out_shape = jax.ShapeDtypeStruct((n_tok, D_MODEL), F32)
    n_rows = ys.shape[0] // PK_ROWS
    return pl.pallas_call(
        functools.partial(_combine_body, tt=tt, n_tiles=n_tiles, token_tile_out=token_tile_out),
        grid=(n_tiles + 1,),
        in_specs=[pl.BlockSpec((TOP_K, tt), lambda j: (0, jnp.minimum(j, n_tiles - 1)),
                               memory_space=pltpu.SMEM),
                  pl.BlockSpec((tt * TOK_ROWS, LANES), prev_tile),
                  pl.BlockSpec((tt, TOP_K), prev_tile),
                  pl.BlockSpec(memory_space=pl.ANY), pl.BlockSpec(memory_space=pl.ANY),
                  full((D_MODEL, D_EXPERT)), full((D_MODEL, D_EXPERT)), full((D_EXPERT, D_MODEL)),
                  full((1, D_MODEL)), full((1, D_MODEL))],
        out_specs=out_spec,
        out_shape=out_shape,
        scratch_shapes=[pltpu.VMEM((2, TOP_K, tt * PK_ROWS, LANES), I32),
                        pltpu.SemaphoreType.DMA((2,))],
        compiler_params=pltpu.CompilerParams(
            dimension_semantics=("arbitrary",), vmem_limit_bytes=VMEM_LIMIT),
        name="moe_combine",
    )(src_rows, h_tiles, gates_tk, ys.reshape(n_rows, PK_ROWS, LANES), ys, ws1.astype(BF16),
      ws3.astype(BF16), ws2.astype(BF16), g.reshape(1, -1), b.reshape(1, -1))


def _moe_layer(h_tiles, w_router, router_bias, w1, w3, w2, layer, ws1, ws3, ws2, g, b,
               token_tile_out):
    n_tok = h_tiles.shape[0] // TOK_ROWS
    tb = EXPERT_TB
    top_idx, gates, rank, counts = _router(h_tiles, w_router, router_bias)

    n_blocks = -(-(n_tok * TOP_K + N_EXPERTS * (tb - 1)) // tb)
    counts = counts.reshape(N_EXPERTS)
    padded = ((counts + tb - 1) // tb) * tb
    pad_end = jnp.cumsum(padded)
    pad_start = (pad_end - padded).astype(jnp.int32)
    block_first_row = jnp.arange(n_blocks, dtype=jnp.int32) * tb
    block_expert = jnp.minimum(
        jnp.sum((pad_end[None, :] <= block_first_row[:, None]).astype(jnp.int32), axis=1),
        N_EXPERTS - 1)
    n_active = (pad_end[-1:] // tb).astype(jnp.int32)
    present = counts > 0
    expert_ids = jnp.arange(N_EXPERTS, dtype=jnp.int32)
    later = lax.cummin(jnp.where(present, expert_ids, N_EXPERTS), reverse=True)
    next_present = jnp.concatenate([later[1:], jnp.full((1,), N_EXPERTS, jnp.int32)])
    next_present = jnp.where(next_present < N_EXPERTS, next_present, -1)
    ordinal = jnp.cumsum(present.astype(jnp.int32)) - 1
    first = jnp.concatenate([jnp.ones((1,), jnp.int32),
                             (block_expert[1:] != block_expert[:-1]).astype(jnp.int32)])
    slot = ordinal[block_expert] % 2
    next_expert = next_present[block_expert]

    rows = _sorted_rows(top_idx, rank, pad_start)
    xs = _dispatch(h_tiles, rows, pad_start, counts, n_blocks * tb)
    ys = _experts(xs.reshape(n_blocks * tb * PK_ROWS, LANES), block_expert, first, slot, next_expert,
                  n_active, w1, w3, w2, layer)
    return _combine(h_tiles, rows, gates.T, ys, ws1, ws3, ws2, g, b, token_tile_out)


def _proj_body(h_ref, w_ref, *o_refs, tt, scales):
    y = _bdot(_rows_load(h_ref, tt).astype(BF16), w_ref[...])
    _store_head_slabs(y, o_refs, scales)


def _project(h_tiles, w, scales):
    n_tok = h_tiles.shape[0] // TOK_ROWS
    tt = PROJ_TT
    n_out = len(scales)
    out_spec = pl.BlockSpec((PAIR_SLABS, tt, LANES), lambda i: (0, i, 0))
    return pl.pallas_call(
        functools.partial(_proj_body, tt=tt, scales=tuple(scales)),
        grid=(n_tok // tt,),
        in_specs=[pl.BlockSpec((tt * TOK_ROWS, LANES), lambda i: (i, 0)),
                  pl.BlockSpec(w.shape, lambda i: (0, 0))],
        out_specs=[out_spec] * n_out,
        out_shape=[jax.ShapeDtypeStruct((PAIR_SLABS, n_tok, LANES), I32)] * n_out,
        compiler_params=pltpu.CompilerParams(
            dimension_semantics=("arbitrary",), vmem_limit_bytes=VMEM_LIMIT),
        name="projection",
    )(h_tiles, w.astype(BF16))


def _attn_body(q_ref, kp_ref, kc_ref, vp_ref, vc_ref, o_ref, lse_ref, *, dilation, n_steps):
    t = ATT_BLOCK
    not_first = pl.program_id(1) > 0
    qi = lax.broadcasted_iota(jnp.int32, (t, 2 * t), 0)
    kj = lax.broadcasted_iota(jnp.int32, (t, 2 * t), 1)
    dist = t + qi - kj
    valid = (dist >= 0) & (dist <= n_steps) & (not_first | (kj >= t))
    token_dist = (dilation * dist).astype(F32)
    lane = lax.broadcasted_iota(jnp.int32, (t, LANES), 1)
    low_half = lane < HEAD_DIM
    lane_head = lane // LSE_LANES
    contract_last = (((1,), (1,)), ((), ()))

    def unpack(words, which):
        return pltpu.unpack_elementwise(words, index=which, packed_dtype=BF16,
                                        unpacked_dtype=F32).astype(BF16)

    def residue(r, carry):
        rows = pl.ds(r, t) if dilation == 1 else pl.ds(r, t, stride=dilation)
        lse_all = jnp.zeros((t, LANES), F32)
        for slab in range(PAIR_SLABS):
            q_words = q_ref.at[slab][rows, :]
            k_words = jnp.concatenate([kp_ref.at[slab][rows, :], kc_ref.at[slab][rows, :]], axis=0)
            v_words = jnp.concatenate([vp_ref.at[slab][rows, :], vc_ref.at[slab][rows, :]], axis=0)
            pair_outs = []
            for which in range(2):
                pair = 2 * slab + which
                q2, k2, v2 = unpack(q_words, which), unpack(k_words, which), unpack(v_words, which)
                outs, lses = [], []
                for half in range(2):
                    head = 2 * pair + half
                    slope = 2.0 ** (-8.0 * (head + 1) / ATT_HEADS)
                    keep = low_half if half == 0 else ~low_half
                    qh = jnp.where(keep, q2, jnp.zeros_like(q2))
                    s = lax.dot_general(qh, k2, contract_last, preferred_element_type=F32)
                    s = jnp.where(valid, s - slope * token_dist, MASK_VALUE)
                    m = jnp.max(s, axis=-1, keepdims=True)
                    p = jnp.exp(s - m)
                    l = jnp.sum(p, axis=-1, keepdims=True)
                    outs.append(_bdot(p.astype(BF16), v2) / l)
                    lse_all = jnp.where(lane_head == head, m + jnp.log(l), lse_all)
                pair_outs.append(jnp.where(low_half, outs[0], outs[1]))
            o_ref.at[slab][rows, :] = pltpu.pack_elementwise(pair_outs, packed_dtype=BF16)
        lse_ref[rows, :] = lse_all
        return carry

    if dilation == 1:
        residue(0, 0)
    else:
        lax.fori_loop(0, dilation, residue, 0)


def _attn_branch(q, k, v, bsz, seq, window, dilation):
    span = ATT_BLOCK * dilation
    n_spans = seq // span
    cur = pl.BlockSpec((PAIR_SLABS, span, LANES), lambda bi, n: (0, bi * n_spans + n, 0))
    prev = pl.BlockSpec((PAIR_SLABS, span, LANES),
                        lambda bi, n: (0, bi * n_spans + jnp.maximum(n - 1, 0), 0))
    return pl.pallas_call(
        functools.partial(_attn_body, dilation=dilation, n_steps=window // dilation),
        grid=(bsz, n_spans),
        in_specs=[cur, prev, cur, prev, cur],
        out_specs=[cur, pl.BlockSpec((span, LANES), lambda bi, n: (bi * n_spans + n, 0))],
        out_shape=[jax.ShapeDtypeStruct((PAIR_SLABS, bsz * seq, LANES), I32),
                   jax.ShapeDtypeStruct((bsz * seq, LANES), F32)],
        compiler_params=pltpu.CompilerParams(
            dimension_semantics=("arbitrary",) * 2, vmem_limit_bytes=VMEM_LIMIT),
        name=f"dilated_attn_d{dilation}",
    )(q, k, k, v, v)


def _merge_body(*refs, tt):
    o_refs = refs[:N_GROUPS]
    lse_refs = refs[N_GROUPS:2 * N_GROUPS]
    h_ref, wo_ref, g_ref, b_ref, out_ref = refs[2 * N_GROUPS:]
    lses = [r[...] for r in lse_refs]
    m = functools.reduce(jnp.maximum, lses)
    es = [jnp.exp(l - m) for l in lses]
    den = functools.reduce(lambda a, c: a + c, es)
    row = lax.broadcasted_iota(jnp.int32, (LANES, O_WIDTH), 0)
    col = lax.broadcasted_iota(jnp.int32, (LANES, O_WIDTH), 1)
    select = jnp.where(row == (col // HEAD_DIM) * LSE_LANES, 1.0, 0.0).astype(BF16)

    def widen(w):
        hi = w.astype(BF16)
        lo = (w - hi.astype(F32)).astype(BF16)
        return _bdot(hi, select) + _bdot(lo, select)

    def branch_out(o_ref):
        return jnp.concatenate(
            [pltpu.unpack_elementwise(o_ref[pair // 2], index=pair % 2, packed_dtype=BF16,
                                      unpacked_dtype=F32) for pair in range(ATT_HEADS // 2)], axis=1)

    o = functools.reduce(lambda a, c: a + c,
                         [widen(e / den) * branch_out(r) for e, r in zip(es, o_refs)])
    mix = _bdot(o.astype(BF16), wo_ref[...])
    z = DEEPNORM_ALPHA * _rows_load(h_ref, tt) + mix
    _rows_store(out_ref, _layer_norm(z, g_ref[...], b_ref[...]), tt)


def _merge(outs, lses, h_tiles, w_o, g, b):
    n_tok = h_tiles.shape[0] // TOK_ROWS
    tt = MERGE_TT
    o_spec = pl.BlockSpec((PAIR_SLABS, tt, LANES), lambda i: (0, i, 0))
    lse_spec = pl.BlockSpec((tt, LANES), lambda i: (i, 0))
    tiles = pl.BlockSpec((tt * TOK_ROWS, LANES), lambda i: (i, 0))
    full = lambda shape: pl.BlockSpec(shape, lambda i: (0,) * len(shape))
    return pl.pallas_call(
        functools.partial(_merge_body, tt=tt),
        grid=(n_tok // tt,),
        in_specs=[o_spec] * N_GROUPS + [lse_spec] * N_GROUPS + [
            tiles, full((O_WIDTH, D_MODEL)), full((1, D_MODEL)), full((1, D_MODEL))],
        out_specs=tiles,
        out_shape=jax.ShapeDtypeStruct(h_tiles.shape, F32),
        compiler_params=pltpu.CompilerParams(
            dimension_semantics=("arbitrary",), vmem_limit_bytes=VMEM_LIMIT),
        name="attn_merge",
    )(*outs, *lses, h_tiles, w_o.astype(BF16), g.reshape(1, -1), b.reshape(1, -1))


def kernel(x, a_w_in, a_conv_w, a_conv_b, a_w_gate_a, a_b_gate_a, a_w_gate_x, a_b_gate_x, a_lambda, a_w_out, w_kv_shared, b_w_q, b_w_o, moe_w_router, moe_router_bias, moe_w1, moe_w3, moe_w2, moe_ws1, moe_ws3, moe_ws2, ln_g, ln_b):
    bsz, seq, d = x.shape
    assert d == D_MODEL and seq % (DILATION_PATTERNS[-1][1] * ATT_BLOCK) == 0

    def moe(h_tiles, layer, token_tile_out):
        return _moe_layer(h_tiles, moe_w_router[layer], moe_router_bias[layer], moe_w1, moe_w3,
                          moe_w2, layer, moe_ws1[layer], moe_ws3[layer], moe_ws2[layer],
                          ln_g[layer, 1], ln_b[layer, 1], token_tile_out)

    h = _rglru_layer(x, a_w_in[0], a_conv_w[0], a_conv_b[0], a_w_gate_a[0], a_b_gate_a[0],
                     a_w_gate_x[0], a_b_gate_x[0], a_lambda[0], a_w_out[0], ln_g[0, 0], ln_b[0, 0])
    h = moe(h, 0, True)
    kvq = _project(h, jnp.concatenate([w_kv_shared, b_w_q[0]], axis=1),
                   (1.0,) * (2 * N_GROUPS) + (HEAD_DIM ** -0.5,) * N_GROUPS)
    ks, vs, qs = kvq[:N_GROUPS], kvq[N_GROUPS:2 * N_GROUPS], kvq[2 * N_GROUPS:]

    outs, lses = [], []
    for grp, (window, dilation) in enumerate(DILATION_PATTERNS):
        o, lse = _attn_branch(qs[grp], ks[grp], vs[grp], bsz, seq, window, dilation)
        outs.append(o)
        lses.append(lse)
    h = _merge(outs, lses, h, b_w_o[0], ln_g[1, 0], ln_b[1, 0])
    h = moe(h, 1, False)
    return h.reshape(bsz, seq, d)
```

```python
import functools
import math

import jax
import jax.numpy as jnp
from jax import lax
from jax.experimental import pallas as pl
from jax.experimental.pallas import tpu as pltpu

F32 = jnp.float32
BF16 = jnp.bfloat16
I32 = jnp.int32

D_MODEL = 1024
LRU_BLOCKS = 4
CONV_WIDTH = 4
LRU_C = 8.0
ATT_HEADS = 8
HEAD_DIM = 64
DILATION_PATTERNS = ((128, 1), (512, 4), (2048, 16))
N_GROUPS = len(DILATION_PATTERNS)
ATT_BLOCK = 128
O_WIDTH = ATT_HEADS * HEAD_DIM
N_EXPERTS = 256
TOP_K = 8
N_EXPERT_GROUPS = 8
TOPK_GROUPS = 4
D_EXPERT = 256
ROUTED_SCALE = 2.5
DEPTH = 2
DEEPNORM_ALPHA = (2 * DEPTH) ** 0.25
LN_EPS = 1e-5
MASK_VALUE = -1e30

SUBLANES = 8
LANES = 128
TOK_ROWS = D_MODEL // LANES
PK_ROWS = D_MODEL // 2 // LANES
PAIR_SLABS = ATT_HEADS * HEAD_DIM // LANES // 2
LSE_LANES = LANES // ATT_HEADS
VMEM_LIMIT = 56 * 1024 * 1024

RGLRU_TS = 512
ROUTER_TT = 1024
SORTED_ROWS_TL = 2048
DISPATCH_TT = 512
EXPERT_TB = 512
COMBINE_TT = 512
PROJ_TT = 1024
MERGE_TT = 512


def _rows_load(ref, n_tok):
    return jnp.concatenate(
        [ref[pl.ds(s, n_tok, stride=TOK_ROWS), :] for s in range(TOK_ROWS)], axis=1)


def _rows_store(ref, val, n_tok):
    for s in range(TOK_ROWS):
        ref[pl.ds(s, n_tok, stride=TOK_ROWS), :] = val[:, s * LANES:(s + 1) * LANES]


def _layer_norm(z, g, b):
    mu = jnp.mean(z, axis=-1, keepdims=True)
    zc = z - mu
    var = jnp.mean(zc * zc, axis=-1, keepdims=True)
    return zc * lax.rsqrt(var + LN_EPS) * g + b


def _silu(x):
    return x * jax.nn.sigmoid(x)


def _gelu_tanh(x):
    c = math.sqrt(2.0 / math.pi)
    return 0.5 * x * (1.0 + jnp.tanh(c * (x + 0.044715 * (x * x * x))))


def _bdot(a, b):
    return jnp.dot(a, b, preferred_element_type=F32)


def _rglru_body(x_ref, win_ref, cw_ref, cb_ref, wga_ref, bga_ref, wgx_ref, bgx_ref, lam_ref,
                wout_ref, g_ref, b_ref, o_ref, hc_ref, tail_ref, *, ts):
    width = D_MODEL
    bw = width // LRU_BLOCKS

    @pl.when(pl.program_id(1) == 0)
    def _():
        hc_ref[...] = jnp.zeros_like(hc_ref)
        tail_ref[...] = jnp.zeros_like(tail_ref)

    x = x_ref[...]
    xz = _bdot(x.astype(BF16), win_ref[...])
    xr = xz[:, :width]
    gate = xz[:, width:]

    tail = tail_ref[...]
    row8 = lax.broadcasted_iota(jnp.int32, (SUBLANES, width), 0)
    cw = cw_ref[...]
    xc = xr * cw[CONV_WIDTH - 1:CONV_WIDTH, :] + cb_ref[...]
    for j in range(1, CONV_WIDTH):
        rx = pltpu.roll(xr, j, 0)
        rp = pltpu.roll(tail, j, 0)
        top = jnp.where(row8 < j, rp, rx[:SUBLANES])
        shifted = jnp.concatenate([top, rx[SUBLANES:]], axis=0)
        xc = xc + shifted * cw[CONV_WIDTH - 1 - j:CONV_WIDTH - j, :]
    tail_ref[...] = xr[ts - SUBLANES:]

    xcb = xc.astype(BF16)

    def block_diag(w_ref):
        return jnp.concatenate(
            [_bdot(xcb[:, n * bw:(n + 1) * bw], w_ref[n]) for n in range(LRU_BLOCKS)], axis=1)

    r = jax.nn.sigmoid(block_diag(wga_ref) + bga_ref[...])
    i = jax.nn.sigmoid(block_diag(wgx_ref) + bgx_ref[...])
    lam = lam_ref[...]
    softplus_neg_lam = jnp.maximum(-lam, 0.0) + jnp.log1p(jnp.exp(-jnp.abs(lam)))
    log_a = (-LRU_C * r) * softplus_neg_lam
    a = jnp.exp(log_a)
    mult = jnp.sqrt(-jnp.tanh(log_a) * (a * a + 1.0))
    u = mult * (i * xc)

    groups = ts // SUBLANES
    a3 = a.reshape(groups, SUBLANES, width)
    u3 = u.reshape(groups, SUBLANES, width)
    sub = lax.broadcasted_iota(jnp.int32, (groups, SUBLANES, width), 1)
    sh = 1
    while sh < SUBLANES:
        a_prev = pltpu.roll(a3, sh, 1)
        u_prev = pltpu.roll(u3, sh, 1)
        live = sub >= sh
        u3 = jnp.where(live, a3 * u_prev, 0.0) + u3
        a3 = jnp.where(live, a3 * a_prev, a3)
        sh *= 2
    carry = hc_ref[...]
    h_groups = []
    for grp in range(groups):
        h_grp = a3[grp] * carry + u3[grp]
        h_groups.append(h_grp)
        carry = h_grp[SUBLANES - 1:SUBLANES]
    h = jnp.concatenate(h_groups, axis=0)
    hc_ref[...] = carry

    y = (h * _gelu_tanh(gate)).astype(BF16)
    mix = _bdot(y, wout_ref[...])
    z = DEEPNORM_ALPHA * x + mix
    _rows_store(o_ref, _layer_norm(z, g_ref[...], b_ref[...]), ts)


def _rglru_layer(x, w_in, conv_w, conv_b, wga, bga, wgx, bgx, lam, w_out, g, b):
    bsz, seq, d = x.shape
    ts = RGLRU_TS
    ns = seq // ts
    row = lambda v: v.reshape(1, -1)
    full = lambda shape: pl.BlockSpec(shape, lambda bi, si: (0,) * len(shape))
    return pl.pallas_call(
        functools.partial(_rglru_body, ts=ts),
        grid=(bsz, ns),
        in_specs=[
            pl.BlockSpec((None, ts, d), lambda bi, si: (bi, si, 0)),
            full((d, 2 * d)), full((CONV_WIDTH, d)), full((1, d)),
            full((LRU_BLOCKS, d // LRU_BLOCKS, d // LRU_BLOCKS)), full((1, d)),
            full((LRU_BLOCKS, d // LRU_BLOCKS, d // LRU_BLOCKS)), full((1, d)),
            full((1, d)), full((d, d)), full((1, d)), full((1, d)),
        ],
        out_specs=pl.BlockSpec((ts * TOK_ROWS, LANES), lambda bi, si: (bi * ns + si, 0)),
        out_shape=jax.ShapeDtypeStruct((bsz * seq * TOK_ROWS, LANES), F32),
        scratch_shapes=[pltpu.VMEM((1, d), F32), pltpu.VMEM((SUBLANES, d), F32)],
        compiler_params=pltpu.CompilerParams(
            dimension_semantics=("arbitrary", "arbitrary"), vmem_limit_bytes=VMEM_LIMIT),
        name="rglru_layer",
    )(x, w_in.astype(BF16), conv_w, row(conv_b), wga.astype(BF16), row(bga), wgx.astype(BF16),
      row(bgx), row(lam), w_out.astype(BF16), row(g), row(b))


def _router_body(h_ref, wrt_ref, bias_ref, idx_ref, gate_ref, rank_ref, cnt_ref, carry_ref, *, tt):
    n_e = N_EXPERTS
    per_group = n_e // N_EXPERT_GROUPS

    @pl.when(pl.program_id(0) == 0)
    def _():
        carry_ref[...] = jnp.zeros_like(carry_ref)

    h = _rows_load(h_ref, tt)
    logits = lax.dot_general(wrt_ref[...], h.astype(BF16), (((1,), (1,)), ((), ())),
                             preferred_element_type=F32)
    scores = jax.nn.sigmoid(logits)
    biased = scores + bias_ref[...]

    j_iota = lax.broadcasted_iota(jnp.int32, (per_group, tt), 0)
    group_score = []
    for g in range(N_EXPERT_GROUPS):
        bg = biased[g * per_group:(g + 1) * per_group]
        m1 = jnp.max(bg, axis=0, keepdims=True)
        i1 = jnp.min(jnp.where(bg == m1, j_iota, per_group), axis=0, keepdims=True)
        m2 = jnp.max(jnp.where(j_iota == i1, -jnp.inf, bg), axis=0, keepdims=True)
        group_score.append(m1 + m2)

    masked = []
    for g in range(N_EXPERT_GROUPS):
        beaten_by = jnp.zeros((1, tt), jnp.int32)
        for o in range(N_EXPERT_GROUPS):
            if o == g:
                continue
            wins = group_score[o] > group_score[g]
            if o < g:
                wins = wins | (group_score[o] == group_score[g])
            beaten_by = beaten_by + wins.astype(jnp.int32)
        keep = beaten_by < TOPK_GROUPS
        masked.append(jnp.where(keep, biased[g * per_group:(g + 1) * per_group], MASK_VALUE))
    cur = jnp.concatenate(masked, axis=0)

    e_iota = lax.broadcasted_iota(jnp.int32, (n_e, tt), 0)
    idx_rows, score_rows, sels = [], [], []
    for _ in range(TOP_K):
        m = jnp.max(cur, axis=0, keepdims=True)
        ik = jnp.min(jnp.where(cur == m, e_iota, n_e), axis=0, keepdims=True)
        sel = e_iota == ik
        score_rows.append(jnp.sum(jnp.where(sel, scores, 0.0), axis=0, keepdims=True))
        cur = jnp.where(sel, -jnp.inf, cur)
        idx_rows.append(ik)
        sels.append(sel)
    top_s = jnp.concatenate(score_rows, axis=0)
    gate_ref[...] = top_s / jnp.sum(top_s, axis=0, keepdims=True) * ROUTED_SCALE
    idx_ref[...] = jnp.concatenate(idx_rows, axis=0)
    multi_hot = jnp.where(cur == -jnp.inf, 1.0, 0.0)

    t_row = lax.broadcasted_iota(jnp.int32, (tt, tt), 0)
    t_col = lax.broadcasted_iota(jnp.int32, (tt, tt), 1)
    strict_upper = jnp.where(t_row < t_col, 1.0, 0.0).astype(BF16)
    before = _bdot(multi_hot.astype(BF16), strict_upper) + carry_ref[...]
    rank_rows = [jnp.sum(jnp.where(sel, before, 0.0), axis=0, keepdims=True) for sel in sels]
    rank_ref[...] = jnp.concatenate(rank_rows, axis=0).astype(jnp.int32)
    carry = carry_ref[...] + jnp.sum(multi_hot, axis=1, keepdims=True)
    carry_ref[...] = carry
    cnt_ref[...] = carry.astype(jnp.int32)


def _router(h_tiles, w_router, router_bias):
    n_tok = h_tiles.shape[0] // TOK_ROWS
    tt = ROUTER_TT
    kt = lambda dt: jax.ShapeDtypeStruct((TOP_K, n_tok), dt)
    tok_spec = pl.BlockSpec((TOP_K, tt), lambda i: (0, i))
    return pl.pallas_call(
        functools.partial(_router_body, tt=tt),
        grid=(n_tok // tt,),
        in_specs=[
            pl.BlockSpec((tt * TOK_ROWS, LANES), lambda i: (i, 0)),
            pl.BlockSpec((N_EXPERTS, D_MODEL), lambda i: (0, 0)),
            pl.BlockSpec((N_EXPERTS, 1), lambda i: (0, 0)),
        ],
        out_specs=[tok_spec, tok_spec, tok_spec, pl.BlockSpec((N_EXPERTS, 1), lambda i: (0, 0))],
        out_shape=[kt(jnp.int32), kt(F32), kt(jnp.int32),
                   jax.ShapeDtypeStruct((N_EXPERTS, 1), jnp.int32)],
        scratch_shapes=[pltpu.VMEM((N_EXPERTS, 1), F32)],
        compiler_params=pltpu.CompilerParams(
            dimension_semantics=("arbitrary",), vmem_limit_bytes=VMEM_LIMIT),
        name="moe_router",
    )(h_tiles, w_router.T.astype(BF16), router_bias.reshape(N_EXPERTS, 1))


def _pack_rows(x):
    half = D_MODEL // 2
    return pltpu.pack_elementwise([x[:, :half], x[:, half:]], packed_dtype=BF16)


def _unpack_rows(p):
    return jnp.concatenate(
        [pltpu.unpack_elementwise(p, index=i, packed_dtype=BF16, unpacked_dtype=F32) for i in range(2)],
        axis=1)


def _packed_load(ref, n_tok):
    return jnp.concatenate(
        [ref[pl.ds(s, n_tok, stride=PK_ROWS), :] for s in range(PK_ROWS)], axis=1)


def _packed_store(ref, val, n_tok):
    for s in range(PK_ROWS):
        ref[pl.ds(s, n_tok, stride=PK_ROWS), :] = val[:, s * LANES:(s + 1) * LANES]


def _sorted_rows_body(idx_ref, rank_ref, start_ref, o_ref):
    tl = idx_ref.shape[1]
    e_iota = lax.broadcasted_iota(jnp.int32, (N_EXPERTS, tl), 0)
    start = start_ref[...]
    rows = []
    for k in range(TOP_K):
        hit = e_iota == idx_ref[k:k + 1, :]
        base = jnp.sum(jnp.where(hit, start, 0.0), axis=0, keepdims=True)
        rows.append(base.astype(jnp.int32) + rank_ref[k:k + 1, :])
    o_ref[...] = jnp.concatenate(rows, axis=0)


def _sorted_rows(top_idx, rank, pad_start):
    n_tok = top_idx.shape[1]
    tl = SORTED_ROWS_TL
    tok = pl.BlockSpec((TOP_K, tl), lambda i: (0, i))
    return pl.pallas_call(
        _sorted_rows_body,
        grid=(n_tok // tl,),
        in_specs=[tok, tok, pl.BlockSpec((N_EXPERTS, 1), lambda i: (0, 0))],
        out_specs=tok,
        out_shape=jax.ShapeDtypeStruct((TOP_K, n_tok), jnp.int32),
        compiler_params=pltpu.CompilerParams(
            dimension_semantics=("arbitrary",), vmem_limit_bytes=VMEM_LIMIT),
        name="moe_sorted_rows",
    )(top_idx, rank, pad_start.astype(F32).reshape(N_EXPERTS, 1))


def _dispatch_body(start_ref, cnt_ref, dest_ref, h_ref, xs_ref, pk_ref, zero_ref, sem, zero_sem,
                   *, tt, tb):
    @pl.when(pl.program_id(0) == 0)
    def _():
        zero_ref[...] = jnp.zeros_like(zero_ref)

        def pad_copies(e, fn):
            n_pad = ((cnt_ref[e] + (tb - 1)) & (-tb)) - cnt_ref[e]
            row = start_ref[e] + cnt_ref[e]
            for bit in range(tb.bit_length() - 1):
                size = 1 << bit
                has_bit = ((n_pad >> bit) & 1) == 1

                @pl.when(has_bit)
                def _():
                    fn(pltpu.make_async_copy(zero_ref.at[pl.ds(0, size)],
                                             xs_ref.at[pl.ds(row, size)], zero_sem))
                row = row + jnp.where(has_bit, size, 0)

        def start_all(e, c):
            pad_copies(e, lambda cp: cp.start())
            return c

        def wait_all(e, c):
            pad_copies(e, lambda cp: cp.wait())
            return c

        lax.fori_loop(0, N_EXPERTS, start_all, 0)
        lax.fori_loop(0, N_EXPERTS, wait_all, 0)

    packed = _pack_rows(_rows_load(h_ref, tt))
    for s in range(PK_ROWS):
        pk_ref[:, s, :] = packed[:, s * LANES:(s + 1) * LANES]
    for t in range(tt):
        for k in range(TOP_K):
            pltpu.make_async_copy(pk_ref.at[t], xs_ref.at[dest_ref[k, t]], sem).start(priority=k % 2)
    for k in range(TOP_K):
        pltpu.make_async_copy(pk_ref, xs_ref.at[pl.ds(0, tt)], sem).wait()


def _dispatch(h_tiles, dest_rows, pad_start, counts, n_rows):
    n_tok = h_tiles.shape[0] // TOK_ROWS
    tt = DISPATCH_TT
    grid_spec = pltpu.PrefetchScalarGridSpec(
        num_scalar_prefetch=2,
        grid=(n_tok // tt,),
        in_specs=[pl.BlockSpec((TOP_K, tt), lambda i, s, c: (0, i), memory_space=pltpu.SMEM),
                  pl.BlockSpec((tt * TOK_ROWS, LANES), lambda i, s, c: (i, 0))],
        out_specs=pl.BlockSpec(memory_space=pl.ANY),
        scratch_shapes=[pltpu.VMEM((tt, PK_ROWS, LANES), I32),
                        pltpu.VMEM((EXPERT_TB // 2, PK_ROWS, LANES), I32),
                        pltpu.SemaphoreType.DMA(()), pltpu.SemaphoreType.DMA(())],
    )
    return pl.pallas_call(
        functools.partial(_dispatch_body, tt=tt, tb=EXPERT_TB),
        grid_spec=grid_spec,
        out_shape=jax.ShapeDtypeStruct((n_rows, PK_ROWS, LANES), I32),
        compiler_params=pltpu.CompilerParams(
            dimension_semantics=("arbitrary",), vmem_limit_bytes=VMEM_LIMIT),
        name="moe_dispatch",
    )(pad_start, counts, dest_rows, h_tiles)


def _expert_body(be_ref, first_ref, slot_ref, next_ref, na_ref, xs_ref, w1_ref, w3_ref, w2_ref, ys_ref,
                 w13_buf, w2_buf, w1b, w3b, w2b, sem, *, tb, layer):
    i = pl.program_id(0)

    def weight_copies(expert, slot):
        return (pltpu.make_async_copy(w1_ref.at[layer, expert], w13_buf.at[slot, 0], sem.at[slot]),
                pltpu.make_async_copy(w3_ref.at[layer, expert], w13_buf.at[slot, 1], sem.at[slot]),
                pltpu.make_async_copy(w2_ref.at[layer, expert], w2_buf.at[slot], sem.at[slot]))

    @pl.when(i < na_ref[0])
    def _():
        @pl.when(first_ref[i] == 1)
        def _():
            slot = slot_ref[i]

            @pl.when(i == 0)
            def _():
                for cp in weight_copies(be_ref[0], slot):
                    cp.start()

            for cp in weight_copies(be_ref[i], slot):
                cp.wait()

            @pl.when(next_ref[i] >= 0)
            def _():
                for cp in weight_copies(next_ref[i], 1 - slot):
                    cp.start()

            w1b[...] = w13_buf[slot, 0].astype(BF16)
            w3b[...] = w13_buf[slot, 1].astype(BF16)
            w2b[...] = w2_buf[slot].astype(BF16)

        x = _unpack_rows(_packed_load(xs_ref, tb)).astype(BF16)
        mid = (_silu(_bdot(x, w1b[...])) * _bdot(x, w3b[...])).astype(BF16)
        _packed_store(ys_ref, _pack_rows(_bdot(mid, w2b[...])), tb)


def _experts(xs, block_expert, first, slot, next_expert, n_active, w1, w3, w2, layer):
    tb = EXPERT_TB
    n_blocks = xs.shape[0] // (tb * PK_ROWS)
    row_block = lambda i, be, fi, sl, nx, na: (jnp.minimum(i, na[0] - 1), 0)
    grid_spec = pltpu.PrefetchScalarGridSpec(
        num_scalar_prefetch=5,
        grid=(n_blocks,),
        in_specs=[
            pl.BlockSpec((tb * PK_ROWS, LANES), row_block),
            pl.BlockSpec(memory_space=pl.ANY), pl.BlockSpec(memory_space=pl.ANY),
            pl.BlockSpec(memory_space=pl.ANY),
        ],
        out_specs=pl.BlockSpec((tb * PK_ROWS, LANES), row_block),
        scratch_shapes=[pltpu.VMEM((2, 2, D_MODEL, D_EXPERT), F32),
                        pltpu.VMEM((2, D_EXPERT, D_MODEL), F32),
                        pltpu.VMEM((D_MODEL, D_EXPERT), BF16), pltpu.VMEM((D_MODEL, D_EXPERT), BF16),
                        pltpu.VMEM((D_EXPERT, D_MODEL), BF16), pltpu.SemaphoreType.DMA((2,))],
    )
    return pl.pallas_call(
        functools.partial(_expert_body, tb=tb, layer=layer),
        grid_spec=grid_spec,
        out_shape=jax.ShapeDtypeStruct(xs.shape, I32),
        compiler_params=pltpu.CompilerParams(
            dimension_semantics=("arbitrary",), vmem_limit_bytes=VMEM_LIMIT),
        name="moe_experts",
    )(block_expert, first, slot, next_expert, n_active, xs, w1, w3, w2)


def _store_head_slabs(y, o_refs, scales):
    for j, (o_ref, scale) in enumerate(zip(o_refs, scales)):
        for s in range(PAIR_SLABS):
            lo = j * O_WIDTH + 2 * s * LANES
            o_ref[s] = pltpu.pack_elementwise(
                [y[:, lo:lo + LANES] * scale, y[:, lo + LANES:lo + 2 * LANES] * scale],
                packed_dtype=BF16)


def _combine_body(src_ref, h_ref, gates_ref, ys_ref, ys_flat_ref, ws1_ref, ws3_ref, ws2_ref,
                  g_ref, b_ref, o_ref, gbuf, sem, *, tt, n_tiles, token_tile_out):
    j = pl.program_id(0)

    def step(issue, finish):
        nxt = j % 2
        cur = (j - 1) % 2
        if finish:
            h = _rows_load(h_ref, tt)
            hb = h.astype(BF16)
            mid = (_silu(_bdot(hb, ws1_ref[...])) * _bdot(hb, ws3_ref[...])).astype(BF16)
            routed = _bdot(mid, ws2_ref[...])
            gates = gates_ref[...]
        for k in range(TOP_K):
            if issue:
                for t in range(tt):
                    pltpu.make_async_copy(ys_ref.at[src_ref[k, t]],
                                          gbuf.at[nxt, k, pl.ds(t * PK_ROWS, PK_ROWS), :],
                                          sem.at[nxt]).start(priority=t % 2)
            if finish:
                pltpu.make_async_copy(ys_flat_ref.at[pl.ds(0, tt * PK_ROWS), :], gbuf.at[cur, k],
                                      sem.at[cur]).wait()
                routed = routed + gates[:, k:k + 1] * _unpack_rows(_packed_load(gbuf.at[cur, k], tt))
        if finish:
            out = _layer_norm(DEEPNORM_ALPHA * h + routed, g_ref[...], b_ref[...])
            if token_tile_out:
                _rows_store(o_ref, out, tt)
            else:
                o_ref[...] = out

    pl.when(j == 0)(lambda: step(True, False))
    pl.when((j >= 1) & (j < n_tiles))(lambda: step(True, True))
    pl.when(j == n_tiles)(lambda: step(False, True))


def _combine(h_tiles, src_rows, gates_tk, ys, ws1, ws3, ws2, g, b, token_tile_out):
    n_tok = h_tiles.shape[0] // TOK_ROWS
    tt = COMBINE_TT
    n_tiles = n_tok // tt
    prev_tile = lambda j: (jnp.maximum(j - 1, 0), 0)
    full = lambda shape: pl.BlockSpec(shape, lambda j: (0,) * len(shape))
    if token_tile_out:
        out_spec = pl.BlockSpec((tt * TOK_ROWS, LANES), prev_tile)
        out_shape = jax.ShapeDtypeStruct((n_tok * TOK_ROWS, LANES), F32)
    else:
        out_spec = pl.BlockSpec((tt, D_MODEL), prev_tile)
        out_shape = jax.ShapeDtypeStruct((n_tok, D_MODEL), F32)
    n_rows = ys.shape[0] // PK_ROWS
    return pl.pallas_call(
        functools.partial(_combine_body, tt=tt, n_tiles=n_tiles, token_tile_out=token_tile_out),
        grid=(n_tiles + 1,),
        in_specs=[pl.BlockSpec((TOP_K, tt), lambda j: (0, jnp.minimum(j, n_tiles - 1)),
                               memory_space=pltpu.SMEM),
                  pl.BlockSpec((tt * TOK_ROWS, LANES), prev_tile),
                  pl.BlockSpec((tt, TOP_K), prev_tile),
                  pl.BlockSpec(memory_space=pl.ANY), pl.BlockSpec(memory_space=pl.ANY),
                  full((D_MODEL, D_EXPERT)), full((D_MODEL, D_EXPERT)), full((D_EXPERT, D_MODEL)),
                  full((1, D_MODEL)), full((1, D_MODEL))],
        out_specs=out_spec,
        out_shape=out_shape,
        scratch_shapes=[pltpu.VMEM((2, TOP_K, tt * PK_ROWS, LANES), I32),
                        pltpu.SemaphoreType.DMA((2,))],
        compiler_params=pltpu.CompilerParams(
            dimension_semantics=("arbitrary",), vmem_limit_bytes=VMEM_LIMIT),
        name="moe_combine",
    )(src_rows, h_tiles, gates_tk, ys.reshape(n_rows, PK_ROWS, LANES), ys, ws1.astype(BF16),
      ws3.astype(BF16), ws2.astype(BF16), g.reshape(1, -1), b.reshape(1, -1))


def _moe_layer(h_tiles, w_router, router_bias, w1, w3, w2, layer, ws1, ws3, ws2, g, b,
               token_tile_out):
    n_tok = h_tiles.shape[0] // TOK_ROWS
    tb = EXPERT_TB
    top_idx, gates, rank, counts = _router(h_tiles, w_router, router_bias)

    n_blocks = -(-(n_tok * TOP_K + N_EXPERTS * (tb - 1)) // tb)
    counts = counts.reshape(N_EXPERTS)
    padded = ((counts + tb - 1) // tb) * tb
    pad_end = jnp.cumsum(padded)
    pad_start = (pad_end - padded).astype(jnp.int32)
    block_first_row = jnp.arange(n_blocks, dtype=jnp.int32) * tb
    block_expert = jnp.minimum(
        jnp.sum((pad_end[None, :] <= block_first_row[:, None]).astype(jnp.int32), axis=1),
        N_EXPERTS - 1)
    n_active = (pad_end[-1:] // tb).astype(jnp.int32)
    present = counts > 0
    expert_ids = jnp.arange(N_EXPERTS, dtype=jnp.int32)
    later = lax.cummin(jnp.where(present, expert_ids, N_EXPERTS), reverse=True)
    next_present = jnp.concatenate([later[1:], jnp.full((1,), N_EXPERTS, jnp.int32)])
    next_present = jnp.where(next_present < N_EXPERTS, next_present, -1)
    ordinal = jnp.cumsum(present.astype(jnp.int32)) - 1
    first = jnp.concatenate([jnp.ones((1,), jnp.int32),
                             (block_expert[1:] != block_expert[:-1]).astype(jnp.int32)])
    slot = ordinal[block_expert] % 2
    next_expert = next_present[block_expert]

    rows = _sorted_rows(top_idx, rank, pad_start)
    xs = _dispatch(h_tiles, rows, pad_start, counts, n_blocks * tb)
    ys = _experts(xs.reshape(n_blocks * tb * PK_ROWS, LANES), block_expert, first, slot, next_expert,
                  n_active, w1, w3, w2, layer)
    return _combine(h_tiles, rows, gates.T, ys, ws1, ws3, ws2, g, b, token_tile_out)


def _proj_body(h_ref, w_ref, *o_refs, tt, scales):
    y = _bdot(_rows_load(h_ref, tt).astype(BF16), w_ref[...])
    _store_head_slabs(y, o_refs, scales)


def _project(h_tiles, w, scales):
    n_tok = h_tiles.shape[0] // TOK_ROWS
    tt = PROJ_TT
    n_out = len(scales)
    out_spec = pl.BlockSpec((PAIR_SLABS, tt, LANES), lambda i: (0, i, 0))
    return pl.pallas_call(
        functools.partial(_proj_body, tt=tt, scales=tuple(scales)),
        grid=(n_tok // tt,),
        in_specs=[pl.BlockSpec((tt * TOK_ROWS, LANES), lambda i: (i, 0)),
                  pl.BlockSpec(w.shape, lambda i: (0, 0))],
        out_specs=[out_spec] * n_out,
        out_shape=[jax.ShapeDtypeStruct((PAIR_SLABS, n_tok, LANES), I32)] * n_out,
        compiler_params=pltpu.CompilerParams(
            dimension_semantics=("arbitrary",), vmem_limit_bytes=VMEM_LIMIT),
        name="projection",
    )(h_tiles, w.astype(BF16))


def _attn_body(q_ref, kp_ref, kc_ref, vp_ref, vc_ref, o_ref, lse_ref, *, dilation, n_steps):
    t = ATT_BLOCK
    not_first = pl.program_id(1) > 0
    qi = lax.broadcasted_iota(jnp.int32, (t, 2 * t), 0)
    kj = lax.broadcasted_iota(jnp.int32, (t, 2 * t), 1)
    dist = t + qi - kj
    valid = (dist >= 0) & (dist <= n_steps) & (not_first | (kj >= t))
    token_dist = (dilation * dist).astype(F32)
    lane = lax.broadcasted_iota(jnp.int32, (t, LANES), 1)
    low_half = lane < HEAD_DIM
    lane_head = lane // LSE_LANES
    contract_last = (((1,), (1,)), ((), ()))

    def unpack(words, which):
        return pltpu.unpack_elementwise(words, index=which, packed_dtype=BF16,
                                        unpacked_dtype=F32).astype(BF16)

    def residue(r, carry):
        rows = pl.ds(r, t) if dilation == 1 else pl.ds(r, t, stride=dilation)
        lse_all = jnp.zeros((t, LANES), F32)
        for slab in range(PAIR_SLABS):
            q_words = q_ref.at[slab][rows, :]
            k_words = jnp.concatenate([kp_ref.at[slab][rows, :], kc_ref.at[slab][rows, :]], axis=0)
            v_words = jnp.concatenate([vp_ref.at[slab][rows, :], vc_ref.at[slab][rows, :]], axis=0)
            pair_outs = []
            for which in range(2):
                pair = 2 * slab + which
                q2, k2, v2 = unpack(q_words, which), unpack(k_words, which), unpack(v_words, which)
                outs, lses = [], []
                for half in range(2):
                    head = 2 * pair + half
                    slope = 2.0 ** (-8.0 * (head + 1) / ATT_HEADS)
                    keep = low_half if half == 0 else ~low_half
                    qh = jnp.where(keep, q2, jnp.zeros_like(q2))
                    s = lax.dot_general(qh, k2, contract_last, preferred_element_type=F32)
                    s = jnp.where(valid, s - slope * token_dist, MASK_VALUE)
                    m = jnp.max(s, axis=-1, keepdims=True)
                    p = jnp.exp(s - m)
                    l = jnp.sum(p, axis=-1, keepdims=True)
                    outs.append(_bdot(p.astype(BF16), v2) / l)
                    lse_all = jnp.where(lane_head == head, m + jnp.log(l), lse_all)
                pair_outs.append(jnp.where(low_half, outs[0], outs[1]))
            o_ref.at[slab][rows, :] = pltpu.pack_elementwise(pair_outs, packed_dtype=BF16)
        lse_ref[rows, :] = lse_all
        return carry

    if dilation == 1:
        residue(0, 0)
    else:
        lax.fori_loop(0, dilation, residue, 0)


def _attn_branch(q, k, v, bsz, seq, window, dilation):
    span = ATT_BLOCK * dilation
    n_spans = seq // span
    cur = pl.BlockSpec((PAIR_SLABS, span, LANES), lambda bi, n: (0, bi * n_spans + n, 0))
    prev = pl.BlockSpec((PAIR_SLABS, span, LANES),
                        lambda bi, n: (0, bi * n_spans + jnp.maximum(n - 1, 0), 0))
    return pl.pallas_call(
        functools.partial(_attn_body, dilation=dilation, n_steps=window // dilation),
        grid=(bsz, n_spans),
        in_specs=[cur, prev, cur, prev, cur],
        out_specs=[cur, pl.BlockSpec((span, LANES), lambda bi, n: (bi * n_spans + n, 0))],
        out_shape=[jax.ShapeDtypeStruct((PAIR_SLABS, bsz * seq, LANES), I32),
                   jax.ShapeDtypeStruct((bsz * seq, LANES), F32)],
        compiler_params=pltpu.CompilerParams(
            dimension_semantics=("arbitrary",) * 2, vmem_limit_bytes=VMEM_LIMIT),
        name=f"dilated_attn_d{dilation}",
    )(q, k, k, v, v)


def _merge_body(*refs, tt):
    o_refs = refs[:N_GROUPS]
    lse_refs = refs[N_GROUPS:2 * N_GROUPS]
    h_ref, wo_ref, g_ref, b_ref, out_ref = refs[2 * N_GROUPS:]
    lses = [r[...] for r in lse_refs]
    m = functools.reduce(jnp.maximum, lses)
    es = [jnp.exp(l - m) for l in lses]
    den = functools.reduce(lambda a, c: a + c, es)
    row = lax.broadcasted_iota(jnp.int32, (LANES, O_WIDTH), 0)
    col = lax.broadcasted_iota(jnp.int32, (LANES, O_WIDTH), 1)
    select = jnp.where(row == (col // HEAD_DIM) * LSE_LANES, 1.0, 0.0).astype(BF16)

    def widen(w):
        hi = w.astype(BF16)
        lo = (w - hi.astype(F32)).astype(BF16)
        return _bdot(hi, select) + _bdot(lo, select)

    def branch_out(o_ref):
        return jnp.concatenate(
            [pltpu.unpack_elementwise(o_ref[pair // 2], index=pair % 2, packed_dtype=BF16,
                                      unpacked_dtype=F32) for pair in range(ATT_HEADS // 2)], axis=1)

    o = functools.reduce(lambda a, c: a + c,
                         [widen(e / den) * branch_out(r) for e, r in zip(es, o_refs)])
    mix = _bdot(o.astype(BF16), wo_ref[...])
    z = DEEPNORM_ALPHA * _rows_load(h_ref, tt) + mix
    _rows_store(out_ref, _layer_norm(z, g_ref[...], b_ref[...]), tt)


def _merge(outs, lses, h_tiles, w_o, g, b):
    n_tok = h_tiles.shape[0] // TOK_ROWS
    tt = MERGE_TT
    o_spec = pl.BlockSpec((PAIR_SLABS, tt, LANES), lambda i: (0, i, 0))
    lse_spec = pl.BlockSpec((tt, LANES), lambda i: (i, 0))
    tiles = pl.BlockSpec((tt * TOK_ROWS, LANES), lambda i: (i, 0))
    full = lambda shape: pl.BlockSpec(shape, lambda i: (0,) * len(shape))
    return pl.pallas_call(
        functools.partial(_merge_body, tt=tt),
        grid=(n_tok // tt,),
        in_specs=[o_spec] * N_GROUPS + [lse_spec] * N_GROUPS + [
            tiles, full((O_WIDTH, D_MODEL)), full((1, D_MODEL)), full((1, D_MODEL))],
        out_specs=tiles,
        out_shape=jax.ShapeDtypeStruct(h_tiles.shape, F32),
        compiler_params=pltpu.CompilerParams(
            dimension_semantics=("arbitrary",), vmem_limit_bytes=VMEM_LIMIT),
        name="attn_merge",
    )(*outs, *lses, h_tiles, w_o.astype(BF16), g.reshape(1, -1), b.reshape(1, -1))


def kernel(x, a_w_in, a_conv_w, a_conv_b, a_w_gate_a, a_b_gate_a, a_w_gate_x, a_b_gate_x, a_lambda, a_w_out, w_kv_shared, b_w_q, b_w_o, moe_w_router, moe_router_bias, moe_w1, moe_w3, moe_w2, moe_ws1, moe_ws3, moe_ws2, ln_g, ln_b):
    bsz, seq, d = x.shape
    assert d == D_MODEL and seq % (DILATION_PATTERNS[-1][1] * ATT_BLOCK) == 0

    def moe(h_tiles, layer, token_tile_out):
        return _moe_layer(h_tiles, moe_w_router[layer], moe_router_bias[layer], moe_w1, moe_w3,
                          moe_w2, layer, moe_ws1[layer], moe_ws3[layer], moe_ws2[layer],
                          ln_g[layer, 1], ln_b[layer, 1], token_tile_out)

    h = _rglru_layer(x, a_w_in[0], a_conv_w[0], a_conv_b[0], a_w_gate_a[0], a_b_gate_a[0],
                     a_w_gate_x[0], a_b_gate_x[0], a_lambda[0], a_w_out[0], ln_g[0, 0], ln_b[0, 0])
    h = moe(h, 0, True)
    kvq = _project(h, jnp.concatenate([w_kv_shared, b_w_q[0]], axis=1),
                   (1.0,) * (2 * N_GROUPS) + (HEAD_DIM ** -0.5,) * N_GROUPS)
    ks, vs, qs = kvq[:N_GROUPS], kvq[N_GROUPS:2 * N_GROUPS], kvq[2 * N_GROUPS:]

    outs, lses = [], []
    for grp, (window, dilation) in enumerate(DILATION_PATTERNS):
        o, lse = _attn_branch(qs[grp], ks[grp], vs[grp], bsz, seq, window, dilation)
        outs.append(o)
        lses.append(lse)
    h = _merge(outs, lses, h, b_w_o[0], ln_g[1, 0], ln_b[1, 0])
    h = moe(h, 1, False)
    return h.reshape(bsz, seq, d)
```

```python
import functools
import math

import jax
import jax.numpy as jnp
from jax import lax
from jax.experimental import pallas as pl
from jax.experimental.pallas import tpu as pltpu

F32 = jnp.float32
BF16 = jnp.bfloat16
I32 = jnp.int32

D_MODEL = 1024
LRU_BLOCKS = 4
CONV_WIDTH = 4
LRU_C = 8.0
ATT_HEADS = 8
HEAD_DIM = 64
DILATION_PATTERNS = ((128, 1), (512, 4), (2048, 16))
N_GROUPS = len(DILATION_PATTERNS)
ATT_BLOCK = 128
O_WIDTH = ATT_HEADS * HEAD_DIM
N_EXPERTS = 256
TOP_K = 8
N_EXPERT_GROUPS = 8
TOPK_GROUPS = 4
D_EXPERT = 256
ROUTED_SCALE = 2.5
DEPTH = 2
DEEPNORM_ALPHA = (2 * DEPTH) ** 0.25
LN_EPS = 1e-5
MASK_VALUE = -1e30

SUBLANES = 8
LANES = 128
TOK_ROWS = D_MODEL // LANES
PK_ROWS = D_MODEL // 2 // LANES
PAIR_SLABS = ATT_HEADS * HEAD_DIM // LANES // 2
LSE_LANES = LANES // ATT_HEADS
VMEM_LIMIT = 56 * 1024 * 1024

RGLRU_TS = 512
ROUTER_TT = 1024
SORTED_ROWS_TL = 2048
DISPATCH_TT = 512
EXPERT_TB = 512
COMBINE_TT = 512
PROJ_TT = 1024
MERGE_TT = 512
ATTN_D1_BLOCKS = 4


def _rows_load(ref, n_tok):
    return jnp.concatenate(
        [ref[pl.ds(s, n_tok, stride=TOK_ROWS), :] for s in range(TOK_ROWS)], axis=1)


def _rows_store(ref, val, n_tok):
    for s in range(TOK_ROWS):
        ref[pl.ds(s, n_tok, stride=TOK_ROWS), :] = val[:, s * LANES:(s + 1) * LANES]


def _layer_norm(z, g, b):
    mu = jnp.mean(z, axis=-1, keepdims=True)
    zc = z - mu
    var = jnp.mean(zc * zc, axis=-1, keepdims=True)
    return zc * lax.rsqrt(var + LN_EPS) * g + b


def _silu(x):
    return x * jax.nn.sigmoid(x)


def _gelu_tanh(x):
    c = math.sqrt(2.0 / math.pi)
    return 0.5 * x * (1.0 + jnp.tanh(c * (x + 0.044715 * (x * x * x))))


def _bdot(a, b):
    return jnp.dot(a, b, preferred_element_type=F32)


def _rglru_body(x_ref, win_ref, cw_ref, cb_ref, wga_ref, bga_ref, wgx_ref, bgx_ref, lam_ref,
                wout_ref, g_ref, b_ref, o_ref, hc_ref, tail_ref, *, ts):
    width = D_MODEL
    bw = width // LRU_BLOCKS

    @pl.when(pl.program_id(1) == 0)
    def _():
        hc_ref[...] = jnp.zeros_like(hc_ref)
        tail_ref[...] = jnp.zeros_like(tail_ref)

    x = x_ref[...]
    xz = _bdot(x.astype(BF16), win_ref[...])
    xr = xz[:, :width]
    gate = xz[:, width:]

    tail = tail_ref[...]
    row8 = lax.broadcasted_iota(jnp.int32, (SUBLANES, width), 0)
    cw = cw_ref[...]
    xc = xr * cw[CONV_WIDTH - 1:CONV_WIDTH, :] + cb_ref[...]
    for j in range(1, CONV_WIDTH):
        rx = pltpu.roll(xr, j, 0)
        rp = pltpu.roll(tail, j, 0)
        top = jnp.where(row8 < j, rp, rx[:SUBLANES])
        shifted = jnp.concatenate([top, rx[SUBLANES:]], axis=0)
        xc = xc + shifted * cw[CONV_WIDTH - 1 - j:CONV_WIDTH - j, :]
    tail_ref[...] = xr[ts - SUBLANES:]

    xcb = xc.astype(BF16)

    def block_diag(w_ref):
        return jnp.concatenate(
            [_bdot(xcb[:, n * bw:(n + 1) * bw], w_ref[n]) for n in range(LRU_BLOCKS)], axis=1)

    r = jax.nn.sigmoid(block_diag(wga_ref) + bga_ref[...])
    i = jax.nn.sigmoid(block_diag(wgx_ref) + bgx_ref[...])
    lam = lam_ref[...]
    softplus_neg_lam = jnp.maximum(-lam, 0.0) + jnp.log1p(jnp.exp(-jnp.abs(lam)))
    log_a = (-LRU_C * r) * softplus_neg_lam
    a = jnp.exp(log_a)
    mult = jnp.sqrt(-jnp.tanh(log_a) * (a * a + 1.0))
    u = mult * (i * xc)

    groups = ts // SUBLANES
    a3 = a.reshape(groups, SUBLANES, width)
    u3 = u.reshape(groups, SUBLANES, width)
    sub = lax.broadcasted_iota(jnp.int32, (groups, SUBLANES, width), 1)
    sh = 1
    while sh < SUBLANES:
        a_prev = pltpu.roll(a3, sh, 1)
        u_prev = pltpu.roll(u3, sh, 1)
        live = sub >= sh
        u3 = jnp.where(live, a3 * u_prev, 0.0) + u3
        a3 = jnp.where(live, a3 * a_prev, a3)
        sh *= 2
    carry = hc_ref[...]
    h_groups = []
    for grp in range(groups):
        h_grp = a3[grp] * carry + u3[grp]
        h_groups.append(h_grp)
        carry = h_grp[SUBLANES - 1:SUBLANES]
    h = jnp.concatenate(h_groups, axis=0)
    hc_ref[...] = carry

    y = (h * _gelu_tanh(gate)).astype(BF16)
    mix = _bdot(y, wout_ref[...])
    z = DEEPNORM_ALPHA * x + mix
    _rows_store(o_ref, _layer_norm(z, g_ref[...], b_ref[...]), ts)


def _rglru_layer(x, w_in, conv_w, conv_b, wga, bga, wgx, bgx, lam, w_out, g, b):
    bsz, seq, d = x.shape
    ts = RGLRU_TS
    ns = seq // ts
    row = lambda v: v.reshape(1, -1)
    full = lambda shape: pl.BlockSpec(shape, lambda bi, si: (0,) * len(shape))
    return pl.pallas_call(
        functools.partial(_rglru_body, ts=ts),
        grid=(bsz, ns),
        in_specs=[
            pl.BlockSpec((None, ts, d), lambda bi, si: (bi, si, 0)),
            full((d, 2 * d)), full((CONV_WIDTH, d)), full((1, d)),
            full((LRU_BLOCKS, d // LRU_BLOCKS, d // LRU_BLOCKS)), full((1, d)),
            full((LRU_BLOCKS, d // LRU_BLOCKS, d // LRU_BLOCKS)), full((1, d)),
            full((1, d)), full((d, d)), full((1, d)), full((1, d)),
        ],
        out_specs=pl.BlockSpec((ts * TOK_ROWS, LANES), lambda bi, si: (bi * ns + si, 0)),
        out_shape=jax.ShapeDtypeStruct((bsz * seq * TOK_ROWS, LANES), F32),
        scratch_shapes=[pltpu.VMEM((1, d), F32), pltpu.VMEM((SUBLANES, d), F32)],
        compiler_params=pltpu.CompilerParams(
            dimension_semantics=("arbitrary", "arbitrary"), vmem_limit_bytes=VMEM_LIMIT),
        name="rglru_layer",
    )(x, w_in.astype(BF16), conv_w, row(conv_b), wga.astype(BF16), row(bga), wgx.astype(BF16),
      row(bgx), row(lam), w_out.astype(BF16), row(g), row(b))


def _router_body(h_ref, wrt_ref, bias_ref, idx_ref, gate_ref, rank_ref, cnt_ref, carry_ref, *, tt):
    n_e = N_EXPERTS
    per_group = n_e // N_EXPERT_GROUPS

    @pl.when(pl.program_id(0) == 0)
    def _():
        carry_ref[...] = jnp.zeros_like(carry_ref)

    h = _rows_load(h_ref, tt)
    logits = lax.dot_general(wrt_ref[...], h.astype(BF16), (((1,), (1,)), ((), ())),
                             preferred_element_type=F32)
    scores = jax.nn.sigmoid(logits)
    biased = scores + bias_ref[...]

    j_iota = lax.broadcasted_iota(jnp.int32, (per_group, tt), 0)
    group_score = []
    for g in range(N_EXPERT_GROUPS):
        bg = biased[g * per_group:(g + 1) * per_group]
        m1 = jnp.max(bg, axis=0, keepdims=True)
        i1 = jnp.min(jnp.where(bg == m1, j_iota, per_group), axis=0, keepdims=True)
        m2 = jnp.max(jnp.where(j_iota == i1, -jnp.inf, bg), axis=0, keepdims=True)
        group_score.append(m1 + m2)

    masked = []
    for g in range(N_EXPERT_GROUPS):
        beaten_by = jnp.zeros((1, tt), jnp.int32)
        for o in range(N_EXPERT_GROUPS):
            if o == g:
                continue
            wins = group_score[o] > group_score[g]
            if o < g:
                wins = wins | (group_score[o] == group_score[g])
            beaten_by = beaten_by + wins.astype(jnp.int32)
        keep = beaten_by < TOPK_GROUPS
        masked.append(jnp.where(keep, biased[g * per_group:(g + 1) * per_group], MASK_VALUE))
    cur = jnp.concatenate(masked, axis=0)

    e_iota = lax.broadcasted_iota(jnp.int32, (n_e, tt), 0)
    idx_rows, score_rows, sels = [], [], []
    for _ in range(TOP_K):
        m = jnp.max(cur, axis=0, keepdims=True)
        ik = jnp.min(jnp.where(cur == m, e_iota, n_e), axis=0, keepdims=True)
        sel = e_iota == ik
        score_rows.append(jnp.sum(jnp.where(sel, scores, 0.0), axis=0, keepdims=True))
        cur = jnp.where(sel, -jnp.inf, cur)
        idx_rows.append(ik)
        sels.append(sel)
    top_s = jnp.concatenate(score_rows, axis=0)
    gate_ref[...] = top_s / jnp.sum(top_s, axis=0, keepdims=True) * ROUTED_SCALE
    idx_ref[...] = jnp.concatenate(idx_rows, axis=0)
    multi_hot = jnp.where(cur == -jnp.inf, 1.0, 0.0)

    t_row = lax.broadcasted_iota(jnp.int32, (tt, tt), 0)
    t_col = lax.broadcasted_iota(jnp.int32, (tt, tt), 1)
    strict_upper = jnp.where(t_row < t_col, 1.0, 0.0).astype(BF16)
    before = _bdot(multi_hot.astype(BF16), strict_upper) + carry_ref[...]
    rank_rows = [jnp.sum(jnp.where(sel, before, 0.0), axis=0, keepdims=True) for sel in sels]
    rank_ref[...] = jnp.concatenate(rank_rows, axis=0).astype(jnp.int32)
    carry = carry_ref[...] + jnp.sum(multi_hot, axis=1, keepdims=True)
    carry_ref[...] = carry
    cnt_ref[...] = carry.astype(jnp.int32)


def _router(h_tiles, w_router, router_bias):
    n_tok = h_tiles.shape[0] // TOK_ROWS
    tt = ROUTER_TT
    kt = lambda dt: jax.ShapeDtypeStruct((TOP_K, n_tok), dt)
    tok_spec = pl.BlockSpec((TOP_K, tt), lambda i: (0, i))
    return pl.pallas_call(
        functools.partial(_router_body, tt=tt),
        grid=(n_tok // tt,),
        in_specs=[
            pl.BlockSpec((tt * TOK_ROWS, LANES), lambda i: (i, 0)),
            pl.BlockSpec((N_EXPERTS, D_MODEL), lambda i: (0, 0)),
            pl.BlockSpec((N_EXPERTS, 1), lambda i: (0, 0)),
        ],
        out_specs=[tok_spec, tok_spec, tok_spec, pl.BlockSpec((N_EXPERTS, 1), lambda i: (0, 0))],
        out_shape=[kt(jnp.int32), kt(F32), kt(jnp.int32),
                   jax.ShapeDtypeStruct((N_EXPERTS, 1), jnp.int32)],
        scratch_shapes=[pltpu.VMEM((N_EXPERTS, 1), F32)],
        compiler_params=pltpu.CompilerParams(
            dimension_semantics=("arbitrary",), vmem_limit_bytes=VMEM_LIMIT),
        name="moe_router",
    )(h_tiles, w_router.T.astype(BF16), router_bias.reshape(N_EXPERTS, 1))


def _pack_rows(x):
    half = D_MODEL // 2
    return pltpu.pack_elementwise([x[:, :half], x[:, half:]], packed_dtype=BF16)


def _unpack_rows(p):
    return jnp.concatenate(
        [pltpu.unpack_elementwise(p, index=i, packed_dtype=BF16, unpacked_dtype=F32) for i in range(2)],
        axis=1)


def _packed_load(ref, n_tok):
    return jnp.concatenate(
        [ref[pl.ds(s, n_tok, stride=PK_ROWS), :] for s in range(PK_ROWS)], axis=1)


def _packed_store(ref, val, n_tok):
    for s in range(PK_ROWS):
        ref[pl.ds(s, n_tok, stride=PK_ROWS), :] = val[:, s * LANES:(s + 1) * LANES]


def _sorted_rows_body(idx_ref, rank_ref, start_ref, o_ref):
    tl = idx_ref.shape[1]
    e_iota = lax.broadcasted_iota(jnp.int32, (N_EXPERTS, tl), 0)
    start = start_ref[...]
    rows = []
    for k in range(TOP_K):
        hit = e_iota == idx_ref[k:k + 1, :]
        base = jnp.sum(jnp.where(hit, start, 0.0), axis=0, keepdims=True)
        rows.append(base.astype(jnp.int32) + rank_ref[k:k + 1, :])
    o_ref[...] = jnp.concatenate(rows, axis=0)


def _sorted_rows(top_idx, rank, pad_start):
    n_tok = top_idx.shape[1]
    tl = SORTED_ROWS_TL
    tok = pl.BlockSpec((TOP_K, tl), lambda i: (0, i))
    return pl.pallas_call(
        _sorted_rows_body,
        grid=(n_tok // tl,),
        in_specs=[tok, tok, pl.BlockSpec((N_EXPERTS, 1), lambda i: (0, 0))],
        out_specs=tok,
        out_shape=jax.ShapeDtypeStruct((TOP_K, n_tok), jnp.int32),
        compiler_params=pltpu.CompilerParams(
            dimension_semantics=("arbitrary",), vmem_limit_bytes=VMEM_LIMIT),
        name="moe_sorted_rows",
    )(top_idx, rank, pad_start.astype(F32).reshape(N_EXPERTS, 1))


def _dispatch_body(start_ref, cnt_ref, dest_ref, h_ref, xs_ref, pk_ref, zero_ref, sem, zero_sem,
                   *, tt, tb):
    @pl.when(pl.program_id(0) == 0)
    def _():
        zero_ref[...] = jnp.zeros_like(zero_ref)

        def pad_copies(e, fn):
            n_pad = ((cnt_ref[e] + (tb - 1)) & (-tb)) - cnt_ref[e]
            row = start_ref[e] + cnt_ref[e]
            for bit in range(tb.bit_length() - 1):
                size = 1 << bit
                has_bit = ((n_pad >> bit) & 1) == 1

                @pl.when(has_bit)
                def _():
                    fn(pltpu.make_async_copy(zero_ref.at[pl.ds(0, size)],
                                             xs_ref.at[pl.ds(row, size)], zero_sem))
                row = row + jnp.where(has_bit, size, 0)

        def start_all(e, c):
            pad_copies(e, lambda cp: cp.start())
            return c

        def wait_all(e, c):
            pad_copies(e, lambda cp: cp.wait())
            return c

        lax.fori_loop(0, N_EXPERTS, start_all, 0)
        lax.fori_loop(0, N_EXPERTS, wait_all, 0)

    packed = _pack_rows(_rows_load(h_ref, tt))
    for s in range(PK_ROWS):
        pk_ref[:, s, :] = packed[:, s * LANES:(s + 1) * LANES]
    for t in range(tt):
        for k in range(TOP_K):
            pltpu.make_async_copy(pk_ref.at[t], xs_ref.at[dest_ref[k, t]], sem).start(priority=k % 2)
    for k in range(TOP_K):
        pltpu.make_async_copy(pk_ref, xs_ref.at[pl.ds(0, tt)], sem).wait()


def _dispatch(h_tiles, dest_rows, pad_start, counts, n_rows):
    n_tok = h_tiles.shape[0] // TOK_ROWS
    tt = DISPATCH_TT
    grid_spec = pltpu.PrefetchScalarGridSpec(
        num_scalar_prefetch=2,
        grid=(n_tok // tt,),
        in_specs=[pl.BlockSpec((TOP_K, tt), lambda i, s, c: (0, i), memory_space=pltpu.SMEM),
                  pl.BlockSpec((tt * TOK_ROWS, LANES), lambda i, s, c: (i, 0))],
        out_specs=pl.BlockSpec(memory_space=pl.ANY),
        scratch_shapes=[pltpu.VMEM((tt, PK_ROWS, LANES), I32),
                        pltpu.VMEM((EXPERT_TB // 2, PK_ROWS, LANES), I32),
                        pltpu.SemaphoreType.DMA(()), pltpu.SemaphoreType.DMA(())],
    )
    return pl.pallas_call(
        functools.partial(_dispatch_body, tt=tt, tb=EXPERT_TB),
        grid_spec=grid_spec,
        out_shape=jax.ShapeDtypeStruct((n_rows, PK_ROWS, LANES), I32),
        compiler_params=pltpu.CompilerParams(
            dimension_semantics=("arbitrary",), vmem_limit_bytes=VMEM_LIMIT),
        name="moe_dispatch",
    )(pad_start, counts, dest_rows, h_tiles)


def _expert_body(be_ref, first_ref, slot_ref, next_ref, na_ref, xs_ref, w1_ref, w3_ref, w2_ref, ys_ref,
                 w13_buf, w2_buf, w1b, w3b, w2b, sem, *, tb, layer):
    i = pl.program_id(0)

    def weight_copies(expert, slot):
        return (pltpu.make_async_copy(w1_ref.at[layer, expert], w13_buf.at[slot, 0], sem.at[slot]),
                pltpu.make_async_copy(w3_ref.at[layer, expert], w13_buf.at[slot, 1], sem.at[slot]),
                pltpu.make_async_copy(w2_ref.at[layer, expert], w2_buf.at[slot], sem.at[slot]))

    @pl.when(i < na_ref[0])
    def _():
        @pl.when(first_ref[i] == 1)
        def _():
            slot = slot_ref[i]

            @pl.when(i == 0)
            def _():
                for cp in weight_copies(be_ref[0], slot):
                    cp.start()

            for cp in weight_copies(be_ref[i], slot):
                cp.wait()

            @pl.when(next_ref[i] >= 0)
            def _():
                for cp in weight_copies(next_ref[i], 1 - slot):
                    cp.start()

            w1b[...] = w13_buf[slot, 0].astype(BF16)
            w3b[...] = w13_buf[slot, 1].astype(BF16)
            w2b[...] = w2_buf[slot].astype(BF16)

        x = _unpack_rows(_packed_load(xs_ref, tb)).astype(BF16)
        mid = (_silu(_bdot(x, w1b[...])) * _bdot(x, w3b[...])).astype(BF16)
        _packed_store(ys_ref, _pack_rows(_bdot(mid, w2b[...])), tb)


def _experts(xs, block_expert, first, slot, next_expert, n_active, w1, w3, w2, layer):
    tb = EXPERT_TB
    n_blocks = xs.shape[0] // (tb * PK_ROWS)
    row_block = lambda i, be, fi, sl, nx, na: (jnp.minimum(i, na[0] - 1), 0)
    grid_spec = pltpu.PrefetchScalarGridSpec(
        num_scalar_prefetch=5,
        grid=(n_blocks,),
        in_specs=[
            pl.BlockSpec((tb * PK_ROWS, LANES), row_block),
            pl.BlockSpec(memory_space=pl.ANY), pl.BlockSpec(memory_space=pl.ANY),
            pl.BlockSpec(memory_space=pl.ANY),
        ],
        out_specs=pl.BlockSpec((tb * PK_ROWS, LANES), row_block),
        scratch_shapes=[pltpu.VMEM((2, 2, D_MODEL, D_EXPERT), F32),
                        pltpu.VMEM((2, D_EXPERT, D_MODEL), F32),
                        pltpu.VMEM((D_MODEL, D_EXPERT), BF16), pltpu.VMEM((D_MODEL, D_EXPERT), BF16),
                        pltpu.VMEM((D_EXPERT, D_MODEL), BF16), pltpu.SemaphoreType.DMA((2,))],
    )
    return pl.pallas_call(
        functools.partial(_expert_body, tb=tb, layer=layer),
        grid_spec=grid_spec,
        out_shape=jax.ShapeDtypeStruct(xs.shape, I32),
        compiler_params=pltpu.CompilerParams(
            dimension_semantics=("arbitrary",), vmem_limit_bytes=VMEM_LIMIT),
        name="moe_experts",
    )(block_expert, first, slot, next_expert, n_active, xs, w1, w3, w2)


def _store_head_slabs(y, o_refs, scales):
    for j, (o_ref, scale) in enumerate(zip(o_refs, scales)):
        for s in range(PAIR_SLABS):
            lo = j * O_WIDTH + 2 * s * LANES
            o_ref[s] = pltpu.pack_elementwise(
                [y[:, lo:lo + LANES] * scale, y[:, lo + LANES:lo + 2 * LANES] * scale],
                packed_dtype=BF16)


def _combine_body(src_ref, h_ref, gates_ref, ys_ref, ys_flat_ref, ws1_ref, ws3_ref, ws2_ref,
                  g_ref, b_ref, o_ref, gbuf, sem, *, tt, n_tiles, token_tile_out):
    j = pl.program_id(0)

    def step(issue, finish):
        nxt = j % 2
        cur = (j - 1) % 2
        if finish:
            h = _rows_load(h_ref, tt)
            hb = h.astype(BF16)
            mid = (_silu(_bdot(hb, ws1_ref[...])) * _bdot(hb, ws3_ref[...])).astype(BF16)
            routed = _bdot(mid, ws2_ref[...])
            gates = gates_ref[...]
        for k in range(TOP_K):
            if issue:
                for t in range(tt):
                    pltpu.make_async_copy(ys_ref.at[src_ref[k, t]],
                                          gbuf.at[nxt, k, pl.ds(t * PK_ROWS, PK_ROWS), :],
                                          sem.at[nxt]).start(priority=t % 2)
            if finish:
                pltpu.make_async_copy(ys_flat_ref.at[pl.ds(0, tt * PK_ROWS), :], gbuf.at[cur, k],
                                      sem.at[cur]).wait()
                routed = routed + gates[:, k:k + 1] * _unpack_rows(_packed_load(gbuf.at[cur, k], tt))
        if finish:
            out = _layer_norm(DEEPNORM_ALPHA * h + routed, g_ref[...], b_ref[...])
            if token_tile_out:
                _rows_store(o_ref, out, tt)
            else:
                o_ref[...] = out

    pl.when(j == 0)(lambda: step(True, False))
    pl.when((j >= 1) & (j < n_tiles))(lambda: step(True, True))
    pl.when(j == n_tiles)(lambda: step(False, True))


def _combine(h_tiles, src_rows, gates_tk, ys, ws1, ws3, ws2, g, b, token_tile_out):
    n_tok = h_tiles.shape[0] // TOK_ROWS
    tt = COMBINE_TT
    n_tiles = n_tok // tt
    prev_tile = lambda j: (jnp.maximum(j - 1, 0), 0)
    full = lambda shape: pl.BlockSpec(shape, lambda j: (0,) * len(shape))
    if token_tile_out:
        out_spec = pl.BlockSpec((tt * TOK_ROWS, LANES), prev_tile)
        out_shape = jax.ShapeDtypeStruct((n_tok * TOK_ROWS, LANES), F32)
    else:
        out_spec = pl.BlockSpec((tt, D_MODEL), prev_tile)
        out_shape = jax.ShapeDtypeStruct((n_tok, D_MODEL), F32)
    n_rows = ys.shape[0] // PK_ROWS
    return pl.pallas_call(
        functools.partial(_combine_body, tt=tt, n_tiles=n_tiles, token_tile_out=token_tile_out),
        grid=(n_tiles + 1,),
        in_specs=[pl.BlockSpec((TOP_K, tt), lambda j: (0, jnp.minimum(j, n_tiles - 1)),
                               memory_space=pltpu.SMEM),
                  pl.BlockSpec((tt * TOK_ROWS, LANES), prev_tile),
                  pl.BlockSpec((tt, TOP_K), prev_tile),
                  pl.BlockSpec(memory_space=pl.ANY), pl.BlockSpec(memory_space=pl.ANY),
                  full((D_MODEL, D_EXPERT)), full((D_MODEL, D_EXPERT)), full((D_EXPERT, D_MODEL)),
                  full((1, D_MODEL)), full((1, D_MODEL))],
        out_specs=out_spec,
        out_shape=out_shape,
        scratch_shapes=[pltpu.VMEM((2, TOP_K, tt * PK_ROWS, LANES), I32),
                        pltpu.SemaphoreType.DMA((2,))],
        compiler_params=pltpu.CompilerParams(
            dimension_semantics=("arbitrary",), vmem_limit_bytes=VMEM_LIMIT),
        name="moe_combine",
    )(src_rows, h_tiles, gates_tk, ys.reshape(n_rows, PK_ROWS, LANES), ys, ws1.astype(BF16),
      ws3.astype(BF16), ws2.astype(BF16), g.reshape(1, -1), b.reshape(1, -1))


def _moe_layer(h_tiles, w_router, router_bias, w1, w3, w2, layer, ws1, ws3, ws2, g, b,
               token_tile_out):
    n_tok = h_tiles.shape[0] // TOK_ROWS
    tb = EXPERT_TB
    top_idx, gates, rank, counts = _router(h_tiles, w_router, router_bias)

    n_blocks = -(-(n_tok * TOP_K + N_EXPERTS * (tb - 1)) // tb)
    counts = counts.reshape(N_EXPERTS)
    padded = ((counts + tb - 1) // tb) * tb
    pad_end = jnp.cumsum(padded)
    pad_start = (pad_end - padded).astype(jnp.int32)
    block_first_row = jnp.arange(n_blocks, dtype=jnp.int32) * tb
    block_expert = jnp.minimum(
        jnp.sum((pad_end[None, :] <= block_first_row[:, None]).astype(jnp.int32), axis=1),
        N_EXPERTS - 1)
    n_active = (pad_end[-1:] // tb).astype(jnp.int32)
    present = counts > 0
    expert_ids = jnp.arange(N_EXPERTS, dtype=jnp.int32)
    later = lax.cummin(jnp.where(present, expert_ids, N_EXPERTS), reverse=True)
    next_present = jnp.concatenate([later[1:], jnp.full((1,), N_EXPERTS, jnp.int32)])
    next_present = jnp.where(next_present < N_EXPERTS, next_present, -1)
    ordinal = jnp.cumsum(present.astype(jnp.int32)) - 1
    first = jnp.concatenate([jnp.ones((1,), jnp.int32),
                             (block_expert[1:] != block_expert[:-1]).astype(jnp.int32)])
    slot = ordinal[block_expert] % 2
    next_expert = next_present[block_expert]

    rows = _sorted_rows(top_idx, rank, pad_start)
    xs = _dispatch(h_tiles, rows, pad_start, counts, n_blocks * tb)
    ys = _experts(xs.reshape(n_blocks * tb * PK_ROWS, LANES), block_expert, first, slot, next_expert,
                  n_active, w1, w3, w2, layer)
    return _combine(h_tiles, rows, gates.T, ys, ws1, ws3, ws2, g, b, token_tile_out)


def _proj_body(h_ref, w_ref, *o_refs, tt, scales):
    y = _bdot(_rows_load(h_ref, tt).astype(BF16), w_ref[...])
    _store_head_slabs(y, o_refs, scales)


def _project(h_tiles, w, scales):
    n_tok = h_tiles.shape[0] // TOK_ROWS
    tt = PROJ_TT
    n_out = len(scales)
    out_spec = pl.BlockSpec((PAIR_SLABS, tt, LANES), lambda i: (0, i, 0))
    return pl.pallas_call(
        functools.partial(_proj_body, tt=tt, scales=tuple(scales)),
        grid=(n_tok // tt,),
        in_specs=[pl.BlockSpec((tt * TOK_ROWS, LANES), lambda i: (i, 0)),
                  pl.BlockSpec(w.shape, lambda i: (0, 0))],
        out_specs=[out_spec] * n_out,
        out_shape=[jax.ShapeDtypeStruct((PAIR_SLABS, n_tok, LANES), I32)] * n_out,
        compiler_params=pltpu.CompilerParams(
            dimension_semantics=("arbitrary",), vmem_limit_bytes=VMEM_LIMIT),
        name="projection",
    )(h_tiles, w.astype(BF16))


def _attn_body(q_ref, kp_ref, kc_ref, vp_ref, vc_ref, o_ref, lse_ref, *, dilation, n_steps, blocks):
    t = ATT_BLOCK
    qi = lax.broadcasted_iota(jnp.int32, (t, 2 * t), 0)
    kj = lax.broadcasted_iota(jnp.int32, (t, 2 * t), 1)
    dist = t + qi - kj
    band = (dist >= 0) & (dist <= n_steps)
    band_at_start = band & ((pl.program_id(1) > 0) | (kj >= t))
    token_dist = (dilation * dist).astype(F32)
    lane = lax.broadcasted_iota(jnp.int32, (t, LANES), 1)
    low_half = lane < HEAD_DIM
    lane_head = lane // LSE_LANES
    contract_last = (((1,), (1,)), ((), ()))

    def unpack(words, which):
        return pltpu.unpack_elementwise(words, index=which, packed_dtype=BF16,
                                        unpacked_dtype=F32).astype(BF16)

    def block(rows, k_prev, v_prev, prev_rows, valid):
        lse_all = jnp.zeros((t, LANES), F32)
        for slab in range(PAIR_SLABS):
            q_words = q_ref.at[slab][rows, :]
            k_words = jnp.concatenate([k_prev.at[slab][prev_rows, :], kc_ref.at[slab][rows, :]], axis=0)
            v_words = jnp.concatenate([v_prev.at[slab][prev_rows, :], vc_ref.at[slab][rows, :]], axis=0)
            pair_outs = []
            for which in range(2):
                pair = 2 * slab + which
                q2, k2, v2 = unpack(q_words, which), unpack(k_words, which), unpack(v_words, which)
                outs = []
                for half in range(2):
                    head = 2 * pair + half
                    slope = 2.0 ** (-8.0 * (head + 1) / ATT_HEADS)
                    keep = low_half if half == 0 else ~low_half
                    qh = jnp.where(keep, q2, jnp.zeros_like(q2))
                    s = lax.dot_general(qh, k2, contract_last, preferred_element_type=F32)
                    s = jnp.where(valid, s - slope * token_dist, MASK_VALUE)
                    m = jnp.max(s, axis=-1, keepdims=True)
                    p = jnp.exp(s - m)
                    l = jnp.sum(p, axis=-1, keepdims=True)
                    outs.append(_bdot(p.astype(BF16), v2) / l)
                    lse_all = jnp.where(lane_head == head, m + jnp.log(l), lse_all)
                pair_outs.append(jnp.where(low_half, outs[0], outs[1]))
            o_ref.at[slab][rows, :] = pltpu.pack_elementwise(pair_outs, packed_dtype=BF16)
        lse_ref[rows, :] = lse_all

    if dilation == 1:
        block(pl.ds(0, t), kp_ref, vp_ref, pl.ds(0, t), band_at_start)
        for b in range(1, blocks):
            block(pl.ds(b * t, t), kc_ref, vc_ref, pl.ds((b - 1) * t, t), band)
    else:
        def residue(r, carry):
            rows = pl.ds(r, t, stride=dilation)
            block(rows, kp_ref, vp_ref, rows, band_at_start)
            return carry
        lax.fori_loop(0, dilation, residue, 0)


def _attn_branch(q, k, v, bsz, seq, window, dilation):
    blocks = ATTN_D1_BLOCKS if dilation == 1 else 1
    span = ATT_BLOCK * dilation * blocks
    n_spans = seq // span
    prev_rows = ATT_BLOCK if dilation == 1 else span
    prev_per_span = span // prev_rows
    cur = pl.BlockSpec((PAIR_SLABS, span, LANES), lambda bi, n: (0, bi * n_spans + n, 0))
    prev = pl.BlockSpec(
        (PAIR_SLABS, prev_rows, LANES),
        lambda bi, n: (0, (bi * n_spans + n) * prev_per_span - jnp.minimum(n, 1), 0))
    return pl.pallas_call(
        functools.partial(_attn_body, dilation=dilation, n_steps=window // dilation, blocks=blocks),
        grid=(bsz, n_spans),
        in_specs=[cur, prev, cur, prev, cur],
        out_specs=[cur, pl.BlockSpec((span, LANES), lambda bi, n: (bi * n_spans + n, 0))],
        out_shape=[jax.ShapeDtypeStruct((PAIR_SLABS, bsz * seq, LANES), I32),
                   jax.ShapeDtypeStruct((bsz * seq, LANES), F32)],
        compiler_params=pltpu.CompilerParams(
            dimension_semantics=("arbitrary",) * 2, vmem_limit_bytes=VMEM_LIMIT),
        name=f"dilated_attn_d{dilation}",
    )(q, k, k, v, v)


def _merge_body(*refs, tt):
    o_refs = refs[:N_GROUPS]
    lse_refs = refs[N_GROUPS:2 * N_GROUPS]
    h_ref, wo_ref, g_ref, b_ref, out_ref = refs[2 * N_GROUPS:]
    lses = [r[...] for r in lse_refs]
    m = functools.reduce(jnp.maximum, lses)
    es = [jnp.exp(l - m) for l in lses]
    den = functools.reduce(lambda a, c: a + c, es)
    row = lax.broadcasted_iota(jnp.int32, (LANES, O_WIDTH), 0)
    col = lax.broadcasted_iota(jnp.int32, (LANES, O_WIDTH), 1)
    select = jnp.where(row == (col // HEAD_DIM) * LSE_LANES, 1.0, 0.0).astype(BF16)

    def widen(w):
        hi = w.astype(BF16)
        lo = (w - hi.astype(F32)).astype(BF16)
        return _bdot(hi, select) + _bdot(lo, select)

    def branch_out(o_ref):
        return jnp.concatenate(
            [pltpu.unpack_elementwise(o_ref[pair // 2], index=pair % 2, packed_dtype=BF16,
                                      unpacked_dtype=F32) for pair in range(ATT_HEADS // 2)], axis=1)

    o = functools.reduce(lambda a, c: a + c,
                         [widen(e / den) * branch_out(r) for e, r in zip(es, o_refs)])
    mix = _bdot(o.astype(BF16), wo_ref[...])
    z = DEEPNORM_ALPHA * _rows_load(h_ref, tt) + mix
    _rows_store(out_ref, _layer_norm(z, g_ref[...], b_ref[...]), tt)


def _merge(outs, lses, h_tiles, w_o, g, b):
    n_tok = h_tiles.shape[0] // TOK_ROWS
    tt = MERGE_TT
    o_spec = pl.BlockSpec((PAIR_SLABS, tt, LANES), lambda i: (0, i, 0))
    lse_spec = pl.BlockSpec((tt, LANES), lambda i: (i, 0))
    tiles = pl.BlockSpec((tt * TOK_ROWS, LANES), lambda i: (i, 0))
    full = lambda shape: pl.BlockSpec(shape, lambda i: (0,) * len(shape))
    return pl.pallas_call(
        functools.partial(_merge_body, tt=tt),
        grid=(n_tok // tt,),
        in_specs=[o_spec] * N_GROUPS + [lse_spec] * N_GROUPS + [
            tiles, full((O_WIDTH, D_MODEL)), full((1, D_MODEL)), full((1, D_MODEL))],
        out_specs=tiles,
        out_shape=jax.ShapeDtypeStruct(h_tiles.shape, F32),
        compiler_params=pltpu.CompilerParams(
            dimension_semantics=("arbitrary",), vmem_limit_bytes=VMEM_LIMIT),
        name="attn_merge",
    )(*outs, *lses, h_tiles, w_o.astype(BF16), g.reshape(1, -1), b.reshape(1, -1))


def kernel(x, a_w_in, a_conv_w, a_conv_b, a_w_gate_a, a_b_gate_a, a_w_gate_x, a_b_gate_x, a_lambda, a_w_out, w_kv_shared, b_w_q, b_w_o, moe_w_router, moe_router_bias, moe_w1, moe_w3, moe_w2, moe_ws1, moe_ws3, moe_ws2, ln_g, ln_b):
    bsz, seq, d = x.shape
    assert d == D_MODEL and seq % (DILATION_PATTERNS[-1][1] * ATT_BLOCK) == 0

    def moe(h_tiles, layer, token_tile_out):
        return _moe_layer(h_tiles, moe_w_router[layer], moe_router_bias[layer], moe_w1, moe_w3,
                          moe_w2, layer, moe_ws1[layer], moe_ws3[layer], moe_ws2[layer],
                          ln_g[layer, 1], ln_b[layer, 1], token_tile_out)

    h = _rglru_layer(x, a_w_in[0], a_conv_w[0], a_conv_b[0], a_w_gate_a[0], a_b_gate_a[0],
                     a_w_gate_x[0], a_b_gate_x[0], a_lambda[0], a_w_out[0], ln_g[0, 0], ln_b[0, 0])
    h = moe(h, 0, True)
    kvq = _project(h, jnp.concatenate([w_kv_shared, b_w_q[0]], axis=1),
                   (1.0,) * (2 * N_GROUPS) + (HEAD_DIM ** -0.5,) * N_GROUPS)
    ks, vs, qs = kvq[:N_GROUPS], kvq[N_GROUPS:2 * N_GROUPS], kvq[2 * N_GROUPS:]

    outs, lses = [], []
    for grp, (window, dilation) in enumerate(DILATION_PATTERNS):
        o, lse = _attn_branch(qs[grp], ks[grp], vs[grp], bsz, seq, window, dilation)
        outs.append(o)
        lses.append(lse)
    h = _merge(outs, lses, h, b_w_o[0], ln_g[1, 0], ln_b[1, 0])
    h = moe(h, 1, False)
    return h.reshape(bsz, seq, d)
```

```python
import functools
import math

import jax
import jax.numpy as jnp
from jax import lax
from jax.experimental import pallas as pl
from jax.experimental.pallas import tpu as pltpu

F32 = jnp.float32
BF16 = jnp.bfloat16
I32 = jnp.int32

D_MODEL = 1024
LRU_BLOCKS = 4
CONV_WIDTH = 4
LRU_C = 8.0
ATT_HEADS = 8
HEAD_DIM = 64
DILATION_PATTERNS = ((128, 1), (512, 4), (2048, 16))
N_GROUPS = len(DILATION_PATTERNS)
ATT_BLOCK = 128
O_WIDTH = ATT_HEADS * HEAD_DIM
N_EXPERTS = 256
TOP_K = 8
N_EXPERT_GROUPS = 8
TOPK_GROUPS = 4
D_EXPERT = 256
ROUTED_SCALE = 2.5
DEPTH = 2
DEEPNORM_ALPHA = (2 * DEPTH) ** 0.25
LN_EPS = 1e-5
MASK_VALUE = -1e30

SUBLANES = 8
LANES = 128
TOK_ROWS = D_MODEL // LANES
PK_ROWS = D_MODEL // 2 // LANES
PAIR_SLABS = ATT_HEADS * HEAD_DIM // LANES // 2
LSE_LANES = LANES // ATT_HEADS
VMEM_LIMIT = 56 * 1024 * 1024

RGLRU_TS = 512
ROUTER_TT = 1024
SORTED_ROWS_TL = 2048
DISPATCH_TT = 512
EXPERT_TB = 512
COMBINE_TT = 512
PROJ_TT = 1024
MERGE_TT = 512
ATTN_D1_BLOCKS = 4
ATTN_RESIDUE_UNROLL = 4


def _rows_load(ref, n_tok):
    return jnp.concatenate(
        [ref[pl.ds(s, n_tok, stride=TOK_ROWS), :] for s in range(TOK_ROWS)], axis=1)


def _rows_store(ref, val, n_tok):
    for s in range(TOK_ROWS):
        ref[pl.ds(s, n_tok, stride=TOK_ROWS), :] = val[:, s * LANES:(s + 1) * LANES]


def _layer_norm(z, g, b):
    mu = jnp.mean(z, axis=-1, keepdims=True)
    zc = z - mu
    var = jnp.mean(zc * zc, axis=-1, keepdims=True)
    return zc * lax.rsqrt(var + LN_EPS) * g + b


def _silu(x):
    return x * jax.nn.sigmoid(x)


def _gelu_tanh(x):
    c = math.sqrt(2.0 / math.pi)
    return 0.5 * x * (1.0 + jnp.tanh(c * (x + 0.044715 * (x * x * x))))


def _bdot(a, b):
    return jnp.dot(a, b, preferred_element_type=F32)


def _rglru_body(x_ref, win_ref, cw_ref, cb_ref, wga_ref, bga_ref, wgx_ref, bgx_ref, lam_ref,
                wout_ref, g_ref, b_ref, o_ref, hc_ref, tail_ref, *, ts):
    width = D_MODEL
    bw = width // LRU_BLOCKS

    @pl.when(pl.program_id(1) == 0)
    def _():
        hc_ref[...] = jnp.zeros_like(hc_ref)
        tail_ref[...] = jnp.zeros_like(tail_ref)

    x = x_ref[...]
    xz = _bdot(x.astype(BF16), win_ref[...])
    xr = xz[:, :width]
    gate = xz[:, width:]

    tail = tail_ref[...]
    row8 = lax.broadcasted_iota(jnp.int32, (SUBLANES, width), 0)
    cw = cw_ref[...]
    xc = xr * cw[CONV_WIDTH - 1:CONV_WIDTH, :] + cb_ref[...]
    for j in range(1, CONV_WIDTH):
        rx = pltpu.roll(xr, j, 0)
        rp = pltpu.roll(tail, j, 0)
        top = jnp.where(row8 < j, rp, rx[:SUBLANES])
        shifted = jnp.concatenate([top, rx[SUBLANES:]], axis=0)
        xc = xc + shifted * cw[CONV_WIDTH - 1 - j:CONV_WIDTH - j, :]
    tail_ref[...] = xr[ts - SUBLANES:]

    xcb = xc.astype(BF16)

    def block_diag(w_ref):
        return jnp.concatenate(
            [_bdot(xcb[:, n * bw:(n + 1) * bw], w_ref[n]) for n in range(LRU_BLOCKS)], axis=1)

    r = jax.nn.sigmoid(block_diag(wga_ref) + bga_ref[...])
    i = jax.nn.sigmoid(block_diag(wgx_ref) + bgx_ref[...])
    lam = lam_ref[...]
    softplus_neg_lam = jnp.maximum(-lam, 0.0) + jnp.log1p(jnp.exp(-jnp.abs(lam)))
    log_a = (-LRU_C * r) * softplus_neg_lam
    a = jnp.exp(log_a)
    mult = jnp.sqrt(-jnp.tanh(log_a) * (a * a + 1.0))
    u = mult * (i * xc)

    groups = ts // SUBLANES
    a3 = a.reshape(groups, SUBLANES, width)
    u3 = u.reshape(groups, SUBLANES, width)
    sub = lax.broadcasted_iota(jnp.int32, (groups, SUBLANES, width), 1)
    sh = 1
    while sh < SUBLANES:
        a_prev = pltpu.roll(a3, sh, 1)
        u_prev = pltpu.roll(u3, sh, 1)
        live = sub >= sh
        u3 = jnp.where(live, a3 * u_prev, 0.0) + u3
        a3 = jnp.where(live, a3 * a_prev, a3)
        sh *= 2
    carry = hc_ref[...]
    h_groups = []
    for grp in range(groups):
        h_grp = a3[grp] * carry + u3[grp]
        h_groups.append(h_grp)
        carry = h_grp[SUBLANES - 1:SUBLANES]
    h = jnp.concatenate(h_groups, axis=0)
    hc_ref[...] = carry

    y = (h * _gelu_tanh(gate)).astype(BF16)
    mix = _bdot(y, wout_ref[...])
    z = DEEPNORM_ALPHA * x + mix
    _rows_store(o_ref, _layer_norm(z, g_ref[...], b_ref[...]), ts)


def _rglru_layer(x, w_in, conv_w, conv_b, wga, bga, wgx, bgx, lam, w_out, g, b):
    bsz, seq, d = x.shape
    ts = RGLRU_TS
    ns = seq // ts
    row = lambda v: v.reshape(1, -1)
    full = lambda shape: pl.BlockSpec(shape, lambda bi, si: (0,) * len(shape))
    return pl.pallas_call(
        functools.partial(_rglru_body, ts=ts),
        grid=(bsz, ns),
        in_specs=[
            pl.BlockSpec((None, ts, d), lambda bi, si: (bi, si, 0)),
            full((d, 2 * d)), full((CONV_WIDTH, d)), full((1, d)),
            full((LRU_BLOCKS, d // LRU_BLOCKS, d // LRU_BLOCKS)), full((1, d)),
            full((LRU_BLOCKS, d // LRU_BLOCKS, d // LRU_BLOCKS)), full((1, d)),
            full((1, d)), full((d, d)), full((1, d)), full((1, d)),
        ],
        out_specs=pl.BlockSpec((ts * TOK_ROWS, LANES), lambda bi, si: (bi * ns + si, 0)),
        out_shape=jax.ShapeDtypeStruct((bsz * seq * TOK_ROWS, LANES), F32),
        scratch_shapes=[pltpu.VMEM((1, d), F32), pltpu.VMEM((SUBLANES, d), F32)],
        compiler_params=pltpu.CompilerParams(
            dimension_semantics=("arbitrary", "arbitrary"), vmem_limit_bytes=VMEM_LIMIT),
        name="rglru_layer",
    )(x, w_in.astype(BF16), conv_w, row(conv_b), wga.astype(BF16), row(bga), wgx.astype(BF16),
      row(bgx), row(lam), w_out.astype(BF16), row(g), row(b))


def _router_body(h_ref, wrt_ref, bias_ref, idx_ref, gate_ref, rank_ref, cnt_ref, carry_ref, *, tt):
    n_e = N_EXPERTS
    per_group = n_e // N_EXPERT_GROUPS

    @pl.when(pl.program_id(0) == 0)
    def _():
        carry_ref[...] = jnp.zeros_like(carry_ref)

    h = _rows_load(h_ref, tt)
    logits = lax.dot_general(wrt_ref[...], h.astype(BF16), (((1,), (1,)), ((), ())),
                             preferred_element_type=F32)
    scores = jax.nn.sigmoid(logits)
    biased = scores + bias_ref[...]

    j_iota = lax.broadcasted_iota(jnp.int32, (per_group, tt), 0)
    group_score = []
    for g in range(N_EXPERT_GROUPS):
        bg = biased[g * per_group:(g + 1) * per_group]
        m1 = jnp.max(bg, axis=0, keepdims=True)
        i1 = jnp.min(jnp.where(bg == m1, j_iota, per_group), axis=0, keepdims=True)
        m2 = jnp.max(jnp.where(j_iota == i1, -jnp.inf, bg), axis=0, keepdims=True)
        group_score.append(m1 + m2)

    masked = []
    for g in range(N_EXPERT_GROUPS):
        beaten_by = jnp.zeros((1, tt), jnp.int32)
        for o in range(N_EXPERT_GROUPS):
            if o == g:
                continue
            wins = group_score[o] > group_score[g]
            if o < g:
                wins = wins | (group_score[o] == group_score[g])
            beaten_by = beaten_by + wins.astype(jnp.int32)
        keep = beaten_by < TOPK_GROUPS
        masked.append(jnp.where(keep, biased[g * per_group:(g + 1) * per_group], MASK_VALUE))
    cur = jnp.concatenate(masked, axis=0)

    e_iota = lax.broadcasted_iota(jnp.int32, (n_e, tt), 0)
    idx_rows, score_rows, sels = [], [], []
    for _ in range(TOP_K):
        m = jnp.max(cur, axis=0, keepdims=True)
        ik = jnp.min(jnp.where(cur == m, e_iota, n_e), axis=0, keepdims=True)
        sel = e_iota == ik
        score_rows.append(jnp.sum(jnp.where(sel, scores, 0.0), axis=0, keepdims=True))
        cur = jnp.where(sel, -jnp.inf, cur)
        idx_rows.append(ik)
        sels.append(sel)
    top_s = jnp.concatenate(score_rows, axis=0)
    gate_ref[...] = top_s / jnp.sum(top_s, axis=0, keepdims=True) * ROUTED_SCALE
    idx_ref[...] = jnp.concatenate(idx_rows, axis=0)
    multi_hot = jnp.where(cur == -jnp.inf, 1.0, 0.0)

    t_row = lax.broadcasted_iota(jnp.int32, (tt, tt), 0)
    t_col = lax.broadcasted_iota(jnp.int32, (tt, tt), 1)
    strict_upper = jnp.where(t_row < t_col, 1.0, 0.0).astype(BF16)
    before = _bdot(multi_hot.astype(BF16), strict_upper) + carry_ref[...]
    rank_rows = [jnp.sum(jnp.where(sel, before, 0.0), axis=0, keepdims=True) for sel in sels]
    rank_ref[...] = jnp.concatenate(rank_rows, axis=0).astype(jnp.int32)
    carry = carry_ref[...] + jnp.sum(multi_hot, axis=1, keepdims=True)
    carry_ref[...] = carry
    cnt_ref[...] = carry.astype(jnp.int32)


def _router(h_tiles, w_router, router_bias):
    n_tok = h_tiles.shape[0] // TOK_ROWS
    tt = ROUTER_TT
    kt = lambda dt: jax.ShapeDtypeStruct((TOP_K, n_tok), dt)
    tok_spec = pl.BlockSpec((TOP_K, tt), lambda i: (0, i))
    return pl.pallas_call(
        functools.partial(_router_body, tt=tt),
        grid=(n_tok // tt,),
        in_specs=[
            pl.BlockSpec((tt * TOK_ROWS, LANES), lambda i: (i, 0)),
            pl.BlockSpec((N_EXPERTS, D_MODEL), lambda i: (0, 0)),
            pl.BlockSpec((N_EXPERTS, 1), lambda i: (0, 0)),
        ],
        out_specs=[tok_spec, tok_spec, tok_spec, pl.BlockSpec((N_EXPERTS, 1), lambda i: (0, 0))],
        out_shape=[kt(jnp.int32), kt(F32), kt(jnp.int32),
                   jax.ShapeDtypeStruct((N_EXPERTS, 1), jnp.int32)],
        scratch_shapes=[pltpu.VMEM((N_EXPERTS, 1), F32)],
        compiler_params=pltpu.CompilerParams(
            dimension_semantics=("arbitrary",), vmem_limit_bytes=VMEM_LIMIT),
        name="moe_router",
    )(h_tiles, w_router.T.astype(BF16), router_bias.reshape(N_EXPERTS, 1))


def _pack_rows(x):
    half = D_MODEL // 2
    return pltpu.pack_elementwise([x[:, :half], x[:, half:]], packed_dtype=BF16)


def _unpack_rows(p):
    return jnp.concatenate(
        [pltpu.unpack_elementwise(p, index=i, packed_dtype=BF16, unpacked_dtype=F32) for i in range(2)],
        axis=1)


def _packed_load(ref, n_tok):
    return jnp.concatenate(
        [ref[pl.ds(s, n_tok, stride=PK_ROWS), :] for s in range(PK_ROWS)], axis=1)


def _packed_store(ref, val, n_tok):
    for s in range(PK_ROWS):
        ref[pl.ds(s, n_tok, stride=PK_ROWS), :] = val[:, s * LANES:(s + 1) * LANES]


def _sorted_rows_body(idx_ref, rank_ref, start_ref, o_ref):
    tl = idx_ref.shape[1]
    e_iota = lax.broadcasted_iota(jnp.int32, (N_EXPERTS, tl), 0)
    start = start_ref[...]
    rows = []
    for k in range(TOP_K):
        hit = e_iota == idx_ref[k:k + 1, :]
        base = jnp.sum(jnp.where(hit, start, 0.0), axis=0, keepdims=True)
        rows.append(base.astype(jnp.int32) + rank_ref[k:k + 1, :])
    o_ref[...] = jnp.concatenate(rows, axis=0)


def _sorted_rows(top_idx, rank, pad_start):
    n_tok = top_idx.shape[1]
    tl = SORTED_ROWS_TL
    tok = pl.BlockSpec((TOP_K, tl), lambda i: (0, i))
    return pl.pallas_call(
        _sorted_rows_body,
        grid=(n_tok // tl,),
        in_specs=[tok, tok, pl.BlockSpec((N_EXPERTS, 1), lambda i: (0, 0))],
        out_specs=tok,
        out_shape=jax.ShapeDtypeStruct((TOP_K, n_tok), jnp.int32),
        compiler_params=pltpu.CompilerParams(
            dimension_semantics=("arbitrary",), vmem_limit_bytes=VMEM_LIMIT),
        name="moe_sorted_rows",
    )(top_idx, rank, pad_start.astype(F32).reshape(N_EXPERTS, 1))


def _dispatch_body(start_ref, cnt_ref, dest_ref, h_ref, xs_ref, pk_ref, zero_ref, sem, zero_sem,
                   *, tt, tb):
    @pl.when(pl.program_id(0) == 0)
    def _():
        zero_ref[...] = jnp.zeros_like(zero_ref)

        def pad_copies(e, fn):
            n_pad = ((cnt_ref[e] + (tb - 1)) & (-tb)) - cnt_ref[e]
            row = start_ref[e] + cnt_ref[e]
            for bit in range(tb.bit_length() - 1):
                size = 1 << bit
                has_bit = ((n_pad >> bit) & 1) == 1

                @pl.when(has_bit)
                def _():
                    fn(pltpu.make_async_copy(zero_ref.at[pl.ds(0, size)],
                                             xs_ref.at[pl.ds(row, size)], zero_sem))
                row = row + jnp.where(has_bit, size, 0)

        def start_all(e, c):
            pad_copies(e, lambda cp: cp.start())
            return c

        def wait_all(e, c):
            pad_copies(e, lambda cp: cp.wait())
            return c

        lax.fori_loop(0, N_EXPERTS, start_all, 0)
        lax.fori_loop(0, N_EXPERTS, wait_all, 0)

    packed = _pack_rows(_rows_load(h_ref, tt))
    for s in range(PK_ROWS):
        pk_ref[:, s, :] = packed[:, s * LANES:(s + 1) * LANES]
    for t in range(tt):
        for k in range(TOP_K):
            pltpu.make_async_copy(pk_ref.at[t], xs_ref.at[dest_ref[k, t]], sem).start(priority=k % 2)
    for k in range(TOP_K):
        pltpu.make_async_copy(pk_ref, xs_ref.at[pl.ds(0, tt)], sem).wait()


def _dispatch(h_tiles, dest_rows, pad_start, counts, n_rows):
    n_tok = h_tiles.shape[0] // TOK_ROWS
    tt = DISPATCH_TT
    grid_spec = pltpu.PrefetchScalarGridSpec(
        num_scalar_prefetch=2,
        grid=(n_tok // tt,),
        in_specs=[pl.BlockSpec((TOP_K, tt), lambda i, s, c: (0, i), memory_space=pltpu.SMEM),
                  pl.BlockSpec((tt * TOK_ROWS, LANES), lambda i, s, c: (i, 0))],
        out_specs=pl.BlockSpec(memory_space=pl.ANY),
        scratch_shapes=[pltpu.VMEM((tt, PK_ROWS, LANES), I32),
                        pltpu.VMEM((EXPERT_TB // 2, PK_ROWS, LANES), I32),
                        pltpu.SemaphoreType.DMA(()), pltpu.SemaphoreType.DMA(())],
    )
    return pl.pallas_call(
        functools.partial(_dispatch_body, tt=tt, tb=EXPERT_TB),
        grid_spec=grid_spec,
        out_shape=jax.ShapeDtypeStruct((n_rows, PK_ROWS, LANES), I32),
        compiler_params=pltpu.CompilerParams(
            dimension_semantics=("arbitrary",), vmem_limit_bytes=VMEM_LIMIT),
        name="moe_dispatch",
    )(pad_start, counts, dest_rows, h_tiles)


def _expert_body(be_ref, first_ref, slot_ref, next_ref, na_ref, xs_ref, w1_ref, w3_ref, w2_ref, ys_ref,
                 w13_buf, w2_buf, w1b, w3b, w2b, sem, *, tb, layer):
    i = pl.program_id(0)

    def weight_copies(expert, slot):
        return (pltpu.make_async_copy(w1_ref.at[layer, expert], w13_buf.at[slot, 0], sem.at[slot]),
                pltpu.make_async_copy(w3_ref.at[layer, expert], w13_buf.at[slot, 1], sem.at[slot]),
                pltpu.make_async_copy(w2_ref.at[layer, expert], w2_buf.at[slot], sem.at[slot]))

    @pl.when(i < na_ref[0])
    def _():
        @pl.when(first_ref[i] == 1)
        def _():
            slot = slot_ref[i]

            @pl.when(i == 0)
            def _():
                for cp in weight_copies(be_ref[0], slot):
                    cp.start()

            for cp in weight_copies(be_ref[i], slot):
                cp.wait()

            @pl.when(next_ref[i] >= 0)
            def _():
                for cp in weight_copies(next_ref[i], 1 - slot):
                    cp.start()

            w1b[...] = w13_buf[slot, 0].astype(BF16)
            w3b[...] = w13_buf[slot, 1].astype(BF16)
            w2b[...] = w2_buf[slot].astype(BF16)

        x = _unpack_rows(_packed_load(xs_ref, tb)).astype(BF16)
        mid = (_silu(_bdot(x, w1b[...])) * _bdot(x, w3b[...])).astype(BF16)
        _packed_store(ys_ref, _pack_rows(_bdot(mid, w2b[...])), tb)


def _experts(xs, block_expert, first, slot, next_expert, n_active, w1, w3, w2, layer):
    tb = EXPERT_TB
    n_blocks = xs.shape[0] // (tb * PK_ROWS)
    row_block = lambda i, be, fi, sl, nx, na: (jnp.minimum(i, na[0] - 1), 0)
    grid_spec = pltpu.PrefetchScalarGridSpec(
        num_scalar_prefetch=5,
        grid=(n_blocks,),
        in_specs=[
            pl.BlockSpec((tb * PK_ROWS, LANES), row_block),
            pl.BlockSpec(memory_space=pl.ANY), pl.BlockSpec(memory_space=pl.ANY),
            pl.BlockSpec(memory_space=pl.ANY),
        ],
        out_specs=pl.BlockSpec((tb * PK_ROWS, LANES), row_block),
        scratch_shapes=[pltpu.VMEM((2, 2, D_MODEL, D_EXPERT), F32),
                        pltpu.VMEM((2, D_EXPERT, D_MODEL), F32),
                        pltpu.VMEM((D_MODEL, D_EXPERT), BF16), pltpu.VMEM((D_MODEL, D_EXPERT), BF16),
                        pltpu.VMEM((D_EXPERT, D_MODEL), BF16), pltpu.SemaphoreType.DMA((2,))],
    )
    return pl.pallas_call(
        functools.partial(_expert_body, tb=tb, layer=layer),
        grid_spec=grid_spec,
        out_shape=jax.ShapeDtypeStruct(xs.shape, I32),
        compiler_params=pltpu.CompilerParams(
            dimension_semantics=("arbitrary",), vmem_limit_bytes=VMEM_LIMIT),
        name="moe_experts",
    )(block_expert, first, slot, next_expert, n_active, xs, w1, w3, w2)


def _store_head_slabs(y, o_refs, scales):
    for j, (o_ref, scale) in enumerate(zip(o_refs, scales)):
        for s in range(PAIR_SLABS):
            lo = j * O_WIDTH + 2 * s * LANES
            o_ref[s] = pltpu.pack_elementwise(
                [y[:, lo:lo + LANES] * scale, y[:, lo + LANES:lo + 2 * LANES] * scale],
                packed_dtype=BF16)


def _combine_body(src_ref, h_ref, gates_ref, ys_ref, ys_flat_ref, ws1_ref, ws3_ref, ws2_ref,
                  g_ref, b_ref, o_ref, gbuf, sem, *, tt, n_tiles, token_tile_out):
    j = pl.program_id(0)

    def step(issue, finish):
        nxt = j % 2
        cur = (j - 1) % 2
        if finish:
            h = _rows_load(h_ref, tt)
            hb = h.astype(BF16)
            mid = (_silu(_bdot(hb, ws1_ref[...])) * _bdot(hb, ws3_ref[...])).astype(BF16)
            routed = _bdot(mid, ws2_ref[...])
            gates = gates_ref[...]
        for k in range(TOP_K):
            if issue:
                for t in range(tt):
                    pltpu.make_async_copy(ys_ref.at[src_ref[k, t]],
                                          gbuf.at[nxt, k, pl.ds(t * PK_ROWS, PK_ROWS), :],
                                          sem.at[nxt]).start(priority=t % 2)
            if finish:
                pltpu.make_async_copy(ys_flat_ref.at[pl.ds(0, tt * PK_ROWS), :], gbuf.at[cur, k],
                                      sem.at[cur]).wait()
                routed = routed + gates[:, k:k + 1] * _unpack_rows(_packed_load(gbuf.at[cur, k], tt))
        if finish:
            out = _layer_norm(DEEPNORM_ALPHA * h + routed, g_ref[...], b_ref[...])
            if token_tile_out:
                _rows_store(o_ref, out, tt)
            else:
                o_ref[...] = out

    pl.when(j == 0)(lambda: step(True, False))
    pl.when((j >= 1) & (j < n_tiles))(lambda: step(True, True))
    pl.when(j == n_tiles)(lambda: step(False, True))


def _combine(h_tiles, src_rows, gates_tk, ys, ws1, ws3, ws2, g, b, token_tile_out):
    n_tok = h_tiles.shape[0] // TOK_ROWS
    tt = COMBINE_TT
    n_tiles = n_tok // tt
    prev_tile = lambda j: (jnp.maximum(j - 1, 0), 0)
    full = lambda shape: pl.BlockSpec(shape, lambda j: (0,) * len(shape))
    if token_tile_out:
        out_spec = pl.BlockSpec((tt * TOK_ROWS, LANES), prev_tile)
        out_shape = jax.ShapeDtypeStruct((n_tok * TOK_ROWS, LANES), F32)
    else:
        out_spec = pl.BlockSpec((tt, D_MODEL), prev_tile)
        out_shape = jax.ShapeDtypeStruct((n_tok, D_MODEL), F32)
    n_rows = ys.shape[0] // PK_ROWS
    return pl.pallas_call(
        functools.partial(_combine_body, tt=tt, n_tiles=n_tiles, token_tile_out=token_tile_out),
        grid=(n_tiles + 1,),
        in_specs=[pl.BlockSpec((TOP_K, tt), lambda j: (0, jnp.minimum(j, n_tiles - 1)),
                               memory_space=pltpu.SMEM),
                  pl.BlockSpec((tt * TOK_ROWS, LANES), prev_tile),
                  pl.BlockSpec((tt, TOP_K), prev_tile),
                  pl.BlockSpec(memory_space=pl.ANY), pl.BlockSpec(memory_space=pl.ANY),
                  full((D_MODEL, D_EXPERT)), full((D_MODEL, D_EXPERT)), full((D_EXPERT, D_MODEL)),
                  full((1, D_MODEL)), full((1, D_MODEL))],
        out_specs=out_spec,
        out_shape=out_shape,
        scratch_shapes=[pltpu.VMEM((2, TOP_K, tt * PK_ROWS, LANES), I32),
                        pltpu.SemaphoreType.DMA((2,))],
        compiler_params=pltpu.CompilerParams(
            dimension_semantics=("arbitrary",), vmem_limit_bytes=VMEM_LIMIT),
        name="moe_combine",
    )(src_rows, h_tiles, gates_tk, ys.reshape(n_rows, PK_ROWS, LANES), ys, ws1.astype(BF16),
      ws3.astype(BF16), ws2.astype(BF16), g.reshape(1, -1), b.reshape(1, -1))


def _moe_layer(h_tiles, w_router, router_bias, w1, w3, w2, layer, ws1, ws3, ws2, g, b,
               token_tile_out):
    n_tok = h_tiles.shape[0] // TOK_ROWS
    tb = EXPERT_TB
    top_idx, gates, rank, counts = _router(h_tiles, w_router, router_bias)

    n_blocks = -(-(n_tok * TOP_K + N_EXPERTS * (tb - 1)) // tb)
    counts = counts.reshape(N_EXPERTS)
    padded = ((counts + tb - 1) // tb) * tb
    pad_end = jnp.cumsum(padded)
    pad_start = (pad_end - padded).astype(jnp.int32)
    block_first_row = jnp.arange(n_blocks, dtype=jnp.int32) * tb
    block_expert = jnp.minimum(
        jnp.sum((pad_end[None, :] <= block_first_row[:, None]).astype(jnp.int32), axis=1),
        N_EXPERTS - 1)
    n_active = (pad_end[-1:] // tb).astype(jnp.int32)
    present = counts > 0
    expert_ids = jnp.arange(N_EXPERTS, dtype=jnp.int32)
    later = lax.cummin(jnp.where(present, expert_ids, N_EXPERTS), reverse=True)
    next_present = jnp.concatenate([later[1:], jnp.full((1,), N_EXPERTS, jnp.int32)])
    next_present = jnp.where(next_present < N_EXPERTS, next_present, -1)
    ordinal = jnp.cumsum(present.astype(jnp.int32)) - 1
    first = jnp.concatenate([jnp.ones((1,), jnp.int32),
                             (block_expert[1:] != block_expert[:-1]).astype(jnp.int32)])
    slot = ordinal[block_expert] % 2
    next_expert = next_present[block_expert]

    rows = _sorted_rows(top_idx, rank, pad_start)
    xs = _dispatch(h_tiles, rows, pad_start, counts, n_blocks * tb)
    ys = _experts(xs.reshape(n_blocks * tb * PK_ROWS, LANES), block_expert, first, slot, next_expert,
                  n_active, w1, w3, w2, layer)
    return _combine(h_tiles, rows, gates.T, ys, ws1, ws3, ws2, g, b, token_tile_out)


def _proj_body(h_ref, w_ref, *o_refs, tt, scales):
    y = _bdot(_rows_load(h_ref, tt).astype(BF16), w_ref[...])
    _store_head_slabs(y, o_refs, scales)


def _project(h_tiles, w, scales):
    n_tok = h_tiles.shape[0] // TOK_ROWS
    tt = PROJ_TT
    n_out = len(scales)
    out_spec = pl.BlockSpec((PAIR_SLABS, tt, LANES), lambda i: (0, i, 0))
    return pl.pallas_call(
        functools.partial(_proj_body, tt=tt, scales=tuple(scales)),
        grid=(n_tok // tt,),
        in_specs=[pl.BlockSpec((tt * TOK_ROWS, LANES), lambda i: (i, 0)),
                  pl.BlockSpec(w.shape, lambda i: (0, 0))],
        out_specs=[out_spec] * n_out,
        out_shape=[jax.ShapeDtypeStruct((PAIR_SLABS, n_tok, LANES), I32)] * n_out,
        compiler_params=pltpu.CompilerParams(
            dimension_semantics=("arbitrary",), vmem_limit_bytes=VMEM_LIMIT),
        name="projection",
    )(h_tiles, w.astype(BF16))


def _attn_body(q_ref, kp_ref, kc_ref, vp_ref, vc_ref, o_ref, lse_ref, *, dilation, n_steps, blocks):
    t = ATT_BLOCK
    qi = lax.broadcasted_iota(jnp.int32, (t, 2 * t), 0)
    kj = lax.broadcasted_iota(jnp.int32, (t, 2 * t), 1)
    dist = t + qi - kj
    band = (dist >= 0) & (dist <= n_steps)
    band_at_start = band & ((pl.program_id(1) > 0) | (kj >= t))
    token_dist = (dilation * dist).astype(F32)
    lane = lax.broadcasted_iota(jnp.int32, (t, LANES), 1)
    low_half = lane < HEAD_DIM
    lane_head = lane // LSE_LANES
    contract_last = (((1,), (1,)), ((), ()))

    def unpack(words, which):
        return pltpu.unpack_elementwise(words, index=which, packed_dtype=BF16,
                                        unpacked_dtype=F32).astype(BF16)

    def block(rows, k_prev, v_prev, prev_rows, valid):
        lse_all = jnp.zeros((t, LANES), F32)
        for slab in range(PAIR_SLABS):
            q_words = q_ref.at[slab][rows, :]
            k_words = jnp.concatenate([k_prev.at[slab][prev_rows, :], kc_ref.at[slab][rows, :]], axis=0)
            v_words = jnp.concatenate([v_prev.at[slab][prev_rows, :], vc_ref.at[slab][rows, :]], axis=0)
            pair_outs = []
            for which in range(2):
                pair = 2 * slab + which
                q2, k2, v2 = unpack(q_words, which), unpack(k_words, which), unpack(v_words, which)
                outs = []
                for half in range(2):
                    head = 2 * pair + half
                    slope = 2.0 ** (-8.0 * (head + 1) / ATT_HEADS)
                    keep = low_half if half == 0 else ~low_half
                    qh = jnp.where(keep, q2, jnp.zeros_like(q2))
                    s = lax.dot_general(qh, k2, contract_last, preferred_element_type=F32)
                    s = jnp.where(valid, s - slope * token_dist, MASK_VALUE)
                    m = jnp.max(s, axis=-1, keepdims=True)
                    p = jnp.exp(s - m)
                    l = jnp.sum(p, axis=-1, keepdims=True)
                    outs.append(_bdot(p.astype(BF16), v2) / l)
                    lse_all = jnp.where(lane_head == head, m + jnp.log(l), lse_all)
                pair_outs.append(jnp.where(low_half, outs[0], outs[1]))
            o_ref.at[slab][rows, :] = pltpu.pack_elementwise(pair_outs, packed_dtype=BF16)
        lse_ref[rows, :] = lse_all

    if dilation == 1:
        block(pl.ds(0, t), kp_ref, vp_ref, pl.ds(0, t), band_at_start)
        for b in range(1, blocks):
            block(pl.ds(b * t, t), kc_ref, vc_ref, pl.ds((b - 1) * t, t), band)
    else:
        def residue(r, carry):
            rows = pl.ds(r, t, stride=dilation)
            block(rows, kp_ref, vp_ref, rows, band_at_start)
            return carry
        lax.fori_loop(0, dilation, residue, 0, unroll=ATTN_RESIDUE_UNROLL)


def _attn_branch(q, k, v, bsz, seq, window, dilation):
    blocks = ATTN_D1_BLOCKS if dilation == 1 else 1
    span = ATT_BLOCK * dilation * blocks
    n_spans = seq // span
    prev_rows = ATT_BLOCK if dilation == 1 else span
    prev_per_span = span // prev_rows
    cur = pl.BlockSpec((PAIR_SLABS, span, LANES), lambda bi, n: (0, bi * n_spans + n, 0))
    prev = pl.BlockSpec(
        (PAIR_SLABS, prev_rows, LANES),
        lambda bi, n: (0, (bi * n_spans + n) * prev_per_span - jnp.minimum(n, 1), 0))
    return pl.pallas_call(
        functools.partial(_attn_body, dilation=dilation, n_steps=window // dilation, blocks=blocks),
        grid=(bsz, n_spans),
        in_specs=[cur, prev, cur, prev, cur],
        out_specs=[cur, pl.BlockSpec((span, LANES), lambda bi, n: (bi * n_spans + n, 0))],
        out_shape=[jax.ShapeDtypeStruct((PAIR_SLABS, bsz * seq, LANES), I32),
                   jax.ShapeDtypeStruct((bsz * seq, LANES), F32)],
        compiler_params=pltpu.CompilerParams(
            dimension_semantics=("arbitrary",) * 2, vmem_limit_bytes=VMEM_LIMIT),
        name=f"dilated_attn_d{dilation}",
    )(q, k, k, v, v)


def _merge_body(*refs, tt):
    o_refs = refs[:N_GROUPS]
    lse_refs = refs[N_GROUPS:2 * N_GROUPS]
    h_ref, wo_ref, g_ref, b_ref, out_ref = refs[2 * N_GROUPS:]
    lses = [r[...] for r in lse_refs]
    m = functools.reduce(jnp.maximum, lses)
    es = [jnp.exp(l - m) for l in lses]
    den = functools.reduce(lambda a, c: a + c, es)
    row = lax.broadcasted_iota(jnp.int32, (LANES, O_WIDTH), 0)
    col = lax.broadcasted_iota(jnp.int32, (LANES, O_WIDTH), 1)
    select = jnp.where(row == (col // HEAD_DIM) * LSE_LANES, 1.0, 0.0).astype(BF16)

    def widen(w):
        hi = w.astype(BF16)
        lo = (w - hi.astype(F32)).astype(BF16)
        return _bdot(hi, select) + _bdot(lo, select)

    def branch_out(o_ref):
        return jnp.concatenate(
            [pltpu.unpack_elementwise(o_ref[pair // 2], index=pair % 2, packed_dtype=BF16,
                                      unpacked_dtype=F32) for pair in range(ATT_HEADS // 2)], axis=1)

    o = functools.reduce(lambda a, c: a + c,
                         [widen(e / den) * branch_out(r) for e, r in zip(es, o_refs)])
    mix = _bdot(o.astype(BF16), wo_ref[...])
    z = DEEPNORM_ALPHA * _rows_load(h_ref, tt) + mix
    _rows_store(out_ref, _layer_norm(z, g_ref[...], b_ref[...]), tt)


def _merge(outs, lses, h_tiles, w_o, g, b):
    n_tok = h_tiles.shape[0] // TOK_ROWS
    tt = MERGE_TT
    o_spec = pl.BlockSpec((PAIR_SLABS, tt, LANES), lambda i: (0, i, 0))
    lse_spec = pl.BlockSpec((tt, LANES), lambda i: (i, 0))
    tiles = pl.BlockSpec((tt * TOK_ROWS, LANES), lambda i: (i, 0))
    full = lambda shape: pl.BlockSpec(shape, lambda i: (0,) * len(shape))
    return pl.pallas_call(
        functools.partial(_merge_body, tt=tt),
        grid=(n_tok // tt,),
        in_specs=[o_spec] * N_GROUPS + [lse_spec] * N_GROUPS + [
            tiles, full((O_WIDTH, D_MODEL)), full((1, D_MODEL)), full((1, D_MODEL))],
        out_specs=tiles,
        out_shape=jax.ShapeDtypeStruct(h_tiles.shape, F32),
        compiler_params=pltpu.CompilerParams(
            dimension_semantics=("arbitrary",), vmem_limit_bytes=VMEM_LIMIT),
        name="attn_merge",
    )(*outs, *lses, h_tiles, w_o.astype(BF16), g.reshape(1, -1), b.reshape(1, -1))


def kernel(x, a_w_in, a_conv_w, a_conv_b, a_w_gate_a, a_b_gate_a, a_w_gate_x, a_b_gate_x, a_lambda, a_w_out, w_kv_shared, b_w_q, b_w_o, moe_w_router, moe_router_bias, moe_w1, moe_w3, moe_w2, moe_ws1, moe_ws3, moe_ws2, ln_g, ln_b):
    bsz, seq, d = x.shape
    assert d == D_MODEL and seq % (DILATION_PATTERNS[-1][1] * ATT_BLOCK) == 0

    def moe(h_tiles, layer, token_tile_out):
        return _moe_layer(h_tiles, moe_w_router[layer], moe_router_bias[layer], moe_w1, moe_w3,
                          moe_w2, layer, moe_ws1[layer], moe_ws3[layer], moe_ws2[layer],
                          ln_g[layer, 1], ln_b[layer, 1], token_tile_out)

    h = _rglru_layer(x, a_w_in[0], a_conv_w[0], a_conv_b[0], a_w_gate_a[0], a_b_gate_a[0],
                     a_w_gate_x[0], a_b_gate_x[0], a_lambda[0], a_w_out[0], ln_g[0, 0], ln_b[0, 0])
    h = moe(h, 0, True)
    kvq = _project(h, jnp.concatenate([w_kv_shared, b_w_q[0]], axis=1),
                   (1.0,) * (2 * N_GROUPS) + (HEAD_DIM ** -0.5,) * N_GROUPS)
    ks, vs, qs = kvq[:N_GROUPS], kvq[N_GROUPS:2 * N_GROUPS], kvq[2 * N_GROUPS:]

    outs, lses = [], []
    for grp, (window, dilation) in enumerate(DILATION_PATTERNS):
        o, lse = _attn_branch(qs[grp], ks[grp], vs[grp], bsz, seq, window, dilation)
        outs.append(o)
        lses.append(lse)
    h = _merge(outs, lses, h, b_w_o[0], ln_g[1, 0], ln_b[1, 0])
    h = moe(h, 1, False)
    return h.reshape(bsz, seq, d)
```

```python
import functools
import math

import jax
import jax.numpy as jnp
from jax import lax
from jax.experimental import pallas as pl
from jax.experimental.pallas import tpu as pltpu

F32 = jnp.float32
BF16 = jnp.bfloat16
I32 = jnp.int32

D_MODEL = 1024
LRU_BLOCKS = 4
CONV_WIDTH = 4
LRU_C = 8.0
ATT_HEADS = 8
HEAD_DIM = 64
DILATION_PATTERNS = ((128, 1), (512, 4), (2048, 16))
N_GROUPS = len(DILATION_PATTERNS)
ATT_BLOCK = 128
O_WIDTH = ATT_HEADS * HEAD_DIM
N_EXPERTS = 256
TOP_K = 8
N_EXPERT_GROUPS = 8
TOPK_GROUPS = 4
D_EXPERT = 256
ROUTED_SCALE = 2.5
DEPTH = 2
DEEPNORM_ALPHA = (2 * DEPTH) ** 0.25
LN_EPS = 1e-5
MASK_VALUE = -1e30

SUBLANES = 8
LANES = 128
TOK_ROWS = D_MODEL // LANES
PK_ROWS = D_MODEL // 2 // LANES
PAIR_SLABS = ATT_HEADS * HEAD_DIM // LANES // 2
LSE_LANES = LANES // ATT_HEADS
VMEM_LIMIT = 56 * 1024 * 1024

RGLRU_TS = 512
ROUTER_TT = 1024
SORTED_ROWS_TL = 2048
DISPATCH_TT = 512
EXPERT_TB = 512
COMBINE_TT = 512
PROJ_TT = 1024
MERGE_TT = 512
ATTN_D1_BLOCKS = 8
ATTN_RESIDUE_UNROLL = 8


def _rows_load(ref, n_tok):
    return jnp.concatenate(
        [ref[pl.ds(s, n_tok, stride=TOK_ROWS), :] for s in range(TOK_ROWS)], axis=1)


def _rows_store(ref, val, n_tok):
    for s in range(TOK_ROWS):
        ref[pl.ds(s, n_tok, stride=TOK_ROWS), :] = val[:, s * LANES:(s + 1) * LANES]


def _layer_norm(z, g, b):
    mu = jnp.mean(z, axis=-1, keepdims=True)
    zc = z - mu
    var = jnp.mean(zc * zc, axis=-1, keepdims=True)
    return zc * lax.rsqrt(var + LN_EPS) * g + b


def _silu(x):
    return x * jax.nn.sigmoid(x)


def _gelu_tanh(x):
    c = math.sqrt(2.0 / math.pi)
    return 0.5 * x * (1.0 + jnp.tanh(c * (x + 0.044715 * (x * x * x))))


def _bdot(a, b):
    return jnp.dot(a, b, preferred_element_type=F32)


def _rglru_body(x_ref, win_ref, cw_ref, cb_ref, wga_ref, bga_ref, wgx_ref, bgx_ref, lam_ref,
                wout_ref, g_ref, b_ref, o_ref, hc_ref, tail_ref, *, ts):
    width = D_MODEL
    bw = width // LRU_BLOCKS

    @pl.when(pl.program_id(1) == 0)
    def _():
        hc_ref[...] = jnp.zeros_like(hc_ref)
        tail_ref[...] = jnp.zeros_like(tail_ref)

    x = x_ref[...]
    xz = _bdot(x.astype(BF16), win_ref[...])
    xr = xz[:, :width]
    gate = xz[:, width:]

    tail = tail_ref[...]
    row8 = lax.broadcasted_iota(jnp.int32, (SUBLANES, width), 0)
    cw = cw_ref[...]
    xc = xr * cw[CONV_WIDTH - 1:CONV_WIDTH, :] + cb_ref[...]
    for j in range(1, CONV_WIDTH):
        rx = pltpu.roll(xr, j, 0)
        rp = pltpu.roll(tail, j, 0)
        top = jnp.where(row8 < j, rp, rx[:SUBLANES])
        shifted = jnp.concatenate([top, rx[SUBLANES:]], axis=0)
        xc = xc + shifted * cw[CONV_WIDTH - 1 - j:CONV_WIDTH - j, :]
    tail_ref[...] = xr[ts - SUBLANES:]

    xcb = xc.astype(BF16)

    def block_diag(w_ref):
        return jnp.concatenate(
            [_bdot(xcb[:, n * bw:(n + 1) * bw], w_ref[n]) for n in range(LRU_BLOCKS)], axis=1)

    r = jax.nn.sigmoid(block_diag(wga_ref) + bga_ref[...])
    i = jax.nn.sigmoid(block_diag(wgx_ref) + bgx_ref[...])
    lam = lam_ref[...]
    softplus_neg_lam = jnp.maximum(-lam, 0.0) + jnp.log1p(jnp.exp(-jnp.abs(lam)))
    log_a = (-LRU_C * r) * softplus_neg_lam
    a = jnp.exp(log_a)
    mult = jnp.sqrt(-jnp.tanh(log_a) * (a * a + 1.0))
    u = mult * (i * xc)

    groups = ts // SUBLANES
    a3 = a.reshape(groups, SUBLANES, width)
    u3 = u.reshape(groups, SUBLANES, width)
    sub = lax.broadcasted_iota(jnp.int32, (groups, SUBLANES, width), 1)
    sh = 1
    while sh < SUBLANES:
        a_prev = pltpu.roll(a3, sh, 1)
        u_prev = pltpu.roll(u3, sh, 1)
        live = sub >= sh
        u3 = jnp.where(live, a3 * u_prev, 0.0) + u3
        a3 = jnp.where(live, a3 * a_prev, a3)
        sh *= 2
    carry = hc_ref[...]
    h_groups = []
    for grp in range(groups):
        h_grp = a3[grp] * carry + u3[grp]
        h_groups.append(h_grp)
        carry = h_grp[SUBLANES - 1:SUBLANES]
    h = jnp.concatenate(h_groups, axis=0)
    hc_ref[...] = carry

    y = (h * _gelu_tanh(gate)).astype(BF16)
    mix = _bdot(y, wout_ref[...])
    z = DEEPNORM_ALPHA * x + mix
    _rows_store(o_ref, _layer_norm(z, g_ref[...], b_ref[...]), ts)


def _rglru_layer(x, w_in, conv_w, conv_b, wga, bga, wgx, bgx, lam, w_out, g, b):
    bsz, seq, d = x.shape
    ts = RGLRU_TS
    ns = seq // ts
    row = lambda v: v.reshape(1, -1)
    full = lambda shape: pl.BlockSpec(shape, lambda bi, si: (0,) * len(shape))
    return pl.pallas_call(
        functools.partial(_rglru_body, ts=ts),
        grid=(bsz, ns),
        in_specs=[
            pl.BlockSpec((None, ts, d), lambda bi, si: (bi, si, 0)),
            full((d, 2 * d)), full((CONV_WIDTH, d)), full((1, d)),
            full((LRU_BLOCKS, d // LRU_BLOCKS, d // LRU_BLOCKS)), full((1, d)),
            full((LRU_BLOCKS, d // LRU_BLOCKS, d // LRU_BLOCKS)), full((1, d)),
            full((1, d)), full((d, d)), full((1, d)), full((1, d)),
        ],
        out_specs=pl.BlockSpec((ts * TOK_ROWS, LANES), lambda bi, si: (bi * ns + si, 0)),
        out_shape=jax.ShapeDtypeStruct((bsz * seq * TOK_ROWS, LANES), F32),
        scratch_shapes=[pltpu.VMEM((1, d), F32), pltpu.VMEM((SUBLANES, d), F32)],
        compiler_params=pltpu.CompilerParams(
            dimension_semantics=("arbitrary", "arbitrary"), vmem_limit_bytes=VMEM_LIMIT),
        name="rglru_layer",
    )(x, w_in.astype(BF16), conv_w, row(conv_b), wga.astype(BF16), row(bga), wgx.astype(BF16),
      row(bgx), row(lam), w_out.astype(BF16), row(g), row(b))


def _router_body(h_ref, wrt_ref, bias_ref, idx_ref, gate_ref, rank_ref, cnt_ref, carry_ref, *, tt):
    n_e = N_EXPERTS
    per_group = n_e // N_EXPERT_GROUPS

    @pl.when(pl.program_id(0) == 0)
    def _():
        carry_ref[...] = jnp.zeros_like(carry_ref)

    h = _rows_load(h_ref, tt)
    logits = lax.dot_general(wrt_ref[...], h.astype(BF16), (((1,), (1,)), ((), ())),
                             preferred_element_type=F32)
    scores = jax.nn.sigmoid(logits)
    biased = scores + bias_ref[...]

    j_iota = lax.broadcasted_iota(jnp.int32, (per_group, tt), 0)
    group_score = []
    for g in range(N_EXPERT_GROUPS):
        bg = biased[g * per_group:(g + 1) * per_group]
        m1 = jnp.max(bg, axis=0, keepdims=True)
        i1 = jnp.min(jnp.where(bg == m1, j_iota, per_group), axis=0, keepdims=True)
        m2 = jnp.max(jnp.where(j_iota == i1, -jnp.inf, bg), axis=0, keepdims=True)
        group_score.append(m1 + m2)

    masked = []
    for g in range(N_EXPERT_GROUPS):
        beaten_by = jnp.zeros((1, tt), jnp.int32)
        for o in range(N_EXPERT_GROUPS):
            if o == g:
                continue
            wins = group_score[o] > group_score[g]
            if o < g:
                wins = wins | (group_score[o] == group_score[g])
            beaten_by = beaten_by + wins.astype(jnp.int32)
        keep = beaten_by < TOPK_GROUPS
        masked.append(jnp.where(keep, biased[g * per_group:(g + 1) * per_group], MASK_VALUE))
    cur = jnp.concatenate(masked, axis=0)

    e_iota = lax.broadcasted_iota(jnp.int32, (n_e, tt), 0)
    idx_rows, score_rows, sels = [], [], []
    for _ in range(TOP_K):
        m = jnp.max(cur, axis=0, keepdims=True)
        ik = jnp.min(jnp.where(cur == m, e_iota, n_e), axis=0, keepdims=True)
        sel = e_iota == ik
        score_rows.append(jnp.sum(jnp.where(sel, scores, 0.0), axis=0, keepdims=True))
        cur = jnp.where(sel, -jnp.inf, cur)
        idx_rows.append(ik)
        sels.append(sel)
    top_s = jnp.concatenate(score_rows, axis=0)
    gate_ref[...] = top_s / jnp.sum(top_s, axis=0, keepdims=True) * ROUTED_SCALE
    idx_ref[...] = jnp.concatenate(idx_rows, axis=0)
    multi_hot = jnp.where(cur == -jnp.inf, 1.0, 0.0)

    t_row = lax.broadcasted_iota(jnp.int32, (tt, tt), 0)
    t_col = lax.broadcasted_iota(jnp.int32, (tt, tt), 1)
    strict_upper = jnp.where(t_row < t_col, 1.0, 0.0).astype(BF16)
    before = _bdot(multi_hot.astype(BF16), strict_upper) + carry_ref[...]
    rank_rows = [jnp.sum(jnp.where(sel, before, 0.0), axis=0, keepdims=True) for sel in sels]
    rank_ref[...] = jnp.concatenate(rank_rows, axis=0).astype(jnp.int32)
    carry = carry_ref[...] + jnp.sum(multi_hot, axis=1, keepdims=True)
    carry_ref[...] = carry
    cnt_ref[...] = carry.astype(jnp.int32)


def _router(h_tiles, w_router, router_bias):
    n_tok = h_tiles.shape[0] // TOK_ROWS
    tt = ROUTER_TT
    kt = lambda dt: jax.ShapeDtypeStruct((TOP_K, n_tok), dt)
    tok_spec = pl.BlockSpec((TOP_K, tt), lambda i: (0, i))
    return pl.pallas_call(
        functools.partial(_router_body, tt=tt),
        grid=(n_tok // tt,),
        in_specs=[
            pl.BlockSpec((tt * TOK_ROWS, LANES), lambda i: (i, 0)),
            pl.BlockSpec((N_EXPERTS, D_MODEL), lambda i: (0, 0)),
            pl.BlockSpec((N_EXPERTS, 1), lambda i: (0, 0)),
        ],
        out_specs=[tok_spec, tok_spec, tok_spec, pl.BlockSpec((N_EXPERTS, 1), lambda i: (0, 0))],
        out_shape=[kt(jnp.int32), kt(F32), kt(jnp.int32),
                   jax.ShapeDtypeStruct((N_EXPERTS, 1), jnp.int32)],
        scratch_shapes=[pltpu.VMEM((N_EXPERTS, 1), F32)],
        compiler_params=pltpu.CompilerParams(
            dimension_semantics=("arbitrary",), vmem_limit_bytes=VMEM_LIMIT),
        name="moe_router",
    )(h_tiles, w_router.T.astype(BF16), router_bias.reshape(N_EXPERTS, 1))


def _pack_rows(x):
    half = D_MODEL // 2
    return pltpu.pack_elementwise([x[:, :half], x[:, half:]], packed_dtype=BF16)


def _unpack_rows(p):
    return jnp.concatenate(
        [pltpu.unpack_elementwise(p, index=i, packed_dtype=BF16, unpacked_dtype=F32) for i in range(2)],
        axis=1)


def _packed_load(ref, n_tok):
    return jnp.concatenate(
        [ref[pl.ds(s, n_tok, stride=PK_ROWS), :] for s in range(PK_ROWS)], axis=1)


def _packed_store(ref, val, n_tok):
    for s in range(PK_ROWS):
        ref[pl.ds(s, n_tok, stride=PK_ROWS), :] = val[:, s * LANES:(s + 1) * LANES]


def _sorted_rows_body(idx_ref, rank_ref, start_ref, o_ref):
    tl = idx_ref.shape[1]
    e_iota = lax.broadcasted_iota(jnp.int32, (N_EXPERTS, tl), 0)
    start = start_ref[...]
    rows = []
    for k in range(TOP_K):
        hit = e_iota == idx_ref[k:k + 1, :]
        base = jnp.sum(jnp.where(hit, start, 0.0), axis=0, keepdims=True)
        rows.append(base.astype(jnp.int32) + rank_ref[k:k + 1, :])
    o_ref[...] = jnp.concatenate(rows, axis=0)


def _sorted_rows(top_idx, rank, pad_start):
    n_tok = top_idx.shape[1]
    tl = SORTED_ROWS_TL
    tok = pl.BlockSpec((TOP_K, tl), lambda i: (0, i))
    return pl.pallas_call(
        _sorted_rows_body,
        grid=(n_tok // tl,),
        in_specs=[tok, tok, pl.BlockSpec((N_EXPERTS, 1), lambda i: (0, 0))],
        out_specs=tok,
        out_shape=jax.ShapeDtypeStruct((TOP_K, n_tok), jnp.int32),
        compiler_params=pltpu.CompilerParams(
            dimension_semantics=("arbitrary",), vmem_limit_bytes=VMEM_LIMIT),
        name="moe_sorted_rows",
    )(top_idx, rank, pad_start.astype(F32).reshape(N_EXPERTS, 1))


def _dispatch_body(start_ref, cnt_ref, dest_ref, h_ref, xs_ref, pk_ref, zero_ref, sem, zero_sem,
                   *, tt, tb):
    @pl.when(pl.program_id(0) == 0)
    def _():
        zero_ref[...] = jnp.zeros_like(zero_ref)

        def pad_copies(e, fn):
            n_pad = ((cnt_ref[e] + (tb - 1)) & (-tb)) - cnt_ref[e]
            row = start_ref[e] + cnt_ref[e]
            for bit in range(tb.bit_length() - 1):
                size = 1 << bit
                has_bit = ((n_pad >> bit) & 1) == 1

                @pl.when(has_bit)
                def _():
                    fn(pltpu.make_async_copy(zero_ref.at[pl.ds(0, size)],
                                             xs_ref.at[pl.ds(row, size)], zero_sem))
                row = row + jnp.where(has_bit, size, 0)

        def start_all(e, c):
            pad_copies(e, lambda cp: cp.start())
            return c

        def wait_all(e, c):
            pad_copies(e, lambda cp: cp.wait())
            return c

        lax.fori_loop(0, N_EXPERTS, start_all, 0)
        lax.fori_loop(0, N_EXPERTS, wait_all, 0)

    packed = _pack_rows(_rows_load(h_ref, tt))
    for s in range(PK_ROWS):
        pk_ref[:, s, :] = packed[:, s * LANES:(s + 1) * LANES]
    for t in range(tt):
        for k in range(TOP_K):
            pltpu.make_async_copy(pk_ref.at[t], xs_ref.at[dest_ref[k, t]], sem).start(priority=k % 2)
    for k in range(TOP_K):
        pltpu.make_async_copy(pk_ref, xs_ref.at[pl.ds(0, tt)], sem).wait()


def _dispatch(h_tiles, dest_rows, pad_start, counts, n_rows):
    n_tok = h_tiles.shape[0] // TOK_ROWS
    tt = DISPATCH_TT
    grid_spec = pltpu.PrefetchScalarGridSpec(
        num_scalar_prefetch=2,
        grid=(n_tok // tt,),
        in_specs=[pl.BlockSpec((TOP_K, tt), lambda i, s, c: (0, i), memory_space=pltpu.SMEM),
                  pl.BlockSpec((tt * TOK_ROWS, LANES), lambda i, s, c: (i, 0))],
        out_specs=pl.BlockSpec(memory_space=pl.ANY),
        scratch_shapes=[pltpu.VMEM((tt, PK_ROWS, LANES), I32),
                        pltpu.VMEM((EXPERT_TB // 2, PK_ROWS, LANES), I32),
                        pltpu.SemaphoreType.DMA(()), pltpu.SemaphoreType.DMA(())],
    )
    return pl.pallas_call(
        functools.partial(_dispatch_body, tt=tt, tb=EXPERT_TB),
        grid_spec=grid_spec,
        out_shape=jax.ShapeDtypeStruct((n_rows, PK_ROWS, LANES), I32),
        compiler_params=pltpu.CompilerParams(
            dimension_semantics=("arbitrary",), vmem_limit_bytes=VMEM_LIMIT),
        name="moe_dispatch",
    )(pad_start, counts, dest_rows, h_tiles)


def _expert_body(be_ref, first_ref, slot_ref, next_ref, na_ref, xs_ref, w1_ref, w3_ref, w2_ref, ys_ref,
                 w13_buf, w2_buf, w1b, w3b, w2b, sem, *, tb, layer):
    i = pl.program_id(0)

    def weight_copies(expert, slot):
        return (pltpu.make_async_copy(w1_ref.at[layer, expert], w13_buf.at[slot, 0], sem.at[slot]),
                pltpu.make_async_copy(w3_ref.at[layer, expert], w13_buf.at[slot, 1], sem.at[slot]),
                pltpu.make_async_copy(w2_ref.at[layer, expert], w2_buf.at[slot], sem.at[slot]))

    @pl.when(i < na_ref[0])
    def _():
        @pl.when(first_ref[i] == 1)
        def _():
            slot = slot_ref[i]

            @pl.when(i == 0)
            def _():
                for cp in weight_copies(be_ref[0], slot):
                    cp.start()

            for cp in weight_copies(be_ref[i], slot):
                cp.wait()

            @pl.when(next_ref[i] >= 0)
            def _():
                for cp in weight_copies(next_ref[i], 1 - slot):
                    cp.start()

            w1b[...] = w13_buf[slot, 0].astype(BF16)
            w3b[...] = w13_buf[slot, 1].astype(BF16)
            w2b[...] = w2_buf[slot].astype(BF16)

        x = _unpack_rows(_packed_load(xs_ref, tb)).astype(BF16)
        mid = (_silu(_bdot(x, w1b[...])) * _bdot(x, w3b[...])).astype(BF16)
        _packed_store(ys_ref, _pack_rows(_bdot(mid, w2b[...])), tb)


def _experts(xs, block_expert, first, slot, next_expert, n_active, w1, w3, w2, layer):
    tb = EXPERT_TB
    n_blocks = xs.shape[0] // (tb * PK_ROWS)
    row_block = lambda i, be, fi, sl, nx, na: (jnp.minimum(i, na[0] - 1), 0)
    grid_spec = pltpu.PrefetchScalarGridSpec(
        num_scalar_prefetch=5,
        grid=(n_blocks,),
        in_specs=[
            pl.BlockSpec((tb * PK_ROWS, LANES), row_block),
            pl.BlockSpec(memory_space=pl.ANY), pl.BlockSpec(memory_space=pl.ANY),
            pl.BlockSpec(memory_space=pl.ANY),
        ],
        out_specs=pl.BlockSpec((tb * PK_ROWS, LANES), row_block),
        scratch_shapes=[pltpu.VMEM((2, 2, D_MODEL, D_EXPERT), F32),
                        pltpu.VMEM((2, D_EXPERT, D_MODEL), F32),
                        pltpu.VMEM((D_MODEL, D_EXPERT), BF16), pltpu.VMEM((D_MODEL, D_EXPERT), BF16),
                        pltpu.VMEM((D_EXPERT, D_MODEL), BF16), pltpu.SemaphoreType.DMA((2,))],
    )
    return pl.pallas_call(
        functools.partial(_expert_body, tb=tb, layer=layer),
        grid_spec=grid_spec,
        out_shape=jax.ShapeDtypeStruct(xs.shape, I32),
        compiler_params=pltpu.CompilerParams(
            dimension_semantics=("arbitrary",), vmem_limit_bytes=VMEM_LIMIT),
        name="moe_experts",
    )(block_expert, first, slot, next_expert, n_active, xs, w1, w3, w2)


def _store_head_slabs(y, o_refs, scales):
    for j, (o_ref, scale) in enumerate(zip(o_refs, scales)):
        for s in range(PAIR_SLABS):
            lo = j * O_WIDTH + 2 * s * LANES
            o_ref[s] = pltpu.pack_elementwise(
                [y[:, lo:lo + LANES] * scale, y[:, lo + LANES:lo + 2 * LANES] * scale],
                packed_dtype=BF16)


def _combine_body(src_ref, h_ref, gates_ref, ys_ref, ys_flat_ref, ws1_ref, ws3_ref, ws2_ref,
                  g_ref, b_ref, o_ref, gbuf, sem, *, tt, n_tiles, token_tile_out):
    j = pl.program_id(0)

    def step(issue, finish):
        nxt = j % 2
        cur = (j - 1) % 2
        if finish:
            h = _rows_load(h_ref, tt)
            hb = h.astype(BF16)
            mid = (_silu(_bdot(hb, ws1_ref[...])) * _bdot(hb, ws3_ref[...])).astype(BF16)
            routed = _bdot(mid, ws2_ref[...])
            gates = gates_ref[...]
        for k in range(TOP_K):
            if issue:
                for t in range(tt):
                    pltpu.make_async_copy(ys_ref.at[src_ref[k, t]],
                                          gbuf.at[nxt, k, pl.ds(t * PK_ROWS, PK_ROWS), :],
                                          sem.at[nxt]).start(priority=t % 2)
            if finish:
                pltpu.make_async_copy(ys_flat_ref.at[pl.ds(0, tt * PK_ROWS), :], gbuf.at[cur, k],
                                      sem.at[cur]).wait()
                routed = routed + gates[:, k:k + 1] * _unpack_rows(_packed_load(gbuf.at[cur, k], tt))
        if finish:
            out = _layer_norm(DEEPNORM_ALPHA * h + routed, g_ref[...], b_ref[...])
            if token_tile_out:
                _rows_store(o_ref, out, tt)
            else:
                o_ref[...] = out

    pl.when(j == 0)(lambda: step(True, False))
    pl.when((j >= 1) & (j < n_tiles))(lambda: step(True, True))
    pl.when(j == n_tiles)(lambda: step(False, True))


def _combine(h_tiles, src_rows, gates_tk, ys, ws1, ws3, ws2, g, b, token_tile_out):
    n_tok = h_tiles.shape[0] // TOK_ROWS
    tt = COMBINE_TT
    n_tiles = n_tok // tt
    prev_tile = lambda j: (jnp.maximum(j - 1, 0), 0)
    full = lambda shape: pl.BlockSpec(shape, lambda j: (0,) * len(shape))
    if token_tile_out:
        out_spec = pl.BlockSpec((tt * TOK_ROWS, LANES), prev_tile)
        out_shape = jax.ShapeDtypeStruct((n_tok * TOK_ROWS, LANES), F32)
    else:
        out_spec = pl.BlockSpec((tt, D_MODEL), prev_tile)
        out_shape = jax.ShapeDtypeStruct((n_tok, D_MODEL), F32)
    n_rows = ys.shape[0] // PK_ROWS
    return pl.pallas_call(
        functools.partial(_combine_body, tt=tt, n_tiles=n_tiles, token_tile_out=token_tile_out),
        grid=(n_tiles + 1,),
        in_specs=[pl.BlockSpec((TOP_K, tt), lambda j: (0, jnp.minimum(j, n_tiles - 1)),
                               memory_space=pltpu.SMEM),
                  pl.BlockSpec((tt * TOK_ROWS, LANES), prev_tile),
                  pl.BlockSpec((tt, TOP_K), prev_tile),
                  pl.BlockSpec(memory_space=pl.ANY), pl.BlockSpec(memory_space=pl.ANY),
                  full((D_MODEL, D_EXPERT)), full((D_MODEL, D_EXPERT)), full((D_EXPERT, D_MODEL)),
                  full((1, D_MODEL)), full((1, D_MODEL))],
        out_specs=out_spec,
        out_shape=out_shape,
        scratch_shapes=[pltpu.VMEM((2, TOP_K, tt * PK_ROWS, LANES), I32),
                        pltpu.SemaphoreType.DMA((2,))],
        compiler_params=pltpu.CompilerParams(
            dimension_semantics=("arbitrary",), vmem_limit_bytes=VMEM_LIMIT),
        name="moe_combine",
    )(src_rows, h_tiles, gates_tk, ys.reshape(n_rows, PK_ROWS, LANES), ys, ws1.astype(BF16),
      ws3.astype(BF16), ws2.astype(BF16), g.reshape(1, -1), b.reshape(1, -1))


def _moe_layer(h_tiles, w_router, router_bias, w1, w3, w2, layer, ws1, ws3, ws2, g, b,
               token_tile_out):
    n_tok = h_tiles.shape[0] // TOK_ROWS
    tb = EXPERT_TB
    top_idx, gates, rank, counts = _router(h_tiles, w_router, router_bias)

    n_blocks = -(-(n_tok * TOP_K + N_EXPERTS * (tb - 1)) // tb)
    counts = counts.reshape(N_EXPERTS)
    padded = ((counts + tb - 1) // tb) * tb
    pad_end = jnp.cumsum(padded)
    pad_start = (pad_end - padded).astype(jnp.int32)
    block_first_row = jnp.arange(n_blocks, dtype=jnp.int32) * tb
    block_expert = jnp.minimum(
        jnp.sum((pad_end[None, :] <= block_first_row[:, None]).astype(jnp.int32), axis=1),
        N_EXPERTS - 1)
    n_active = (pad_end[-1:] // tb).astype(jnp.int32)
    present = counts > 0
    expert_ids = jnp.arange(N_EXPERTS, dtype=jnp.int32)
    later = lax.cummin(jnp.where(present, expert_ids, N_EXPERTS), reverse=True)
    next_present = jnp.concatenate([later[1:], jnp.full((1,), N_EXPERTS, jnp.int32)])
    next_present = jnp.where(next_present < N_EXPERTS, next_present, -1)
    ordinal = jnp.cumsum(present.astype(jnp.int32)) - 1
    first = jnp.concatenate([jnp.ones((1,), jnp.int32),
                             (block_expert[1:] != block_expert[:-1]).astype(jnp.int32)])
    slot = ordinal[block_expert] % 2
    next_expert = next_present[block_expert]

    rows = _sorted_rows(top_idx, rank, pad_start)
    xs = _dispatch(h_tiles, rows, pad_start, counts, n_blocks * tb)
    ys = _experts(xs.reshape(n_blocks * tb * PK_ROWS, LANES), block_expert, first, slot, next_expert,
                  n_active, w1, w3, w2, layer)
    return _combine(h_tiles, rows, gates.T, ys, ws1, ws3, ws2, g, b, token_tile_out)


def _proj_body(h_ref, w_ref, *o_refs, tt, scales):
    y = _bdot(_rows_load(h_ref, tt).astype(BF16), w_ref[...])
    _store_head_slabs(y, o_refs, scales)


def _project(h_tiles, w, scales):
    n_tok = h_tiles.shape[0] // TOK_ROWS
    tt = PROJ_TT
    n_out = len(scales)
    out_spec = pl.BlockSpec((PAIR_SLABS, tt, LANES), lambda i: (0, i, 0))
    return pl.pallas_call(
        functools.partial(_proj_body, tt=tt, scales=tuple(scales)),
        grid=(n_tok // tt,),
        in_specs=[pl.BlockSpec((tt * TOK_ROWS, LANES), lambda i: (i, 0)),
                  pl.BlockSpec(w.shape, lambda i: (0, 0))],
        out_specs=[out_spec] * n_out,
        out_shape=[jax.ShapeDtypeStruct((PAIR_SLABS, n_tok, LANES), I32)] * n_out,
        compiler_params=pltpu.CompilerParams(
            dimension_semantics=("arbitrary",), vmem_limit_bytes=VMEM_LIMIT),
        name="projection",
    )(h_tiles, w.astype(BF16))


def _attn_body(q_ref, kp_ref, kc_ref, vp_ref, vc_ref, o_ref, lse_ref, *, dilation, n_steps, blocks):
    t = ATT_BLOCK
    qi = lax.broadcasted_iota(jnp.int32, (t, 2 * t), 0)
    kj = lax.broadcasted_iota(jnp.int32, (t, 2 * t), 1)
    dist = t + qi - kj
    band = (dist >= 0) & (dist <= n_steps)
    band_at_start = band & ((pl.program_id(1) > 0) | (kj >= t))
    token_dist = (dilation * dist).astype(F32)
    lane = lax.broadcasted_iota(jnp.int32, (t, LANES), 1)
    low_half = lane < HEAD_DIM
    lane_head = lane // LSE_LANES
    contract_last = (((1,), (1,)), ((), ()))

    def unpack(words, which):
        return pltpu.unpack_elementwise(words, index=which, packed_dtype=BF16,
                                        unpacked_dtype=F32).astype(BF16)

    def block(rows, k_prev, v_prev, prev_rows, valid):
        lse_all = jnp.zeros((t, LANES), F32)
        for slab in range(PAIR_SLABS):
            q_words = q_ref.at[slab][rows, :]
            k_words = jnp.concatenate([k_prev.at[slab][prev_rows, :], kc_ref.at[slab][rows, :]], axis=0)
            v_words = jnp.concatenate([v_prev.at[slab][prev_rows, :], vc_ref.at[slab][rows, :]], axis=0)
            pair_outs = []
            for which in range(2):
                pair = 2 * slab + which
                q2, k2, v2 = unpack(q_words, which), unpack(k_words, which), unpack(v_words, which)
                outs = []
                for half in range(2):
                    head = 2 * pair + half
                    slope = 2.0 ** (-8.0 * (head + 1) / ATT_HEADS)
                    keep = low_half if half == 0 else ~low_half
                    qh = jnp.where(keep, q2, jnp.zeros_like(q2))
                    s = lax.dot_general(qh, k2, contract_last, preferred_element_type=F32)
                    s = jnp.where(valid, s - slope * token_dist, MASK_VALUE)
                    m = jnp.max(s, axis=-1, keepdims=True)
                    p = jnp.exp(s - m)
                    l = jnp.sum(p, axis=-1, keepdims=True)
                    outs.append(_bdot(p.astype(BF16), v2) / l)
                    lse_all = jnp.where(lane_head == head, m + jnp.log(l), lse_all)
                pair_outs.append(jnp.where(low_half, outs[0], outs[1]))
            o_ref.at[slab][rows, :] = pltpu.pack_elementwise(pair_outs, packed_dtype=BF16)
        lse_ref[rows, :] = lse_all

    if dilation == 1:
        block(pl.ds(0, t), kp_ref, vp_ref, pl.ds(0, t), band_at_start)
        for b in range(1, blocks):
            block(pl.ds(b * t, t), kc_ref, vc_ref, pl.ds((b - 1) * t, t), band)
    else:
        def residue(r, carry):
            rows = pl.ds(r, t, stride=dilation)
            block(rows, kp_ref, vp_ref, rows, band_at_start)
            return carry
        lax.fori_loop(0, dilation, residue, 0, unroll=min(dilation, ATTN_RESIDUE_UNROLL))


def _attn_branch(q, k, v, bsz, seq, window, dilation):
    blocks = ATTN_D1_BLOCKS if dilation == 1 else 1
    span = ATT_BLOCK * dilation * blocks
    n_spans = seq // span
    prev_rows = ATT_BLOCK if dilation == 1 else span
    prev_per_span = span // prev_rows
    cur = pl.BlockSpec((PAIR_SLABS, span, LANES), lambda bi, n: (0, bi * n_spans + n, 0))
    prev = pl.BlockSpec(
        (PAIR_SLABS, prev_rows, LANES),
        lambda bi, n: (0, (bi * n_spans + n) * prev_per_span - jnp.minimum(n, 1), 0))
    return pl.pallas_call(
        functools.partial(_attn_body, dilation=dilation, n_steps=window // dilation, blocks=blocks),
        grid=(bsz, n_spans),
        in_specs=[cur, prev, cur, prev, cur],
        out_specs=[cur, pl.BlockSpec((span, LANES), lambda bi, n: (bi * n_spans + n, 0))],
        out_shape=[jax.ShapeDtypeStruct((PAIR_SLABS, bsz * seq, LANES), I32),
                   jax.ShapeDtypeStruct((bsz * seq, LANES), F32)],
        compiler_params=pltpu.CompilerParams(
            dimension_semantics=("arbitrary",) * 2, vmem_limit_bytes=VMEM_LIMIT),
        name=f"dilated_attn_d{dilation}",
    )(q, k, k, v, v)


def _merge_body(*refs, tt):
    o_refs = refs[:N_GROUPS]
    lse_refs = refs[N_GROUPS:2 * N_GROUPS]
    h_ref, wo_ref, g_ref, b_ref, out_ref = refs[2 * N_GROUPS:]
    lses = [r[...] for r in lse_refs]
    m = functools.reduce(jnp.maximum, lses)
    es = [jnp.exp(l - m) for l in lses]
    den = functools.reduce(lambda a, c: a + c, es)
    row = lax.broadcasted_iota(jnp.int32, (LANES, O_WIDTH), 0)
    col = lax.broadcasted_iota(jnp.int32, (LANES, O_WIDTH), 1)
    select = jnp.where(row == (col // HEAD_DIM) * LSE_LANES, 1.0, 0.0).astype(BF16)

    def widen(w):
        hi = w.astype(BF16)
        lo = (w - hi.astype(F32)).astype(BF16)
        return _bdot(hi, select) + _bdot(lo, select)

    def branch_out(o_ref):
        return jnp.concatenate(
            [pltpu.unpack_elementwise(o_ref[pair // 2], index=pair % 2, packed_dtype=BF16,
                                      unpacked_dtype=F32) for pair in range(ATT_HEADS // 2)], axis=1)

    o = functools.reduce(lambda a, c: a + c,
                         [widen(e / den) * branch_out(r) for e, r in zip(es, o_refs)])
    mix = _bdot(o.astype(BF16), wo_ref[...])
    z = DEEPNORM_ALPHA * _rows_load(h_ref, tt) + mix
    _rows_store(out_ref, _layer_norm(z, g_ref[...], b_ref[...]), tt)


def _merge(outs, lses, h_tiles, w_o, g, b):
    n_tok = h_tiles.shape[0] // TOK_ROWS
    tt = MERGE_TT
    o_spec = pl.BlockSpec((PAIR_SLABS, tt, LANES), lambda i: (0, i, 0))
    lse_spec = pl.BlockSpec((tt, LANES), lambda i: (i, 0))
    tiles = pl.BlockSpec((tt * TOK_ROWS, LANES), lambda i: (i, 0))
    full = lambda shape: pl.BlockSpec(shape, lambda i: (0,) * len(shape))
    return pl.pallas_call(
        functools.partial(_merge_body, tt=tt),
        grid=(n_tok // tt,),
        in_specs=[o_spec] * N_GROUPS + [lse_spec] * N_GROUPS + [
            tiles, full((O_WIDTH, D_MODEL)), full((1, D_MODEL)), full((1, D_MODEL))],
        out_specs=tiles,
        out_shape=jax.ShapeDtypeStruct(h_tiles.shape, F32),
        compiler_params=pltpu.CompilerParams(
            dimension_semantics=("arbitrary",), vmem_limit_bytes=VMEM_LIMIT),
        name="attn_merge",
    )(*outs, *lses, h_tiles, w_o.astype(BF16), g.reshape(1, -1), b.reshape(1, -1))


def kernel(x, a_w_in, a_conv_w, a_conv_b, a_w_gate_a, a_b_gate_a, a_w_gate_x, a_b_gate_x, a_lambda, a_w_out, w_kv_shared, b_w_q, b_w_o, moe_w_router, moe_router_bias, moe_w1, moe_w3, moe_w2, moe_ws1, moe_ws3, moe_ws2, ln_g, ln_b):
    bsz, seq, d = x.shape
    assert d == D_MODEL and seq % (DILATION_PATTERNS[-1][1] * ATT_BLOCK) == 0

    def moe(h_tiles, layer, token_tile_out):
        return _moe_layer(h_tiles, moe_w_router[layer], moe_router_bias[layer], moe_w1, moe_w3,
                          moe_w2, layer, moe_ws1[layer], moe_ws3[layer], moe_ws2[layer],
                          ln_g[layer, 1], ln_b[layer, 1], token_tile_out)

    h = _rglru_layer(x, a_w_in[0], a_conv_w[0], a_conv_b[0], a_w_gate_a[0], a_b_gate_a[0],
                     a_w_gate_x[0], a_b_gate_x[0], a_lambda[0], a_w_out[0], ln_g[0, 0], ln_b[0, 0])
    h = moe(h, 0, True)
    kvq = _project(h, jnp.concatenate([w_kv_shared, b_w_q[0]], axis=1),
                   (1.0,) * (2 * N_GROUPS) + (HEAD_DIM ** -0.5,) * N_GROUPS)
    ks, vs, qs = kvq[:N_GROUPS], kvq[N_GROUPS:2 * N_GROUPS], kvq[2 * N_GROUPS:]

    outs, lses = [], []
    for grp, (window, dilation) in enumerate(DILATION_PATTERNS):
        o, lse = _attn_branch(qs[grp], ks[grp], vs[grp], bsz, seq, window, dilation)
        outs.append(o)
        lses.append(lse)
    h = _merge(outs, lses, h, b_w_o[0], ln_g[1, 0], ln_b[1, 0])
    h = moe(h, 1, False)
    return h.reshape(bsz, seq, d)
```

```python
import functools
import math

import jax
import jax.numpy as jnp
from jax import lax
from jax.experimental import pallas as pl
from jax.experimental.pallas import tpu as pltpu

F32 = jnp.float32
BF16 = jnp.bfloat16
I32 = jnp.int32

D_MODEL = 1024
LRU_BLOCKS = 4
CONV_WIDTH = 4
LRU_C = 8.0
ATT_HEADS = 8
HEAD_DIM = 64
DILATION_PATTERNS = ((128, 1), (512, 4), (2048, 16))
N_GROUPS = len(DILATION_PATTERNS)
ATT_BLOCK = 128
O_WIDTH = ATT_HEADS * HEAD_DIM
N_EXPERTS = 256
TOP_K = 8
N_EXPERT_GROUPS = 8
TOPK_GROUPS = 4
D_EXPERT = 256
ROUTED_SCALE = 2.5
DEPTH = 2
DEEPNORM_ALPHA = (2 * DEPTH) ** 0.25
LN_EPS = 1e-5
MASK_VALUE = -1e30

SUBLANES = 8
LANES = 128
TOK_ROWS = D_MODEL // LANES
PK_ROWS = D_MODEL // 2 // LANES
PAIR_SLABS = ATT_HEADS * HEAD_DIM // LANES // 2
LSE_LANES = LANES // ATT_HEADS
VMEM_LIMIT = 56 * 1024 * 1024

RGLRU_TS = 512
ROUTER_TT = 1024
SORTED_ROWS_TL = 2048
DISPATCH_TT = 512
EXPERT_TB = 512
COMBINE_TT = 512
PROJ_TT = 1024
MERGE_TT = 512
ATTN_D1_BLOCKS = 8
ATTN_RESIDUE_UNROLL = 8


def _rows_load(ref, n_tok):
    return jnp.concatenate(
        [ref[pl.ds(s, n_tok, stride=TOK_ROWS), :] for s in range(TOK_ROWS)], axis=1)


def _rows_store(ref, val, n_tok):
    for s in range(TOK_ROWS):
        ref[pl.ds(s, n_tok, stride=TOK_ROWS), :] = val[:, s * LANES:(s + 1) * LANES]


def _layer_norm(z, g, b):
    mu = jnp.mean(z, axis=-1, keepdims=True)
    zc = z - mu
    var = jnp.mean(zc * zc, axis=-1, keepdims=True)
    return zc * lax.rsqrt(var + LN_EPS) * g + b


def _silu(x):
    return x * jax.nn.sigmoid(x)


def _gelu_tanh(x):
    c = math.sqrt(2.0 / math.pi)
    return 0.5 * x * (1.0 + jnp.tanh(c * (x + 0.044715 * (x * x * x))))


def _bdot(a, b):
    return jnp.dot(a, b, preferred_element_type=F32)


def _rglru_body(x_ref, win_ref, cw_ref, cb_ref, wga_ref, bga_ref, wgx_ref, bgx_ref, lam_ref,
                wout_ref, g_ref, b_ref, o_ref, hc_ref, tail_ref, *, ts):
    width = D_MODEL
    bw = width // LRU_BLOCKS

    @pl.when(pl.program_id(1) == 0)
    def _():
        hc_ref[...] = jnp.zeros_like(hc_ref)
        tail_ref[...] = jnp.zeros_like(tail_ref)

    x = x_ref[...]
    xz = _bdot(x.astype(BF16), win_ref[...])
    xr = xz[:, :width]
    gate = xz[:, width:]

    tail = tail_ref[...]
    row8 = lax.broadcasted_iota(jnp.int32, (SUBLANES, width), 0)
    cw = cw_ref[...]
    xc = xr * cw[CONV_WIDTH - 1:CONV_WIDTH, :] + cb_ref[...]
    for j in range(1, CONV_WIDTH):
        rx = pltpu.roll(xr, j, 0)
        rp = pltpu.roll(tail, j, 0)
        top = jnp.where(row8 < j, rp, rx[:SUBLANES])
        shifted = jnp.concatenate([top, rx[SUBLANES:]], axis=0)
        xc = xc + shifted * cw[CONV_WIDTH - 1 - j:CONV_WIDTH - j, :]
    tail_ref[...] = xr[ts - SUBLANES:]

    xcb = xc.astype(BF16)

    def block_diag(w_ref):
        return jnp.concatenate(
            [_bdot(xcb[:, n * bw:(n + 1) * bw], w_ref[n]) for n in range(LRU_BLOCKS)], axis=1)

    r = jax.nn.sigmoid(block_diag(wga_ref) + bga_ref[...])
    i = jax.nn.sigmoid(block_diag(wgx_ref) + bgx_ref[...])
    lam = lam_ref[...]
    softplus_neg_lam = jnp.maximum(-lam, 0.0) + jnp.log1p(jnp.exp(-jnp.abs(lam)))
    log_a = (-LRU_C * r) * softplus_neg_lam
    a = jnp.exp(log_a)
    mult = jnp.sqrt(-jnp.tanh(log_a) * (a * a + 1.0))
    u = mult * (i * xc)

    groups = ts // SUBLANES
    a3 = a.reshape(groups, SUBLANES, width)
    u3 = u.reshape(groups, SUBLANES, width)
    sub = lax.broadcasted_iota(jnp.int32, (groups, SUBLANES, width), 1)
    sh = 1
    while sh < SUBLANES:
        a_prev = pltpu.roll(a3, sh, 1)
        u_prev = pltpu.roll(u3, sh, 1)
        live = sub >= sh
        u3 = jnp.where(live, a3 * u_prev, 0.0) + u3
        a3 = jnp.where(live, a3 * a_prev, a3)
        sh *= 2
    carry = hc_ref[...]
    h_groups = []
    for grp in range(groups):
        h_grp = a3[grp] * carry + u3[grp]
        h_groups.append(h_grp)
        carry = h_grp[SUBLANES - 1:SUBLANES]
    h = jnp.concatenate(h_groups, axis=0)
    hc_ref[...] = carry

    y = (h * _gelu_tanh(gate)).astype(BF16)
    mix = _bdot(y, wout_ref[...])
    z = DEEPNORM_ALPHA * x + mix
    _rows_store(o_ref, _layer_norm(z, g_ref[...], b_ref[...]), ts)


def _rglru_layer(x, w_in, conv_w, conv_b, wga, bga, wgx, bgx, lam, w_out, g, b):
    bsz, seq, d = x.shape
    ts = RGLRU_TS
    ns = seq // ts
    row = lambda v: v.reshape(1, -1)
    full = lambda shape: pl.BlockSpec(shape, lambda bi, si: (0,) * len(shape))
    return pl.pallas_call(
        functools.partial(_rglru_body, ts=ts),
        grid=(bsz, ns),
        in_specs=[
            pl.BlockSpec((None, ts, d), lambda bi, si: (bi, si, 0)),
            full((d, 2 * d)), full((CONV_WIDTH, d)), full((1, d)),
            full((LRU_BLOCKS, d // LRU_BLOCKS, d // LRU_BLOCKS)), full((1, d)),
            full((LRU_BLOCKS, d // LRU_BLOCKS, d // LRU_BLOCKS)), full((1, d)),
            full((1, d)), full((d, d)), full((1, d)), full((1, d)),
        ],
        out_specs=pl.BlockSpec((ts * TOK_ROWS, LANES), lambda bi, si: (bi * ns + si, 0)),
        out_shape=jax.ShapeDtypeStruct((bsz * seq * TOK_ROWS, LANES), F32),
        scratch_shapes=[pltpu.VMEM((1, d), F32), pltpu.VMEM((SUBLANES, d), F32)],
        compiler_params=pltpu.CompilerParams(
            dimension_semantics=("arbitrary", "arbitrary"), vmem_limit_bytes=VMEM_LIMIT),
        name="rglru_layer",
    )(x, w_in.astype(BF16), conv_w, row(conv_b), wga.astype(BF16), row(bga), wgx.astype(BF16),
      row(bgx), row(lam), w_out.astype(BF16), row(g), row(b))


def _router_body(h_ref, wrt_ref, bias_ref, idx_ref, gate_ref, rank_ref, cnt_ref, carry_ref, *, tt):
    _route(_rows_load(h_ref, tt), wrt_ref, bias_ref, idx_ref, gate_ref, rank_ref, cnt_ref, carry_ref, tt)


def _route(h, wrt_ref, bias_ref, idx_ref, gate_ref, rank_ref, cnt_ref, carry_ref, tt):
    n_e = N_EXPERTS
    per_group = n_e // N_EXPERT_GROUPS

    @pl.when(pl.program_id(0) == 0)
    def _():
        carry_ref[...] = jnp.zeros_like(carry_ref)

    logits = lax.dot_general(wrt_ref[...], h.astype(BF16), (((1,), (1,)), ((), ())),
                             preferred_element_type=F32)
    scores = jax.nn.sigmoid(logits)
    biased = scores + bias_ref[...]

    j_iota = lax.broadcasted_iota(jnp.int32, (per_group, tt), 0)
    group_score = []
    for g in range(N_EXPERT_GROUPS):
        bg = biased[g * per_group:(g + 1) * per_group]
        m1 = jnp.max(bg, axis=0, keepdims=True)
        i1 = jnp.min(jnp.where(bg == m1, j_iota, per_group), axis=0, keepdims=True)
        m2 = jnp.max(jnp.where(j_iota == i1, -jnp.inf, bg), axis=0, keepdims=True)
        group_score.append(m1 + m2)

    masked = []
    for g in range(N_EXPERT_GROUPS):
        beaten_by = jnp.zeros((1, tt), jnp.int32)
        for o in range(N_EXPERT_GROUPS):
            if o == g:
                continue
            wins = group_score[o] > group_score[g]
            if o < g:
                wins = wins | (group_score[o] == group_score[g])
            beaten_by = beaten_by + wins.astype(jnp.int32)
        keep = beaten_by < TOPK_GROUPS
        masked.append(jnp.where(keep, biased[g * per_group:(g + 1) * per_group], MASK_VALUE))
    cur = jnp.concatenate(masked, axis=0)

    e_iota = lax.broadcasted_iota(jnp.int32, (n_e, tt), 0)
    idx_rows, score_rows, sels = [], [], []
    for _ in range(TOP_K):
        m = jnp.max(cur, axis=0, keepdims=True)
        ik = jnp.min(jnp.where(cur == m, e_iota, n_e), axis=0, keepdims=True)
        sel = e_iota == ik
        score_rows.append(jnp.sum(jnp.where(sel, scores, 0.0), axis=0, keepdims=True))
        cur = jnp.where(sel, -jnp.inf, cur)
        idx_rows.append(ik)
        sels.append(sel)
    top_s = jnp.concatenate(score_rows, axis=0)
    gate_ref[...] = top_s / jnp.sum(top_s, axis=0, keepdims=True) * ROUTED_SCALE
    idx_ref[...] = jnp.concatenate(idx_rows, axis=0)
    multi_hot = jnp.where(cur == -jnp.inf, 1.0, 0.0)

    t_row = lax.broadcasted_iota(jnp.int32, (tt, tt), 0)
    t_col = lax.broadcasted_iota(jnp.int32, (tt, tt), 1)
    strict_upper = jnp.where(t_row < t_col, 1.0, 0.0).astype(BF16)
    before = _bdot(multi_hot.astype(BF16), strict_upper) + carry_ref[...]
    rank_rows = [jnp.sum(jnp.where(sel, before, 0.0), axis=0, keepdims=True) for sel in sels]
    rank_ref[...] = jnp.concatenate(rank_rows, axis=0).astype(jnp.int32)
    carry = carry_ref[...] + jnp.sum(multi_hot, axis=1, keepdims=True)
    carry_ref[...] = carry
    cnt_ref[...] = carry.astype(jnp.int32)


def _router(h_tiles, w_router, router_bias):
    n_tok = h_tiles.shape[0] // TOK_ROWS
    tt = ROUTER_TT
    kt = lambda dt: jax.ShapeDtypeStruct((TOP_K, n_tok), dt)
    tok_spec = pl.BlockSpec((TOP_K, tt), lambda i: (0, i))
    return pl.pallas_call(
        functools.partial(_router_body, tt=tt),
        grid=(n_tok // tt,),
        in_specs=[
            pl.BlockSpec((tt * TOK_ROWS, LANES), lambda i: (i, 0)),
            pl.BlockSpec((N_EXPERTS, D_MODEL), lambda i: (0, 0)),
            pl.BlockSpec((N_EXPERTS, 1), lambda i: (0, 0)),
        ],
        out_specs=[tok_spec, tok_spec, tok_spec, pl.BlockSpec((N_EXPERTS, 1), lambda i: (0, 0))],
        out_shape=[kt(jnp.int32), kt(F32), kt(jnp.int32),
                   jax.ShapeDtypeStruct((N_EXPERTS, 1), jnp.int32)],
        scratch_shapes=[pltpu.VMEM((N_EXPERTS, 1), F32)],
        compiler_params=pltpu.CompilerParams(
            dimension_semantics=("arbitrary",), vmem_limit_bytes=VMEM_LIMIT),
        name="moe_router",
    )(h_tiles, w_router.T.astype(BF16), router_bias.reshape(N_EXPERTS, 1))


def _pack_rows(x):
    half = D_MODEL // 2
    return pltpu.pack_elementwise([x[:, :half], x[:, half:]], packed_dtype=BF16)


def _unpack_rows(p):
    return jnp.concatenate(
        [pltpu.unpack_elementwise(p, index=i, packed_dtype=BF16, unpacked_dtype=F32) for i in range(2)],
        axis=1)


def _packed_load(ref, n_tok):
    return jnp.concatenate(
        [ref[pl.ds(s, n_tok, stride=PK_ROWS), :] for s in range(PK_ROWS)], axis=1)


def _packed_store(ref, val, n_tok):
    for s in range(PK_ROWS):
        ref[pl.ds(s, n_tok, stride=PK_ROWS), :] = val[:, s * LANES:(s + 1) * LANES]


def _sorted_rows_body(idx_ref, rank_ref, start_ref, o_ref):
    tl = idx_ref.shape[1]
    e_iota = lax.broadcasted_iota(jnp.int32, (N_EXPERTS, tl), 0)
    start = start_ref[...]
    rows = []
    for k in range(TOP_K):
        hit = e_iota == idx_ref[k:k + 1, :]
        base = jnp.sum(jnp.where(hit, start, 0.0), axis=0, keepdims=True)
        rows.append(base.astype(jnp.int32) + rank_ref[k:k + 1, :])
    o_ref[...] = jnp.concatenate(rows, axis=0)


def _sorted_rows(top_idx, rank, pad_start):
    n_tok = top_idx.shape[1]
    tl = SORTED_ROWS_TL
    tok = pl.BlockSpec((TOP_K, tl), lambda i: (0, i))
    return pl.pallas_call(
        _sorted_rows_body,
        grid=(n_tok // tl,),
        in_specs=[tok, tok, pl.BlockSpec((N_EXPERTS, 1), lambda i: (0, 0))],
        out_specs=tok,
        out_shape=jax.ShapeDtypeStruct((TOP_K, n_tok), jnp.int32),
        compiler_params=pltpu.CompilerParams(
            dimension_semantics=("arbitrary",), vmem_limit_bytes=VMEM_LIMIT),
        name="moe_sorted_rows",
    )(top_idx, rank, pad_start.astype(F32).reshape(N_EXPERTS, 1))


def _dispatch_body(start_ref, cnt_ref, dest_ref, h_ref, xs_ref, pk_ref, zero_ref, sem, zero_sem,
                   *, tt, tb):
    @pl.when(pl.program_id(0) == 0)
    def _():
        zero_ref[...] = jnp.zeros_like(zero_ref)

        def pad_copies(e, fn):
            n_pad = ((cnt_ref[e] + (tb - 1)) & (-tb)) - cnt_ref[e]
            row = start_ref[e] + cnt_ref[e]
            for bit in range(tb.bit_length() - 1):
                size = 1 << bit
                has_bit = ((n_pad >> bit) & 1) == 1

                @pl.when(has_bit)
                def _():
                    fn(pltpu.make_async_copy(zero_ref.at[pl.ds(0, size)],
                                             xs_ref.at[pl.ds(row, size)], zero_sem))
                row = row + jnp.where(has_bit, size, 0)

        def start_all(e, c):
            pad_copies(e, lambda cp: cp.start())
            return c

        def wait_all(e, c):
            pad_copies(e, lambda cp: cp.wait())
            return c

        lax.fori_loop(0, N_EXPERTS, start_all, 0)
        lax.fori_loop(0, N_EXPERTS, wait_all, 0)

    packed = _pack_rows(_rows_load(h_ref, tt))
    for s in range(PK_ROWS):
        pk_ref[:, s, :] = packed[:, s * LANES:(s + 1) * LANES]
    for t in range(tt):
        for k in range(TOP_K):
            pltpu.make_async_copy(pk_ref.at[t], xs_ref.at[dest_ref[k, t]], sem).start(priority=k % 2)
    for k in range(TOP_K):
        pltpu.make_async_copy(pk_ref, xs_ref.at[pl.ds(0, tt)], sem).wait()


def _dispatch(h_tiles, dest_rows, pad_start, counts, n_rows):
    n_tok = h_tiles.shape[0] // TOK_ROWS
    tt = DISPATCH_TT
    grid_spec = pltpu.PrefetchScalarGridSpec(
        num_scalar_prefetch=2,
        grid=(n_tok // tt,),
        in_specs=[pl.BlockSpec((TOP_K, tt), lambda i, s, c: (0, i), memory_space=pltpu.SMEM),
                  pl.BlockSpec((tt * TOK_ROWS, LANES), lambda i, s, c: (i, 0))],
        out_specs=pl.BlockSpec(memory_space=pl.ANY),
        scratch_shapes=[pltpu.VMEM((tt, PK_ROWS, LANES), I32),
                        pltpu.VMEM((EXPERT_TB // 2, PK_ROWS, LANES), I32),
                        pltpu.SemaphoreType.DMA(()), pltpu.SemaphoreType.DMA(())],
    )
    return pl.pallas_call(
        functools.partial(_dispatch_body, tt=tt, tb=EXPERT_TB),
        grid_spec=grid_spec,
        out_shape=jax.ShapeDtypeStruct((n_rows, PK_ROWS, LANES), I32),
        compiler_params=pltpu.CompilerParams(
            dimension_semantics=("arbitrary",), vmem_limit_bytes=VMEM_LIMIT),
        name="moe_dispatch",
    )(pad_start, counts, dest_rows, h_tiles)


def _expert_body(be_ref, first_ref, slot_ref, next_ref, na_ref, xs_ref, w1_ref, w3_ref, w2_ref, ys_ref,
                 w13_buf, w2_buf, w1b, w3b, w2b, sem, *, tb, layer):
    i = pl.program_id(0)

    def weight_copies(expert, slot):
        return (pltpu.make_async_copy(w1_ref.at[layer, expert], w13_buf.at[slot, 0], sem.at[slot]),
                pltpu.make_async_copy(w3_ref.at[layer, expert], w13_buf.at[slot, 1], sem.at[slot]),
                pltpu.make_async_copy(w2_ref.at[layer, expert], w2_buf.at[slot], sem.at[slot]))

    @pl.when(i < na_ref[0])
    def _():
        @pl.when(first_ref[i] == 1)
        def _():
            slot = slot_ref[i]

            @pl.when(i == 0)
            def _():
                for cp in weight_copies(be_ref[0], slot):
                    cp.start()

            for cp in weight_copies(be_ref[i], slot):
                cp.wait()

            @pl.when(next_ref[i] >= 0)
            def _():
                for cp in weight_copies(next_ref[i], 1 - slot):
                    cp.start()

            w1b[...] = w13_buf[slot, 0].astype(BF16)
            w3b[...] = w13_buf[slot, 1].astype(BF16)
            w2b[...] = w2_buf[slot].astype(BF16)

        x = _unpack_rows(_packed_load(xs_ref, tb)).astype(BF16)
        mid = (_silu(_bdot(x, w1b[...])) * _bdot(x, w3b[...])).astype(BF16)
        _packed_store(ys_ref, _pack_rows(_bdot(mid, w2b[...])), tb)


def _experts(xs, block_expert, first, slot, next_expert, n_active, w1, w3, w2, layer):
    tb = EXPERT_TB
    n_blocks = xs.shape[0] // (tb * PK_ROWS)
    row_block = lambda i, be, fi, sl, nx, na: (jnp.minimum(i, na[0] - 1), 0)
    grid_spec = pltpu.PrefetchScalarGridSpec(
        num_scalar_prefetch=5,
        grid=(n_blocks,),
        in_specs=[
            pl.BlockSpec((tb * PK_ROWS, LANES), row_block),
            pl.BlockSpec(memory_space=pl.ANY), pl.BlockSpec(memory_space=pl.ANY),
            pl.BlockSpec(memory_space=pl.ANY),
        ],
        out_specs=pl.BlockSpec((tb * PK_ROWS, LANES), row_block),
        scratch_shapes=[pltpu.VMEM((2, 2, D_MODEL, D_EXPERT), F32),
                        pltpu.VMEM((2, D_EXPERT, D_MODEL), F32),
                        pltpu.VMEM((D_MODEL, D_EXPERT), BF16), pltpu.VMEM((D_MODEL, D_EXPERT), BF16),
                        pltpu.VMEM((D_EXPERT, D_MODEL), BF16), pltpu.SemaphoreType.DMA((2,))],
    )
    return pl.pallas_call(
        functools.partial(_expert_body, tb=tb, layer=layer),
        grid_spec=grid_spec,
        out_shape=jax.ShapeDtypeStruct(xs.shape, I32),
        compiler_params=pltpu.CompilerParams(
            dimension_semantics=("arbitrary",), vmem_limit_bytes=VMEM_LIMIT),
        name="moe_experts",
    )(block_expert, first, slot, next_expert, n_active, xs, w1, w3, w2)


def _store_head_slabs(y, o_refs, scales):
    for j, (o_ref, scale) in enumerate(zip(o_refs, scales)):
        for s in range(PAIR_SLABS):
            lo = j * O_WIDTH + 2 * s * LANES
            o_ref[s] = pltpu.pack_elementwise(
                [y[:, lo:lo + LANES] * scale, y[:, lo + LANES:lo + 2 * LANES] * scale],
                packed_dtype=BF16)


def _combine_body(src_ref, h_ref, gates_ref, ys_ref, ys_flat_ref, ws1_ref, ws3_ref, ws2_ref,
                  g_ref, b_ref, o_ref, gbuf, sem, *, tt, n_tiles, token_tile_out):
    j = pl.program_id(0)

    def step(issue, finish):
        nxt = j % 2
        cur = (j - 1) % 2
        if finish:
            h = _rows_load(h_ref, tt)
            hb = h.astype(BF16)
            mid = (_silu(_bdot(hb, ws1_ref[...])) * _bdot(hb, ws3_ref[...])).astype(BF16)
            routed = _bdot(mid, ws2_ref[...])
            gates = gates_ref[...]
        for k in range(TOP_K):
            if issue:
                for t in range(tt):
                    pltpu.make_async_copy(ys_ref.at[src_ref[k, t]],
                                          gbuf.at[nxt, k, pl.ds(t * PK_ROWS, PK_ROWS), :],
                                          sem.at[nxt]).start(priority=t % 2)
            if finish:
                pltpu.make_async_copy(ys_flat_ref.at[pl.ds(0, tt * PK_ROWS), :], gbuf.at[cur, k],
                                      sem.at[cur]).wait()
                routed = routed + gates[:, k:k + 1] * _unpack_rows(_packed_load(gbuf.at[cur, k], tt))
        if finish:
            out = _layer_norm(DEEPNORM_ALPHA * h + routed, g_ref[...], b_ref[...])
            if token_tile_out:
                _rows_store(o_ref, out, tt)
            else:
                o_ref[...] = out

    pl.when(j == 0)(lambda: step(True, False))
    pl.when((j >= 1) & (j < n_tiles))(lambda: step(True, True))
    pl.when(j == n_tiles)(lambda: step(False, True))


def _combine(h_tiles, src_rows, gates_tk, ys, ws1, ws3, ws2, g, b, token_tile_out):
    n_tok = h_tiles.shape[0] // TOK_ROWS
    tt = COMBINE_TT
    n_tiles = n_tok // tt
    prev_tile = lambda j: (jnp.maximum(j - 1, 0), 0)
    full = lambda shape: pl.BlockSpec(shape, lambda j: (0,) * len(shape))
    if token_tile_out:
        out_spec = pl.BlockSpec((tt * TOK_ROWS, LANES), prev_tile)
        out_shape = jax.ShapeDtypeStruct((n_tok * TOK_ROWS, LANES), F32)
    else:
        out_spec = pl.BlockSpec((tt, D_MODEL), prev_tile)
        out_shape = jax.ShapeDtypeStruct((n_tok, D_MODEL), F32)
    n_rows = ys.shape[0] // PK_ROWS
    return pl.pallas_call(
        functools.partial(_combine_body, tt=tt, n_tiles=n_tiles, token_tile_out=token_tile_out),
        grid=(n_tiles + 1,),
        in_specs=[pl.BlockSpec((TOP_K, tt), lambda j: (0, jnp.minimum(j, n_tiles - 1)),
                               memory_space=pltpu.SMEM),
                  pl.BlockSpec((tt * TOK_ROWS, LANES), prev_tile),
                  pl.BlockSpec((tt, TOP_K), prev_tile),
                  pl.BlockSpec(memory_space=pl.ANY), pl.BlockSpec(memory_space=pl.ANY),
                  full((D_MODEL, D_EXPERT)), full((D_MODEL, D_EXPERT)), full((D_EXPERT, D_MODEL)),
                  full((1, D_MODEL)), full((1, D_MODEL))],
        out_specs=out_spec,
        out_shape=out_shape,
        scratch_shapes=[pltpu.VMEM((2, TOP_K, tt * PK_ROWS, LANES), I32),
                        pltpu.SemaphoreType.DMA((2,))],
        compiler_params=pltpu.CompilerParams(
            dimension_semantics=("arbitrary",), vmem_limit_bytes=VMEM_LIMIT),
        name="moe_combine",
    )(src_rows, h_tiles, gates_tk, ys.reshape(n_rows, PK_ROWS, LANES), ys, ws1.astype(BF16),
      ws3.astype(BF16), ws2.astype(BF16), g.reshape(1, -1), b.reshape(1, -1))


def _moe_layer(h_tiles, w_router, router_bias, w1, w3, w2, layer, ws1, ws3, ws2, g, b,
               token_tile_out, routing=None):
    n_tok = h_tiles.shape[0] // TOK_ROWS
    tb = EXPERT_TB
    top_idx, gates, rank, counts = routing or _router(h_tiles, w_router, router_bias)

    n_blocks = -(-(n_tok * TOP_K + N_EXPERTS * (tb - 1)) // tb)
    counts = counts.reshape(N_EXPERTS)
    padded = ((counts + tb - 1) // tb) * tb
    pad_end = jnp.cumsum(padded)
    pad_start = (pad_end - padded).astype(jnp.int32)
    block_first_row = jnp.arange(n_blocks, dtype=jnp.int32) * tb
    block_expert = jnp.minimum(
        jnp.sum((pad_end[None, :] <= block_first_row[:, None]).astype(jnp.int32), axis=1),
        N_EXPERTS - 1)
    n_active = (pad_end[-1:] // tb).astype(jnp.int32)
    present = counts > 0
    expert_ids = jnp.arange(N_EXPERTS, dtype=jnp.int32)
    later = lax.cummin(jnp.where(present, expert_ids, N_EXPERTS), reverse=True)
    next_present = jnp.concatenate([later[1:], jnp.full((1,), N_EXPERTS, jnp.int32)])
    next_present = jnp.where(next_present < N_EXPERTS, next_present, -1)
    ordinal = jnp.cumsum(present.astype(jnp.int32)) - 1
    first = jnp.concatenate([jnp.ones((1,), jnp.int32),
                             (block_expert[1:] != block_expert[:-1]).astype(jnp.int32)])
    slot = ordinal[block_expert] % 2
    next_expert = next_present[block_expert]

    rows = _sorted_rows(top_idx, rank, pad_start)
    xs = _dispatch(h_tiles, rows, pad_start, counts, n_blocks * tb)
    ys = _experts(xs.reshape(n_blocks * tb * PK_ROWS, LANES), block_expert, first, slot, next_expert,
                  n_active, w1, w3, w2, layer)
    return _combine(h_tiles, rows, gates.T, ys, ws1, ws3, ws2, g, b, token_tile_out)


def _proj_body(h_ref, w_ref, *o_refs, tt, scales):
    y = _bdot(_rows_load(h_ref, tt).astype(BF16), w_ref[...])
    _store_head_slabs(y, o_refs, scales)


def _project(h_tiles, w, scales):
    n_tok = h_tiles.shape[0] // TOK_ROWS
    tt = PROJ_TT
    n_out = len(scales)
    out_spec = pl.BlockSpec((PAIR_SLABS, tt, LANES), lambda i: (0, i, 0))
    return pl.pallas_call(
        functools.partial(_proj_body, tt=tt, scales=tuple(scales)),
        grid=(n_tok // tt,),
        in_specs=[pl.BlockSpec((tt * TOK_ROWS, LANES), lambda i: (i, 0)),
                  pl.BlockSpec(w.shape, lambda i: (0, 0))],
        out_specs=[out_spec] * n_out,
        out_shape=[jax.ShapeDtypeStruct((PAIR_SLABS, n_tok, LANES), I32)] * n_out,
        compiler_params=pltpu.CompilerParams(
            dimension_semantics=("arbitrary",), vmem_limit_bytes=VMEM_LIMIT),
        name="projection",
    )(h_tiles, w.astype(BF16))


def _attn_body(q_ref, kp_ref, kc_ref, vp_ref, vc_ref, o_ref, lse_ref, *, dilation, n_steps, blocks):
    t = ATT_BLOCK
    qi = lax.broadcasted_iota(jnp.int32, (t, 2 * t), 0)
    kj = lax.broadcasted_iota(jnp.int32, (t, 2 * t), 1)
    dist = t + qi - kj
    band = (dist >= 0) & (dist <= n_steps)
    band_at_start = band & ((pl.program_id(1) > 0) | (kj >= t))
    token_dist = (dilation * dist).astype(F32)
    lane = lax.broadcasted_iota(jnp.int32, (t, LANES), 1)
    low_half = lane < HEAD_DIM
    lane_head = lane // LSE_LANES
    contract_last = (((1,), (1,)), ((), ()))

    def unpack(words, which):
        return pltpu.unpack_elementwise(words, index=which, packed_dtype=BF16,
                                        unpacked_dtype=F32).astype(BF16)

    def block(rows, k_prev, v_prev, prev_rows, valid):
        lse_all = jnp.zeros((t, LANES), F32)
        for slab in range(PAIR_SLABS):
            q_words = q_ref.at[slab][rows, :]
            k_words = jnp.concatenate([k_prev.at[slab][prev_rows, :], kc_ref.at[slab][rows, :]], axis=0)
            v_words = jnp.concatenate([v_prev.at[slab][prev_rows, :], vc_ref.at[slab][rows, :]], axis=0)
            pair_outs = []
            for which in range(2):
                pair = 2 * slab + which
                q2, k2, v2 = unpack(q_words, which), unpack(k_words, which), unpack(v_words, which)
                outs = []
                for half in range(2):
                    head = 2 * pair + half
                    slope = 2.0 ** (-8.0 * (head + 1) / ATT_HEADS)
                    keep = low_half if half == 0 else ~low_half
                    qh = jnp.where(keep, q2, jnp.zeros_like(q2))
                    s = lax.dot_general(qh, k2, contract_last, preferred_element_type=F32)
                    s = jnp.where(valid, s - slope * token_dist, MASK_VALUE)
                    m = jnp.max(s, axis=-1, keepdims=True)
                    p = jnp.exp(s - m)
                    l = jnp.sum(p, axis=-1, keepdims=True)
                    outs.append(_bdot(p.astype(BF16), v2) / l)
                    lse_all = jnp.where(lane_head == head, m + jnp.log(l), lse_all)
                pair_outs.append(jnp.where(low_half, outs[0], outs[1]))
            o_ref.at[slab][rows, :] = pltpu.pack_elementwise(pair_outs, packed_dtype=BF16)
        lse_ref[rows, :] = lse_all

    if dilation == 1:
        block(pl.ds(0, t), kp_ref, vp_ref, pl.ds(0, t), band_at_start)
        for b in range(1, blocks):
            block(pl.ds(b * t, t), kc_ref, vc_ref, pl.ds((b - 1) * t, t), band)
    else:
        def residue(r, carry):
            rows = pl.ds(r, t, stride=dilation)
            block(rows, kp_ref, vp_ref, rows, band_at_start)
            return carry
        lax.fori_loop(0, dilation, residue, 0, unroll=min(dilation, ATTN_RESIDUE_UNROLL))


def _attn_branch(q, k, v, bsz, seq, window, dilation):
    blocks = ATTN_D1_BLOCKS if dilation == 1 else 1
    span = ATT_BLOCK * dilation * blocks
    n_spans = seq // span
    prev_rows = ATT_BLOCK if dilation == 1 else span
    prev_per_span = span // prev_rows
    cur = pl.BlockSpec((PAIR_SLABS, span, LANES), lambda bi, n: (0, bi * n_spans + n, 0))
    prev = pl.BlockSpec(
        (PAIR_SLABS, prev_rows, LANES),
        lambda bi, n: (0, (bi * n_spans + n) * prev_per_span - jnp.minimum(n, 1), 0))
    return pl.pallas_call(
        functools.partial(_attn_body, dilation=dilation, n_steps=window // dilation, blocks=blocks),
        grid=(bsz, n_spans),
        in_specs=[cur, prev, cur, prev, cur],
        out_specs=[cur, pl.BlockSpec((span, LANES), lambda bi, n: (bi * n_spans + n, 0))],
        out_shape=[jax.ShapeDtypeStruct((PAIR_SLABS, bsz * seq, LANES), I32),
                   jax.ShapeDtypeStruct((bsz * seq, LANES), F32)],
        compiler_params=pltpu.CompilerParams(
            dimension_semantics=("arbitrary",) * 2, vmem_limit_bytes=VMEM_LIMIT),
        name=f"dilated_attn_d{dilation}",
    )(q, k, k, v, v)


def _merge_body(*refs, tt):
    o_refs = refs[:N_GROUPS]
    lse_refs = refs[N_GROUPS:2 * N_GROUPS]
    h_ref, wo_ref, g_ref, b_ref, wrt_ref, bias_ref, out_ref, *route_refs = refs[2 * N_GROUPS:]
    lses = [r[...] for r in lse_refs]
    m = functools.reduce(jnp.maximum, lses)
    es = [jnp.exp(l - m) for l in lses]
    den = functools.reduce(lambda a, c: a + c, es)
    row = lax.broadcasted_iota(jnp.int32, (LANES, O_WIDTH), 0)
    col = lax.broadcasted_iota(jnp.int32, (LANES, O_WIDTH), 1)
    select = jnp.where(row == (col // HEAD_DIM) * LSE_LANES, 1.0, 0.0).astype(BF16)

    def widen(w):
        hi = w.astype(BF16)
        lo = (w - hi.astype(F32)).astype(BF16)
        return _bdot(hi, select) + _bdot(lo, select)

    def branch_out(o_ref):
        return jnp.concatenate(
            [pltpu.unpack_elementwise(o_ref[pair // 2], index=pair % 2, packed_dtype=BF16,
                                      unpacked_dtype=F32) for pair in range(ATT_HEADS // 2)], axis=1)

    o = functools.reduce(lambda a, c: a + c,
                         [widen(e / den) * branch_out(r) for e, r in zip(es, o_refs)])
    mix = _bdot(o.astype(BF16), wo_ref[...])
    z = DEEPNORM_ALPHA * _rows_load(h_ref, tt) + mix
    merged = _layer_norm(z, g_ref[...], b_ref[...])
    _rows_store(out_ref, merged, tt)
    _route(merged, wrt_ref, bias_ref, *route_refs, tt)


def _merge(outs, lses, h_tiles, w_o, g, b, w_router, router_bias):
    n_tok = h_tiles.shape[0] // TOK_ROWS
    tt = MERGE_TT
    kt = lambda dt: jax.ShapeDtypeStruct((TOP_K, n_tok), dt)
    tok_spec = pl.BlockSpec((TOP_K, tt), lambda i: (0, i))
    cnt_spec = pl.BlockSpec((N_EXPERTS, 1), lambda i: (0, 0))
    o_spec = pl.BlockSpec((PAIR_SLABS, tt, LANES), lambda i: (0, i, 0))
    lse_spec = pl.BlockSpec((tt, LANES), lambda i: (i, 0))
    tiles = pl.BlockSpec((tt * TOK_ROWS, LANES), lambda i: (i, 0))
    full = lambda shape: pl.BlockSpec(shape, lambda i: (0,) * len(shape))
    return pl.pallas_call(
        functools.partial(_merge_body, tt=tt),
        grid=(n_tok // tt,),
        in_specs=[o_spec] * N_GROUPS + [lse_spec] * N_GROUPS + [
            tiles, full((O_WIDTH, D_MODEL)), full((1, D_MODEL)), full((1, D_MODEL)),
            full((N_EXPERTS, D_MODEL)), cnt_spec],
        out_specs=[tiles, tok_spec, tok_spec, tok_spec, cnt_spec],
        out_shape=[jax.ShapeDtypeStruct(h_tiles.shape, F32), kt(jnp.int32), kt(F32), kt(jnp.int32),
                   jax.ShapeDtypeStruct((N_EXPERTS, 1), jnp.int32)],
        scratch_shapes=[pltpu.VMEM((N_EXPERTS, 1), F32)],
        compiler_params=pltpu.CompilerParams(
            dimension_semantics=("arbitrary",), vmem_limit_bytes=VMEM_LIMIT),
        name="attn_merge",
    )(*outs, *lses, h_tiles, w_o.astype(BF16), g.reshape(1, -1), b.reshape(1, -1),
      w_router.T.astype(BF16), router_bias.reshape(N_EXPERTS, 1))


def kernel(x, a_w_in, a_conv_w, a_conv_b, a_w_gate_a, a_b_gate_a, a_w_gate_x, a_b_gate_x, a_lambda, a_w_out, w_kv_shared, b_w_q, b_w_o, moe_w_router, moe_router_bias, moe_w1, moe_w3, moe_w2, moe_ws1, moe_ws3, moe_ws2, ln_g, ln_b):
    bsz, seq, d = x.shape
    assert d == D_MODEL and seq % (DILATION_PATTERNS[-1][1] * ATT_BLOCK) == 0

    def moe(h_tiles, layer, token_tile_out, routing=None):
        return _moe_layer(h_tiles, moe_w_router[layer], moe_router_bias[layer], moe_w1, moe_w3,
                          moe_w2, layer, moe_ws1[layer], moe_ws3[layer], moe_ws2[layer],
                          ln_g[layer, 1], ln_b[layer, 1], token_tile_out, routing)

    h = _rglru_layer(x, a_w_in[0], a_conv_w[0], a_conv_b[0], a_w_gate_a[0], a_b_gate_a[0],
                     a_w_gate_x[0], a_b_gate_x[0], a_lambda[0], a_w_out[0], ln_g[0, 0], ln_b[0, 0])
    h = moe(h, 0, True)
    kvq = _project(h, jnp.concatenate([w_kv_shared, b_w_q[0]], axis=1),
                   (1.0,) * (2 * N_GROUPS) + (HEAD_DIM ** -0.5,) * N_GROUPS)
    ks, vs, qs = kvq[:N_GROUPS], kvq[N_GROUPS:2 * N_GROUPS], kvq[2 * N_GROUPS:]

    outs, lses = [], []
    for grp, (window, dilation) in enumerate(DILATION_PATTERNS):
        o, lse = _attn_branch(qs[grp], ks[grp], vs[grp], bsz, seq, window, dilation)
        outs.append(o)
        lses.append(lse)
    h, *routing = _merge(outs, lses, h, b_w_o[0], ln_g[1, 0], ln_b[1, 0], moe_w_router[1],
                         moe_router_bias[1])
    h = moe(h, 1, False, tuple(routing))
    return h.reshape(bsz, seq, d)
```
